```python
import functools
import jax
import jax.numpy as jnp
from jax import lax
import numpy as np

D_MODEL = 2048
BATCH = 8
SEQ = 4096
DEPTH = 4

CTX_LEN = 256
GRID_W = 64
HEAD_DIM = 128
N_HEADS = D_MODEL // HEAD_DIM
ROPE_BASE = 10000.0
NORM_EPS = 1e-6
NEG_INF = -1e30
N_MOD = 6

NA_ROWS = 8
NA_COLS = 16
NA_QCOLS = 16
NA_KCOLS = 32

SWA_WINDOW = 128
SWA_BLOCK = 128
N_KV_HEADS = N_HEADS // 4

MLA_Q_RANK = D_MODEL // 4
MLA_KV_RANK = D_MODEL // 8
MLA_NOPE_DIM = 128
MLA_ROPE_DIM = 64
MLA_V_DIM = 128
MLA_BLOCK = 128

N_GROUPS = 4
EXPERTS_PER_GROUP = 8
N_EXPERTS = N_GROUPS * EXPERTS_PER_GROUP
TOP_K_IN_GROUP = 2
D_EXPERT = D_MODEL // 4
MOE_BLOCK = 256

kernel_name = 'hybrid_dit_na_swa_mla_hmoe'


def rms_norm(x, g):
    x32 = x.astype(jnp.float32)
    y = x32 * lax.rsqrt(jnp.mean(x32 * x32, axis=-1, keepdims=True) + NORM_EPS)
    return (y * g.astype(jnp.float32)).astype(x.dtype)


def modulate(h, shift, scale):
    return h * (1.0 + scale) + shift


def softmax_f32(logits):
    return jax.nn.softmax(logits.astype(jnp.float32), axis=-1)


def axial_angles(n, rot_dim):
    t = jnp.arange(n, dtype=jnp.int32)
    row = (t // GRID_W).astype(jnp.float32)
    col = (t % GRID_W).astype(jnp.float32)
    n_freq = rot_dim // 4
    inv = ROPE_BASE ** (-jnp.arange(n_freq, dtype=jnp.float32) / n_freq)
    ang = jnp.concatenate([row[:, None] * inv, col[:, None] * inv], axis=-1)
    return jnp.cos(ang), jnp.sin(ang)


def apply_rope(x, cos, sin):
    half = x.shape[-1] // 2
    x1, x2 = x[..., :half], x[..., half:]
    c, s = cos.astype(x.dtype), sin.astype(x.dtype)
    return jnp.concatenate([x1 * c - x2 * s, x2 * c + x1 * s], axis=-1)


def dense_mha(q, k, v, scale):
    p = softmax_f32(jnp.einsum('bqhd,bkhd->bhqk', q, k).astype(jnp.float32) * scale)
    return jnp.einsum('bhqk,bkhd->bqhd', p.astype(v.dtype), v)


def neighbourhood_attention(h_lat, h_ctx, with_ctx_out, *, w_qkv, rpb, w_o):
    B, S, _ = h_lat.shape
    L = h_ctx.shape[1]
    rows = S // GRID_W
    kr = min(NA_ROWS, rows)
    n_cb = GRID_W // NA_QCOLS
    scale = HEAD_DIM ** -0.5
    qkv = (h_lat @ w_qkv).reshape(B, rows, GRID_W, 3, N_HEADS, HEAD_DIM)
    q, k, v = qkv[..., 0, :, :], qkv[..., 1, :, :], qkv[..., 2, :, :]
    qkv_c = (h_ctx @ w_qkv).reshape(B, L, 3, N_HEADS, HEAD_DIM)
    qc, kc, vc = qkv_c[:, :, 0], qkv_c[:, :, 1], qkv_c[:, :, 2]

    qcol = np.arange(GRID_W).reshape(n_cb, NA_QCOLS)
    qcol0 = np.clip(qcol - NA_COLS // 2, 0, GRID_W - NA_COLS)
    kcol = np.clip(np.arange(n_cb) * NA_QCOLS - NA_COLS // 2, 0, GRID_W - NA_KCOLS)[:, None] + np.arange(NA_KCOLS)
    col_ok = (kcol[:, None, :] >= qcol0[:, :, None]) & (kcol[:, None, :] < qcol0[:, :, None] + NA_COLS)
    dcol = np.clip(kcol[:, None, :] - qcol[:, :, None] + NA_COLS - 1, 0, 2 * NA_COLS - 2)
    col_ok_b = jnp.asarray(col_ok[:, :, None, :])

    def row_block(r):
        r0 = jnp.clip(r - kr // 2, 0, rows - kr)
        k_blk = lax.dynamic_slice_in_dim(k, r0, kr, axis=1)[:, :, kcol]
        v_blk = lax.dynamic_slice_in_dim(v, r0, kr, axis=1)[:, :, kcol]
        q_r = lax.dynamic_index_in_dim(q, r, axis=1, keepdims=False).reshape(B, n_cb, NA_QCOLS, N_HEADS, HEAD_DIM)
        s_loc = jnp.einsum('bjqhd,bmjkhd->bhjqmk', q_r, k_blk).astype(jnp.float32) * scale
        drow = r0 + jnp.arange(kr) - r + NA_ROWS - 1
        bias = jnp.transpose(rpb[:, drow][:, :, dcol], (0, 2, 3, 1, 4))
        s_loc = jnp.where(col_ok_b, s_loc + bias.astype(jnp.float32), NEG_INF)
        s_loc = s_loc.reshape(B, N_HEADS, n_cb, NA_QCOLS, kr * NA_KCOLS)
        s_ctx = jnp.einsum('bjqhd,blhd->bhjql', q_r, kc).astype(jnp.float32) * scale
        p = softmax_f32(jnp.concatenate([s_loc, s_ctx], axis=-1)).astype(v.dtype)
        p_loc = p[..., :kr * NA_KCOLS].reshape(B, N_HEADS, n_cb, NA_QCOLS, kr, NA_KCOLS)
        o = (jnp.einsum('bhjqmk,bmjkhd->bjqhd', p_loc, v_blk)
             + jnp.einsum('bhjql,blhd->bjqhd', p[..., kr * NA_KCOLS:], vc))
        return o.reshape(B, GRID_W, N_HEADS * HEAD_DIM)

    o = lax.map(row_block, jnp.arange(rows))
    y_lat = jnp.moveaxis(o, 0, 1).reshape(B, S, N_HEADS * HEAD_DIM) @ w_o
    y_ctx = None
    if with_ctx_out:
        y_ctx = dense_mha(qc, kc, vc, scale).reshape(B, L, N_HEADS * HEAD_DIM) @ w_o
    return y_lat, y_ctx


def sink_softmax_values(s, v, sink_g):
    sink_col = jnp.broadcast_to(sink_g.astype(jnp.float32)[None, :, :, None, None], s.shape[:-1] + (1,))
    p = softmax_f32(jnp.concatenate([s, sink_col], axis=-1))[..., :-1]
    return jnp.einsum('bkgqm,bmkd->bqkgd', p.astype(v.dtype), v)


def window_attention(h_lat, h_ctx, with_ctx_out, *, w_qkv, sink, w_o, cos, sin):
    B, S, _ = h_lat.shape
    L = h_ctx.shape[1]
    G = N_HEADS // N_KV_HEADS
    scale = HEAD_DIM ** -0.5
    nq, nkv = N_HEADS * HEAD_DIM, N_KV_HEADS * HEAD_DIM

    def split_qkv(h):
        n = h.shape[1]
        qkv = h @ w_qkv
        q = qkv[..., :nq].reshape(B, n, N_HEADS, HEAD_DIM)
        k = qkv[..., nq:nq + nkv].reshape(B, n, N_KV_HEADS, HEAD_DIM)
        v = qkv[..., nq + nkv:].reshape(B, n, N_KV_HEADS, HEAD_DIM)
        return q, k, v

    q, k, v = split_qkv(h_lat)
    q = apply_rope(q, cos[:, None, :], sin[:, None, :]).reshape(B, S, N_KV_HEADS, G, HEAD_DIM)
    k = apply_rope(k, cos[:, None, :], sin[:, None, :])
    qc, kc, vc = split_qkv(h_ctx)
    qc = qc.reshape(B, L, N_KV_HEADS, G, HEAD_DIM)
    sink_g = sink.reshape(N_KV_HEADS, G)

    kp = jnp.pad(k, ((0, 0), (SWA_BLOCK, SWA_BLOCK), (0, 0), (0, 0)))
    vp = jnp.pad(v, ((0, 0), (SWA_BLOCK, SWA_BLOCK), (0, 0), (0, 0)))
    offs = np.arange(3 * SWA_BLOCK) - SWA_BLOCK
    band_ok = np.abs(offs[None, :] - np.arange(SWA_BLOCK)[:, None]) <= SWA_WINDOW
    ctx_ok = jnp.ones((SWA_BLOCK, L), bool)

    def band_block(i):
        start = i * SWA_BLOCK
        q_i = lax.dynamic_slice_in_dim(q, start, SWA_BLOCK, axis=1)
        k_i = jnp.concatenate([lax.dynamic_slice_in_dim(kp, start, 3 * SWA_BLOCK, axis=1), kc], axis=1)
        v_i = jnp.concatenate([lax.dynamic_slice_in_dim(vp, start, 3 * SWA_BLOCK, axis=1), vc], axis=1)
        kpos = start + jnp.asarray(offs)
        valid = jnp.concatenate([band_ok & ((kpos >= 0) & (kpos < S))[None, :], ctx_ok], axis=-1)
        s = jnp.einsum('bqkgd,bmkd->bkgqm', q_i, k_i).astype(jnp.float32) * scale
        o = sink_softmax_values(jnp.where(valid, s, NEG_INF), v_i, sink_g)
        return o.reshape(B, SWA_BLOCK, nq)

    o = lax.map(band_block, jnp.arange(S // SWA_BLOCK))
    y_lat = jnp.moveaxis(o, 0, 1).reshape(B, S, nq) @ w_o
    y_ctx = None
    if with_ctx_out:
        s_c = jnp.einsum('bqkgd,bmkd->bkgqm', qc, kc).astype(jnp.float32) * scale
        y_ctx = sink_softmax_values(s_c, vc, sink_g).reshape(B, L, nq) @ w_o
    return y_lat, y_ctx


def latent_attention(h_lat, h_ctx, with_ctx_out, *, w_dq, q_norm_g, w_uq, w_dkv, kv_norm_g, w_ukv, w_o, cos, sin):
    B, S, _ = h_lat.shape
    L = h_ctx.shape[1]
    scale = (MLA_NOPE_DIM + MLA_ROPE_DIM) ** -0.5

    def project_q(h, rotary):
        n = h.shape[1]
        q = (rms_norm(h @ w_dq, q_norm_g) @ w_uq).reshape(B, n, N_HEADS, MLA_NOPE_DIM + MLA_ROPE_DIM)
        q_rope = q[..., MLA_NOPE_DIM:]
        if rotary:
            q_rope = apply_rope(q_rope, cos[:, None, :], sin[:, None, :])
        return q[..., :MLA_NOPE_DIM], q_rope

    def project_kv(h, rotary):
        n = h.shape[1]
        ckv = h @ w_dkv
        kv = (rms_norm(ckv[..., :MLA_KV_RANK], kv_norm_g) @ w_ukv).reshape(B, n, N_HEADS, MLA_NOPE_DIM + MLA_V_DIM)
        k_rope = ckv[..., MLA_KV_RANK:]
        if rotary:
            k_rope = apply_rope(k_rope, cos, sin)
        return kv[..., :MLA_NOPE_DIM], k_rope, kv[..., MLA_NOPE_DIM:]

    def scores(qn, qr, kn, kr):
        s = jnp.einsum('bqhd,bkhd->bhqk', qn, kn) + jnp.einsum('bqhr,bkr->bhqk', qr, kr)
        return s.astype(jnp.float32) * scale

    qn, qr = project_q(h_lat, True)
    kn, kr, v = project_kv(h_lat, True)
    kn_c, kr_c, v_c = project_kv(h_ctx, False)
    kn_all = jnp.concatenate([kn_c, kn], axis=1)
    kr_all = jnp.concatenate([kr_c, kr], axis=1)
    v_all = jnp.concatenate([v_c, v], axis=1)

    def query_block(i):
        start = i * MLA_BLOCK
        qn_i = lax.dynamic_slice_in_dim(qn, start, MLA_BLOCK, axis=1)
        qr_i = lax.dynamic_slice_in_dim(qr, start, MLA_BLOCK, axis=1)
        p = softmax_f32(scores(qn_i, qr_i, kn_all, kr_all))
        o = jnp.einsum('bhqk,bkhd->bqhd', p.astype(v_all.dtype), v_all)
        return o.reshape(B, MLA_BLOCK, N_HEADS * MLA_V_DIM)

    o = lax.map(query_block, jnp.arange(S // MLA_BLOCK))
    y_lat = jnp.moveaxis(o, 0, 1).reshape(B, S, N_HEADS * MLA_V_DIM) @ w_o
    y_ctx = None
    if with_ctx_out:
        qn_c, qr_c = project_q(h_ctx, False)
        p_c = softmax_f32(scores(qn_c, qr_c, kn_c, kr_c))
        o_c = jnp.einsum('bhqk,bkhd->bqhd', p_c.astype(v_c.dtype), v_c)
        y_ctx = o_c.reshape(B, L, N_HEADS * MLA_V_DIM) @ w_o
    return y_lat, y_ctx


def hier_moe(h, w_grp, b_grp, w_rt, b_rt, w_gate, w_up, w_down):
    lead = h.shape[:-1]
    d = h.shape[-1]
    xt = h.reshape(-1, d)
    t = xt.shape[0]
    g_w, g_idx = lax.top_k(softmax_f32(xt @ w_grp + b_grp), 1)
    e_logits = (xt @ w_rt + b_rt).reshape(t, N_GROUPS, EXPERTS_PER_GROUP)
    e_logits = jnp.take_along_axis(e_logits, g_idx[:, :, None], axis=1)[:, 0]
    e_w, e_idx = lax.top_k(softmax_f32(e_logits), TOP_K_IN_GROUP)
    wt = (g_w * e_w / jnp.sum(e_w, axis=-1, keepdims=True)).reshape(-1)
    eid = (g_idx * EXPERTS_PER_GROUP + e_idx).reshape(-1).astype(jnp.int32)
    n_assign = t * TOP_K_IN_GROUP
    order = jnp.argsort(eid)
    eid_s = eid[order]
    tok_s = (order // TOP_K_IN_GROUP).astype(jnp.int32)
    counts = jax.ops.segment_sum(jnp.ones((n_assign,), jnp.int32), eid, num_segments=N_EXPERTS)
    start = jnp.cumsum(counts) - counts
    pcounts = (counts + MOE_BLOCK - 1) // MOE_BLOCK * MOE_BLOCK
    pend = jnp.cumsum(pcounts)
    dest = (pend - pcounts)[eid_s] + jnp.arange(n_assign, dtype=jnp.int32) - start[eid_s]
    n_blk = -(-n_assign // MOE_BLOCK) + N_EXPERTS
    rows_total = n_blk * MOE_BLOCK
    row_tok = jnp.full((rows_total,), t, jnp.int32).at[dest].set(tok_s)
    row_w = jnp.zeros((rows_total,), h.dtype).at[dest].set(wt[order].astype(h.dtype))
    blk_e = jnp.minimum(jnp.searchsorted(pend, jnp.arange(n_blk, dtype=jnp.int32) * MOE_BLOCK, side='right'), N_EXPERTS - 1)
    x_disp = xt[jnp.minimum(row_tok, t - 1)].reshape(n_blk, MOE_BLOCK, d)

    def expert_block(args):
        xb, e = args
        return (jax.nn.silu(xb @ w_gate[e]) * (xb @ w_up[e])) @ w_down[e]

    y = lax.map(expert_block, (x_disp, blk_e)).reshape(rows_total, d)
    out = jax.ops.segment_sum(y * row_w[:, None], row_tok, num_segments=t + 1)[:t]
    return out.reshape(lead + (d,))


def setup_inputs(seed: int = 0) -> dict:
    key = jax.random.key(seed)
    ks = jax.random.split(key, 32)
    D = D_MODEL
    H = N_HEADS

    def nrm(i, shape, scale):
        return jax.random.normal(ks[i], shape, jnp.float32) * scale

    return {
        'x': nrm(0, (BATCH, SEQ, D), 1.0),
        'c': nrm(1, (BATCH, D), 1.0),
        'ctx': nrm(2, (BATCH, CTX_LEN, D), 1.0),
        'c_ctx': nrm(3, (D,), 1.0),
        'mod_w': nrm(4, (DEPTH, D, N_MOD * D), 0.5 * D ** -0.5),
        'mod_b': nrm(5, (DEPTH, N_MOD * D), 0.02),
        'norm_mix_g': 1.0 + nrm(6, (DEPTH, D), 0.02),
        'norm_ffn_g': 1.0 + nrm(7, (DEPTH, D), 0.02),
        'router_grp_w': nrm(8, (DEPTH, D, N_GROUPS), D ** -0.5),
        'router_grp_b': nrm(9, (DEPTH, N_GROUPS), 0.01),
        'router_exp_w': nrm(10, (DEPTH, D, N_EXPERTS), D ** -0.5),
        'router_exp_b': nrm(11, (DEPTH, N_EXPERTS), 0.01),
        'exp_w_gate': nrm(12, (DEPTH, N_EXPERTS, D, D_EXPERT), D ** -0.5),
        'exp_w_up': nrm(13, (DEPTH, N_EXPERTS, D, D_EXPERT), D ** -0.5),
        'exp_w_down': nrm(14, (DEPTH, N_EXPERTS, D_EXPERT, D), D_EXPERT ** -0.5),
        'l0_na_w_qkv': nrm(15, (D, 3 * H * HEAD_DIM), D ** -0.5),
        'l0_na_rpb': nrm(16, (H, 2 * NA_ROWS - 1, 2 * NA_COLS - 1), 0.1),
        'l0_na_w_o': nrm(17, (H * HEAD_DIM, D), (H * HEAD_DIM) ** -0.5),
        'l1_swa_w_qkv': nrm(18, (D, (H + 2 * N_KV_HEADS) * HEAD_DIM), D ** -0.5),
        'l1_swa_sink': nrm(19, (H,), 1.0),
        'l1_swa_w_o': nrm(20, (H * HEAD_DIM, D), (H * HEAD_DIM) ** -0.5),
        'l2_mla_w_dq': nrm(21, (D, MLA_Q_RANK), D ** -0.5),
        'l2_mla_q_norm_g': 1.0 + nrm(22, (MLA_Q_RANK,), 0.02),
        'l2_mla_w_uq': nrm(23, (MLA_Q_RANK, H * (MLA_NOPE_DIM + MLA_ROPE_DIM)), MLA_Q_RANK ** -0.5),
        'l2_mla_w_dkv': nrm(24, (D, MLA_KV_RANK + MLA_ROPE_DIM), D ** -0.5),
        'l2_mla_kv_norm_g': 1.0 + nrm(25, (MLA_KV_RANK,), 0.02),
        'l2_mla_w_ukv': nrm(26, (MLA_KV_RANK, H * (MLA_NOPE_DIM + MLA_V_DIM)), MLA_KV_RANK ** -0.5),
        'l2_mla_w_o': nrm(27, (H * MLA_V_DIM, D), (H * MLA_V_DIM) ** -0.5),
        'l3_na_w_qkv': nrm(28, (D, 3 * H * HEAD_DIM), D ** -0.5),
        'l3_na_rpb': nrm(29, (H, 2 * NA_ROWS - 1, 2 * NA_COLS - 1), 0.1),
        'l3_na_w_o': nrm(30, (H * HEAD_DIM, D), (H * HEAD_DIM) ** -0.5),
        'final_norm_g': 1.0 + nrm(31, (D,), 0.02),
    }


def reference(x, c, ctx, c_ctx, mod_w, mod_b, norm_mix_g, norm_ffn_g,
              router_grp_w, router_grp_b, router_exp_w, router_exp_b,
              exp_w_gate, exp_w_up, exp_w_down,
              l0_na_w_qkv, l0_na_rpb, l0_na_w_o,
              l1_swa_w_qkv, l1_swa_sink, l1_swa_w_o,
              l2_mla_w_dq, l2_mla_q_norm_g, l2_mla_w_uq, l2_mla_w_dkv, l2_mla_kv_norm_g, l2_mla_w_ukv, l2_mla_w_o,
              l3_na_w_qkv, l3_na_rpb, l3_na_w_o,
              final_norm_g):
    B, S, D = x.shape
    L = ctx.shape[1]
    cos_swa, sin_swa = axial_angles(S, HEAD_DIM)
    cos_mla, sin_mla = axial_angles(S, MLA_ROPE_DIM)
    layer_mixers = [
        functools.partial(neighbourhood_attention, w_qkv=l0_na_w_qkv, rpb=l0_na_rpb, w_o=l0_na_w_o),
        functools.partial(window_attention, w_qkv=l1_swa_w_qkv, sink=l1_swa_sink, w_o=l1_swa_w_o,
                          cos=cos_swa, sin=sin_swa),
        functools.partial(latent_attention, w_dq=l2_mla_w_dq, q_norm_g=l2_mla_q_norm_g, w_uq=l2_mla_w_uq,
                          w_dkv=l2_mla_w_dkv, kv_norm_g=l2_mla_kv_norm_g, w_ukv=l2_mla_w_ukv,
                          w_o=l2_mla_w_o, cos=cos_mla, sin=sin_mla),
        functools.partial(neighbourhood_attention, w_qkv=l3_na_w_qkv, rpb=l3_na_rpb, w_o=l3_na_w_o),
    ]
    silu_c = jax.nn.silu(c)
    silu_cc = jax.nn.silu(c_ctx)[None, :]
    h_lat, h_ctx = x, ctx
    for i in range(DEPTH):
        last = i == DEPTH - 1
        m_lat = jnp.split((silu_c @ mod_w[i] + mod_b[i])[:, None, :], N_MOD, axis=-1)
        m_ctx = jnp.split((silu_cc @ mod_w[i] + mod_b[i])[:, None, :], N_MOD, axis=-1)
        y_lat, y_ctx = layer_mixers[i](
            modulate(rms_norm(h_lat, norm_mix_g[i]), m_lat[0], m_lat[1]),
            modulate(rms_norm(h_ctx, norm_mix_g[i]), m_ctx[0], m_ctx[1]),
            not last)
        h_lat = h_lat + m_lat[2] * y_lat
        moe = functools.partial(hier_moe, w_grp=router_grp_w[i], b_grp=router_grp_b[i],
                                w_rt=router_exp_w[i], b_rt=router_exp_b[i],
                                w_gate=exp_w_gate[i], w_up=exp_w_up[i], w_down=exp_w_down[i])
        f_lat = modulate(rms_norm(h_lat, norm_ffn_g[i]), m_lat[3], m_lat[4])
        if last:
            h_lat = h_lat + m_lat[5] * moe(f_lat)
        else:
            h_ctx = h_ctx + m_ctx[2] * y_ctx
            f_ctx = modulate(rms_norm(h_ctx, norm_ffn_g[i]), m_ctx[3], m_ctx[4])
            f_out = moe(jnp.concatenate([f_ctx, f_lat], axis=1))
            h_ctx = h_ctx + m_ctx[5] * f_out[:, :L]
            h_lat = h_lat + m_lat[5] * f_out[:, L:]
    return rms_norm(h_lat, final_norm_g)
```

```python
import functools

import numpy as np
import jax
import jax.numpy as jnp
from jax import lax
from jax.experimental import pallas as pl
from jax.experimental.pallas import tpu as pltpu

GRID_W = 64
HEAD_DIM = 128
ROPE_BASE = 10000.0
NORM_EPS = 1e-6
NEG_INF = -1e30
N_MOD = 6

NA_ROWS = 8
NA_COLS = 16
NA_TILE_ROWS = 4
NA_WIN_ROWS = 12

SWA_WINDOW = 128
SWA_TQ = 256
SWA_TK = 512

MLA_NOPE_DIM = 128
MLA_ROPE_DIM = 64
MLA_TQ = 256

N_GROUPS = 4
EXPERTS_PER_GROUP = 8
N_EXPERTS = N_GROUPS * EXPERTS_PER_GROUP
MOE_BLOCK = 256
GATHER_CHUNK = 512

LANES = 128
ROW_TILE = 256
VMEM_LIMIT = 56 * 1024 * 1024

BF16 = jnp.bfloat16
F32 = jnp.float32


def _cparams(sem):
    return pltpu.CompilerParams(dimension_semantics=sem, vmem_limit_bytes=VMEM_LIMIT)


def _dot(a, b):
    return jnp.dot(a, b, preferred_element_type=F32)


def _dot_nt(a, b):
    return lax.dot_general(a, b, (((1,), (1,)), ((), ())), preferred_element_type=F32)


def _mod_kernel(x_ref, w_ref, b_ref, o_ref):
    x = x_ref[...]
    sx = (x * jax.nn.sigmoid(x)).astype(BF16)
    o_ref[0] = _dot(sx, w_ref[0].astype(BF16)) + b_ref[0]


def modulation_tables(c, c_ctx, mod_w, mod_b):
    depth, d, n_out = mod_w.shape
    b = c.shape[0]
    rows = 16
    xin = jnp.zeros((rows, d), F32).at[:b].set(c).at[b].set(c_ctx)
    tn = 1024
    out = pl.pallas_call(
        _mod_kernel,
        grid=(depth, n_out // tn),
        in_specs=[pl.BlockSpec((rows, d), lambda i, j: (0, 0)),
                  pl.BlockSpec((1, d, tn), lambda i, j: (i, 0, j)),
                  pl.BlockSpec((1, 1, tn), lambda i, j: (i, 0, j))],
        out_specs=pl.BlockSpec((1, rows, tn), lambda i, j: (i, 0, j)),
        out_shape=jax.ShapeDtypeStruct((depth, rows, n_out), F32),
        compiler_params=_cparams(("parallel", "parallel")),
        name="adaln_mod",
    )(xin, mod_w, mod_b.reshape(depth, 1, n_out))
    lat = out[:, :b].reshape(depth, b, 1, N_MOD, d)
    ctx = jnp.broadcast_to(out[:, b].reshape(depth, 1, 1, N_MOD, d), (depth, b, 1, N_MOD, d))
    return jnp.concatenate([ctx, lat], axis=2)


def _rms_mod(x, g, shift, scale):
    ms = jnp.mean(x * x, axis=-1, keepdims=True)
    y = x * lax.rsqrt(ms + NORM_EPS) * g
    return y * (1.0 + scale) + shift


def _norm_mod_kernel(h_ref, g_ref, mod_ref, o_ref, *, shift_idx):
    f = _rms_mod(h_ref[0], g_ref[...], mod_ref[0, shift_idx:shift_idx + 1, :],
                 mod_ref[0, shift_idx + 1:shift_idx + 2, :])
    o_ref[0] = f.astype(o_ref.dtype)


def _route(logits):
    lane = lax.broadcasted_iota(jnp.int32, logits.shape, 1).astype(F32)
    big = float(LANES)

    def first_lane(mask):
        return jnp.min(jnp.where(mask, lane, big), axis=-1, keepdims=True)

    in_grp = lane < N_GROUPS
    lg = jnp.where(in_grp, logits, NEG_INF)
    m_g = jnp.max(lg, axis=-1, keepdims=True)
    g_idx = first_lane(in_grp & (lg == m_g))
    g_w = 1.0 / jnp.sum(jnp.where(in_grp, jnp.exp(lg - m_g), 0.0), axis=-1, keepdims=True)
    e_lo = N_GROUPS + g_idx * EXPERTS_PER_GROUP
    in_e = (lane >= e_lo) & (lane < e_lo + EXPERTS_PER_GROUP)
    le = jnp.where(in_e, logits, NEG_INF)
    m1 = jnp.max(le, axis=-1, keepdims=True)
    e1 = first_lane(in_e & (le == m1))
    s_e = jnp.sum(jnp.where(in_e, jnp.exp(le - m1), 0.0), axis=-1, keepdims=True)
    in_e2 = in_e & (lane != e1)
    le2 = jnp.where(in_e2, logits, NEG_INF)
    m2 = jnp.max(le2, axis=-1, keepdims=True)
    e2 = first_lane(in_e2 & (le2 == m2))
    p1 = 1.0 / s_e
    p2 = jnp.exp(m2 - m1) / s_e
    den = p1 + p2
    return ((e1 - N_GROUPS).astype(jnp.int32), (e2 - N_GROUPS).astype(jnp.int32),
            g_w * p1 / den, g_w * p2 / den)


def _norm_mod_route_kernel(h_ref, g_ref, mod_ref, wr_ref, br_ref, f_ref, eid_ref, wt_ref, *, shift_idx):
    f = _rms_mod(h_ref[0], g_ref[...], mod_ref[0, shift_idx:shift_idx + 1, :],
                 mod_ref[0, shift_idx + 1:shift_idx + 2, :])
    f_ref[0] = f
    logits = jnp.dot(f, wr_ref[...], preferred_element_type=F32,
                     precision=lax.Precision.HIGHEST) + br_ref[...]
    e1, e2, w1, w2 = _route(logits)
    lane = lax.broadcasted_iota(jnp.int32, logits.shape, 1)
    eid_ref[0] = jnp.where(lane == 0, e1, jnp.where(lane == 1, e2, 0))
    wt_ref[0] = jnp.where(lane == 0, w1, jnp.where(lane == 1, w2, 0.0))


def _mod_spec(d):
    return pl.BlockSpec((1, N_MOD, d), lambda b, j: (2 * b + jnp.minimum(j, 1), 0, 0))


def norm_modulate(h, g, modtab, shift_idx):
    b, p, d = h.shape
    row = pl.BlockSpec((1, ROW_TILE, d), lambda b, j: (b, j, 0))
    return pl.pallas_call(
        functools.partial(_norm_mod_kernel, shift_idx=shift_idx),
        grid=(b, p // ROW_TILE),
        in_specs=[row, pl.BlockSpec((1, d), lambda b, j: (0, 0)), _mod_spec(d)],
        out_specs=row,
        out_shape=jax.ShapeDtypeStruct((b, p, d), BF16),
        compiler_params=_cparams(("parallel", "parallel")),
        name="norm_mod",
    )(h, g.reshape(1, d), modtab.reshape(b * 2, N_MOD, d))


def norm_modulate_route(h, g, modtab, shift_idx, w_grp, b_grp, w_rt, b_rt):
    b, p, d = h.shape
    n_r = N_GROUPS + N_EXPERTS
    wr = jnp.zeros((d, LANES), F32).at[:, :N_GROUPS].set(w_grp).at[:, N_GROUPS:n_r].set(w_rt)
    br = jnp.zeros((1, LANES), F32).at[0, :N_GROUPS].set(b_grp).at[0, N_GROUPS:n_r].set(b_rt)
    row = pl.BlockSpec((1, ROW_TILE, d), lambda b, j: (b, j, 0))
    small = pl.BlockSpec((1, ROW_TILE, LANES), lambda b, j: (b, j, 0))
    return pl.pallas_call(
        functools.partial(_norm_mod_route_kernel, shift_idx=shift_idx),
        grid=(b, p // ROW_TILE),
        in_specs=[row, pl.BlockSpec((1, d), lambda b, j: (0, 0)), _mod_spec(d),
                  pl.BlockSpec((d, LANES), lambda b, j: (0, 0)),
                  pl.BlockSpec((1, LANES), lambda b, j: (0, 0))],
        out_specs=[row, small, small],
        out_shape=[jax.ShapeDtypeStruct((b, p, d), F32),
                   jax.ShapeDtypeStruct((b, p, LANES), jnp.int32),
                   jax.ShapeDtypeStruct((b, p, LANES), F32)],
        compiler_params=_cparams(("parallel", "parallel")),
        name="norm_mod_route",
    )(h, g.reshape(1, d), modtab.reshape(b * 2, N_MOD, d), wr, br)


def _final_norm_kernel(h_ref, g_ref, o_ref):
    x = h_ref[0]
    ms = jnp.mean(x * x, axis=-1, keepdims=True)
    o_ref[0] = x * lax.rsqrt(ms + NORM_EPS) * g_ref[...]


def final_norm(h, g, ctx_len):
    b, p, d = h.shape
    skip = ctx_len // ROW_TILE
    return pl.pallas_call(
        _final_norm_kernel,
        grid=(b, (p - ctx_len) // ROW_TILE),
        in_specs=[pl.BlockSpec((1, ROW_TILE, d), lambda b, j: (b, j + skip, 0)),
                  pl.BlockSpec((1, d), lambda b, j: (0, 0))],
        out_specs=pl.BlockSpec((1, ROW_TILE, d), lambda b, j: (b, j, 0)),
        out_shape=jax.ShapeDtypeStruct((b, p - ctx_len, d), F32),
        compiler_params=_cparams(("parallel", "parallel")),
        name="final_norm",
    )(h, g.reshape(1, d))


def _rope(acc, cos, sin):
    n_blk = acc.shape[1] // LANES
    outs = []
    for c in range(n_blk):
        x = acc[:, c * LANES:(c + 1) * LANES]
        outs.append(x * cos + pltpu.roll(x, LANES // 2, 1) * sin)
    return outs[0] if n_blk == 1 else jnp.concatenate(outs, axis=1)


def _proj_kernel(*refs, rope_lo, rope_hi, resid, gate_idx, ctx_len, tm):
    x_ref, w_ref = refs[0], refs[1]
    o_ref = refs[-1]
    acc = _dot(x_ref[0], w_ref[...])
    if resid:
        res_ref, mod_ref = refs[2], refs[3]
        row = pl.program_id(1) * tm + lax.broadcasted_iota(jnp.int32, (tm, 1), 0)
        gate = jnp.where(row < ctx_len, mod_ref[0, 0, gate_idx:gate_idx + 1, :],
                         mod_ref[0, 1, gate_idx:gate_idx + 1, :])
        o_ref[0] = res_ref[0] + gate * acc
    elif rope_hi > rope_lo:
        cos_ref, sin_ref = refs[2], refs[3]
        j = pl.program_id(2)
        roped = (j >= rope_lo) & (j < rope_hi)

        @pl.when(roped)
        def _():
            o_ref[0] = _rope(acc, cos_ref[...], sin_ref[...]).astype(o_ref.dtype)

        @pl.when(jnp.logical_not(roped))
        def _():
            o_ref[0] = acc.astype(o_ref.dtype)
    else:
        o_ref[0] = acc.astype(o_ref.dtype)


def _row_tile(p):
    for cand in (1088, 1024, 544, 512, 272, 256, 128, 64, 32, 16):
        if p % cand == 0:
            return cand
    raise ValueError(p)


def project(x, w, *, tn=512, out_dtype=BF16, rope=None, resid=None, ctx_len=0):
    b, p, k = x.shape
    n = w.shape[1]
    tm = _row_tile(p)
    tn = min(tn, n)
    assert n % tn == 0
    in_specs = [pl.BlockSpec((1, tm, k), lambda b, i, j: (b, i, 0)),
                pl.BlockSpec((k, tn), lambda b, i, j: (0, j))]
    args = [x, w]
    kw = dict(rope_lo=0, rope_hi=0, resid=False, gate_idx=0, ctx_len=ctx_len, tm=tm)
    if resid is not None:
        h, modtab, gate_idx = resid
        in_specs += [pl.BlockSpec((1, tm, tn), lambda b, i, j: (b, i, j)),
                     pl.BlockSpec((1, 2, N_MOD, tn), lambda b, i, j: (b, 0, 0, j))]
        args += [h, modtab]
        kw.update(resid=True, gate_idx=gate_idx)
        out_dtype = F32
    elif rope is not None:
        cos, sin, lo, hi = rope
        in_specs += [pl.BlockSpec((tm, LANES), lambda b, i, j: (i, 0)),
                     pl.BlockSpec((tm, LANES), lambda b, i, j: (i, 0))]
        args += [cos, sin]
        kw.update(rope_lo=lo, rope_hi=hi)
    return pl.pallas_call(
        functools.partial(_proj_kernel, **kw),
        grid=(b, p // tm, n // tn),
        in_specs=in_specs,
        out_specs=pl.BlockSpec((1, tm, tn), lambda b, i, j: (b, i, j)),
        out_shape=jax.ShapeDtypeStruct((b, p, n), out_dtype),
        compiler_params=_cparams(("parallel", "parallel", "arbitrary")),
        name="project",
    )(*args)


def _axial_cos_sin(n, rot_dim):
    t = jnp.arange(n, dtype=jnp.int32)
    row = (t // GRID_W).astype(F32)
    col = (t % GRID_W).astype(F32)
    n_freq = rot_dim // 4
    inv = ROPE_BASE ** (-jnp.arange(n_freq, dtype=F32) / n_freq)
    ang = jnp.concatenate([row[:, None] * inv, col[:, None] * inv], axis=-1)
    return jnp.cos(ang), jnp.sin(ang)


def rope_tables_full(s, ctx_len):
    c, sn = _axial_cos_sin(s, HEAD_DIM)
    cos = jnp.concatenate([c, c], axis=1)
    sin = jnp.concatenate([-sn, sn], axis=1)
    ident_c = jnp.ones((ctx_len, LANES), F32)
    ident_s = jnp.zeros((ctx_len, LANES), F32)
    return jnp.concatenate([ident_c, cos], axis=0), jnp.concatenate([ident_s, sin], axis=0)


def rope_tables_mla(s, ctx_len):
    c, sn = _axial_cos_sin(s, MLA_ROPE_DIM)
    one = jnp.ones_like(c)
    zero = jnp.zeros_like(c)
    cos = jnp.concatenate([c, one, c, one], axis=1)
    sin = jnp.concatenate([-sn, zero, sn, zero], axis=1)
    ident_c = jnp.ones((ctx_len, LANES), F32)
    ident_s = jnp.zeros((ctx_len, LANES), F32)
    return jnp.concatenate([ident_c, cos], axis=0), jnp.concatenate([ident_s, sin], axis=0)


def _spread_rope_cols(w_rope):
    k = w_rope.shape[0]
    half = MLA_ROPE_DIM // 2
    z = jnp.zeros((k, half), w_rope.dtype)
    return jnp.concatenate([w_rope[:, :half], z, w_rope[:, half:], z], axis=1)


def _softmax_parts(parts, extra=None):
    m = parts[0].max(axis=-1, keepdims=True)
    for s in parts[1:]:
        m = jnp.maximum(m, s.max(axis=-1, keepdims=True))
    if extra is not None:
        m = jnp.maximum(m, extra)
    ps = [jnp.exp(s - m) for s in parts]
    den = ps[0].sum(axis=-1, keepdims=True)
    for p in ps[1:]:
        den = den + p.sum(axis=-1, keepdims=True)
    if extra is not None:
        den = den + jnp.exp(extra - m)
    return ps, 1.0 / den


def na_bias_table(rpb, rows):
    n_tiles = rows // NA_TILE_ROWS
    tabs = []
    for tile in (0, 1, n_tiles - 1):
        kr0 = int(np.clip(NA_TILE_ROWS * tile - NA_ROWS // 2, 0, rows - NA_WIN_ROWS))
        r = NA_TILE_ROWS * tile + np.arange(NA_TILE_ROWS)
        r0 = np.clip(r - NA_ROWS // 2, 0, rows - NA_ROWS)
        krow = kr0 + np.arange(NA_WIN_ROWS)
        row_ok = (krow[None, :] >= r0[:, None]) & (krow[None, :] < r0[:, None] + NA_ROWS)
        drow = np.clip(krow[None, :] - r[:, None] + NA_ROWS - 1, 0, 2 * NA_ROWS - 2)
        qc = np.arange(GRID_W)
        qcol0 = np.clip(qc - NA_COLS // 2, 0, GRID_W - NA_COLS)
        kc = np.arange(GRID_W)
        col_ok = (kc[None, :] >= qcol0[:, None]) & (kc[None, :] < qcol0[:, None] + NA_COLS)
        dcol = np.clip(kc[None, :] - qc[:, None] + NA_COLS - 1, 0, 2 * NA_COLS - 2)
        ok = row_ok[:, None, :, None] & col_ok[None, :, None, :]
        drow_b = np.broadcast_to(drow[:, None, :, None], ok.shape)
        dcol_b = np.broadcast_to(dcol[None, :, None, :], ok.shape)
        vals = rpb[:, drow_b, dcol_b].astype(F32)
        vals = jnp.where(jnp.asarray(ok)[None], vals, NEG_INF)
        tabs.append(vals.reshape(rpb.shape[0], NA_TILE_ROWS * GRID_W, NA_WIN_ROWS * GRID_W))
    return jnp.stack(tabs, axis=1)


def _na_kernel(q_ref, k_ref, v_ref, bias_ref, o_ref, *, ctx_len, rows, scale):
    tq = NA_TILE_ROWS * GRID_W
    tk = NA_WIN_ROWS * GRID_W
    n_tiles = rows // NA_TILE_ROWS
    lc = ctx_len

    s = _dot_nt(q_ref[0, 0:lc, :], k_ref[0, 0:lc, :]) * scale
    (p,), inv = _softmax_parts([s])
    o_ref[0, 0:lc, :] = (_dot(p.astype(BF16), v_ref[0, 0:lc, :]) * inv).astype(o_ref.dtype)

    def tile(i, carry):
        qs = pl.multiple_of(lc + i * tq, tq)
        kr0 = jnp.clip(NA_TILE_ROWS * i - NA_ROWS // 2, 0, rows - NA_WIN_ROWS)
        ks = pl.multiple_of(lc + kr0 * GRID_W, NA_TILE_ROWS * GRID_W)
        pat = jnp.where(i == 0, 0, jnp.where(i == n_tiles - 1, 2, 1))
        q = q_ref[0, pl.ds(qs, tq), :]
        s_loc = _dot_nt(q, k_ref[0, pl.ds(ks, tk), :]) * scale + bias_ref[0, pat]
        s_ctx = _dot_nt(q, k_ref[0, 0:lc, :]) * scale
        (p_loc, p_ctx), inv = _softmax_parts([s_loc, s_ctx])
        o = _dot(p_loc.astype(BF16), v_ref[0, pl.ds(ks, tk), :]) + _dot(p_ctx.astype(BF16), v_ref[0, 0:lc, :])
        o_ref[0, pl.ds(qs, tq), :] = (o * inv).astype(o_ref.dtype)
        return carry

    lax.fori_loop(0, n_tiles, tile, 0)


def neighbourhood_attention(qkv, bias, ctx_len, n_heads):
    b, p, _ = qkv.shape
    rows = (p - ctx_len) // GRID_W
    assert rows % NA_TILE_ROWS == 0 and rows >= NA_WIN_ROWS
    tq, tk = NA_TILE_ROWS * GRID_W, NA_WIN_ROWS * GRID_W
    assert ctx_len % 16 == 0 and ctx_len % tq == 0
    blk = lambda off: pl.BlockSpec((1, p, HEAD_DIM), lambda h, b: (b, 0, off + h))
    return pl.pallas_call(
        functools.partial(_na_kernel, ctx_len=ctx_len, rows=rows, scale=HEAD_DIM ** -0.5),
        grid=(n_heads, b),
        in_specs=[blk(0), blk(n_heads), blk(2 * n_heads),
                  pl.BlockSpec((1, 3, tq, tk), lambda h, b: (h, 0, 0, 0))],
        out_specs=blk(0),
        out_shape=jax.ShapeDtypeStruct((b, p, n_heads * HEAD_DIM), BF16),
        compiler_params=_cparams(("parallel", "parallel")),
        name="na_attention",
    )(qkv, qkv, qkv, bias)


def _swa_kernel(sink_ref, q_ref, k_ref, v_ref, o_ref, *, ctx_len, seq, group, scale):
    lc = ctx_len
    kvh = pl.program_id(1)

    def stack_heads(q):
        return jnp.concatenate([q[:, g * HEAD_DIM:(g + 1) * HEAD_DIM] for g in range(group)], axis=0)

    def finish(parts_fn, n_q, pv_fn, store):
        ps_all, invs = [], []
        for g in range(group):
            sink = sink_ref[kvh * group + g]
            ps, inv = _softmax_parts(parts_fn(g), extra=sink)
            ps_all.append(ps)
            invs.append(inv)
        n_parts = len(ps_all[0])
        stacked = [jnp.concatenate([ps_all[g][k] for g in range(group)], axis=0).astype(BF16)
                   for k in range(n_parts)]
        o = pv_fn(stacked)
        store(jnp.concatenate([o[g * n_q:(g + 1) * n_q] * invs[g] for g in range(group)], axis=1))

    s_c = _dot_nt(stack_heads(q_ref[0, 0:lc, :]), k_ref[0, 0:lc, :]) * scale

    def store_ctx(o):
        o_ref[0, 0:lc, :] = o.astype(o_ref.dtype)

    finish(lambda g: [s_c[g * lc:(g + 1) * lc]], lc,
           lambda st: _dot(st[0], v_ref[0, 0:lc, :]), store_ctx)

    n_tiles = seq // SWA_TQ

    def tile(t, carry):
        q0 = t * SWA_TQ
        k0 = jnp.clip(q0 - SWA_WINDOW, 0, seq - SWA_TK)
        qs = pl.multiple_of(lc + q0, SWA_WINDOW)
        ks = pl.multiple_of(lc + k0, SWA_WINDOW)
        q4 = stack_heads(q_ref[0, pl.ds(qs, SWA_TQ), :])
        s_loc = _dot_nt(q4, k_ref[0, pl.ds(ks, SWA_TK), :]) * scale
        s_ctx = _dot_nt(q4, k_ref[0, 0:lc, :]) * scale
        dpos = (lax.broadcasted_iota(jnp.int32, (SWA_TQ, SWA_TK), 1)
                - lax.broadcasted_iota(jnp.int32, (SWA_TQ, SWA_TK), 0)) + (k0 - q0)
        valid = jnp.abs(dpos) <= SWA_WINDOW

        def parts(g):
            sl = slice(g * SWA_TQ, (g + 1) * SWA_TQ)
            return [jnp.where(valid, s_loc[sl], NEG_INF), s_ctx[sl]]

        def store(o):
            o_ref[0, pl.ds(qs, SWA_TQ), :] = o.astype(o_ref.dtype)

        finish(parts, SWA_TQ,
               lambda st: _dot(st[0], v_ref[0, pl.ds(ks, SWA_TK), :]) + _dot(st[1], v_ref[0, 0:lc, :]),
               store)
        return carry

    lax.fori_loop(0, n_tiles, tile, 0)


def window_attention(qkv, sink, ctx_len, n_heads, n_kv_heads):
    b, p, _ = qkv.shape
    seq = p - ctx_len
    group = n_heads // n_kv_heads
    assert seq % SWA_TQ == 0 and seq >= SWA_TK and ctx_len % SWA_WINDOW == 0
    kv = lambda off: pl.BlockSpec((1, p, HEAD_DIM), lambda b, h: (b, 0, off + h))
    qo = pl.BlockSpec((1, p, group * HEAD_DIM), lambda b, h: (b, 0, h))
    return pl.pallas_call(
        functools.partial(_swa_kernel, ctx_len=ctx_len, seq=seq, group=group, scale=HEAD_DIM ** -0.5),
        grid=(b, n_kv_heads),
        in_specs=[pl.BlockSpec(memory_space=pltpu.SMEM), qo, kv(n_heads), kv(n_heads + n_kv_heads)],
        out_specs=qo,
        out_shape=jax.ShapeDtypeStruct((b, p, n_heads * HEAD_DIM), BF16),
        compiler_params=_cparams(("parallel", "parallel")),
        name="swa_attention",
    )(sink.astype(F32), qkv, qkv, qkv)


def _mla_down_kernel(x_ref, w_ref, gq_ref, gkv_ref, cos_ref, sin_ref, cq_ref, ckv_ref, kr_ref, *, q_rank, kv_rank):
    acc = _dot(x_ref[0], w_ref[...])

    def rms(x, g):
        ms = jnp.mean(x * x, axis=-1, keepdims=True)
        return x * lax.rsqrt(ms + NORM_EPS) * g

    cq_ref[0] = rms(acc[:, :q_rank], gq_ref[...]).astype(cq_ref.dtype)
    ckv_ref[0] = rms(acc[:, q_rank:q_rank + kv_rank], gkv_ref[...]).astype(ckv_ref.dtype)
    kr_ref[0] = _rope(acc[:, q_rank + kv_rank:], cos_ref[...], sin_ref[...]).astype(kr_ref.dtype)


def mla_down(x, w_down, gq, gkv, cos, sin, q_rank, kv_rank):
    b, p, k = x.shape
    n = w_down.shape[1]
    tm = 544 if p % 544 == 0 else _row_tile(p)
    row = lambda width: pl.BlockSpec((1, tm, width), lambda b, i: (b, i, 0))
    return pl.pallas_call(
        functools.partial(_mla_down_kernel, q_rank=q_rank, kv_rank=kv_rank),
        grid=(b, p // tm),
        in_specs=[row(k), pl.BlockSpec((k, n), lambda b, i: (0, 0)),
                  pl.BlockSpec((1, q_rank), lambda b, i: (0, 0)),
                  pl.BlockSpec((1, kv_rank), lambda b, i: (0, 0)),
                  pl.BlockSpec((tm, LANES), lambda b, i: (i, 0)),
                  pl.BlockSpec((tm, LANES), lambda b, i: (i, 0))],
        out_specs=[row(q_rank), row(kv_rank), row(LANES)],
        out_shape=[jax.ShapeDtypeStruct((b, p, q_rank), BF16),
                   jax.ShapeDtypeStruct((b, p, kv_rank), BF16),
                   jax.ShapeDtypeStruct((b, p, LANES), BF16)],
        compiler_params=_cparams(("parallel", "parallel")),
        name="mla_down",
    )(x, w_down, gq.reshape(1, q_rank), gkv.reshape(1, kv_rank), cos, sin)


def _mla_kernel(qn_ref, qr_ref, kn_ref, kr_ref, v_ref, o_ref, kcat_ref, *, ctx_len, scale):
    lc = ctx_len
    p_all = kcat_ref.shape[0]
    kcat_ref[:, 0:LANES] = kn_ref[0]
    kcat_ref[:, LANES:2 * LANES] = kr_ref[0]

    def attend(qs, n_q, n_k):
        q = jnp.concatenate([qn_ref[0, pl.ds(qs, n_q), :], qr_ref[0, pl.ds(qs, n_q), :]], axis=1)
        s = _dot_nt(q, kcat_ref[0:n_k, :]) * scale
        (p,), inv = _softmax_parts([s])
        o = _dot(p.astype(BF16), v_ref[0, 0:n_k, :])
        o_ref[0, pl.ds(qs, n_q), :] = (o * inv).astype(o_ref.dtype)

    attend(0, lc, lc)

    def tile(i, carry):
        attend(pl.multiple_of(lc + i * MLA_TQ, MLA_TQ), MLA_TQ, p_all)
        return carry

    lax.fori_loop(0, (p_all - lc) // MLA_TQ, tile, 0)


def latent_attention(q, kv, kr, ctx_len, n_heads):
    b, p, _ = q.shape
    assert (p - ctx_len) % MLA_TQ == 0 and ctx_len % MLA_TQ == 0
    blk = lambda f: pl.BlockSpec((1, p, LANES), f)
    scale = (MLA_NOPE_DIM + MLA_ROPE_DIM) ** -0.5
    return pl.pallas_call(
        functools.partial(_mla_kernel, ctx_len=ctx_len, scale=scale),
        grid=(b, n_heads),
        in_specs=[blk(lambda b, h: (b, 0, h)), blk(lambda b, h: (b, 0, n_heads + h)),
                  blk(lambda b, h: (b, 0, 2 * h)), blk(lambda b, h: (b, 0, 0)),
                  blk(lambda b, h: (b, 0, 2 * h + 1))],
        out_specs=blk(lambda b, h: (b, 0, h)),
        out_shape=jax.ShapeDtypeStruct((b, p, n_heads * LANES), BF16),
        scratch_shapes=[pltpu.VMEM((p, 2 * LANES), BF16)],
        compiler_params=_cparams(("parallel", "parallel")),
        name="mla_attention",
    )(q, q, kv, kr, kv)


def _gather_kernel(idx_ref, src_ref, out_ref, sem):
    base = pl.program_id(0) * GATHER_CHUNK

    def copy(r):
        return pltpu.make_async_copy(src_ref.at[pl.ds(idx_ref[0, 0, r], 1), :],
                                     out_ref.at[pl.ds(base + r, 1), :], sem)

    def issue(r, carry):
        copy(r).start()
        return carry

    def drain(r, carry):
        copy(r).wait()
        return carry

    lax.fori_loop(0, GATHER_CHUNK, issue, 0)
    lax.fori_loop(0, GATHER_CHUNK, drain, 0)


def gather_rows(src, idx):
    n = idx.shape[0]
    assert n % GATHER_CHUNK == 0
    n_chunks = n // GATHER_CHUNK
    return pl.pallas_call(
        _gather_kernel,
        grid=(n_chunks,),
        in_specs=[pl.BlockSpec((1, 1, GATHER_CHUNK), lambda i: (i, 0, 0), memory_space=pltpu.SMEM),
                  pl.BlockSpec(memory_space=pl.ANY)],
        out_specs=pl.BlockSpec(memory_space=pl.ANY),
        out_shape=jax.ShapeDtypeStruct((n, src.shape[1]), src.dtype),
        scratch_shapes=[pltpu.SemaphoreType.DMA(())],
        compiler_params=_cparams(("arbitrary",)),
        name="gather_rows",
    )(idx.reshape(n_chunks, 1, GATHER_CHUNK), src)


def _expert_kernel(blk_e_ref, n_used_ref, x_ref, wg_ref, wu_ref, wd_ref, rw_ref, y_ref):
    i = pl.program_id(0)

    @pl.when(i < n_used_ref[0])
    def _():
        x = x_ref[...].astype(BF16)
        g = _dot(x, wg_ref[0])
        u = _dot(x, wu_ref[0])
        a = (g * jax.nn.sigmoid(g) * u).astype(BF16)
        y_ref[...] = _dot(a, wd_ref[0]) * rw_ref[...]

    @pl.when(i >= n_used_ref[0])
    def _():
        y_ref[...] = jnp.zeros_like(y_ref)


def expert_ffn(x_disp, blk_e, n_used, w_gate, w_up, w_down, row_w):
    rows, d = x_disp.shape
    n_blk = rows // MOE_BLOCK
    de = w_gate.shape[2]
    grid_spec = pltpu.PrefetchScalarGridSpec(
        num_scalar_prefetch=2,
        grid=(n_blk,),
        in_specs=[pl.BlockSpec((MOE_BLOCK, d), lambda i, be, nu: (i, 0)),
                  pl.BlockSpec((1, d, de), lambda i, be, nu: (be[i], 0, 0)),
                  pl.BlockSpec((1, d, de), lambda i, be, nu: (be[i], 0, 0)),
                  pl.BlockSpec((1, de, d), lambda i, be, nu: (be[i], 0, 0)),
                  pl.BlockSpec((MOE_BLOCK, 1), lambda i, be, nu: (i, 0))],
        out_specs=pl.BlockSpec((MOE_BLOCK, d), lambda i, be, nu: (i, 0)),
    )
    return pl.pallas_call(
        _expert_kernel,
        grid_spec=grid_spec,
        out_shape=jax.ShapeDtypeStruct((rows, d), F32),
        compiler_params=_cparams(("arbitrary",)),
        name="expert_ffn",
    )(blk_e, n_used, x_disp, w_gate, w_up, w_down, row_w)


def _combine_kernel(h_ref, y_ref, mod_ref, o_ref, *, gate_idx):
    d = h_ref.shape[2]
    y = y_ref[0]
    o_ref[0] = h_ref[0] + mod_ref[0, gate_idx:gate_idx + 1, :] * (y[:, :d] + y[:, d:])


def combine(h, y_pairs, modtab, gate_idx):
    b, p, d = h.shape
    row = pl.BlockSpec((1, ROW_TILE, d), lambda b, j: (b, j, 0))
    return pl.pallas_call(
        functools.partial(_combine_kernel, gate_idx=gate_idx),
        grid=(b, p // ROW_TILE),
        in_specs=[row, pl.BlockSpec((1, ROW_TILE, 2 * d), lambda b, j: (b, j, 0)), _mod_spec(d)],
        out_specs=row,
        out_shape=jax.ShapeDtypeStruct((b, p, d), F32),
        compiler_params=_cparams(("parallel", "parallel")),
        name="moe_combine",
    )(h, y_pairs, modtab.reshape(b * 2, N_MOD, d))


def hier_moe(h, f, eid, wt, modtab, gate_idx, w_gate, w_up, w_down):
    b, p, d = h.shape
    n_tok = b * p
    n_assign = 2 * n_tok
    eid_f = eid.reshape(n_assign)
    wt_f = wt.reshape(n_assign)
    onehot = (eid_f[:, None] == jnp.arange(N_EXPERTS, dtype=jnp.int32)[None, :]).astype(jnp.int32)
    csum = jnp.cumsum(onehot, axis=0)
    counts = csum[-1]
    pos = jnp.take_along_axis(csum, eid_f[:, None], axis=1)[:, 0] - 1
    pcounts = (counts + MOE_BLOCK - 1) // MOE_BLOCK * MOE_BLOCK
    pend = jnp.cumsum(pcounts)
    dest = ((pend - pcounts)[eid_f] + pos).astype(jnp.int32)
    n_blk = -(-n_assign // MOE_BLOCK) + N_EXPERTS
    n_blk = -(-n_blk * MOE_BLOCK // GATHER_CHUNK) * GATHER_CHUNK // MOE_BLOCK
    rows_total = n_blk * MOE_BLOCK
    row_tok = jnp.zeros((rows_total,), jnp.int32).at[dest].set(jnp.arange(n_assign, dtype=jnp.int32) // 2)
    row_w = jnp.zeros((rows_total,), F32).at[dest].set(wt_f)
    blk_e = jnp.minimum(jnp.searchsorted(pend, jnp.arange(n_blk, dtype=jnp.int32) * MOE_BLOCK, side='right'),
                        N_EXPERTS - 1).astype(jnp.int32)
    n_used = (pend[-1] // MOE_BLOCK).astype(jnp.int32).reshape(1)

    x_disp = gather_rows(f.reshape(n_tok, d), row_tok)
    y = expert_ffn(x_disp, blk_e, n_used, w_gate, w_up, w_down, row_w.reshape(rows_total, 1))
    y_pairs = gather_rows(y, dest)
    return combine(h, y_pairs.reshape(b, p, 2 * d), modtab, gate_idx)


def kernel(x, c, ctx, c_ctx, mod_w, mod_b, norm_mix_g, norm_ffn_g, router_grp_w, router_grp_b, router_exp_w, router_exp_b, exp_w_gate, exp_w_up, exp_w_down, l0_na_w_qkv, l0_na_rpb, l0_na_w_o, l1_swa_w_qkv, l1_swa_sink, l1_swa_w_o, l2_mla_w_dq, l2_mla_q_norm_g, l2_mla_w_uq, l2_mla_w_dkv, l2_mla_kv_norm_g, l2_mla_w_ukv, l2_mla_w_o, l3_na_w_qkv, l3_na_rpb, l3_na_w_o, final_norm_g):
    b, s, d = x.shape
    lc = ctx.shape[1]
    n_heads = d // HEAD_DIM
    n_kv_heads = n_heads // 4
    depth = mod_w.shape[0]
    rows = s // GRID_W

    h = jnp.concatenate([ctx, x], axis=1)
    modtabs = modulation_tables(c, c_ctx, mod_w, mod_b)

    def na_mixer(hm, w_qkv, rpb):
        qkv = project(hm, w_qkv.astype(BF16))
        return neighbourhood_attention(qkv, na_bias_table(rpb, rows), lc, n_heads)

    def swa_mixer(hm):
        cos, sin = rope_tables_full(s, lc)
        n_rope = (n_heads + n_kv_heads) * HEAD_DIM // 512
        qkv = project(hm, l1_swa_w_qkv.astype(BF16), tn=512, rope=(cos, sin, 0, n_rope))
        return window_attention(qkv, l1_swa_sink, lc, n_heads, n_kv_heads)

    def mla_mixer(hm):
        q_rank = l2_mla_w_dq.shape[1]
        kv_rank = l2_mla_kv_norm_g.shape[0]
        cos, sin = rope_tables_mla(s, lc)
        w_down = jnp.concatenate([l2_mla_w_dq, l2_mla_w_dkv[:, :kv_rank],
                                  _spread_rope_cols(l2_mla_w_dkv[:, kv_rank:])], axis=1).astype(BF16)
        cq, ckv, kr = mla_down(hm, w_down, l2_mla_q_norm_g, l2_mla_kv_norm_g, cos, sin, q_rank, kv_rank)
        w_uq = l2_mla_w_uq.reshape(q_rank, n_heads, MLA_NOPE_DIM + MLA_ROPE_DIM)
        w_q_nope = w_uq[:, :, :MLA_NOPE_DIM].reshape(q_rank, n_heads * MLA_NOPE_DIM)
        w_q_rope = jnp.concatenate([_spread_rope_cols(w_uq[:, hh, MLA_NOPE_DIM:]) for hh in range(n_heads)], axis=1)
        w_q = jnp.concatenate([w_q_nope, w_q_rope], axis=1).astype(BF16)
        n_nope_tiles = n_heads * MLA_NOPE_DIM // 512
        q = project(cq, w_q, tn=512, rope=(cos, sin, n_nope_tiles, 2 * n_nope_tiles))
        kv = project(ckv, l2_mla_w_ukv.astype(BF16), tn=512)
        return latent_attention(q, kv, kr, lc, n_heads)

    for i in range(depth):
        modtab = modtabs[i]
        hm = norm_modulate(h, norm_mix_g[i], modtab, 0)
        mixer = i % 3
        if mixer == 0:
            w_qkv, rpb, w_o = (l0_na_w_qkv, l0_na_rpb, l0_na_w_o) if i == 0 else (l3_na_w_qkv, l3_na_rpb, l3_na_w_o)
            y = na_mixer(hm, w_qkv, rpb)
        elif mixer == 1:
            y, w_o = swa_mixer(hm), l1_swa_w_o
        else:
            y, w_o = mla_mixer(hm), l2_mla_w_o
        h = project(y, w_o.astype(BF16), tn=512, resid=(h, modtab, 2), ctx_len=lc)
        f, eid, wt = norm_modulate_route(h, norm_ffn_g[i], modtab, 3, router_grp_w[i], router_grp_b[i],
                                         router_exp_w[i], router_exp_b[i])
        h = hier_moe(h, f, eid[:, :, :2], wt[:, :, :2], modtab, 5,
                     exp_w_gate[i].astype(BF16), exp_w_up[i].astype(BF16), exp_w_down[i].astype(BF16))
    return final_norm(h, final_norm_g, lc)
```

```python
import functools

import numpy as np
import jax
import jax.numpy as jnp
from jax import lax
from jax.experimental import pallas as pl
from jax.experimental.pallas import tpu as pltpu
from jax.experimental.pallas import tpu_sc as plsc

GRID_W = 64
HEAD_DIM = 128
ROPE_BASE = 10000.0
NORM_EPS = 1e-6
NEG_INF = -1e30
N_MOD = 6

NA_ROWS = 8
NA_COLS = 16
NA_TILE_ROWS = 4
NA_WIN_ROWS = 12

SWA_WINDOW = 128
SWA_TQ = 256
SWA_TK = 512

MLA_NOPE_DIM = 128
MLA_ROPE_DIM = 64
MLA_TQ = 256

N_GROUPS = 4
EXPERTS_PER_GROUP = 8
N_EXPERTS = N_GROUPS * EXPERTS_PER_GROUP
MOE_BLOCK = 256
GATHER_WINDOW = 16

LANES = 128
ROW_TILE = 256
VMEM_LIMIT = 56 * 1024 * 1024

BF16 = jnp.bfloat16
F32 = jnp.float32


def _cparams(sem):
    return pltpu.CompilerParams(dimension_semantics=sem, vmem_limit_bytes=VMEM_LIMIT)


def _dot(a, b):
    return jnp.dot(a, b, preferred_element_type=F32)


def _dot_nt(a, b):
    return lax.dot_general(a, b, (((1,), (1,)), ((), ())), preferred_element_type=F32)


def _mod_kernel(x_ref, w_ref, b_ref, o_ref):
    x = x_ref[...]
    sx = (x * jax.nn.sigmoid(x)).astype(BF16)
    o_ref[0] = _dot(sx, w_ref[0].astype(BF16)) + b_ref[0]


def modulation_tables(c, c_ctx, mod_w, mod_b):
    depth, d, n_out = mod_w.shape
    b = c.shape[0]
    rows = 16
    xin = jnp.zeros((rows, d), F32).at[:b].set(c).at[b].set(c_ctx)
    tn = 1024
    out = pl.pallas_call(
        _mod_kernel,
        grid=(depth, n_out // tn),
        in_specs=[pl.BlockSpec((rows, d), lambda i, j: (0, 0)),
                  pl.BlockSpec((1, d, tn), lambda i, j: (i, 0, j)),
                  pl.BlockSpec((1, 1, tn), lambda i, j: (i, 0, j))],
        out_specs=pl.BlockSpec((1, rows, tn), lambda i, j: (i, 0, j)),
        out_shape=jax.ShapeDtypeStruct((depth, rows, n_out), F32),
        compiler_params=_cparams(("parallel", "parallel")),
        name="adaln_mod",
    )(xin, mod_w, mod_b.reshape(depth, 1, n_out))
    lat = out[:, :b].reshape(depth, b, 1, N_MOD, d)
    ctx = jnp.broadcast_to(out[:, b].reshape(depth, 1, 1, N_MOD, d), (depth, b, 1, N_MOD, d))
    return jnp.concatenate([ctx, lat], axis=2)


def _rms_mod(x, g, shift, scale):
    ms = jnp.mean(x * x, axis=-1, keepdims=True)
    y = x * lax.rsqrt(ms + NORM_EPS) * g
    return y * (1.0 + scale) + shift


def _norm_mod_kernel(h_ref, g_ref, mod_ref, o_ref, *, shift_idx):
    f = _rms_mod(h_ref[0], g_ref[...], mod_ref[0, shift_idx:shift_idx + 1, :],
                 mod_ref[0, shift_idx + 1:shift_idx + 2, :])
    o_ref[0] = f.astype(o_ref.dtype)


def _route(logits):
    lane = lax.broadcasted_iota(jnp.int32, logits.shape, 1).astype(F32)
    big = float(LANES)

    def first_lane(mask):
        return jnp.min(jnp.where(mask, lane, big), axis=-1, keepdims=True)

    in_grp = lane < N_GROUPS
    lg = jnp.where(in_grp, logits, NEG_INF)
    m_g = jnp.max(lg, axis=-1, keepdims=True)
    g_idx = first_lane(in_grp & (lg == m_g))
    g_w = 1.0 / jnp.sum(jnp.where(in_grp, jnp.exp(lg - m_g), 0.0), axis=-1, keepdims=True)
    e_lo = N_GROUPS + g_idx * EXPERTS_PER_GROUP
    in_e = (lane >= e_lo) & (lane < e_lo + EXPERTS_PER_GROUP)
    le = jnp.where(in_e, logits, NEG_INF)
    m1 = jnp.max(le, axis=-1, keepdims=True)
    e1 = first_lane(in_e & (le == m1))
    s_e = jnp.sum(jnp.where(in_e, jnp.exp(le - m1), 0.0), axis=-1, keepdims=True)
    in_e2 = in_e & (lane != e1)
    le2 = jnp.where(in_e2, logits, NEG_INF)
    m2 = jnp.max(le2, axis=-1, keepdims=True)
    e2 = first_lane(in_e2 & (le2 == m2))
    p1 = 1.0 / s_e
    p2 = jnp.exp(m2 - m1) / s_e
    den = p1 + p2
    return ((e1 - N_GROUPS).astype(jnp.int32), (e2 - N_GROUPS).astype(jnp.int32),
            g_w * p1 / den, g_w * p2 / den)


def _norm_mod_route_kernel(h_ref, g_ref, mod_ref, wr_ref, br_ref, f_ref, eid_ref, wt_ref, *, shift_idx):
    f = _rms_mod(h_ref[0], g_ref[...], mod_ref[0, shift_idx:shift_idx + 1, :],
                 mod_ref[0, shift_idx + 1:shift_idx + 2, :])
    f_ref[0] = f
    logits = jnp.dot(f, wr_ref[...], preferred_element_type=F32,
                     precision=lax.Precision.HIGHEST) + br_ref[...]
    e1, e2, w1, w2 = _route(logits)
    lane = lax.broadcasted_iota(jnp.int32, logits.shape, 1)
    eid_ref[0] = jnp.where(lane == 0, e1, jnp.where(lane == 1, e2, 0))
    wt_ref[0] = jnp.where(lane == 0, w1, jnp.where(lane == 1, w2, 0.0))


def _mod_spec(d):
    return pl.BlockSpec((1, N_MOD, d), lambda b, j: (2 * b + jnp.minimum(j, 1), 0, 0))


def norm_modulate(h, g, modtab, shift_idx):
    b, p, d = h.shape
    row = pl.BlockSpec((1, ROW_TILE, d), lambda b, j: (b, j, 0))
    return pl.pallas_call(
        functools.partial(_norm_mod_kernel, shift_idx=shift_idx),
        grid=(b, p // ROW_TILE),
        in_specs=[row, pl.BlockSpec((1, d), lambda b, j: (0, 0)), _mod_spec(d)],
        out_specs=row,
        out_shape=jax.ShapeDtypeStruct((b, p, d), BF16),
        compiler_params=_cparams(("parallel", "parallel")),
        name="norm_mod",
    )(h, g.reshape(1, d), modtab.reshape(b * 2, N_MOD, d))


def norm_modulate_route(h, g, modtab, shift_idx, w_grp, b_grp, w_rt, b_rt):
    b, p, d = h.shape
    n_r = N_GROUPS + N_EXPERTS
    wr = jnp.zeros((d, LANES), F32).at[:, :N_GROUPS].set(w_grp).at[:, N_GROUPS:n_r].set(w_rt)
    br = jnp.zeros((1, LANES), F32).at[0, :N_GROUPS].set(b_grp).at[0, N_GROUPS:n_r].set(b_rt)
    row = pl.BlockSpec((1, ROW_TILE, d), lambda b, j: (b, j, 0))
    small = pl.BlockSpec((1, ROW_TILE, LANES), lambda b, j: (b, j, 0))
    return pl.pallas_call(
        functools.partial(_norm_mod_route_kernel, shift_idx=shift_idx),
        grid=(b, p // ROW_TILE),
        in_specs=[row, pl.BlockSpec((1, d), lambda b, j: (0, 0)), _mod_spec(d),
                  pl.BlockSpec((d, LANES), lambda b, j: (0, 0)),
                  pl.BlockSpec((1, LANES), lambda b, j: (0, 0))],
        out_specs=[row, small, small],
        out_shape=[jax.ShapeDtypeStruct((b, p, d), F32),
                   jax.ShapeDtypeStruct((b, p, LANES), jnp.int32),
                   jax.ShapeDtypeStruct((b, p, LANES), F32)],
        compiler_params=_cparams(("parallel", "parallel")),
        name="norm_mod_route",
    )(h, g.reshape(1, d), modtab.reshape(b * 2, N_MOD, d), wr, br)


def _final_norm_kernel(h_ref, g_ref, o_ref):
    x = h_ref[0]
    ms = jnp.mean(x * x, axis=-1, keepdims=True)
    o_ref[0] = x * lax.rsqrt(ms + NORM_EPS) * g_ref[...]


def final_norm(h, g, ctx_len):
    b, p, d = h.shape
    skip = ctx_len // ROW_TILE
    return pl.pallas_call(
        _final_norm_kernel,
        grid=(b, (p - ctx_len) // ROW_TILE),
        in_specs=[pl.BlockSpec((1, ROW_TILE, d), lambda b, j: (b, j + skip, 0)),
                  pl.BlockSpec((1, d), lambda b, j: (0, 0))],
        out_specs=pl.BlockSpec((1, ROW_TILE, d), lambda b, j: (b, j, 0)),
        out_shape=jax.ShapeDtypeStruct((b, p - ctx_len, d), F32),
        compiler_params=_cparams(("parallel", "parallel")),
        name="final_norm",
    )(h, g.reshape(1, d))


def _rope(acc, cos, sin):
    n_blk = acc.shape[1] // LANES
    outs = []
    for c in range(n_blk):
        x = acc[:, c * LANES:(c + 1) * LANES]
        outs.append(x * cos + pltpu.roll(x, LANES // 2, 1) * sin)
    return outs[0] if n_blk == 1 else jnp.concatenate(outs, axis=1)


def _proj_kernel(*refs, rope_lo, rope_hi, resid, gate_idx, ctx_len, tm):
    x_ref, w_ref = refs[0], refs[1]
    o_ref = refs[-1]
    acc = _dot(x_ref[0], w_ref[...])
    if resid:
        res_ref, mod_ref = refs[2], refs[3]
        row = pl.program_id(1) * tm + lax.broadcasted_iota(jnp.int32, (tm, 1), 0)
        gate = jnp.where(row < ctx_len, mod_ref[0, 0, gate_idx:gate_idx + 1, :],
                         mod_ref[0, 1, gate_idx:gate_idx + 1, :])
        o_ref[0] = res_ref[0] + gate * acc
    elif rope_hi > rope_lo:
        cos_ref, sin_ref = refs[2], refs[3]
        j = pl.program_id(2)
        roped = (j >= rope_lo) & (j < rope_hi)

        @pl.when(roped)
        def _():
            o_ref[0] = _rope(acc, cos_ref[...], sin_ref[...]).astype(o_ref.dtype)

        @pl.when(jnp.logical_not(roped))
        def _():
            o_ref[0] = acc.astype(o_ref.dtype)
    else:
        o_ref[0] = acc.astype(o_ref.dtype)


def _row_tile(p):
    for cand in (1088, 1024, 544, 512, 272, 256, 128, 64, 32, 16):
        if p % cand == 0:
            return cand
    raise ValueError(p)


def project(x, w, *, tn=512, out_dtype=BF16, rope=None, resid=None, ctx_len=0):
    b, p, k = x.shape
    n = w.shape[1]
    tm = _row_tile(p)
    tn = min(tn, n)
    assert n % tn == 0
    in_specs = [pl.BlockSpec((1, tm, k), lambda b, i, j: (b, i, 0)),
                pl.BlockSpec((k, tn), lambda b, i, j: (0, j))]
    args = [x, w]
    kw = dict(rope_lo=0, rope_hi=0, resid=False, gate_idx=0, ctx_len=ctx_len, tm=tm)
    if resid is not None:
        h, modtab, gate_idx = resid
        in_specs += [pl.BlockSpec((1, tm, tn), lambda b, i, j: (b, i, j)),
                     pl.BlockSpec((1, 2, N_MOD, tn), lambda b, i, j: (b, 0, 0, j))]
        args += [h, modtab]
        kw.update(resid=True, gate_idx=gate_idx)
        out_dtype = F32
    elif rope is not None:
        cos, sin, lo, hi = rope
        in_specs += [pl.BlockSpec((tm, LANES), lambda b, i, j: (i, 0)),
                     pl.BlockSpec((tm, LANES), lambda b, i, j: (i, 0))]
        args += [cos, sin]
        kw.update(rope_lo=lo, rope_hi=hi)
    return pl.pallas_call(
        functools.partial(_proj_kernel, **kw),
        grid=(b, p // tm, n // tn),
        in_specs=in_specs,
        out_specs=pl.BlockSpec((1, tm, tn), lambda b, i, j: (b, i, j)),
        out_shape=jax.ShapeDtypeStruct((b, p, n), out_dtype),
        compiler_params=_cparams(("parallel", "parallel", "arbitrary")),
        name="project",
    )(*args)


def _axial_cos_sin(n, rot_dim):
    t = jnp.arange(n, dtype=jnp.int32)
    row = (t // GRID_W).astype(F32)
    col = (t % GRID_W).astype(F32)
    n_freq = rot_dim // 4
    inv = ROPE_BASE ** (-jnp.arange(n_freq, dtype=F32) / n_freq)
    ang = jnp.concatenate([row[:, None] * inv, col[:, None] * inv], axis=-1)
    return jnp.cos(ang), jnp.sin(ang)


def rope_tables_full(s, ctx_len):
    c, sn = _axial_cos_sin(s, HEAD_DIM)
    cos = jnp.concatenate([c, c], axis=1)
    sin = jnp.concatenate([-sn, sn], axis=1)
    ident_c = jnp.ones((ctx_len, LANES), F32)
    ident_s = jnp.zeros((ctx_len, LANES), F32)
    return jnp.concatenate([ident_c, cos], axis=0), jnp.concatenate([ident_s, sin], axis=0)


def rope_tables_mla(s, ctx_len):
    c, sn = _axial_cos_sin(s, MLA_ROPE_DIM)
    one = jnp.ones_like(c)
    zero = jnp.zeros_like(c)
    cos = jnp.concatenate([c, one, c, one], axis=1)
    sin = jnp.concatenate([-sn, zero, sn, zero], axis=1)
    ident_c = jnp.ones((ctx_len, LANES), F32)
    ident_s = jnp.zeros((ctx_len, LANES), F32)
    return jnp.concatenate([ident_c, cos], axis=0), jnp.concatenate([ident_s, sin], axis=0)


def _spread_rope_cols(w_rope):
    k = w_rope.shape[0]
    half = MLA_ROPE_DIM // 2
    z = jnp.zeros((k, half), w_rope.dtype)
    return jnp.concatenate([w_rope[:, :half], z, w_rope[:, half:], z], axis=1)


def _softmax_parts(parts, extra=None):
    m = parts[0].max(axis=-1, keepdims=True)
    for s in parts[1:]:
        m = jnp.maximum(m, s.max(axis=-1, keepdims=True))
    if extra is not None:
        m = jnp.maximum(m, extra)
    ps = [jnp.exp(s - m) for s in parts]
    den = ps[0].sum(axis=-1, keepdims=True)
    for p in ps[1:]:
        den = den + p.sum(axis=-1, keepdims=True)
    if extra is not None:
        den = den + jnp.exp(extra - m)
    return ps, 1.0 / den


def na_bias_table(rpb, rows):
    n_tiles = rows // NA_TILE_ROWS
    n_heads, _, n_dcol = rpb.shape
    drow, row_ok = [], []
    for tile in (0, 1, n_tiles - 1):
        kr0 = int(np.clip(NA_TILE_ROWS * tile - NA_ROWS // 2, 0, rows - NA_WIN_ROWS))
        r = NA_TILE_ROWS * tile + np.arange(NA_TILE_ROWS)
        r0 = np.clip(r - NA_ROWS // 2, 0, rows - NA_ROWS)
        krow = kr0 + np.arange(NA_WIN_ROWS)
        row_ok.append((krow[None, :] >= r0[:, None]) & (krow[None, :] < r0[:, None] + NA_ROWS))
        drow.append(np.clip(krow[None, :] - r[:, None] + NA_ROWS - 1, 0, 2 * NA_ROWS - 2))
    drow, row_ok = np.stack(drow), np.stack(row_ok)
    qc = np.arange(GRID_W)
    qcol0 = np.clip(qc - NA_COLS // 2, 0, GRID_W - NA_COLS)
    kc = np.arange(GRID_W)
    col_ok = (kc[None, :] >= qcol0[:, None]) & (kc[None, :] < qcol0[:, None] + NA_COLS)
    dcol = np.clip(kc[None, :] - qc[:, None] + NA_COLS - 1, 0, 2 * NA_COLS - 2)
    by_row = rpb.astype(F32)[:, drow, :]
    pick_col = jnp.asarray(dcol[None] == np.arange(n_dcol)[:, None, None], F32)
    vals = jnp.einsum('hpamd,dqk->hpaqmk', by_row, pick_col, precision=lax.Precision.HIGHEST)
    ok = row_ok[None, :, :, None, :, None] & col_ok[None, None, None, :, None, :]
    vals = jnp.where(jnp.asarray(ok), vals, NEG_INF)
    return vals.reshape(n_heads, 3, NA_TILE_ROWS * GRID_W, NA_WIN_ROWS * GRID_W)


def _na_kernel(q_ref, k_ref, v_ref, bias_ref, o_ref, *, ctx_len, rows, scale):
    tq = NA_TILE_ROWS * GRID_W
    tk = NA_WIN_ROWS * GRID_W
    n_tiles = rows // NA_TILE_ROWS
    lc = ctx_len

    s = _dot_nt(q_ref[0, 0:lc, :], k_ref[0, 0:lc, :]) * scale
    (p,), inv = _softmax_parts([s])
    o_ref[0, 0:lc, :] = (_dot(p.astype(BF16), v_ref[0, 0:lc, :]) * inv).astype(o_ref.dtype)

    def tile(i, carry):
        qs = pl.multiple_of(lc + i * tq, tq)
        kr0 = jnp.clip(NA_TILE_ROWS * i - NA_ROWS // 2, 0, rows - NA_WIN_ROWS)
        ks = pl.multiple_of(lc + kr0 * GRID_W, NA_TILE_ROWS * GRID_W)
        pat = jnp.where(i == 0, 0, jnp.where(i == n_tiles - 1, 2, 1))
        q = q_ref[0, pl.ds(qs, tq), :]
        s_loc = _dot_nt(q, k_ref[0, pl.ds(ks, tk), :]) * scale + bias_ref[0, pat]
        s_ctx = _dot_nt(q, k_ref[0, 0:lc, :]) * scale
        (p_loc, p_ctx), inv = _softmax_parts([s_loc, s_ctx])
        o = _dot(p_loc.astype(BF16), v_ref[0, pl.ds(ks, tk), :]) + _dot(p_ctx.astype(BF16), v_ref[0, 0:lc, :])
        o_ref[0, pl.ds(qs, tq), :] = (o * inv).astype(o_ref.dtype)
        return carry

    lax.fori_loop(0, n_tiles, tile, 0)


def neighbourhood_attention(qkv, bias, ctx_len, n_heads):
    b, p, _ = qkv.shape
    rows = (p - ctx_len) // GRID_W
    assert rows % NA_TILE_ROWS == 0 and rows >= NA_WIN_ROWS
    tq, tk = NA_TILE_ROWS * GRID_W, NA_WIN_ROWS * GRID_W
    assert ctx_len % 16 == 0 and ctx_len % tq == 0
    blk = lambda off: pl.BlockSpec((1, p, HEAD_DIM), lambda h, b: (b, 0, off + h))
    return pl.pallas_call(
        functools.partial(_na_kernel, ctx_len=ctx_len, rows=rows, scale=HEAD_DIM ** -0.5),
        grid=(n_heads, b),
        in_specs=[blk(0), blk(n_heads), blk(2 * n_heads),
                  pl.BlockSpec((1, 3, tq, tk), lambda h, b: (h, 0, 0, 0))],
        out_specs=blk(0),
        out_shape=jax.ShapeDtypeStruct((b, p, n_heads * HEAD_DIM), BF16),
        compiler_params=_cparams(("parallel", "parallel")),
        name="na_attention",
    )(qkv, qkv, qkv, bias)


def _swa_kernel(sink_ref, q_ref, k_ref, v_ref, o_ref, *, ctx_len, seq, group, scale):
    lc = ctx_len
    kvh = pl.program_id(1)

    def stack_heads(q):
        return jnp.concatenate([q[:, g * HEAD_DIM:(g + 1) * HEAD_DIM] for g in range(group)], axis=0)

    def finish(parts_fn, n_q, pv_fn, store):
        ps_all, invs = [], []
        for g in range(group):
            sink = sink_ref[kvh * group + g]
            ps, inv = _softmax_parts(parts_fn(g), extra=sink)
            ps_all.append(ps)
            invs.append(inv)
        n_parts = len(ps_all[0])
        stacked = [jnp.concatenate([ps_all[g][k] for g in range(group)], axis=0).astype(BF16)
                   for k in range(n_parts)]
        o = pv_fn(stacked)
        store(jnp.concatenate([o[g * n_q:(g + 1) * n_q] * invs[g] for g in range(group)], axis=1))

    s_c = _dot_nt(stack_heads(q_ref[0, 0:lc, :]), k_ref[0, 0:lc, :]) * scale

    def store_ctx(o):
        o_ref[0, 0:lc, :] = o.astype(o_ref.dtype)

    finish(lambda g: [s_c[g * lc:(g + 1) * lc]], lc,
           lambda st: _dot(st[0], v_ref[0, 0:lc, :]), store_ctx)

    n_tiles = seq // SWA_TQ

    def tile(t, carry):
        q0 = t * SWA_TQ
        k0 = jnp.clip(q0 - SWA_WINDOW, 0, seq - SWA_TK)
        qs = pl.multiple_of(lc + q0, SWA_WINDOW)
        ks = pl.multiple_of(lc + k0, SWA_WINDOW)
        q4 = stack_heads(q_ref[0, pl.ds(qs, SWA_TQ), :])
        s_loc = _dot_nt(q4, k_ref[0, pl.ds(ks, SWA_TK), :]) * scale
        s_ctx = _dot_nt(q4, k_ref[0, 0:lc, :]) * scale
        dpos = (lax.broadcasted_iota(jnp.int32, (SWA_TQ, SWA_TK), 1)
                - lax.broadcasted_iota(jnp.int32, (SWA_TQ, SWA_TK), 0)) + (k0 - q0)
        valid = jnp.abs(dpos) <= SWA_WINDOW

        def parts(g):
            sl = slice(g * SWA_TQ, (g + 1) * SWA_TQ)
            return [jnp.where(valid, s_loc[sl], NEG_INF), s_ctx[sl]]

        def store(o):
            o_ref[0, pl.ds(qs, SWA_TQ), :] = o.astype(o_ref.dtype)

        finish(parts, SWA_TQ,
               lambda st: _dot(st[0], v_ref[0, pl.ds(ks, SWA_TK), :]) + _dot(st[1], v_ref[0, 0:lc, :]),
               store)
        return carry

    lax.fori_loop(0, n_tiles, tile, 0)


def window_attention(qkv, sink, ctx_len, n_heads, n_kv_heads):
    b, p, _ = qkv.shape
    seq = p - ctx_len
    group = n_heads // n_kv_heads
    assert seq % SWA_TQ == 0 and seq >= SWA_TK and ctx_len % SWA_WINDOW == 0
    kv = lambda off: pl.BlockSpec((1, p, HEAD_DIM), lambda b, h: (b, 0, off + h))
    qo = pl.BlockSpec((1, p, group * HEAD_DIM), lambda b, h: (b, 0, h))
    return pl.pallas_call(
        functools.partial(_swa_kernel, ctx_len=ctx_len, seq=seq, group=group, scale=HEAD_DIM ** -0.5),
        grid=(b, n_kv_heads),
        in_specs=[pl.BlockSpec(memory_space=pltpu.SMEM), qo, kv(n_heads), kv(n_heads + n_kv_heads)],
        out_specs=qo,
        out_shape=jax.ShapeDtypeStruct((b, p, n_heads * HEAD_DIM), BF16),
        compiler_params=_cparams(("parallel", "parallel")),
        name="swa_attention",
    )(sink.astype(F32), qkv, qkv, qkv)


def _mla_down_kernel(x_ref, w_ref, gq_ref, gkv_ref, cos_ref, sin_ref, cq_ref, ckv_ref, kr_ref, *, q_rank, kv_rank):
    acc = _dot(x_ref[0], w_ref[...])

    def rms(x, g):
        ms = jnp.mean(x * x, axis=-1, keepdims=True)
        return x * lax.rsqrt(ms + NORM_EPS) * g

    cq_ref[0] = rms(acc[:, :q_rank], gq_ref[...]).astype(cq_ref.dtype)
    ckv_ref[0] = rms(acc[:, q_rank:q_rank + kv_rank], gkv_ref[...]).astype(ckv_ref.dtype)
    kr_ref[0] = _rope(acc[:, q_rank + kv_rank:], cos_ref[...], sin_ref[...]).astype(kr_ref.dtype)


def mla_down(x, w_down, gq, gkv, cos, sin, q_rank, kv_rank):
    b, p, k = x.shape
    n = w_down.shape[1]
    tm = 544 if p % 544 == 0 else _row_tile(p)
    row = lambda width: pl.BlockSpec((1, tm, width), lambda b, i: (b, i, 0))
    return pl.pallas_call(
        functools.partial(_mla_down_kernel, q_rank=q_rank, kv_rank=kv_rank),
        grid=(b, p // tm),
        in_specs=[row(k), pl.BlockSpec((k, n), lambda b, i: (0, 0)),
                  pl.BlockSpec((1, q_rank), lambda b, i: (0, 0)),
                  pl.BlockSpec((1, kv_rank), lambda b, i: (0, 0)),
                  pl.BlockSpec((tm, LANES), lambda b, i: (i, 0)),
                  pl.BlockSpec((tm, LANES), lambda b, i: (i, 0))],
        out_specs=[row(q_rank), row(kv_rank), row(LANES)],
        out_shape=[jax.ShapeDtypeStruct((b, p, q_rank), BF16),
                   jax.ShapeDtypeStruct((b, p, kv_rank), BF16),
                   jax.ShapeDtypeStruct((b, p, LANES), BF16)],
        compiler_params=_cparams(("parallel", "parallel")),
        name="mla_down",
    )(x, w_down, gq.reshape(1, q_rank), gkv.reshape(1, kv_rank), cos, sin)


def _mla_kernel(qn_ref, qr_ref, kn_ref, kr_ref, v_ref, o_ref, kcat_ref, *, ctx_len, scale):
    lc = ctx_len
    p_all = kcat_ref.shape[0]
    kcat_ref[:, 0:LANES] = kn_ref[0]
    kcat_ref[:, LANES:2 * LANES] = kr_ref[0]

    def attend(qs, n_q, n_k):
        q = jnp.concatenate([qn_ref[0, pl.ds(qs, n_q), :], qr_ref[0, pl.ds(qs, n_q), :]], axis=1)
        s = _dot_nt(q, kcat_ref[0:n_k, :]) * scale
        (p,), inv = _softmax_parts([s])
        o = _dot(p.astype(BF16), v_ref[0, 0:n_k, :])
        o_ref[0, pl.ds(qs, n_q), :] = (o * inv).astype(o_ref.dtype)

    attend(0, lc, lc)

    def tile(i, carry):
        attend(pl.multiple_of(lc + i * MLA_TQ, MLA_TQ), MLA_TQ, p_all)
        return carry

    lax.fori_loop(0, (p_all - lc) // MLA_TQ, tile, 0)


def latent_attention(q, kv, kr, ctx_len, n_heads):
    b, p, _ = q.shape
    assert (p - ctx_len) % MLA_TQ == 0 and ctx_len % MLA_TQ == 0
    blk = lambda f: pl.BlockSpec((1, p, LANES), f)
    scale = (MLA_NOPE_DIM + MLA_ROPE_DIM) ** -0.5
    return pl.pallas_call(
        functools.partial(_mla_kernel, ctx_len=ctx_len, scale=scale),
        grid=(b, n_heads),
        in_specs=[blk(lambda b, h: (b, 0, h)), blk(lambda b, h: (b, 0, n_heads + h)),
                  blk(lambda b, h: (b, 0, 2 * h)), blk(lambda b, h: (b, 0, 0)),
                  blk(lambda b, h: (b, 0, 2 * h + 1))],
        out_specs=blk(lambda b, h: (b, 0, h)),
        out_shape=jax.ShapeDtypeStruct((b, p, n_heads * LANES), BF16),
        scratch_shapes=[pltpu.VMEM((p, 2 * LANES), BF16)],
        compiler_params=_cparams(("parallel", "parallel")),
        name="mla_attention",
    )(q, q, kv, kr, kv)


def gather_rows(src, idx):
    n = idx.shape[0]
    width = src.shape[1]
    info = plsc.get_sparse_core_info()
    n_workers = info.num_cores * info.num_subcores
    assert n % (n_workers * GATHER_WINDOW) == 0
    per_worker = n // n_workers
    mesh = plsc.VectorSubcoreMesh(core_axis_name="core", subcore_axis_name="subcore")

    @functools.partial(
        pl.kernel, out_type=jax.ShapeDtypeStruct((n, width), src.dtype), mesh=mesh,
        scratch_types=[pltpu.VMEM((per_worker,), jnp.int32),
                       pltpu.VMEM((GATHER_WINDOW, width), src.dtype),
                       pltpu.SemaphoreType.DMA],
        name="gather_rows")
    def gather(src_hbm, idx_hbm, out_hbm, idx_vmem, rows_vmem, sem):
        worker = lax.axis_index("subcore") * info.num_cores + lax.axis_index("core")
        base = worker * per_worker
        pltpu.sync_copy(idx_hbm.at[pl.ds(base, per_worker)], idx_vmem)

        @pl.loop(0, per_worker, step=GATHER_WINDOW)
        def _(r):
            pltpu.async_copy(src_hbm.at[idx_vmem.at[pl.ds(r, GATHER_WINDOW)]], rows_vmem, sem).wait()
            pltpu.sync_copy(rows_vmem, out_hbm.at[pl.ds(base + r, GATHER_WINDOW)])

    return gather(src, idx)


def _expert_kernel(blk_e_ref, n_used_ref, x_ref, wg_ref, wu_ref, wd_ref, y_ref):
    i = pl.program_id(0)

    @pl.when(i < n_used_ref[0])
    def _():
        x = x_ref[...].astype(BF16)
        g = _dot(x, wg_ref[0])
        u = _dot(x, wu_ref[0])
        a = (g * jax.nn.sigmoid(g) * u).astype(BF16)
        y_ref[...] = _dot(a, wd_ref[0])

    @pl.when(i >= n_used_ref[0])
    def _():
        y_ref[...] = jnp.zeros_like(y_ref)


def expert_ffn(x_disp, blk_e, n_used, w_gate, w_up, w_down):
    rows, d = x_disp.shape
    n_blk = rows // MOE_BLOCK
    de = w_gate.shape[2]
    grid_spec = pltpu.PrefetchScalarGridSpec(
        num_scalar_prefetch=2,
        grid=(n_blk,),
        in_specs=[pl.BlockSpec((MOE_BLOCK, d), lambda i, be, nu: (i, 0)),
                  pl.BlockSpec((1, d, de), lambda i, be, nu: (be[i], 0, 0)),
                  pl.BlockSpec((1, d, de), lambda i, be, nu: (be[i], 0, 0)),
                  pl.BlockSpec((1, de, d), lambda i, be, nu: (be[i], 0, 0))],
        out_specs=pl.BlockSpec((MOE_BLOCK, d), lambda i, be, nu: (i, 0)),
    )
    return pl.pallas_call(
        _expert_kernel,
        grid_spec=grid_spec,
        out_shape=jax.ShapeDtypeStruct((rows, d), F32),
        compiler_params=_cparams(("arbitrary",)),
        name="expert_ffn",
    )(blk_e, n_used, x_disp, w_gate, w_up, w_down)


def _combine_kernel(h_ref, y0_ref, y1_ref, wt_ref, mod_ref, o_ref, *, gate_idx):
    wt = wt_ref[0]
    moe = wt[:, 0:1] * y0_ref[0, 0] + wt[:, 1:2] * y1_ref[0, 0]
    o_ref[0] = h_ref[0] + mod_ref[0, gate_idx:gate_idx + 1, :] * moe


def combine(h, y_pairs, wt, modtab, gate_idx):
    b, p, d = h.shape
    row = pl.BlockSpec((1, ROW_TILE, d), lambda b, j: (b, j, 0))
    return pl.pallas_call(
        functools.partial(_combine_kernel, gate_idx=gate_idx),
        grid=(b, p // ROW_TILE),
        in_specs=[row,
                  pl.BlockSpec((1, 1, ROW_TILE, d), lambda b, j: (0, b, j, 0)),
                  pl.BlockSpec((1, 1, ROW_TILE, d), lambda b, j: (1, b, j, 0)),
                  pl.BlockSpec((1, ROW_TILE, LANES), lambda b, j: (b, j, 0)),
                  _mod_spec(d)],
        out_specs=row,
        out_shape=jax.ShapeDtypeStruct((b, p, d), F32),
        compiler_params=_cparams(("parallel", "parallel")),
        name="moe_combine",
    )(h, y_pairs, y_pairs, wt, modtab.reshape(b * 2, N_MOD, d))


def hier_moe(h, f, eid, wt, modtab, gate_idx, w_gate, w_up, w_down):
    b, p, d = h.shape
    n_tok = b * p
    n_assign = 2 * n_tok
    eid_f = eid.reshape(n_assign)
    onehot = (eid_f[:, None] == jnp.arange(N_EXPERTS, dtype=jnp.int32)[None, :]).astype(jnp.int32)
    csum = jnp.cumsum(onehot, axis=0)
    counts = csum[-1]
    pos = jnp.take_along_axis(csum, eid_f[:, None], axis=1)[:, 0] - 1
    pcounts = (counts + MOE_BLOCK - 1) // MOE_BLOCK * MOE_BLOCK
    pend = jnp.cumsum(pcounts)
    dest = ((pend - pcounts)[eid_f] + pos).astype(jnp.int32)
    n_blk = -(-n_assign // MOE_BLOCK) + N_EXPERTS
    rows_total = n_blk * MOE_BLOCK
    assert rows_total % GATHER_WINDOW == 0
    row_tok = jnp.zeros((rows_total,), jnp.int32).at[dest].set(jnp.arange(n_assign, dtype=jnp.int32) // 2)
    blk_e = jnp.minimum(jnp.searchsorted(pend, jnp.arange(n_blk, dtype=jnp.int32) * MOE_BLOCK, side='right'),
                        N_EXPERTS - 1).astype(jnp.int32)
    n_used = (pend[-1] // MOE_BLOCK).astype(jnp.int32).reshape(1)

    x_disp = gather_rows(f.reshape(n_tok, d), row_tok)
    y = expert_ffn(x_disp, blk_e, n_used, w_gate, w_up, w_down)
    dest_by_slot = jnp.concatenate([dest[0::2], dest[1::2]])
    y_pairs = gather_rows(y, dest_by_slot).reshape(2, b, p, d)
    return combine(h, y_pairs, wt, modtab, gate_idx)


def kernel(x, c, ctx, c_ctx, mod_w, mod_b, norm_mix_g, norm_ffn_g, router_grp_w, router_grp_b, router_exp_w, router_exp_b, exp_w_gate, exp_w_up, exp_w_down, l0_na_w_qkv, l0_na_rpb, l0_na_w_o, l1_swa_w_qkv, l1_swa_sink, l1_swa_w_o, l2_mla_w_dq, l2_mla_q_norm_g, l2_mla_w_uq, l2_mla_w_dkv, l2_mla_kv_norm_g, l2_mla_w_ukv, l2_mla_w_o, l3_na_w_qkv, l3_na_rpb, l3_na_w_o, final_norm_g):
    b, s, d = x.shape
    lc = ctx.shape[1]
    n_heads = d // HEAD_DIM
    n_kv_heads = n_heads // 4
    depth = mod_w.shape[0]
    rows = s // GRID_W

    h = jnp.concatenate([ctx, x], axis=1)
    modtabs = modulation_tables(c, c_ctx, mod_w, mod_b)

    def na_mixer(hm, w_qkv, rpb):
        qkv = project(hm, w_qkv.astype(BF16))
        return neighbourhood_attention(qkv, na_bias_table(rpb, rows), lc, n_heads)

    def swa_mixer(hm):
        cos, sin = rope_tables_full(s, lc)
        n_rope = (n_heads + n_kv_heads) * HEAD_DIM // 512
        qkv = project(hm, l1_swa_w_qkv.astype(BF16), tn=512, rope=(cos, sin, 0, n_rope))
        return window_attention(qkv, l1_swa_sink, lc, n_heads, n_kv_heads)

    def mla_mixer(hm):
        q_rank = l2_mla_w_dq.shape[1]
        kv_rank = l2_mla_kv_norm_g.shape[0]
        cos, sin = rope_tables_mla(s, lc)
        w_down = jnp.concatenate([l2_mla_w_dq, l2_mla_w_dkv[:, :kv_rank],
                                  _spread_rope_cols(l2_mla_w_dkv[:, kv_rank:])], axis=1).astype(BF16)
        cq, ckv, kr = mla_down(hm, w_down, l2_mla_q_norm_g, l2_mla_kv_norm_g, cos, sin, q_rank, kv_rank)
        w_uq = l2_mla_w_uq.reshape(q_rank, n_heads, MLA_NOPE_DIM + MLA_ROPE_DIM)
        w_q_nope = w_uq[:, :, :MLA_NOPE_DIM].reshape(q_rank, n_heads * MLA_NOPE_DIM)
        w_q_rope = jnp.concatenate([_spread_rope_cols(w_uq[:, hh, MLA_NOPE_DIM:]) for hh in range(n_heads)], axis=1)
        w_q = jnp.concatenate([w_q_nope, w_q_rope], axis=1).astype(BF16)
        n_nope_tiles = n_heads * MLA_NOPE_DIM // 512
        q = project(cq, w_q, tn=512, rope=(cos, sin, n_nope_tiles, 2 * n_nope_tiles))
        kv = project(ckv, l2_mla_w_ukv.astype(BF16), tn=512)
        return latent_attention(q, kv, kr, lc, n_heads)

    for i in range(depth):
        modtab = modtabs[i]
        hm = norm_modulate(h, norm_mix_g[i], modtab, 0)
        mixer = i % 3
        if mixer == 0:
            w_qkv, rpb, w_o = (l0_na_w_qkv, l0_na_rpb, l0_na_w_o) if i == 0 else (l3_na_w_qkv, l3_na_rpb, l3_na_w_o)
            y = na_mixer(hm, w_qkv, rpb)
        elif mixer == 1:
            y, w_o = swa_mixer(hm), l1_swa_w_o
        else:
            y, w_o = mla_mixer(hm), l2_mla_w_o
        h = project(y, w_o.astype(BF16), tn=512, resid=(h, modtab, 2), ctx_len=lc)
        f, eid, wt = norm_modulate_route(h, norm_ffn_g[i], modtab, 3, router_grp_w[i], router_grp_b[i],
                                         router_exp_w[i], router_exp_b[i])
        h = hier_moe(h, f, eid[:, :, :2], wt, modtab, 5,
                     exp_w_gate[i].astype(BF16), exp_w_up[i].astype(BF16), exp_w_down[i].astype(BF16))
    return final_norm(h, final_norm_g, lc)
```

```python
import functools

import numpy as np
import jax
import jax.numpy as jnp
from jax import lax
from jax.experimental import pallas as pl
from jax.experimental.pallas import tpu as pltpu
from jax.experimental.pallas import tpu_sc as plsc

GRID_W = 64
HEAD_DIM = 128
ROPE_BASE = 10000.0
NORM_EPS = 1e-6
NEG_INF = -1e30
N_MOD = 6

NA_ROWS = 8
NA_COLS = 16
NA_TILE_ROWS = 4
NA_WIN_ROWS = 12

SWA_WINDOW = 128
SWA_TQ = 256
SWA_TK = 512

MLA_NOPE_DIM = 128
MLA_ROPE_DIM = 64
MLA_TQ = 512
MLA_TK = 512
LOG2E = 1.4426950408889634

N_GROUPS = 4
EXPERTS_PER_GROUP = 8
N_EXPERTS = N_GROUPS * EXPERTS_PER_GROUP
MOE_BLOCK = 256
GATHER_WINDOW = 32

LANES = 128
ROW_TILE = 256
VMEM_LIMIT = 56 * 1024 * 1024

BF16 = jnp.bfloat16
F32 = jnp.float32


def _cparams(sem):
    return pltpu.CompilerParams(dimension_semantics=sem, vmem_limit_bytes=VMEM_LIMIT)


def _dot(a, b):
    return jnp.dot(a, b, preferred_element_type=F32)


def _dot_nt(a, b):
    return lax.dot_general(a, b, (((1,), (1,)), ((), ())), preferred_element_type=F32)


def _mod_kernel(x_ref, w_ref, b_ref, o_ref):
    x = x_ref[...]
    sx = (x * jax.nn.sigmoid(x)).astype(BF16)
    o_ref[0] = _dot(sx, w_ref[0].astype(BF16)) + b_ref[0]


def modulation_tables(c, c_ctx, mod_w, mod_b):
    depth, d, n_out = mod_w.shape
    b = c.shape[0]
    rows = 16
    xin = jnp.zeros((rows, d), F32).at[:b].set(c).at[b].set(c_ctx)
    tn = 1024
    out = pl.pallas_call(
        _mod_kernel,
        grid=(depth, n_out // tn),
        in_specs=[pl.BlockSpec((rows, d), lambda i, j: (0, 0)),
                  pl.BlockSpec((1, d, tn), lambda i, j: (i, 0, j)),
                  pl.BlockSpec((1, 1, tn), lambda i, j: (i, 0, j))],
        out_specs=pl.BlockSpec((1, rows, tn), lambda i, j: (i, 0, j)),
        out_shape=jax.ShapeDtypeStruct((depth, rows, n_out), F32),
        compiler_params=_cparams(("parallel", "parallel")),
        name="adaln_mod",
    )(xin, mod_w, mod_b.reshape(depth, 1, n_out))
    lat = out[:, :b].reshape(depth, b, 1, N_MOD, d)
    ctx = jnp.broadcast_to(out[:, b].reshape(depth, 1, 1, N_MOD, d), (depth, b, 1, N_MOD, d))
    return jnp.concatenate([ctx, lat], axis=2)


def _rms_mod(x, g, shift, scale):
    ms = jnp.mean(x * x, axis=-1, keepdims=True)
    y = x * lax.rsqrt(ms + NORM_EPS) * g
    return y * (1.0 + scale) + shift


def _norm_mod_kernel(h_ref, g_ref, mod_ref, o_ref, *, shift_idx):
    f = _rms_mod(h_ref[0], g_ref[...], mod_ref[0, shift_idx:shift_idx + 1, :],
                 mod_ref[0, shift_idx + 1:shift_idx + 2, :])
    o_ref[0] = f.astype(o_ref.dtype)


def _route(logits):
    lane = lax.broadcasted_iota(jnp.int32, logits.shape, 1).astype(F32)
    big = float(LANES)

    def first_lane(mask):
        return jnp.min(jnp.where(mask, lane, big), axis=-1, keepdims=True)

    in_grp = lane < N_GROUPS
    lg = jnp.where(in_grp, logits, NEG_INF)
    m_g = jnp.max(lg, axis=-1, keepdims=True)
    g_idx = first_lane(in_grp & (lg == m_g))
    g_w = 1.0 / jnp.sum(jnp.where(in_grp, jnp.exp(lg - m_g), 0.0), axis=-1, keepdims=True)
    e_lo = N_GROUPS + g_idx * EXPERTS_PER_GROUP
    in_e = (lane >= e_lo) & (lane < e_lo + EXPERTS_PER_GROUP)
    le = jnp.where(in_e, logits, NEG_INF)
    m1 = jnp.max(le, axis=-1, keepdims=True)
    e1 = first_lane(in_e & (le == m1))
    s_e = jnp.sum(jnp.where(in_e, jnp.exp(le - m1), 0.0), axis=-1, keepdims=True)
    in_e2 = in_e & (lane != e1)
    le2 = jnp.where(in_e2, logits, NEG_INF)
    m2 = jnp.max(le2, axis=-1, keepdims=True)
    e2 = first_lane(in_e2 & (le2 == m2))
    p1 = 1.0 / s_e
    p2 = jnp.exp(m2 - m1) / s_e
    den = p1 + p2
    return ((e1 - N_GROUPS).astype(jnp.int32), (e2 - N_GROUPS).astype(jnp.int32),
            g_w * p1 / den, g_w * p2 / den)


def _pack_bf16_pairs(x):
    n = x.shape[1] // 2
    xb = x.astype(BF16).astype(F32)
    hi = lax.bitcast_convert_type(xb[:, :n], jnp.int32)
    lo = lax.bitcast_convert_type(xb[:, n:], jnp.int32)
    return (hi & jnp.int32(-65536)) | lax.shift_right_logical(lo, jnp.int32(16))


def _unpack_bf16_pairs(w):
    hi = lax.bitcast_convert_type(w & jnp.int32(-65536), F32)
    lo = lax.bitcast_convert_type(lax.shift_left(w, jnp.int32(16)), F32)
    return jnp.concatenate([hi, lo], axis=1)


def _norm_mod_route_kernel(h_ref, g_ref, mod_ref, wr_ref, br_ref, f_ref, eid_ref, wt_ref, cnt_ref, run_ref, *,
                           shift_idx):
    first = (pl.program_id(0) == 0) & (pl.program_id(1) == 0)

    @pl.when(first)
    def _():
        run_ref[...] = jnp.zeros_like(run_ref)

    f = _rms_mod(h_ref[0], g_ref[...], mod_ref[0, shift_idx:shift_idx + 1, :],
                 mod_ref[0, shift_idx + 1:shift_idx + 2, :])
    f_ref[0] = _pack_bf16_pairs(f)
    f_hi = f.astype(BF16)
    f_lo = (f - f_hi.astype(F32)).astype(BF16)
    logits = (_dot(f_hi, wr_ref[0]) + _dot(f_lo, wr_ref[0]) + _dot(f_hi, wr_ref[1])) + br_ref[...]
    e1, e2, w1, w2 = _route(logits)
    lane = lax.broadcasted_iota(jnp.int32, logits.shape, 1)
    pick1, pick2 = lane == e1, lane == e2
    chosen = (pick1 | pick2).astype(F32)
    n_rows = chosen.shape[0]
    earlier = (lax.broadcasted_iota(jnp.int32, (n_rows, n_rows), 1)
               < lax.broadcasted_iota(jnp.int32, (n_rows, n_rows), 0)).astype(BF16)
    before = run_ref[...] + _dot(earlier, chosen.astype(BF16))
    rank1 = jnp.sum(jnp.where(pick1, before, 0.0), axis=-1, keepdims=True).astype(jnp.int32)
    rank2 = jnp.sum(jnp.where(pick2, before, 0.0), axis=-1, keepdims=True).astype(jnp.int32)
    run_ref[...] = run_ref[...] + jnp.sum(chosen, axis=0, keepdims=True)
    cnt_ref[...] = run_ref[...]
    eid_ref[0] = jnp.where(lane == 0, e1, jnp.where(lane == 1, e2, jnp.where(lane == 2, rank1,
                                                                              jnp.where(lane == 3, rank2, 0))))
    wt_ref[0] = jnp.where(lane == 0, w1, jnp.where(lane == 1, w2, 0.0))


def _mod_spec(d):
    return pl.BlockSpec((1, N_MOD, d), lambda b, j: (2 * b + jnp.minimum(j, 1), 0, 0))


def norm_modulate(h, g, modtab, shift_idx):
    b, p, d = h.shape
    row = pl.BlockSpec((1, ROW_TILE, d), lambda b, j: (b, j, 0))
    return pl.pallas_call(
        functools.partial(_norm_mod_kernel, shift_idx=shift_idx),
        grid=(b, p // ROW_TILE),
        in_specs=[row, pl.BlockSpec((1, d), lambda b, j: (0, 0)), _mod_spec(d)],
        out_specs=row,
        out_shape=jax.ShapeDtypeStruct((b, p, d), BF16),
        compiler_params=_cparams(("parallel", "parallel")),
        name="norm_mod",
    )(h, g.reshape(1, d), modtab.reshape(b * 2, N_MOD, d))


def norm_modulate_route(h, g, modtab, shift_idx, w_grp, b_grp, w_rt, b_rt):
    b, p, d = h.shape
    n_r = N_GROUPS + N_EXPERTS
    wr = jnp.zeros((d, LANES), F32).at[:, :N_GROUPS].set(w_grp).at[:, N_GROUPS:n_r].set(w_rt)
    br = jnp.zeros((1, LANES), F32).at[0, :N_GROUPS].set(b_grp).at[0, N_GROUPS:n_r].set(b_rt)
    wr_hi = wr.astype(BF16)
    wr = jnp.stack([wr_hi, (wr - wr_hi.astype(F32)).astype(BF16)])
    row = pl.BlockSpec((1, ROW_TILE, d), lambda b, j: (b, j, 0))
    packed = pl.BlockSpec((1, ROW_TILE, d // 2), lambda b, j: (b, j, 0))
    small = pl.BlockSpec((1, ROW_TILE, LANES), lambda b, j: (b, j, 0))
    return pl.pallas_call(
        functools.partial(_norm_mod_route_kernel, shift_idx=shift_idx),
        grid=(b, p // ROW_TILE),
        in_specs=[row, pl.BlockSpec((1, d), lambda b, j: (0, 0)), _mod_spec(d),
                  pl.BlockSpec((2, d, LANES), lambda b, j: (0, 0, 0)),
                  pl.BlockSpec((1, LANES), lambda b, j: (0, 0))],
        out_specs=[packed, small, small, pl.BlockSpec((1, LANES), lambda b, j: (0, 0))],
        out_shape=[jax.ShapeDtypeStruct((b, p, d // 2), jnp.int32),
                   jax.ShapeDtypeStruct((b, p, LANES), jnp.int32),
                   jax.ShapeDtypeStruct((b, p, LANES), F32),
                   jax.ShapeDtypeStruct((1, LANES), F32)],
        scratch_shapes=[pltpu.VMEM((1, LANES), F32)],
        compiler_params=_cparams(("arbitrary", "arbitrary")),
        name="norm_mod_route",
    )(h, g.reshape(1, d), modtab.reshape(b * 2, N_MOD, d), wr, br)


def _final_norm_kernel(h_ref, g_ref, o_ref):
    x = h_ref[0]
    ms = jnp.mean(x * x, axis=-1, keepdims=True)
    o_ref[0] = x * lax.rsqrt(ms + NORM_EPS) * g_ref[...]


def final_norm(h, g, ctx_len):
    b, p, d = h.shape
    skip = ctx_len // ROW_TILE
    return pl.pallas_call(
        _final_norm_kernel,
        grid=(b, (p - ctx_len) // ROW_TILE),
        in_specs=[pl.BlockSpec((1, ROW_TILE, d), lambda b, j: (b, j + skip, 0)),
                  pl.BlockSpec((1, d), lambda b, j: (0, 0))],
        out_specs=pl.BlockSpec((1, ROW_TILE, d), lambda b, j: (b, j, 0)),
        out_shape=jax.ShapeDtypeStruct((b, p - ctx_len, d), F32),
        compiler_params=_cparams(("parallel", "parallel")),
        name="final_norm",
    )(h, g.reshape(1, d))


def _rope(acc, cos, sin):
    n_blk = acc.shape[1] // LANES
    outs = []
    for c in range(n_blk):
        x = acc[:, c * LANES:(c + 1) * LANES]
        outs.append(x * cos + pltpu.roll(x, LANES // 2, 1) * sin)
    return outs[0] if n_blk == 1 else jnp.concatenate(outs, axis=1)


def _proj_kernel(*refs, rope_lo, rope_hi, resid, gate_idx, ctx_len, tm):
    x_ref, w_ref = refs[0], refs[1]
    o_ref = refs[-1]
    acc = _dot(x_ref[0], w_ref[...])
    if resid:
        res_ref, mod_ref = refs[2], refs[3]
        row = pl.program_id(1) * tm + lax.broadcasted_iota(jnp.int32, (tm, 1), 0)
        gate = jnp.where(row < ctx_len, mod_ref[0, 0, gate_idx:gate_idx + 1, :],
                         mod_ref[0, 1, gate_idx:gate_idx + 1, :])
        o_ref[0] = res_ref[0] + gate * acc
    elif rope_hi > rope_lo:
        cos_ref, sin_ref = refs[2], refs[3]
        j = pl.program_id(2)
        roped = (j >= rope_lo) & (j < rope_hi)

        @pl.when(roped)
        def _():
            o_ref[0] = _rope(acc, cos_ref[...], sin_ref[...]).astype(o_ref.dtype)

        @pl.when(jnp.logical_not(roped))
        def _():
            o_ref[0] = acc.astype(o_ref.dtype)
    else:
        o_ref[0] = acc.astype(o_ref.dtype)


def _row_tile(p):
    for cand in (1088, 1024, 544, 512, 272, 256, 128, 64, 32, 16):
        if p % cand == 0:
            return cand
    raise ValueError(p)


def project(x, w, *, tn=512, out_dtype=BF16, rope=None, resid=None, ctx_len=0):
    b, p, k = x.shape
    n = w.shape[1]
    tm = _row_tile(p)
    tn = min(tn, n)
    assert n % tn == 0
    in_specs = [pl.BlockSpec((1, tm, k), lambda b, i, j: (b, i, 0)),
                pl.BlockSpec((k, tn), lambda b, i, j: (0, j))]
    args = [x, w]
    kw = dict(rope_lo=0, rope_hi=0, resid=False, gate_idx=0, ctx_len=ctx_len, tm=tm)
    if resid is not None:
        h, modtab, gate_idx = resid
        in_specs += [pl.BlockSpec((1, tm, tn), lambda b, i, j: (b, i, j)),
                     pl.BlockSpec((1, 2, N_MOD, tn), lambda b, i, j: (b, 0, 0, j))]
        args += [h, modtab]
        kw.update(resid=True, gate_idx=gate_idx)
        out_dtype = F32
    elif rope is not None:
        cos, sin, lo, hi = rope
        in_specs += [pl.BlockSpec((tm, LANES), lambda b, i, j: (i, 0)),
                     pl.BlockSpec((tm, LANES), lambda b, i, j: (i, 0))]
        args += [cos, sin]
        kw.update(rope_lo=lo, rope_hi=hi)
    return pl.pallas_call(
        functools.partial(_proj_kernel, **kw),
        grid=(b, p // tm, n // tn),
        in_specs=in_specs,
        out_specs=pl.BlockSpec((1, tm, tn), lambda b, i, j: (b, i, j)),
        out_shape=jax.ShapeDtypeStruct((b, p, n), out_dtype),
        compiler_params=_cparams(("parallel", "parallel", "arbitrary")),
        name="project",
    )(*args)


def _axial_cos_sin(n, rot_dim):
    t = jnp.arange(n, dtype=jnp.int32)
    row = (t // GRID_W).astype(F32)
    col = (t % GRID_W).astype(F32)
    n_freq = rot_dim // 4
    inv = ROPE_BASE ** (-jnp.arange(n_freq, dtype=F32) / n_freq)
    ang = jnp.concatenate([row[:, None] * inv, col[:, None] * inv], axis=-1)
    return jnp.cos(ang), jnp.sin(ang)


def rope_tables_full(s, ctx_len):
    c, sn = _axial_cos_sin(s, HEAD_DIM)
    cos = jnp.concatenate([c, c], axis=1)
    sin = jnp.concatenate([-sn, sn], axis=1)
    ident_c = jnp.ones((ctx_len, LANES), F32)
    ident_s = jnp.zeros((ctx_len, LANES), F32)
    return jnp.concatenate([ident_c, cos], axis=0), jnp.concatenate([ident_s, sin], axis=0)


def rope_tables_mla(s, ctx_len):
    c, sn = _axial_cos_sin(s, MLA_ROPE_DIM)
    one = jnp.ones_like(c)
    zero = jnp.zeros_like(c)
    cos = jnp.concatenate([c, one, c, one], axis=1)
    sin = jnp.concatenate([-sn, zero, sn, zero], axis=1)
    ident_c = jnp.ones((ctx_len, LANES), F32)
    ident_s = jnp.zeros((ctx_len, LANES), F32)
    return jnp.concatenate([ident_c, cos], axis=0), jnp.concatenate([ident_s, sin], axis=0)


def _spread_rope_cols(w_rope):
    k = w_rope.shape[0]
    half = MLA_ROPE_DIM // 2
    z = jnp.zeros((k, half), w_rope.dtype)
    return jnp.concatenate([w_rope[:, :half], z, w_rope[:, half:], z], axis=1)


def _softmax_parts(parts, extra=None):
    m = _row_reduce(jnp.maximum, jnp.max, parts)
    if extra is not None:
        m = jnp.maximum(m, extra)
    ps = [jnp.exp2(s - m) for s in parts]
    den = _row_reduce(jnp.add, jnp.sum, ps)
    if extra is not None:
        den = den + jnp.exp2(extra - m)
    return ps, 1.0 / den


def _row_reduce(combine, reduce, parts):
    blocks = [s[:, c:c + LANES] for s in parts for c in range(0, s.shape[1], LANES)]
    acc = blocks[0]
    for blk in blocks[1:]:
        acc = combine(acc, blk)
    return reduce(acc, axis=-1, keepdims=True)


def na_bias_table(rpb, rows):
    n_tiles = rows // NA_TILE_ROWS
    n_heads, _, n_dcol = rpb.shape
    drow, row_ok = [], []
    for tile in (0, 1, n_tiles - 1):
        kr0 = int(np.clip(NA_TILE_ROWS * tile - NA_ROWS // 2, 0, rows - NA_WIN_ROWS))
        r = NA_TILE_ROWS * tile + np.arange(NA_TILE_ROWS)
        r0 = np.clip(r - NA_ROWS // 2, 0, rows - NA_ROWS)
        krow = kr0 + np.arange(NA_WIN_ROWS)
        row_ok.append((krow[None, :] >= r0[:, None]) & (krow[None, :] < r0[:, None] + NA_ROWS))
        drow.append(np.clip(krow[None, :] - r[:, None] + NA_ROWS - 1, 0, 2 * NA_ROWS - 2))
    drow, row_ok = np.stack(drow), np.stack(row_ok)
    qc = np.arange(GRID_W)
    qcol0 = np.clip(qc - NA_COLS // 2, 0, GRID_W - NA_COLS)
    kc = np.arange(GRID_W)
    col_ok = (kc[None, :] >= qcol0[:, None]) & (kc[None, :] < qcol0[:, None] + NA_COLS)
    dcol = np.clip(kc[None, :] - qc[:, None] + NA_COLS - 1, 0, 2 * NA_COLS - 2)
    by_row = rpb.astype(F32)[:, drow, :]
    pick_col = jnp.asarray(dcol[None] == np.arange(n_dcol)[:, None, None], F32)
    vals = jnp.einsum('hpamd,dqk->hpaqmk', by_row, pick_col, precision=lax.Precision.HIGHEST)
    ok = row_ok[None, :, :, None, :, None] & col_ok[None, None, None, :, None, :]
    vals = jnp.where(jnp.asarray(ok), vals * LOG2E, NEG_INF)
    return vals.reshape(n_heads, 3, NA_TILE_ROWS * GRID_W, NA_WIN_ROWS * GRID_W)


def _na_kernel(q_ref, k_ref, v_ref, bias_ref, o_ref, *, ctx_len, rows):
    tq = NA_TILE_ROWS * GRID_W
    tk = NA_WIN_ROWS * GRID_W
    n_tiles = rows // NA_TILE_ROWS
    lc = ctx_len

    s = _dot_nt(q_ref[0, 0:lc, :], k_ref[0, 0:lc, :])
    (p,), inv = _softmax_parts([s])
    o_ref[0, 0:lc, :] = (_dot(p.astype(BF16), v_ref[0, 0:lc, :]) * inv).astype(o_ref.dtype)

    def tile(i, carry):
        qs = pl.multiple_of(lc + i * tq, tq)
        kr0 = jnp.clip(NA_TILE_ROWS * i - NA_ROWS // 2, 0, rows - NA_WIN_ROWS)
        ks = pl.multiple_of(lc + kr0 * GRID_W, NA_TILE_ROWS * GRID_W)
        pat = jnp.where(i == 0, 0, jnp.where(i == n_tiles - 1, 2, 1))
        q = q_ref[0, pl.ds(qs, tq), :]
        s_loc = _dot_nt(q, k_ref[0, pl.ds(ks, tk), :]) + bias_ref[0, pat]
        s_ctx = _dot_nt(q, k_ref[0, 0:lc, :])
        (p_loc, p_ctx), inv = _softmax_parts([s_loc, s_ctx])
        o = _dot(p_loc.astype(BF16), v_ref[0, pl.ds(ks, tk), :]) + _dot(p_ctx.astype(BF16), v_ref[0, 0:lc, :])
        o_ref[0, pl.ds(qs, tq), :] = (o * inv).astype(o_ref.dtype)
        return carry

    lax.fori_loop(0, n_tiles, tile, 0, unroll=2)


def neighbourhood_attention(qkv, bias, ctx_len, n_heads):
    b, p, _ = qkv.shape
    rows = (p - ctx_len) // GRID_W
    assert rows % NA_TILE_ROWS == 0 and rows >= NA_WIN_ROWS
    tq, tk = NA_TILE_ROWS * GRID_W, NA_WIN_ROWS * GRID_W
    assert ctx_len % 16 == 0 and ctx_len % tq == 0
    blk = lambda off: pl.BlockSpec((1, p, HEAD_DIM), lambda h, b: (b, 0, off + h))
    return pl.pallas_call(
        functools.partial(_na_kernel, ctx_len=ctx_len, rows=rows),
        grid=(n_heads, b),
        in_specs=[blk(0), blk(n_heads), blk(2 * n_heads),
                  pl.BlockSpec((1, 3, tq, tk), lambda h, b: (h, 0, 0, 0))],
        out_specs=blk(0),
        out_shape=jax.ShapeDtypeStruct((b, p, n_heads * HEAD_DIM), BF16),
        compiler_params=_cparams(("parallel", "parallel")),
        name="na_attention",
    )(qkv, qkv, qkv, bias)


def _swa_kernel(sink_ref, q_ref, k_ref, v_ref, o_ref, *, ctx_len, seq, group):
    lc = ctx_len
    kvh = pl.program_id(1)

    def stack_heads(q):
        return jnp.concatenate([q[:, g * HEAD_DIM:(g + 1) * HEAD_DIM] for g in range(group)], axis=0)

    def finish(parts_fn, n_q, pv_fn, store):
        ps_all, invs = [], []
        for g in range(group):
            sink = sink_ref[kvh * group + g]
            ps, inv = _softmax_parts(parts_fn(g), extra=sink)
            ps_all.append(ps)
            invs.append(inv)
        n_parts = len(ps_all[0])
        stacked = [jnp.concatenate([ps_all[g][k] for g in range(group)], axis=0).astype(BF16)
                   for k in range(n_parts)]
        o = pv_fn(stacked)
        store(jnp.concatenate([o[g * n_q:(g + 1) * n_q] * invs[g] for g in range(group)], axis=1))

    s_c = _dot_nt(stack_heads(q_ref[0, 0:lc, :]), k_ref[0, 0:lc, :])

    def store_ctx(o):
        o_ref[0, 0:lc, :] = o.astype(o_ref.dtype)

    finish(lambda g: [s_c[g * lc:(g + 1) * lc]], lc,
           lambda st: _dot(st[0], v_ref[0, 0:lc, :]), store_ctx)

    n_tiles = seq // SWA_TQ

    def tile(t, carry):
        q0 = t * SWA_TQ
        k0 = jnp.clip(q0 - SWA_WINDOW, 0, seq - SWA_TK)
        qs = pl.multiple_of(lc + q0, SWA_WINDOW)
        ks = pl.multiple_of(lc + k0, SWA_WINDOW)
        q4 = stack_heads(q_ref[0, pl.ds(qs, SWA_TQ), :])
        s_loc = _dot_nt(q4, k_ref[0, pl.ds(ks, SWA_TK), :])
        s_ctx = _dot_nt(q4, k_ref[0, 0:lc, :])
        dpos = (lax.broadcasted_iota(jnp.int32, (SWA_TQ, SWA_TK), 1)
                - lax.broadcasted_iota(jnp.int32, (SWA_TQ, SWA_TK), 0)) + (k0 - q0)
        valid = jnp.abs(dpos) <= SWA_WINDOW

        def parts(g):
            sl = slice(g * SWA_TQ, (g + 1) * SWA_TQ)
            return [jnp.where(valid, s_loc[sl], NEG_INF), s_ctx[sl]]

        def store(o):
            o_ref[0, pl.ds(qs, SWA_TQ), :] = o.astype(o_ref.dtype)

        finish(parts, SWA_TQ,
               lambda st: _dot(st[0], v_ref[0, pl.ds(ks, SWA_TK), :]) + _dot(st[1], v_ref[0, 0:lc, :]),
               store)
        return carry

    lax.fori_loop(0, n_tiles, tile, 0, unroll=2)


def window_attention(qkv, sink, ctx_len, n_heads, n_kv_heads):
    b, p, _ = qkv.shape
    seq = p - ctx_len
    group = n_heads // n_kv_heads
    assert seq % SWA_TQ == 0 and seq >= SWA_TK and ctx_len % SWA_WINDOW == 0
    kv = lambda off: pl.BlockSpec((1, p, HEAD_DIM), lambda b, h: (b, 0, off + h))
    qo = pl.BlockSpec((1, p, group * HEAD_DIM), lambda b, h: (b, 0, h))
    return pl.pallas_call(
        functools.partial(_swa_kernel, ctx_len=ctx_len, seq=seq, group=group),
        grid=(b, n_kv_heads),
        in_specs=[pl.BlockSpec(memory_space=pltpu.SMEM), qo, kv(n_heads), kv(n_heads + n_kv_heads)],
        out_specs=qo,
        out_shape=jax.ShapeDtypeStruct((b, p, n_heads * HEAD_DIM), BF16),
        compiler_params=_cparams(("parallel", "parallel")),
        name="swa_attention",
    )(sink.astype(F32) * LOG2E, qkv, qkv, qkv)


def _mla_down_kernel(x_ref, w_ref, gq_ref, gkv_ref, cos_ref, sin_ref, cq_ref, ckv_ref, kr_ref, *, q_rank, kv_rank):
    acc = _dot(x_ref[0], w_ref[...])

    def rms(x, g):
        ms = jnp.mean(x * x, axis=-1, keepdims=True)
        return x * lax.rsqrt(ms + NORM_EPS) * g

    cq_ref[0] = rms(acc[:, :q_rank], gq_ref[...]).astype(cq_ref.dtype)
    ckv_ref[0] = rms(acc[:, q_rank:q_rank + kv_rank], gkv_ref[...]).astype(ckv_ref.dtype)
    kr_ref[0] = _rope(acc[:, q_rank + kv_rank:], cos_ref[...], sin_ref[...]).astype(kr_ref.dtype)


def mla_down(x, w_down, gq, gkv, cos, sin, q_rank, kv_rank):
    b, p, k = x.shape
    n = w_down.shape[1]
    tm = 544 if p % 544 == 0 else _row_tile(p)
    row = lambda width: pl.BlockSpec((1, tm, width), lambda b, i: (b, i, 0))
    return pl.pallas_call(
        functools.partial(_mla_down_kernel, q_rank=q_rank, kv_rank=kv_rank),
        grid=(b, p // tm),
        in_specs=[row(k), pl.BlockSpec((k, n), lambda b, i: (0, 0)),
                  pl.BlockSpec((1, q_rank), lambda b, i: (0, 0)),
                  pl.BlockSpec((1, kv_rank), lambda b, i: (0, 0)),
                  pl.BlockSpec((tm, LANES), lambda b, i: (i, 0)),
                  pl.BlockSpec((tm, LANES), lambda b, i: (i, 0))],
        out_specs=[row(q_rank), row(kv_rank), row(LANES)],
        out_shape=[jax.ShapeDtypeStruct((b, p, q_rank), BF16),
                   jax.ShapeDtypeStruct((b, p, kv_rank), BF16),
                   jax.ShapeDtypeStruct((b, p, LANES), BF16)],
        compiler_params=_cparams(("parallel", "parallel")),
        name="mla_down",
    )(x, w_down, gq.reshape(1, q_rank), gkv.reshape(1, kv_rank), cos, sin)


def _mla_kernel(qn_ref, qr_ref, kn_ref, kr_ref, v_ref, o_ref, kcat_ref, vone_ref, *, ctx_len):
    lc = ctx_len
    p_all = kcat_ref.shape[0]
    kcat_ref[:, 0:LANES] = kn_ref[0]
    kcat_ref[:, LANES:2 * LANES] = kr_ref[0]
    vone_ref[:, 0:LANES] = v_ref[0]
    vone_ref[:, LANES:2 * LANES] = jnp.ones((p_all, LANES), BF16)

    def attend(qs, n_q, chunks):
        q = jnp.concatenate([qn_ref[0, pl.ds(qs, n_q), :], qr_ref[0, pl.ds(qs, n_q), :]], axis=1)
        m = jnp.full((n_q, 1), NEG_INF, F32)
        acc = jnp.zeros((n_q, 2 * LANES), F32)
        for c0, c1 in chunks:
            s = _dot_nt(q, kcat_ref[c0:c1, :])
            m_new = jnp.maximum(m, _row_reduce(jnp.maximum, jnp.max, [s]))
            p = jnp.exp2(s - m_new)
            acc = jnp.exp2(m - m_new) * acc + _dot(p.astype(BF16), vone_ref[c0:c1, :])
            m = m_new
        o_ref[0, pl.ds(qs, n_q), :] = (acc[:, :LANES] * (1.0 / acc[:, LANES:LANES + 1])).astype(o_ref.dtype)

    attend(0, lc, [(0, lc)])
    all_chunks = [(0, lc)] + [(c, c + MLA_TK) for c in range(lc, p_all, MLA_TK)]

    def tile(i, carry):
        attend(pl.multiple_of(lc + i * MLA_TQ, MLA_TQ), MLA_TQ, all_chunks)
        return carry

    lax.fori_loop(0, (p_all - lc) // MLA_TQ, tile, 0)


def latent_attention(q, kv, kr, ctx_len, n_heads):
    b, p, _ = q.shape
    assert (p - ctx_len) % MLA_TQ == 0 and (p - ctx_len) % MLA_TK == 0 and ctx_len % 16 == 0
    blk = lambda f: pl.BlockSpec((1, p, LANES), f)
    return pl.pallas_call(
        functools.partial(_mla_kernel, ctx_len=ctx_len),
        grid=(b, n_heads),
        in_specs=[blk(lambda b, h: (b, 0, h)), blk(lambda b, h: (b, 0, n_heads + h)),
                  blk(lambda b, h: (b, 0, 2 * h)), blk(lambda b, h: (b, 0, 0)),
                  blk(lambda b, h: (b, 0, 2 * h + 1))],
        out_specs=blk(lambda b, h: (b, 0, h)),
        out_shape=jax.ShapeDtypeStruct((b, p, n_heads * LANES), BF16),
        scratch_shapes=[pltpu.VMEM((p, 2 * LANES), BF16), pltpu.VMEM((p, 2 * LANES), BF16)],
        compiler_params=_cparams(("parallel", "parallel")),
        name="mla_attention",
    )(q, q, kv, kr, kv)


def gather_rows(src, idx):
    n = idx.shape[0]
    width = src.shape[1]
    win = GATHER_WINDOW
    info = plsc.get_sparse_core_info()
    n_workers = info.num_cores * info.num_subcores
    assert n % (n_workers * 2 * win) == 0
    per_worker = n // n_workers
    mesh = plsc.VectorSubcoreMesh(core_axis_name="core", subcore_axis_name="subcore")

    @functools.partial(
        pl.kernel, out_type=jax.ShapeDtypeStruct((n, width), src.dtype), mesh=mesh,
        scratch_types=[pltpu.VMEM((per_worker,), jnp.int32),
                       pltpu.VMEM((win, width), src.dtype), pltpu.VMEM((win, width), src.dtype),
                       pltpu.SemaphoreType.DMA, pltpu.SemaphoreType.DMA,
                       pltpu.SemaphoreType.DMA, pltpu.SemaphoreType.DMA],
        name="gather_rows")
    def gather(src_hbm, idx_hbm, out_hbm, idx_vmem, rows0, rows1, gsem0, gsem1, osem0, osem1):
        worker = lax.axis_index("subcore") * info.num_cores + lax.axis_index("core")
        base = worker * per_worker
        pltpu.sync_copy(idx_hbm.at[pl.ds(base, per_worker)], idx_vmem)

        def fetch(r, rows, sem):
            return pltpu.make_async_copy(src_hbm.at[idx_vmem.at[pl.ds(r, win)]], rows, sem)

        def flush(r, rows, sem):
            return pltpu.make_async_copy(rows, out_hbm.at[pl.ds(base + r, win)], sem)

        fetch(0, rows0, gsem0).start()

        @pl.loop(0, per_worker, step=2 * win)
        def _(r):
            fetch(r, rows0, gsem0).wait()

            @pl.when(r > 0)
            def _():
                flush(r - win, rows1, osem1).wait()

            fetch(r + win, rows1, gsem1).start()
            flush(r, rows0, osem0).start()
            fetch(r + win, rows1, gsem1).wait()
            flush(r, rows0, osem0).wait()

            @pl.when(r + 2 * win < per_worker)
            def _():
                fetch(r + 2 * win, rows0, gsem0).start()

            flush(r + win, rows1, osem1).start()

        flush(per_worker - win, rows1, osem1).wait()

    return gather(src, idx)


def _expert_kernel(blk_e_ref, n_used_ref, x_ref, wg_ref, wu_ref, wd_ref, y_ref, wg_bf, wu_bf, wd_bf):
    i = pl.program_id(0)
    used = i < n_used_ref[0]
    new_expert = (i == 0) | (blk_e_ref[i] != blk_e_ref[jnp.maximum(i - 1, 0)])

    @pl.when(used & new_expert)
    def _():
        wg_bf[...] = wg_ref[0].astype(BF16)
        wu_bf[...] = wu_ref[0].astype(BF16)
        wd_bf[...] = wd_ref[0].astype(BF16)

    @pl.when(used)
    def _():
        x = _unpack_bf16_pairs(x_ref[...]).astype(BF16)
        g = _dot(x, wg_bf[...])
        u = _dot(x, wu_bf[...])
        a = (g * jax.nn.sigmoid(g) * u).astype(BF16)
        y_ref[...] = _pack_bf16_pairs(_dot(a, wd_bf[...]))

    @pl.when(jnp.logical_not(used))
    def _():
        y_ref[...] = jnp.zeros_like(y_ref)


def expert_ffn(x_disp, blk_e, n_used, w_gate, w_up, w_down, layer):
    rows, half = x_disp.shape
    d = 2 * half
    n_blk = rows // MOE_BLOCK
    de = w_gate.shape[3]
    grid_spec = pltpu.PrefetchScalarGridSpec(
        num_scalar_prefetch=2,
        grid=(n_blk,),
        in_specs=[pl.BlockSpec((MOE_BLOCK, half), lambda i, be, nu: (i, 0)),
                  pl.BlockSpec((None, 1, d, de), lambda i, be, nu: (layer, be[i], 0, 0)),
                  pl.BlockSpec((None, 1, d, de), lambda i, be, nu: (layer, be[i], 0, 0)),
                  pl.BlockSpec((None, 1, de, d), lambda i, be, nu: (layer, be[i], 0, 0))],
        out_specs=pl.BlockSpec((MOE_BLOCK, half), lambda i, be, nu: (i, 0)),
        scratch_shapes=[pltpu.VMEM((d, de), BF16), pltpu.VMEM((d, de), BF16), pltpu.VMEM((de, d), BF16)],
    )
    return pl.pallas_call(
        _expert_kernel,
        grid_spec=grid_spec,
        out_shape=jax.ShapeDtypeStruct((rows, half), jnp.int32),
        compiler_params=_cparams(("arbitrary",)),
        name="expert_ffn",
    )(blk_e, n_used, x_disp, w_gate, w_up, w_down)


def _combine_kernel(h_ref, y0_ref, y1_ref, wt_ref, mod_ref, o_ref, *, gate_idx):
    wt = wt_ref[0]
    moe = wt[:, 0:1] * _unpack_bf16_pairs(y0_ref[0, 0]) + wt[:, 1:2] * _unpack_bf16_pairs(y1_ref[0, 0])
    o_ref[0] = h_ref[0] + mod_ref[0, gate_idx:gate_idx + 1, :] * moe


def combine(h, y_pairs, wt, modtab, gate_idx):
    b, p, d = h.shape
    row = pl.BlockSpec((1, ROW_TILE, d), lambda b, j: (b, j, 0))
    return pl.pallas_call(
        functools.partial(_combine_kernel, gate_idx=gate_idx),
        grid=(b, p // ROW_TILE),
        in_specs=[row,
                  pl.BlockSpec((1, 1, ROW_TILE, d // 2), lambda b, j: (0, b, j, 0)),
                  pl.BlockSpec((1, 1, ROW_TILE, d // 2), lambda b, j: (1, b, j, 0)),
                  pl.BlockSpec((1, ROW_TILE, LANES), lambda b, j: (b, j, 0)),
                  _mod_spec(d)],
        out_specs=row,
        out_shape=jax.ShapeDtypeStruct((b, p, d), F32),
        compiler_params=_cparams(("parallel", "parallel")),
        name="moe_combine",
    )(h, y_pairs, y_pairs, wt, modtab.reshape(b * 2, N_MOD, d))


def hier_moe(h, f, route, wt, counts, modtab, gate_idx, w_gate, w_up, w_down, layer):
    b, p, d = h.shape
    n_tok = b * p
    n_assign = 2 * n_tok
    eid_f = route[:, :, 0:2].reshape(n_assign)
    rank_f = route[:, :, 2:4].reshape(n_assign)
    cnt = counts[0, :N_EXPERTS].astype(jnp.int32)
    pcounts = (cnt + MOE_BLOCK - 1) // MOE_BLOCK * MOE_BLOCK
    pend = jnp.cumsum(pcounts)
    dest = ((pend - pcounts)[eid_f] + rank_f).astype(jnp.int32)
    n_blk = -(-n_assign // MOE_BLOCK) + N_EXPERTS
    rows_total = n_blk * MOE_BLOCK
    row_tok = jnp.zeros((rows_total,), jnp.int32).at[dest].set(jnp.arange(n_assign, dtype=jnp.int32) // 2)
    blk_e = jnp.minimum(jnp.searchsorted(pend, jnp.arange(n_blk, dtype=jnp.int32) * MOE_BLOCK, side='right'),
                        N_EXPERTS - 1).astype(jnp.int32)
    n_used = (pend[-1] // MOE_BLOCK).astype(jnp.int32).reshape(1)

    x_disp = gather_rows(f.reshape(n_tok, d // 2), row_tok)
    y = expert_ffn(x_disp, blk_e, n_used, w_gate, w_up, w_down, layer)
    dest_by_slot = jnp.concatenate([dest[0::2], dest[1::2]])
    y_pairs = gather_rows(y, dest_by_slot).reshape(2, b, p, d // 2)
    return combine(h, y_pairs, wt, modtab, gate_idx)


def kernel(x, c, ctx, c_ctx, mod_w, mod_b, norm_mix_g, norm_ffn_g, router_grp_w, router_grp_b, router_exp_w, router_exp_b, exp_w_gate, exp_w_up, exp_w_down, l0_na_w_qkv, l0_na_rpb, l0_na_w_o, l1_swa_w_qkv, l1_swa_sink, l1_swa_w_o, l2_mla_w_dq, l2_mla_q_norm_g, l2_mla_w_uq, l2_mla_w_dkv, l2_mla_kv_norm_g, l2_mla_w_ukv, l2_mla_w_o, l3_na_w_qkv, l3_na_rpb, l3_na_w_o, final_norm_g):
    b, s, d = x.shape
    lc = ctx.shape[1]
    n_heads = d // HEAD_DIM
    n_kv_heads = n_heads // 4
    depth = mod_w.shape[0]
    rows = s // GRID_W

    h = jnp.concatenate([ctx, x], axis=1)
    modtabs = modulation_tables(c, c_ctx, mod_w, mod_b)

    def scale_q_cols(w):
        n_q = n_heads * HEAD_DIM
        q_scale = HEAD_DIM ** -0.5 * LOG2E
        return jnp.concatenate([w[:, :n_q] * q_scale, w[:, n_q:]], axis=1).astype(BF16)

    def na_mixer(hm, w_qkv, rpb):
        qkv = project(hm, scale_q_cols(w_qkv))
        return neighbourhood_attention(qkv, na_bias_table(rpb, rows), lc, n_heads)

    def swa_mixer(hm):
        cos, sin = rope_tables_full(s, lc)
        n_rope = (n_heads + n_kv_heads) * HEAD_DIM // 512
        qkv = project(hm, scale_q_cols(l1_swa_w_qkv), tn=512, rope=(cos, sin, 0, n_rope))
        return window_attention(qkv, l1_swa_sink, lc, n_heads, n_kv_heads)

    def mla_mixer(hm):
        q_rank = l2_mla_w_dq.shape[1]
        kv_rank = l2_mla_kv_norm_g.shape[0]
        cos, sin = rope_tables_mla(s, lc)
        w_down = jnp.concatenate([l2_mla_w_dq, l2_mla_w_dkv[:, :kv_rank],
                                  _spread_rope_cols(l2_mla_w_dkv[:, kv_rank:])], axis=1).astype(BF16)
        cq, ckv, kr = mla_down(hm, w_down, l2_mla_q_norm_g, l2_mla_kv_norm_g, cos, sin, q_rank, kv_rank)
        w_uq = l2_mla_w_uq.reshape(q_rank, n_heads, MLA_NOPE_DIM + MLA_ROPE_DIM)
        w_q_nope = w_uq[:, :, :MLA_NOPE_DIM].reshape(q_rank, n_heads * MLA_NOPE_DIM)
        w_q_rope = jnp.concatenate([_spread_rope_cols(w_uq[:, hh, MLA_NOPE_DIM:]) for hh in range(n_heads)], axis=1)
        q_scale = (MLA_NOPE_DIM + MLA_ROPE_DIM) ** -0.5 * LOG2E
        w_q = (jnp.concatenate([w_q_nope, w_q_rope], axis=1) * q_scale).astype(BF16)
        n_nope_tiles = n_heads * MLA_NOPE_DIM // 512
        q = project(cq, w_q, tn=512, rope=(cos, sin, n_nope_tiles, 2 * n_nope_tiles))
        kv = project(ckv, l2_mla_w_ukv.astype(BF16), tn=512)
        return latent_attention(q, kv, kr, lc, n_heads)

    for i in range(depth):
        modtab = modtabs[i]
        hm = norm_modulate(h, norm_mix_g[i], modtab, 0)
        mixer = i % 3
        if mixer == 0:
            w_qkv, rpb, w_o = (l0_na_w_qkv, l0_na_rpb, l0_na_w_o) if i == 0 else (l3_na_w_qkv, l3_na_rpb, l3_na_w_o)
            y = na_mixer(hm, w_qkv, rpb)
        elif mixer == 1:
            y, w_o = swa_mixer(hm), l1_swa_w_o
        else:
            y, w_o = mla_mixer(hm), l2_mla_w_o
        h = project(y, w_o.astype(BF16), tn=512, resid=(h, modtab, 2), ctx_len=lc)
        f, route, wt, counts = norm_modulate_route(h, norm_ffn_g[i], modtab, 3, router_grp_w[i], router_grp_b[i],
                                                   router_exp_w[i], router_exp_b[i])
        h = hier_moe(h, f, route, wt, counts, modtab, 5, exp_w_gate, exp_w_up, exp_w_down, i)
    return final_norm(h, final_norm_g, lc)
```

```python
import functools

import numpy as np
import jax
import jax.numpy as jnp
from jax import lax
from jax.experimental import pallas as pl
from jax.experimental.pallas import tpu as pltpu
from jax.experimental.pallas import tpu_sc as plsc

GRID_W = 64
HEAD_DIM = 128
ROPE_BASE = 10000.0
NORM_EPS = 1e-6
NEG_INF = -1e30
N_MOD = 6

NA_ROWS = 8
NA_COLS = 16
NA_TILE_ROWS = 4
NA_WIN_ROWS = 12

SWA_WINDOW = 128
SWA_TQ = 256
SWA_TK = 512

MLA_NOPE_DIM = 128
MLA_ROPE_DIM = 64
MLA_TQ = 512
MLA_TK = 512
LOG2E = 1.4426950408889634

N_GROUPS = 4
EXPERTS_PER_GROUP = 8
N_EXPERTS = N_GROUPS * EXPERTS_PER_GROUP
MOE_BLOCK = 512
GATHER_WINDOW = 32

LANES = 128
ROW_TILE = 256
VMEM_LIMIT = 56 * 1024 * 1024

BF16 = jnp.bfloat16
F32 = jnp.float32


def _cparams(sem):
    return pltpu.CompilerParams(dimension_semantics=sem, vmem_limit_bytes=VMEM_LIMIT)


def _dot(a, b):
    return jnp.dot(a, b, preferred_element_type=F32)


def _dot_nt(a, b):
    return lax.dot_general(a, b, (((1,), (1,)), ((), ())), preferred_element_type=F32)


def _mod_kernel(x_ref, w_ref, b_ref, o_ref):
    x = x_ref[...]
    sx = (x * jax.nn.sigmoid(x)).astype(BF16)
    o_ref[0] = _dot(sx, w_ref[0].astype(BF16)) + b_ref[0]


def modulation_tables(c, c_ctx, mod_w, mod_b):
    depth, d, n_out = mod_w.shape
    b = c.shape[0]
    rows = 16
    xin = jnp.zeros((rows, d), F32).at[:b].set(c).at[b].set(c_ctx)
    tn = 1024
    out = pl.pallas_call(
        _mod_kernel,
        grid=(depth, n_out // tn),
        in_specs=[pl.BlockSpec((rows, d), lambda i, j: (0, 0)),
                  pl.BlockSpec((1, d, tn), lambda i, j: (i, 0, j)),
                  pl.BlockSpec((1, 1, tn), lambda i, j: (i, 0, j))],
        out_specs=pl.BlockSpec((1, rows, tn), lambda i, j: (i, 0, j)),
        out_shape=jax.ShapeDtypeStruct((depth, rows, n_out), F32),
        compiler_params=_cparams(("parallel", "parallel")),
        name="adaln_mod",
    )(xin, mod_w, mod_b.reshape(depth, 1, n_out))
    lat = out[:, :b].reshape(depth, b, 1, N_MOD, d)
    ctx = jnp.broadcast_to(out[:, b].reshape(depth, 1, 1, N_MOD, d), (depth, b, 1, N_MOD, d))
    return jnp.concatenate([ctx, lat], axis=2)


def _rms_mod(x, g, shift, scale):
    ms = jnp.mean(x * x, axis=-1, keepdims=True)
    y = x * lax.rsqrt(ms + NORM_EPS) * g
    return y * (1.0 + scale) + shift


def _norm_mod_kernel(h_ref, g_ref, mod_ref, o_ref, *, shift_idx):
    f = _rms_mod(h_ref[0], g_ref[...], mod_ref[0, shift_idx:shift_idx + 1, :],
                 mod_ref[0, shift_idx + 1:shift_idx + 2, :])
    o_ref[0] = f.astype(o_ref.dtype)


def _route(logits):
    lane = lax.broadcasted_iota(jnp.int32, logits.shape, 1).astype(F32)
    big = float(LANES)

    def first_lane(mask):
        return jnp.min(jnp.where(mask, lane, big), axis=-1, keepdims=True)

    in_grp = lane < N_GROUPS
    lg = jnp.where(in_grp, logits, NEG_INF)
    m_g = jnp.max(lg, axis=-1, keepdims=True)
    g_idx = first_lane(in_grp & (lg == m_g))
    g_w = 1.0 / jnp.sum(jnp.where(in_grp, jnp.exp(lg - m_g), 0.0), axis=-1, keepdims=True)
    e_lo = N_GROUPS + g_idx * EXPERTS_PER_GROUP
    in_e = (lane >= e_lo) & (lane < e_lo + EXPERTS_PER_GROUP)
    le = jnp.where(in_e, logits, NEG_INF)
    m1 = jnp.max(le, axis=-1, keepdims=True)
    e1 = first_lane(in_e & (le == m1))
    s_e = jnp.sum(jnp.where(in_e, jnp.exp(le - m1), 0.0), axis=-1, keepdims=True)
    in_e2 = in_e & (lane != e1)
    le2 = jnp.where(in_e2, logits, NEG_INF)
    m2 = jnp.max(le2, axis=-1, keepdims=True)
    e2 = first_lane(in_e2 & (le2 == m2))
    p1 = 1.0 / s_e
    p2 = jnp.exp(m2 - m1) / s_e
    den = p1 + p2
    return ((e1 - N_GROUPS).astype(jnp.int32), (e2 - N_GROUPS).astype(jnp.int32),
            g_w * p1 / den, g_w * p2 / den)


def _pack_bf16_pairs(x):
    n = x.shape[1] // 2
    xb = x.astype(BF16).astype(F32)
    hi = lax.bitcast_convert_type(xb[:, :n], jnp.int32)
    lo = lax.bitcast_convert_type(xb[:, n:], jnp.int32)
    return (hi & jnp.int32(-65536)) | lax.shift_right_logical(lo, jnp.int32(16))


def _unpack_bf16_pairs(w):
    hi = lax.bitcast_convert_type(w & jnp.int32(-65536), F32)
    lo = lax.bitcast_convert_type(lax.shift_left(w, jnp.int32(16)), F32)
    return jnp.concatenate([hi, lo], axis=1)


def _norm_mod_route_kernel(h_ref, g_ref, mod_ref, wr_ref, br_ref, f_ref, eid_ref, wt_ref, cnt_ref, run_ref, *,
                           shift_idx):
    first = (pl.program_id(0) == 0) & (pl.program_id(1) == 0)

    @pl.when(first)
    def _():
        run_ref[...] = jnp.zeros_like(run_ref)

    f = _rms_mod(h_ref[0], g_ref[...], mod_ref[0, shift_idx:shift_idx + 1, :],
                 mod_ref[0, shift_idx + 1:shift_idx + 2, :])
    f_ref[0] = _pack_bf16_pairs(f)
    f_hi = f.astype(BF16)
    f_lo = (f - f_hi.astype(F32)).astype(BF16)
    logits = (_dot(f_hi, wr_ref[0]) + _dot(f_lo, wr_ref[0]) + _dot(f_hi, wr_ref[1])) + br_ref[...]
    e1, e2, w1, w2 = _route(logits)
    lane = lax.broadcasted_iota(jnp.int32, logits.shape, 1)
    pick1, pick2 = lane == e1, lane == e2
    chosen = (pick1 | pick2).astype(F32)
    n_rows = chosen.shape[0]
    earlier = (lax.broadcasted_iota(jnp.int32, (n_rows, n_rows), 1)
               < lax.broadcasted_iota(jnp.int32, (n_rows, n_rows), 0)).astype(BF16)
    before = run_ref[...] + _dot(earlier, chosen.astype(BF16))
    rank1 = jnp.sum(jnp.where(pick1, before, 0.0), axis=-1, keepdims=True).astype(jnp.int32)
    rank2 = jnp.sum(jnp.where(pick2, before, 0.0), axis=-1, keepdims=True).astype(jnp.int32)
    run_ref[...] = run_ref[...] + jnp.sum(chosen, axis=0, keepdims=True)
    cnt_ref[...] = run_ref[...]
    eid_ref[0] = jnp.where(lane == 0, e1, jnp.where(lane == 1, e2, jnp.where(lane == 2, rank1,
                                                                              jnp.where(lane == 3, rank2, 0))))
    wt_ref[0] = jnp.where(lane == 0, w1, jnp.where(lane == 1, w2, 0.0))


def _mod_spec(d):
    return pl.BlockSpec((1, N_MOD, d), lambda b, j: (2 * b + jnp.minimum(j, 1), 0, 0))


def norm_modulate(h, g, modtab, shift_idx):
    b, p, d = h.shape
    row = pl.BlockSpec((1, ROW_TILE, d), lambda b, j: (b, j, 0))
    return pl.pallas_call(
        functools.partial(_norm_mod_kernel, shift_idx=shift_idx),
        grid=(b, p // ROW_TILE),
        in_specs=[row, pl.BlockSpec((1, d), lambda b, j: (0, 0)), _mod_spec(d)],
        out_specs=row,
        out_shape=jax.ShapeDtypeStruct((b, p, d), BF16),
        compiler_params=_cparams(("parallel", "parallel")),
        name="norm_mod",
    )(h, g.reshape(1, d), modtab.reshape(b * 2, N_MOD, d))


def norm_modulate_route(h, g, modtab, shift_idx, w_grp, b_grp, w_rt, b_rt):
    b, p, d = h.shape
    n_r = N_GROUPS + N_EXPERTS
    wr = jnp.zeros((d, LANES), F32).at[:, :N_GROUPS].set(w_grp).at[:, N_GROUPS:n_r].set(w_rt)
    br = jnp.zeros((1, LANES), F32).at[0, :N_GROUPS].set(b_grp).at[0, N_GROUPS:n_r].set(b_rt)
    wr_hi = wr.astype(BF16)
    wr = jnp.stack([wr_hi, (wr - wr_hi.astype(F32)).astype(BF16)])
    row = pl.BlockSpec((1, ROW_TILE, d), lambda b, j: (b, j, 0))
    packed = pl.BlockSpec((1, ROW_TILE, d // 2), lambda b, j: (b, j, 0))
    small = pl.BlockSpec((1, ROW_TILE, LANES), lambda b, j: (b, j, 0))
    return pl.pallas_call(
        functools.partial(_norm_mod_route_kernel, shift_idx=shift_idx),
        grid=(b, p // ROW_TILE),
        in_specs=[row, pl.BlockSpec((1, d), lambda b, j: (0, 0)), _mod_spec(d),
                  pl.BlockSpec((2, d, LANES), lambda b, j: (0, 0, 0)),
                  pl.BlockSpec((1, LANES), lambda b, j: (0, 0))],
        out_specs=[packed, small, small, pl.BlockSpec((1, LANES), lambda b, j: (0, 0))],
        out_shape=[jax.ShapeDtypeStruct((b, p, d // 2), jnp.int32),
                   jax.ShapeDtypeStruct((b, p, LANES), jnp.int32),
                   jax.ShapeDtypeStruct((b, p, LANES), F32),
                   jax.ShapeDtypeStruct((1, LANES), F32)],
        scratch_shapes=[pltpu.VMEM((1, LANES), F32)],
        compiler_params=_cparams(("arbitrary", "arbitrary")),
        name="norm_mod_route",
    )(h, g.reshape(1, d), modtab.reshape(b * 2, N_MOD, d), wr, br)


def _final_norm_kernel(h_ref, g_ref, o_ref):
    x = h_ref[0]
    ms = jnp.mean(x * x, axis=-1, keepdims=True)
    o_ref[0] = x * lax.rsqrt(ms + NORM_EPS) * g_ref[...]


def final_norm(h, g, ctx_len):
    b, p, d = h.shape
    skip = ctx_len // ROW_TILE
    return pl.pallas_call(
        _final_norm_kernel,
        grid=(b, (p - ctx_len) // ROW_TILE),
        in_specs=[pl.BlockSpec((1, ROW_TILE, d), lambda b, j: (b, j + skip, 0)),
                  pl.BlockSpec((1, d), lambda b, j: (0, 0))],
        out_specs=pl.BlockSpec((1, ROW_TILE, d), lambda b, j: (b, j, 0)),
        out_shape=jax.ShapeDtypeStruct((b, p - ctx_len, d), F32),
        compiler_params=_cparams(("parallel", "parallel")),
        name="final_norm",
    )(h, g.reshape(1, d))


def _rope(acc, cos, sin):
    n_blk = acc.shape[1] // LANES
    outs = []
    for c in range(n_blk):
        x = acc[:, c * LANES:(c + 1) * LANES]
        outs.append(x * cos + pltpu.roll(x, LANES // 2, 1) * sin)
    return outs[0] if n_blk == 1 else jnp.concatenate(outs, axis=1)


def _proj_kernel(*refs, rope_lo, rope_hi, resid, gate_idx, ctx_len, tm):
    x_ref, w_ref = refs[0], refs[1]
    o_ref = refs[-1]
    acc = _dot(x_ref[0], w_ref[...])
    if resid:
        res_ref, mod_ref = refs[2], refs[3]
        row = pl.program_id(1) * tm + lax.broadcasted_iota(jnp.int32, (tm, 1), 0)
        gate = jnp.where(row < ctx_len, mod_ref[0, 0, gate_idx:gate_idx + 1, :],
                         mod_ref[0, 1, gate_idx:gate_idx + 1, :])
        o_ref[0] = res_ref[0] + gate * acc
    elif rope_hi > rope_lo:
        cos_ref, sin_ref = refs[2], refs[3]
        j = pl.program_id(2)
        roped = (j >= rope_lo) & (j < rope_hi)

        @pl.when(roped)
        def _():
            o_ref[0] = _rope(acc, cos_ref[...], sin_ref[...]).astype(o_ref.dtype)

        @pl.when(jnp.logical_not(roped))
        def _():
            o_ref[0] = acc.astype(o_ref.dtype)
    else:
        o_ref[0] = acc.astype(o_ref.dtype)


def _row_tile(p):
    for cand in (1088, 1024, 544, 512, 272, 256, 128, 64, 32, 16):
        if p % cand == 0:
            return cand
    raise ValueError(p)


def project(x, w, *, tn=512, out_dtype=BF16, rope=None, resid=None, ctx_len=0):
    b, p, k = x.shape
    n = w.shape[1]
    tm = _row_tile(p)
    tn = min(tn, n)
    assert n % tn == 0
    in_specs = [pl.BlockSpec((1, tm, k), lambda b, i, j: (b, i, 0)),
                pl.BlockSpec((k, tn), lambda b, i, j: (0, j))]
    args = [x, w]
    kw = dict(rope_lo=0, rope_hi=0, resid=False, gate_idx=0, ctx_len=ctx_len, tm=tm)
    if resid is not None:
        h, modtab, gate_idx = resid
        in_specs += [pl.BlockSpec((1, tm, tn), lambda b, i, j: (b, i, j)),
                     pl.BlockSpec((1, 2, N_MOD, tn), lambda b, i, j: (b, 0, 0, j))]
        args += [h, modtab]
        kw.update(resid=True, gate_idx=gate_idx)
        out_dtype = F32
    elif rope is not None:
        cos, sin, lo, hi = rope
        in_specs += [pl.BlockSpec((tm, LANES), lambda b, i, j: (i, 0)),
                     pl.BlockSpec((tm, LANES), lambda b, i, j: (i, 0))]
        args += [cos, sin]
        kw.update(rope_lo=lo, rope_hi=hi)
    return pl.pallas_call(
        functools.partial(_proj_kernel, **kw),
        grid=(b, p // tm, n // tn),
        in_specs=in_specs,
        out_specs=pl.BlockSpec((1, tm, tn), lambda b, i, j: (b, i, j)),
        out_shape=jax.ShapeDtypeStruct((b, p, n), out_dtype),
        compiler_params=_cparams(("parallel", "parallel", "arbitrary")),
        name="project",
    )(*args)


def _axial_cos_sin(n, rot_dim):
    t = jnp.arange(n, dtype=jnp.int32)
    row = (t // GRID_W).astype(F32)
    col = (t % GRID_W).astype(F32)
    n_freq = rot_dim // 4
    inv = ROPE_BASE ** (-jnp.arange(n_freq, dtype=F32) / n_freq)
    ang = jnp.concatenate([row[:, None] * inv, col[:, None] * inv], axis=-1)
    return jnp.cos(ang), jnp.sin(ang)


def rope_tables_full(s, ctx_len):
    c, sn = _axial_cos_sin(s, HEAD_DIM)
    cos = jnp.concatenate([c, c], axis=1)
    sin = jnp.concatenate([-sn, sn], axis=1)
    ident_c = jnp.ones((ctx_len, LANES), F32)
    ident_s = jnp.zeros((ctx_len, LANES), F32)
    return jnp.concatenate([ident_c, cos], axis=0), jnp.concatenate([ident_s, sin], axis=0)


def rope_tables_mla(s, ctx_len):
    c, sn = _axial_cos_sin(s, MLA_ROPE_DIM)
    one = jnp.ones_like(c)
    zero = jnp.zeros_like(c)
    cos = jnp.concatenate([c, one, c, one], axis=1)
    sin = jnp.concatenate([-sn, zero, sn, zero], axis=1)
    ident_c = jnp.ones((ctx_len, LANES), F32)
    ident_s = jnp.zeros((ctx_len, LANES), F32)
    return jnp.concatenate([ident_c, cos], axis=0), jnp.concatenate([ident_s, sin], axis=0)


def _spread_rope_cols(w_rope):
    k = w_rope.shape[0]
    half = MLA_ROPE_DIM // 2
    z = jnp.zeros((k, half), w_rope.dtype)
    return jnp.concatenate([w_rope[:, :half], z, w_rope[:, half:], z], axis=1)


def _softmax_parts(parts, extra=None):
    m = _row_reduce(jnp.maximum, jnp.max, parts)
    if extra is not None:
        m = jnp.maximum(m, extra)
    ps = [jnp.exp2(s - m) for s in parts]
    den = _row_reduce(jnp.add, jnp.sum, ps)
    if extra is not None:
        den = den + jnp.exp2(extra - m)
    return ps, 1.0 / den


def _row_reduce(combine, reduce, parts):
    blocks = [s[:, c:c + LANES] for s in parts for c in range(0, s.shape[1], LANES)]
    acc = blocks[0]
    for blk in blocks[1:]:
        acc = combine(acc, blk)
    return reduce(acc, axis=-1, keepdims=True)


def na_bias_table(rpb, rows):
    n_tiles = rows // NA_TILE_ROWS
    n_heads, _, n_dcol = rpb.shape
    drow, row_ok = [], []
    for tile in (0, 1, n_tiles - 1):
        kr0 = int(np.clip(NA_TILE_ROWS * tile - NA_ROWS // 2, 0, rows - NA_WIN_ROWS))
        r = NA_TILE_ROWS * tile + np.arange(NA_TILE_ROWS)
        r0 = np.clip(r - NA_ROWS // 2, 0, rows - NA_ROWS)
        krow = kr0 + np.arange(NA_WIN_ROWS)
        row_ok.append((krow[None, :] >= r0[:, None]) & (krow[None, :] < r0[:, None] + NA_ROWS))
        drow.append(np.clip(krow[None, :] - r[:, None] + NA_ROWS - 1, 0, 2 * NA_ROWS - 2))
    drow, row_ok = np.stack(drow), np.stack(row_ok)
    qc = np.arange(GRID_W)
    qcol0 = np.clip(qc - NA_COLS // 2, 0, GRID_W - NA_COLS)
    kc = np.arange(GRID_W)
    col_ok = (kc[None, :] >= qcol0[:, None]) & (kc[None, :] < qcol0[:, None] + NA_COLS)
    dcol = np.clip(kc[None, :] - qc[:, None] + NA_COLS - 1, 0, 2 * NA_COLS - 2)
    by_row = rpb.astype(F32)[:, drow, :]
    pick_col = jnp.asarray(dcol[None] == np.arange(n_dcol)[:, None, None], F32)
    vals = jnp.einsum('hpamd,dqk->hpaqmk', by_row, pick_col, precision=lax.Precision.HIGHEST)
    ok = row_ok[None, :, :, None, :, None] & col_ok[None, None, None, :, None, :]
    vals = jnp.where(jnp.asarray(ok), vals * LOG2E, NEG_INF)
    return vals.reshape(n_heads, 3, NA_TILE_ROWS * GRID_W, NA_WIN_ROWS * GRID_W)


def _na_kernel(q_ref, k_ref, v_ref, bias_ref, o_ref, *, ctx_len, rows):
    tq = NA_TILE_ROWS * GRID_W
    tk = NA_WIN_ROWS * GRID_W
    n_tiles = rows // NA_TILE_ROWS
    lc = ctx_len

    s = _dot_nt(q_ref[0, 0:lc, :], k_ref[0, 0:lc, :])
    (p,), inv = _softmax_parts([s])
    o_ref[0, 0:lc, :] = (_dot(p.astype(BF16), v_ref[0, 0:lc, :]) * inv).astype(o_ref.dtype)

    def tile(i, carry):
        qs = pl.multiple_of(lc + i * tq, tq)
        kr0 = jnp.clip(NA_TILE_ROWS * i - NA_ROWS // 2, 0, rows - NA_WIN_ROWS)
        ks = pl.multiple_of(lc + kr0 * GRID_W, NA_TILE_ROWS * GRID_W)
        pat = jnp.where(i == 0, 0, jnp.where(i == n_tiles - 1, 2, 1))
        q = q_ref[0, pl.ds(qs, tq), :]
        s_loc = _dot_nt(q, k_ref[0, pl.ds(ks, tk), :]) + bias_ref[0, pat]
        s_ctx = _dot_nt(q, k_ref[0, 0:lc, :])
        (p_loc, p_ctx), inv = _softmax_parts([s_loc, s_ctx])
        o = _dot(p_loc.astype(BF16), v_ref[0, pl.ds(ks, tk), :]) + _dot(p_ctx.astype(BF16), v_ref[0, 0:lc, :])
        o_ref[0, pl.ds(qs, tq), :] = (o * inv).astype(o_ref.dtype)
        return carry

    lax.fori_loop(0, n_tiles, tile, 0, unroll=2)


def neighbourhood_attention(qkv, bias, ctx_len, n_heads):
    b, p, _ = qkv.shape
    rows = (p - ctx_len) // GRID_W
    assert rows % NA_TILE_ROWS == 0 and rows >= NA_WIN_ROWS
    tq, tk = NA_TILE_ROWS * GRID_W, NA_WIN_ROWS * GRID_W
    assert ctx_len % 16 == 0 and ctx_len % tq == 0
    blk = lambda off: pl.BlockSpec((1, p, HEAD_DIM), lambda h, b: (b, 0, off + h))
    return pl.pallas_call(
        functools.partial(_na_kernel, ctx_len=ctx_len, rows=rows),
        grid=(n_heads, b),
        in_specs=[blk(0), blk(n_heads), blk(2 * n_heads),
                  pl.BlockSpec((1, 3, tq, tk), lambda h, b: (h, 0, 0, 0))],
        out_specs=blk(0),
        out_shape=jax.ShapeDtypeStruct((b, p, n_heads * HEAD_DIM), BF16),
        compiler_params=_cparams(("parallel", "parallel")),
        name="na_attention",
    )(qkv, qkv, qkv, bias)


def _swa_kernel(sink_ref, q_ref, k_ref, v_ref, o_ref, *, ctx_len, seq, group):
    lc = ctx_len
    kvh = pl.program_id(1)

    def stack_heads(q):
        return jnp.concatenate([q[:, g * HEAD_DIM:(g + 1) * HEAD_DIM] for g in range(group)], axis=0)

    def finish(parts_fn, n_q, pv_fn, store):
        ps_all, invs = [], []
        for g in range(group):
            sink = sink_ref[kvh * group + g]
            ps, inv = _softmax_parts(parts_fn(g), extra=sink)
            ps_all.append(ps)
            invs.append(inv)
        n_parts = len(ps_all[0])
        stacked = [jnp.concatenate([ps_all[g][k] for g in range(group)], axis=0).astype(BF16)
                   for k in range(n_parts)]
        o = pv_fn(stacked)
        store(jnp.concatenate([o[g * n_q:(g + 1) * n_q] * invs[g] for g in range(group)], axis=1))

    s_c = _dot_nt(stack_heads(q_ref[0, 0:lc, :]), k_ref[0, 0:lc, :])

    def store_ctx(o):
        o_ref[0, 0:lc, :] = o.astype(o_ref.dtype)

    finish(lambda g: [s_c[g * lc:(g + 1) * lc]], lc,
           lambda st: _dot(st[0], v_ref[0, 0:lc, :]), store_ctx)

    n_tiles = seq // SWA_TQ

    def tile(t, carry):
        q0 = t * SWA_TQ
        k0 = jnp.clip(q0 - SWA_WINDOW, 0, seq - SWA_TK)
        qs = pl.multiple_of(lc + q0, SWA_WINDOW)
        ks = pl.multiple_of(lc + k0, SWA_WINDOW)
        q4 = stack_heads(q_ref[0, pl.ds(qs, SWA_TQ), :])
        s_loc = _dot_nt(q4, k_ref[0, pl.ds(ks, SWA_TK), :])
        s_ctx = _dot_nt(q4, k_ref[0, 0:lc, :])
        dpos = (lax.broadcasted_iota(jnp.int32, (SWA_TQ, SWA_TK), 1)
                - lax.broadcasted_iota(jnp.int32, (SWA_TQ, SWA_TK), 0)) + (k0 - q0)
        valid = jnp.abs(dpos) <= SWA_WINDOW

        def parts(g):
            sl = slice(g * SWA_TQ, (g + 1) * SWA_TQ)
            return [jnp.where(valid, s_loc[sl], NEG_INF), s_ctx[sl]]

        def store(o):
            o_ref[0, pl.ds(qs, SWA_TQ), :] = o.astype(o_ref.dtype)

        finish(parts, SWA_TQ,
               lambda st: _dot(st[0], v_ref[0, pl.ds(ks, SWA_TK), :]) + _dot(st[1], v_ref[0, 0:lc, :]),
               store)
        return carry

    lax.fori_loop(0, n_tiles, tile, 0, unroll=2)


def window_attention(qkv, sink, ctx_len, n_heads, n_kv_heads):
    b, p, _ = qkv.shape
    seq = p - ctx_len
    group = n_heads // n_kv_heads
    assert seq % SWA_TQ == 0 and seq >= SWA_TK and ctx_len % SWA_WINDOW == 0
    kv = lambda off: pl.BlockSpec((1, p, HEAD_DIM), lambda b, h: (b, 0, off + h))
    qo = pl.BlockSpec((1, p, group * HEAD_DIM), lambda b, h: (b, 0, h))
    return pl.pallas_call(
        functools.partial(_swa_kernel, ctx_len=ctx_len, seq=seq, group=group),
        grid=(b, n_kv_heads),
        in_specs=[pl.BlockSpec(memory_space=pltpu.SMEM), qo, kv(n_heads), kv(n_heads + n_kv_heads)],
        out_specs=qo,
        out_shape=jax.ShapeDtypeStruct((b, p, n_heads * HEAD_DIM), BF16),
        compiler_params=_cparams(("parallel", "parallel")),
        name="swa_attention",
    )(sink.astype(F32) * LOG2E, qkv, qkv, qkv)


def _mla_down_kernel(x_ref, w_ref, gq_ref, gkv_ref, cos_ref, sin_ref, cq_ref, ckv_ref, kr_ref, *, q_rank, kv_rank):
    acc = _dot(x_ref[0], w_ref[...])

    def rms(x, g):
        ms = jnp.mean(x * x, axis=-1, keepdims=True)
        return x * lax.rsqrt(ms + NORM_EPS) * g

    cq_ref[0] = rms(acc[:, :q_rank], gq_ref[...]).astype(cq_ref.dtype)
    ckv_ref[0] = rms(acc[:, q_rank:q_rank + kv_rank], gkv_ref[...]).astype(ckv_ref.dtype)
    kr_ref[0] = _rope(acc[:, q_rank + kv_rank:], cos_ref[...], sin_ref[...]).astype(kr_ref.dtype)


def mla_down(x, w_down, gq, gkv, cos, sin, q_rank, kv_rank):
    b, p, k = x.shape
    n = w_down.shape[1]
    tm = 544 if p % 544 == 0 else _row_tile(p)
    row = lambda width: pl.BlockSpec((1, tm, width), lambda b, i: (b, i, 0))
    return pl.pallas_call(
        functools.partial(_mla_down_kernel, q_rank=q_rank, kv_rank=kv_rank),
        grid=(b, p // tm),
        in_specs=[row(k), pl.BlockSpec((k, n), lambda b, i: (0, 0)),
                  pl.BlockSpec((1, q_rank), lambda b, i: (0, 0)),
                  pl.BlockSpec((1, kv_rank), lambda b, i: (0, 0)),
                  pl.BlockSpec((tm, LANES), lambda b, i: (i, 0)),
                  pl.BlockSpec((tm, LANES), lambda b, i: (i, 0))],
        out_specs=[row(q_rank), row(kv_rank), row(LANES)],
        out_shape=[jax.ShapeDtypeStruct((b, p, q_rank), BF16),
                   jax.ShapeDtypeStruct((b, p, kv_rank), BF16),
                   jax.ShapeDtypeStruct((b, p, LANES), BF16)],
        compiler_params=_cparams(("parallel", "parallel")),
        name="mla_down",
    )(x, w_down, gq.reshape(1, q_rank), gkv.reshape(1, kv_rank), cos, sin)


def _mla_kernel(qn_ref, qr_ref, kn_ref, kr_ref, v_ref, o_ref, kcat_ref, vone_ref, *, ctx_len):
    lc = ctx_len
    p_all = kcat_ref.shape[0]
    kcat_ref[:, 0:LANES] = kn_ref[0]
    kcat_ref[:, LANES:2 * LANES] = kr_ref[0]
    vone_ref[:, 0:LANES] = v_ref[0]
    vone_ref[:, LANES:2 * LANES] = jnp.ones((p_all, LANES), BF16)

    def attend(qs, n_q, chunks):
        q = jnp.concatenate([qn_ref[0, pl.ds(qs, n_q), :], qr_ref[0, pl.ds(qs, n_q), :]], axis=1)
        m = jnp.full((n_q, 1), NEG_INF, F32)
        acc = jnp.zeros((n_q, 2 * LANES), F32)
        for c0, c1 in chunks:
            s = _dot_nt(q, kcat_ref[c0:c1, :])
            m_new = jnp.maximum(m, _row_reduce(jnp.maximum, jnp.max, [s]))
            p = jnp.exp2(s - m_new)
            acc = jnp.exp2(m - m_new) * acc + _dot(p.astype(BF16), vone_ref[c0:c1, :])
            m = m_new
        o_ref[0, pl.ds(qs, n_q), :] = (acc[:, :LANES] * (1.0 / acc[:, LANES:LANES + 1])).astype(o_ref.dtype)

    attend(0, lc, [(0, lc)])
    all_chunks = [(0, lc)] + [(c, c + MLA_TK) for c in range(lc, p_all, MLA_TK)]

    def tile(i, carry):
        attend(pl.multiple_of(lc + i * MLA_TQ, MLA_TQ), MLA_TQ, all_chunks)
        return carry

    lax.fori_loop(0, (p_all - lc) // MLA_TQ, tile, 0)


def latent_attention(q, kv, kr, ctx_len, n_heads):
    b, p, _ = q.shape
    assert (p - ctx_len) % MLA_TQ == 0 and (p - ctx_len) % MLA_TK == 0 and ctx_len % 16 == 0
    blk = lambda f: pl.BlockSpec((1, p, LANES), f)
    return pl.pallas_call(
        functools.partial(_mla_kernel, ctx_len=ctx_len),
        grid=(b, n_heads),
        in_specs=[blk(lambda b, h: (b, 0, h)), blk(lambda b, h: (b, 0, n_heads + h)),
                  blk(lambda b, h: (b, 0, 2 * h)), blk(lambda b, h: (b, 0, 0)),
                  blk(lambda b, h: (b, 0, 2 * h + 1))],
        out_specs=blk(lambda b, h: (b, 0, h)),
        out_shape=jax.ShapeDtypeStruct((b, p, n_heads * LANES), BF16),
        scratch_shapes=[pltpu.VMEM((p, 2 * LANES), BF16), pltpu.VMEM((p, 2 * LANES), BF16)],
        compiler_params=_cparams(("parallel", "parallel")),
        name="mla_attention",
    )(q, q, kv, kr, kv)


def gather_rows(src, idx):
    n = idx.shape[0]
    width = src.shape[1]
    win = GATHER_WINDOW
    info = plsc.get_sparse_core_info()
    n_workers = info.num_cores * info.num_subcores
    assert n % (n_workers * 2 * win) == 0
    per_worker = n // n_workers
    mesh = plsc.VectorSubcoreMesh(core_axis_name="core", subcore_axis_name="subcore")

    @functools.partial(
        pl.kernel, out_type=jax.ShapeDtypeStruct((n, width), src.dtype), mesh=mesh,
        scratch_types=[pltpu.VMEM((per_worker,), jnp.int32),
                       pltpu.VMEM((win, width), src.dtype), pltpu.VMEM((win, width), src.dtype),
                       pltpu.SemaphoreType.DMA, pltpu.SemaphoreType.DMA,
                       pltpu.SemaphoreType.DMA, pltpu.SemaphoreType.DMA],
        name="gather_rows")
    def gather(src_hbm, idx_hbm, out_hbm, idx_vmem, rows0, rows1, gsem0, gsem1, osem0, osem1):
        worker = lax.axis_index("subcore") * info.num_cores + lax.axis_index("core")
        base = worker * per_worker
        pltpu.sync_copy(idx_hbm.at[pl.ds(base, per_worker)], idx_vmem)

        def fetch(r, rows, sem):
            return pltpu.make_async_copy(src_hbm.at[idx_vmem.at[pl.ds(r, win)]], rows, sem)

        def flush(r, rows, sem):
            return pltpu.make_async_copy(rows, out_hbm.at[pl.ds(base + r, win)], sem)

        fetch(0, rows0, gsem0).start()

        @pl.loop(0, per_worker, step=2 * win)
        def _(r):
            fetch(r, rows0, gsem0).wait()

            @pl.when(r > 0)
            def _():
                flush(r - win, rows1, osem1).wait()

            fetch(r + win, rows1, gsem1).start()
            flush(r, rows0, osem0).start()
            fetch(r + win, rows1, gsem1).wait()
            flush(r, rows0, osem0).wait()

            @pl.when(r + 2 * win < per_worker)
            def _():
                fetch(r + 2 * win, rows0, gsem0).start()

            flush(r + win, rows1, osem1).start()

        flush(per_worker - win, rows1, osem1).wait()

    return gather(src, idx)


def scatter_rows(src, idx, n_out):
    n = idx.shape[0]
    n_src, width = src.shape
    win = GATHER_WINDOW
    info = plsc.get_sparse_core_info()
    n_workers = info.num_cores * info.num_subcores
    assert n % (n_workers * 2 * win) == 0
    per_worker = n // n_workers
    assert n_src % per_worker == 0
    n_steps = per_worker // win
    mesh = plsc.VectorSubcoreMesh(core_axis_name="core", subcore_axis_name="subcore")

    @functools.partial(
        pl.kernel, out_type=jax.ShapeDtypeStruct((n_out, width), src.dtype), mesh=mesh,
        scratch_types=[pltpu.VMEM((n_steps, win), jnp.int32),
                       pltpu.VMEM((win, width), src.dtype), pltpu.VMEM((win, width), src.dtype),
                       pltpu.SemaphoreType.DMA, pltpu.SemaphoreType.DMA,
                       pltpu.SemaphoreType.DMA, pltpu.SemaphoreType.DMA],
        name="scatter_rows")
    def scatter(src_hbm, idx_hbm, out_hbm, idx_vmem, rows0, rows1, lsem0, lsem1, ssem0, ssem1):
        worker = lax.axis_index("subcore") * info.num_cores + lax.axis_index("core")
        src_base = lax.rem(worker * per_worker, n_src)
        pltpu.sync_copy(idx_hbm.at[worker], idx_vmem)

        def load(j, rows, sem):
            return pltpu.make_async_copy(src_hbm.at[pl.ds(src_base + j * win, win)], rows, sem)

        def store(j, rows, sem):
            return pltpu.make_async_copy(rows, out_hbm.at[idx_vmem.at[j]], sem)

        load(0, rows0, lsem0).start()

        @pl.loop(0, n_steps, step=2)
        def _(j):
            load(j, rows0, lsem0).wait()

            @pl.when(j > 0)
            def _():
                store(j - 1, rows1, ssem1).wait()

            load(j + 1, rows1, lsem1).start()
            store(j, rows0, ssem0).start()
            load(j + 1, rows1, lsem1).wait()
            store(j, rows0, ssem0).wait()

            @pl.when(j + 2 < n_steps)
            def _():
                load(j + 2, rows0, lsem0).start()

            store(j + 1, rows1, ssem1).start()

        store(n_steps - 1, rows1, ssem1).wait()

    return scatter(src, idx.reshape(n_workers, n_steps, win))


def _expert_kernel(blk_e_ref, n_used_ref, n_valid_ref, x_ref, wg_ref, wu_ref, wd_ref, y_ref, wg_bf, wu_bf, wd_bf):
    i = pl.program_id(0)
    used = i < n_used_ref[0]
    new_expert = (i == 0) | (blk_e_ref[i] != blk_e_ref[jnp.maximum(i - 1, 0)])

    @pl.when(used & new_expert)
    def _():
        wg_bf[...] = wg_ref[0].astype(BF16)
        wu_bf[...] = wu_ref[0].astype(BF16)
        wd_bf[...] = wd_ref[0].astype(BF16)

    @pl.when(used)
    def _():
        row = lax.broadcasted_iota(jnp.int32, (x_ref.shape[0], 1), 0)
        x = _unpack_bf16_pairs(jnp.where(row < n_valid_ref[i], x_ref[...], 0)).astype(BF16)
        g = _dot(x, wg_bf[...])
        u = _dot(x, wu_bf[...])
        a = (g * jax.nn.sigmoid(g) * u).astype(BF16)
        y_ref[...] = _pack_bf16_pairs(_dot(a, wd_bf[...]))

    @pl.when(jnp.logical_not(used))
    def _():
        y_ref[...] = jnp.zeros_like(y_ref)


def expert_ffn(x_disp, blk_e, n_used, n_valid, w_gate, w_up, w_down, layer):
    rows, half = x_disp.shape
    d = 2 * half
    n_blk = rows // MOE_BLOCK
    de = w_gate.shape[3]
    grid_spec = pltpu.PrefetchScalarGridSpec(
        num_scalar_prefetch=3,
        grid=(n_blk,),
        in_specs=[pl.BlockSpec((MOE_BLOCK, half), lambda i, be, nu, nv: (i, 0)),
                  pl.BlockSpec((None, 1, d, de), lambda i, be, nu, nv: (layer, be[i], 0, 0)),
                  pl.BlockSpec((None, 1, d, de), lambda i, be, nu, nv: (layer, be[i], 0, 0)),
                  pl.BlockSpec((None, 1, de, d), lambda i, be, nu, nv: (layer, be[i], 0, 0))],
        out_specs=pl.BlockSpec((MOE_BLOCK, half), lambda i, be, nu, nv: (i, 0)),
        scratch_shapes=[pltpu.VMEM((d, de), BF16), pltpu.VMEM((d, de), BF16), pltpu.VMEM((de, d), BF16)],
    )
    return pl.pallas_call(
        _expert_kernel,
        grid_spec=grid_spec,
        out_shape=jax.ShapeDtypeStruct((rows, half), jnp.int32),
        compiler_params=_cparams(("arbitrary",)),
        name="expert_ffn",
    )(blk_e, n_used, n_valid, x_disp, w_gate, w_up, w_down)


def _combine_kernel(h_ref, y0_ref, y1_ref, wt_ref, mod_ref, o_ref, *, gate_idx):
    wt = wt_ref[0]
    moe = wt[:, 0:1] * _unpack_bf16_pairs(y0_ref[0, 0]) + wt[:, 1:2] * _unpack_bf16_pairs(y1_ref[0, 0])
    o_ref[0] = h_ref[0] + mod_ref[0, gate_idx:gate_idx + 1, :] * moe


def combine(h, y_pairs, wt, modtab, gate_idx):
    b, p, d = h.shape
    row = pl.BlockSpec((1, ROW_TILE, d), lambda b, j: (b, j, 0))
    return pl.pallas_call(
        functools.partial(_combine_kernel, gate_idx=gate_idx),
        grid=(b, p // ROW_TILE),
        in_specs=[row,
                  pl.BlockSpec((1, 1, ROW_TILE, d // 2), lambda b, j: (0, b, j, 0)),
                  pl.BlockSpec((1, 1, ROW_TILE, d // 2), lambda b, j: (1, b, j, 0)),
                  pl.BlockSpec((1, ROW_TILE, LANES), lambda b, j: (b, j, 0)),
                  _mod_spec(d)],
        out_specs=row,
        out_shape=jax.ShapeDtypeStruct((b, p, d), F32),
        compiler_params=_cparams(("parallel", "parallel")),
        name="moe_combine",
    )(h, y_pairs, y_pairs, wt, modtab.reshape(b * 2, N_MOD, d))


def hier_moe(h, f, route, wt, counts, modtab, gate_idx, w_gate, w_up, w_down, layer):
    b, p, d = h.shape
    n_tok = b * p
    n_assign = 2 * n_tok
    eid_f = route[:, :, 0:2].reshape(n_assign)
    rank_f = route[:, :, 2:4].reshape(n_assign)
    cnt = counts[0, :N_EXPERTS].astype(jnp.int32)
    pcounts = (cnt + MOE_BLOCK - 1) // MOE_BLOCK * MOE_BLOCK
    pend = jnp.cumsum(pcounts)
    pstart = pend - pcounts
    dest = (pstart[eid_f] + rank_f).astype(jnp.int32)
    n_blk = -(-n_assign // MOE_BLOCK) + N_EXPERTS
    rows_total = n_blk * MOE_BLOCK
    blk_row0 = jnp.arange(n_blk, dtype=jnp.int32) * MOE_BLOCK
    blk_e = jnp.minimum(jnp.sum((pend[None, :] <= blk_row0[:, None]).astype(jnp.int32), axis=1), N_EXPERTS - 1)
    n_valid = jnp.clip(cnt[blk_e] - (blk_row0 - pstart[blk_e]), 0, MOE_BLOCK).astype(jnp.int32)
    n_used = (pend[-1] // MOE_BLOCK).astype(jnp.int32).reshape(1)

    dest_by_slot = jnp.concatenate([dest[0::2], dest[1::2]])
    x_disp = scatter_rows(f.reshape(n_tok, d // 2), dest_by_slot, rows_total)
    y = expert_ffn(x_disp, blk_e, n_used, n_valid, w_gate, w_up, w_down, layer)
    y_pairs = gather_rows(y, dest_by_slot).reshape(2, b, p, d // 2)
    return combine(h, y_pairs, wt, modtab, gate_idx)


def kernel(x, c, ctx, c_ctx, mod_w, mod_b, norm_mix_g, norm_ffn_g, router_grp_w, router_grp_b, router_exp_w, router_exp_b, exp_w_gate, exp_w_up, exp_w_down, l0_na_w_qkv, l0_na_rpb, l0_na_w_o, l1_swa_w_qkv, l1_swa_sink, l1_swa_w_o, l2_mla_w_dq, l2_mla_q_norm_g, l2_mla_w_uq, l2_mla_w_dkv, l2_mla_kv_norm_g, l2_mla_w_ukv, l2_mla_w_o, l3_na_w_qkv, l3_na_rpb, l3_na_w_o, final_norm_g):
    b, s, d = x.shape
    lc = ctx.shape[1]
    n_heads = d // HEAD_DIM
    n_kv_heads = n_heads // 4
    depth = mod_w.shape[0]
    rows = s // GRID_W

    h = jnp.concatenate([ctx, x], axis=1)
    modtabs = modulation_tables(c, c_ctx, mod_w, mod_b)

    def scale_q_cols(w):
        n_q = n_heads * HEAD_DIM
        q_scale = HEAD_DIM ** -0.5 * LOG2E
        return jnp.concatenate([w[:, :n_q] * q_scale, w[:, n_q:]], axis=1).astype(BF16)

    def na_mixer(hm, w_qkv, rpb):
        qkv = project(hm, scale_q_cols(w_qkv))
        return neighbourhood_attention(qkv, na_bias_table(rpb, rows), lc, n_heads)

    def swa_mixer(hm):
        cos, sin = rope_tables_full(s, lc)
        n_rope = (n_heads + n_kv_heads) * HEAD_DIM // 512
        qkv = project(hm, scale_q_cols(l1_swa_w_qkv), tn=512, rope=(cos, sin, 0, n_rope))
        return window_attention(qkv, l1_swa_sink, lc, n_heads, n_kv_heads)

    def mla_mixer(hm):
        q_rank = l2_mla_w_dq.shape[1]
        kv_rank = l2_mla_kv_norm_g.shape[0]
        cos, sin = rope_tables_mla(s, lc)
        w_down = jnp.concatenate([l2_mla_w_dq, l2_mla_w_dkv[:, :kv_rank],
                                  _spread_rope_cols(l2_mla_w_dkv[:, kv_rank:])], axis=1).astype(BF16)
        cq, ckv, kr = mla_down(hm, w_down, l2_mla_q_norm_g, l2_mla_kv_norm_g, cos, sin, q_rank, kv_rank)
        w_uq = l2_mla_w_uq.reshape(q_rank, n_heads, MLA_NOPE_DIM + MLA_ROPE_DIM)
        w_q_nope = w_uq[:, :, :MLA_NOPE_DIM].reshape(q_rank, n_heads * MLA_NOPE_DIM)
        w_q_rope = jnp.concatenate([_spread_rope_cols(w_uq[:, hh, MLA_NOPE_DIM:]) for hh in range(n_heads)], axis=1)
        q_scale = (MLA_NOPE_DIM + MLA_ROPE_DIM) ** -0.5 * LOG2E
        w_q = (jnp.concatenate([w_q_nope, w_q_rope], axis=1) * q_scale).astype(BF16)
        n_nope_tiles = n_heads * MLA_NOPE_DIM // 512
        q = project(cq, w_q, tn=512, rope=(cos, sin, n_nope_tiles, 2 * n_nope_tiles))
        kv = project(ckv, l2_mla_w_ukv.astype(BF16), tn=512)
        return latent_attention(q, kv, kr, lc, n_heads)

    for i in range(depth):
        modtab = modtabs[i]
        hm = norm_modulate(h, norm_mix_g[i], modtab, 0)
        mixer = i % 3
        if mixer == 0:
            w_qkv, rpb, w_o = (l0_na_w_qkv, l0_na_rpb, l0_na_w_o) if i == 0 else (l3_na_w_qkv, l3_na_rpb, l3_na_w_o)
            y = na_mixer(hm, w_qkv, rpb)
        elif mixer == 1:
            y, w_o = swa_mixer(hm), l1_swa_w_o
        else:
            y, w_o = mla_mixer(hm), l2_mla_w_o
        h = project(y, w_o.astype(BF16), tn=512, resid=(h, modtab, 2), ctx_len=lc)
        f, route, wt, counts = norm_modulate_route(h, norm_ffn_g[i], modtab, 3, router_grp_w[i], router_grp_b[i],
                                                   router_exp_w[i], router_exp_b[i])
        h = hier_moe(h, f, route, wt, counts, modtab, 5, exp_w_gate, exp_w_up, exp_w_down, i)
    return final_norm(h, final_norm_g, lc)
```

```python
import functools

import numpy as np
import jax
import jax.numpy as jnp
from jax import lax
from jax.experimental import pallas as pl
from jax.experimental.pallas import tpu as pltpu
from jax.experimental.pallas import tpu_sc as plsc

GRID_W = 64
HEAD_DIM = 128
ROPE_BASE = 10000.0
NORM_EPS = 1e-6
NEG_INF = -1e30
N_MOD = 6

NA_ROWS = 8
NA_COLS = 16
NA_TILE_ROWS = 4
NA_WIN_ROWS = 12

SWA_WINDOW = 128
SWA_TQ = 256
SWA_TK = 512

MLA_NOPE_DIM = 128
MLA_ROPE_DIM = 64
MLA_TQ = 512
MLA_TK = 512
LOG2E = 1.4426950408889634

N_GROUPS = 4
EXPERTS_PER_GROUP = 8
N_EXPERTS = N_GROUPS * EXPERTS_PER_GROUP
MOE_BLOCK = 512
GATHER_WINDOW = 32

LANES = 128
ROW_TILE = 256
VMEM_LIMIT = 56 * 1024 * 1024

BF16 = jnp.bfloat16
F32 = jnp.float32


def _cparams(sem):
    return pltpu.CompilerParams(dimension_semantics=sem, vmem_limit_bytes=VMEM_LIMIT)


def _dot(a, b):
    return jnp.dot(a, b, preferred_element_type=F32)


def _dot_nt(a, b):
    return lax.dot_general(a, b, (((1,), (1,)), ((), ())), preferred_element_type=F32)


def _mod_kernel(x_ref, w_ref, b_ref, o_ref):
    x = x_ref[...]
    sx = (x * jax.nn.sigmoid(x)).astype(BF16)
    o_ref[0] = _dot(sx, w_ref[0].astype(BF16)) + b_ref[0]


def modulation_tables(c, c_ctx, mod_w, mod_b):
    depth, d, n_out = mod_w.shape
    b = c.shape[0]
    rows = 16
    xin = jnp.zeros((rows, d), F32).at[:b].set(c).at[b].set(c_ctx)
    tn = 1024
    out = pl.pallas_call(
        _mod_kernel,
        grid=(depth, n_out // tn),
        in_specs=[pl.BlockSpec((rows, d), lambda i, j: (0, 0)),
                  pl.BlockSpec((1, d, tn), lambda i, j: (i, 0, j)),
                  pl.BlockSpec((1, 1, tn), lambda i, j: (i, 0, j))],
        out_specs=pl.BlockSpec((1, rows, tn), lambda i, j: (i, 0, j)),
        out_shape=jax.ShapeDtypeStruct((depth, rows, n_out), F32),
        compiler_params=_cparams(("parallel", "parallel")),
        name="adaln_mod",
    )(xin, mod_w, mod_b.reshape(depth, 1, n_out))
    lat = out[:, :b].reshape(depth, b, 1, N_MOD, d)
    ctx = jnp.broadcast_to(out[:, b].reshape(depth, 1, 1, N_MOD, d), (depth, b, 1, N_MOD, d))
    return jnp.concatenate([ctx, lat], axis=2)


def _rms_mod(x, g, shift, scale):
    ms = jnp.mean(x * x, axis=-1, keepdims=True)
    y = x * lax.rsqrt(ms + NORM_EPS) * g
    return y * (1.0 + scale) + shift


def _norm_mod_kernel(h_ref, g_ref, mod_ref, o_ref, *, shift_idx):
    f = _rms_mod(h_ref[0], g_ref[...], mod_ref[0, shift_idx:shift_idx + 1, :],
                 mod_ref[0, shift_idx + 1:shift_idx + 2, :])
    o_ref[0] = f.astype(o_ref.dtype)


def _route(logits):
    lane = lax.broadcasted_iota(jnp.int32, logits.shape, 1).astype(F32)
    big = float(LANES)

    def first_lane(mask):
        return jnp.min(jnp.where(mask, lane, big), axis=-1, keepdims=True)

    in_grp = lane < N_GROUPS
    lg = jnp.where(in_grp, logits, NEG_INF)
    m_g = jnp.max(lg, axis=-1, keepdims=True)
    g_idx = first_lane(in_grp & (lg == m_g))
    g_w = 1.0 / jnp.sum(jnp.where(in_grp, jnp.exp(lg - m_g), 0.0), axis=-1, keepdims=True)
    e_lo = N_GROUPS + g_idx * EXPERTS_PER_GROUP
    in_e = (lane >= e_lo) & (lane < e_lo + EXPERTS_PER_GROUP)
    le = jnp.where(in_e, logits, NEG_INF)
    m1 = jnp.max(le, axis=-1, keepdims=True)
    e1 = first_lane(in_e & (le == m1))
    s_e = jnp.sum(jnp.where(in_e, jnp.exp(le - m1), 0.0), axis=-1, keepdims=True)
    in_e2 = in_e & (lane != e1)
    le2 = jnp.where(in_e2, logits, NEG_INF)
    m2 = jnp.max(le2, axis=-1, keepdims=True)
    e2 = first_lane(in_e2 & (le2 == m2))
    p1 = 1.0 / s_e
    p2 = jnp.exp(m2 - m1) / s_e
    den = p1 + p2
    return ((e1 - N_GROUPS).astype(jnp.int32), (e2 - N_GROUPS).astype(jnp.int32),
            g_w * p1 / den, g_w * p2 / den)


def _pack_bf16_pairs(x):
    n = x.shape[1] // 2
    xb = x.astype(BF16).astype(F32)
    hi = lax.bitcast_convert_type(xb[:, :n], jnp.int32)
    lo = lax.bitcast_convert_type(xb[:, n:], jnp.int32)
    return (hi & jnp.int32(-65536)) | lax.shift_right_logical(lo, jnp.int32(16))


def _unpack_bf16_pairs(w):
    hi = lax.bitcast_convert_type(w & jnp.int32(-65536), F32)
    lo = lax.bitcast_convert_type(lax.shift_left(w, jnp.int32(16)), F32)
    return jnp.concatenate([hi, lo], axis=1)


def _route_and_rank(f, wr_ref, br_ref, run_ref):
    f_hi = f.astype(BF16)
    f_lo = (f - f_hi.astype(F32)).astype(BF16)
    hi_terms = _dot(f_hi, wr_ref[...])
    logits = (hi_terms[:, :LANES] + _dot(f_lo, wr_ref[:, :LANES]) + hi_terms[:, LANES:]) + br_ref[...]
    e1, e2, w1, w2 = _route(logits)
    lane = lax.broadcasted_iota(jnp.int32, logits.shape, 1)
    pick1, pick2 = lane == e1, lane == e2
    chosen = (pick1 | pick2).astype(F32)
    n_rows = chosen.shape[0]
    earlier = (lax.broadcasted_iota(jnp.int32, (n_rows, n_rows), 1)
               < lax.broadcasted_iota(jnp.int32, (n_rows, n_rows), 0)).astype(BF16)
    before = run_ref[...] + _dot(earlier, chosen.astype(BF16))
    rank1 = jnp.sum(jnp.where(pick1, before, 0.0), axis=-1, keepdims=True).astype(jnp.int32)
    rank2 = jnp.sum(jnp.where(pick2, before, 0.0), axis=-1, keepdims=True).astype(jnp.int32)
    run_ref[...] = run_ref[...] + jnp.sum(chosen, axis=0, keepdims=True)
    route = jnp.where(lane == 0, e1, jnp.where(lane == 1, e2, jnp.where(lane == 2, rank1,
                                                                         jnp.where(lane == 3, rank2, 0))))
    return route, jnp.where(lane == 0, w1, jnp.where(lane == 1, w2, 0.0))


def _attn_out_route_kernel(y_ref, w_ref, h_ref, mod_ref, g_ref, wr_ref, br_ref,
                           ho_ref, f_ref, route_ref, wt_ref, cnt_ref, run_ref, *, ctx_len, tm, n_sub):
    first = (pl.program_id(0) == 0) & (pl.program_id(1) == 0)

    @pl.when(first)
    def _():
        run_ref[...] = jnp.zeros_like(run_ref)

    sub = tm // n_sub
    for s in range(n_sub):
        rows = slice(s * sub, (s + 1) * sub)
        acc = _dot(y_ref[0, rows, :], w_ref[...])
        pos = pl.program_id(1) * tm + s * sub + lax.broadcasted_iota(jnp.int32, (sub, 1), 0)
        is_ctx = pos < ctx_len

        def mod_row(k):
            return jnp.where(is_ctx, mod_ref[0, 0, k:k + 1, :], mod_ref[0, 1, k:k + 1, :])

        h_new = h_ref[0, rows, :] + mod_row(2) * acc
        ho_ref[0, rows, :] = h_new
        f = _rms_mod(h_new, g_ref[...], mod_row(3), mod_row(4))
        f_ref[0, rows, :] = _pack_bf16_pairs(f)
        route, wt = _route_and_rank(f, wr_ref, br_ref, run_ref)
        route_ref[0, rows, :] = route
        wt_ref[0, rows, :] = wt
    cnt_ref[...] = run_ref[...]


def _mod_spec(d):
    return pl.BlockSpec((1, N_MOD, d), lambda b, j: (2 * b + jnp.minimum(j, 1), 0, 0))


def norm_modulate(h, g, modtab, shift_idx):
    b, p, d = h.shape
    row = pl.BlockSpec((1, ROW_TILE, d), lambda b, j: (b, j, 0))
    return pl.pallas_call(
        functools.partial(_norm_mod_kernel, shift_idx=shift_idx),
        grid=(b, p // ROW_TILE),
        in_specs=[row, pl.BlockSpec((1, d), lambda b, j: (0, 0)), _mod_spec(d)],
        out_specs=row,
        out_shape=jax.ShapeDtypeStruct((b, p, d), BF16),
        compiler_params=_cparams(("parallel", "parallel")),
        name="norm_mod",
    )(h, g.reshape(1, d), modtab.reshape(b * 2, N_MOD, d))


def attn_out_route(y, w_o, h, modtab, g, w_grp, b_grp, w_rt, b_rt, ctx_len):
    b, p, d = h.shape
    n_r = N_GROUPS + N_EXPERTS
    wr = jnp.zeros((d, LANES), F32).at[:, :N_GROUPS].set(w_grp).at[:, N_GROUPS:n_r].set(w_rt)
    br = jnp.zeros((1, LANES), F32).at[0, :N_GROUPS].set(b_grp).at[0, N_GROUPS:n_r].set(b_rt)
    wr_hi = wr.astype(BF16)
    wr = jnp.concatenate([wr_hi, (wr - wr_hi.astype(F32)).astype(BF16)], axis=1)
    tm = 544 if p % 544 == 0 else ROW_TILE
    n_sub = 2
    assert (tm // n_sub) % 16 == 0
    row = lambda width: pl.BlockSpec((1, tm, width), lambda b, i: (b, i, 0))
    once = lambda shape: pl.BlockSpec(shape, lambda b, i: (0,) * len(shape), pipeline_mode=pl.Buffered(1))
    return pl.pallas_call(
        functools.partial(_attn_out_route_kernel, ctx_len=ctx_len, tm=tm, n_sub=n_sub),
        grid=(b, p // tm),
        in_specs=[row(d), once((d, d)), row(d),
                  pl.BlockSpec((1, 2, N_MOD, d), lambda b, i: (b, 0, 0, 0)),
                  once((1, d)), once((d, 2 * LANES)), once((1, LANES))],
        out_specs=[row(d), row(d // 2), row(LANES), row(LANES), pl.BlockSpec((1, LANES), lambda b, i: (0, 0))],
        out_shape=[jax.ShapeDtypeStruct((b, p, d), F32),
                   jax.ShapeDtypeStruct((b, p, d // 2), jnp.int32),
                   jax.ShapeDtypeStruct((b, p, LANES), jnp.int32),
                   jax.ShapeDtypeStruct((b, p, LANES), F32),
                   jax.ShapeDtypeStruct((1, LANES), F32)],
        scratch_shapes=[pltpu.VMEM((1, LANES), F32)],
        compiler_params=_cparams(("arbitrary", "arbitrary")),
        name="attn_out_route",
    )(y, w_o, h, modtab, g.reshape(1, d), wr, br)


def _rope(acc, cos, sin):
    n_blk = acc.shape[1] // LANES
    outs = []
    for c in range(n_blk):
        x = acc[:, c * LANES:(c + 1) * LANES]
        outs.append(x * cos + pltpu.roll(x, LANES // 2, 1) * sin)
    return outs[0] if n_blk == 1 else jnp.concatenate(outs, axis=1)


def _proj_kernel(*refs, rope_lo, rope_hi):
    x_ref, w_ref = refs[0], refs[1]
    o_ref = refs[-1]
    acc = _dot(x_ref[0], w_ref[...])
    if rope_hi > rope_lo:
        cos_ref, sin_ref = refs[2], refs[3]
        j = pl.program_id(2)
        roped = (j >= rope_lo) & (j < rope_hi)

        @pl.when(roped)
        def _():
            o_ref[0] = _rope(acc, cos_ref[...], sin_ref[...]).astype(o_ref.dtype)

        @pl.when(jnp.logical_not(roped))
        def _():
            o_ref[0] = acc.astype(o_ref.dtype)
    else:
        o_ref[0] = acc.astype(o_ref.dtype)


def _row_tile(p):
    for cand in (1088, 1024, 544, 512, 272, 256, 128, 64, 32, 16):
        if p % cand == 0:
            return cand
    raise ValueError(p)


def project(x, w, *, tn=512, out_dtype=BF16, rope=None):
    b, p, k = x.shape
    n = w.shape[1]
    tm = _row_tile(p)
    tn = min(tn, n)
    assert n % tn == 0
    in_specs = [pl.BlockSpec((1, tm, k), lambda b, i, j: (b, i, 0)),
                pl.BlockSpec((k, tn), lambda b, i, j: (0, j))]
    args = [x, w]
    kw = dict(rope_lo=0, rope_hi=0)
    if rope is not None:
        cos, sin, lo, hi = rope
        in_specs += [pl.BlockSpec((tm, LANES), lambda b, i, j: (i, 0)),
                     pl.BlockSpec((tm, LANES), lambda b, i, j: (i, 0))]
        args += [cos, sin]
        kw.update(rope_lo=lo, rope_hi=hi)
    return pl.pallas_call(
        functools.partial(_proj_kernel, **kw),
        grid=(b, p // tm, n // tn),
        in_specs=in_specs,
        out_specs=pl.BlockSpec((1, tm, tn), lambda b, i, j: (b, i, j)),
        out_shape=jax.ShapeDtypeStruct((b, p, n), out_dtype),
        compiler_params=_cparams(("parallel", "parallel", "arbitrary")),
        name="project",
    )(*args)


def _axial_cos_sin(n, rot_dim):
    t = jnp.arange(n, dtype=jnp.int32)
    row = (t // GRID_W).astype(F32)
    col = (t % GRID_W).astype(F32)
    n_freq = rot_dim // 4
    inv = ROPE_BASE ** (-jnp.arange(n_freq, dtype=F32) / n_freq)
    ang = jnp.concatenate([row[:, None] * inv, col[:, None] * inv], axis=-1)
    return jnp.cos(ang), jnp.sin(ang)


def rope_tables_full(s, ctx_len):
    c, sn = _axial_cos_sin(s, HEAD_DIM)
    cos = jnp.concatenate([c, c], axis=1)
    sin = jnp.concatenate([-sn, sn], axis=1)
    ident_c = jnp.ones((ctx_len, LANES), F32)
    ident_s = jnp.zeros((ctx_len, LANES), F32)
    return jnp.concatenate([ident_c, cos], axis=0), jnp.concatenate([ident_s, sin], axis=0)


def rope_tables_mla(s, ctx_len):
    c, sn = _axial_cos_sin(s, MLA_ROPE_DIM)
    one = jnp.ones_like(c)
    zero = jnp.zeros_like(c)
    cos = jnp.concatenate([c, one, c, one], axis=1)
    sin = jnp.concatenate([-sn, zero, sn, zero], axis=1)
    ident_c = jnp.ones((ctx_len, LANES), F32)
    ident_s = jnp.zeros((ctx_len, LANES), F32)
    return jnp.concatenate([ident_c, cos], axis=0), jnp.concatenate([ident_s, sin], axis=0)


def _spread_rope_cols(w_rope):
    k = w_rope.shape[0]
    half = MLA_ROPE_DIM // 2
    z = jnp.zeros((k, half), w_rope.dtype)
    return jnp.concatenate([w_rope[:, :half], z, w_rope[:, half:], z], axis=1)


def _softmax_parts(parts, extra=None):
    m = _row_reduce(jnp.maximum, jnp.max, parts)
    if extra is not None:
        m = jnp.maximum(m, extra)
    ps = [jnp.exp2(s - m) for s in parts]
    den = _row_reduce(jnp.add, jnp.sum, ps)
    if extra is not None:
        den = den + jnp.exp2(extra - m)
    return ps, 1.0 / den


def _row_reduce(combine, reduce, parts):
    blocks = [s[:, c:c + LANES] for s in parts for c in range(0, s.shape[1], LANES)]
    acc = blocks[0]
    for blk in blocks[1:]:
        acc = combine(acc, blk)
    return reduce(acc, axis=-1, keepdims=True)


def na_bias_table(rpb, rows):
    n_tiles = rows // NA_TILE_ROWS
    n_heads, _, n_dcol = rpb.shape
    drow, row_ok = [], []
    for tile in (0, 1, n_tiles - 1):
        kr0 = int(np.clip(NA_TILE_ROWS * tile - NA_ROWS // 2, 0, rows - NA_WIN_ROWS))
        r = NA_TILE_ROWS * tile + np.arange(NA_TILE_ROWS)
        r0 = np.clip(r - NA_ROWS // 2, 0, rows - NA_ROWS)
        krow = kr0 + np.arange(NA_WIN_ROWS)
        row_ok.append((krow[None, :] >= r0[:, None]) & (krow[None, :] < r0[:, None] + NA_ROWS))
        drow.append(np.clip(krow[None, :] - r[:, None] + NA_ROWS - 1, 0, 2 * NA_ROWS - 2))
    drow, row_ok = np.stack(drow), np.stack(row_ok)
    qc = np.arange(GRID_W)
    qcol0 = np.clip(qc - NA_COLS // 2, 0, GRID_W - NA_COLS)
    kc = np.arange(GRID_W)
    col_ok = (kc[None, :] >= qcol0[:, None]) & (kc[None, :] < qcol0[:, None] + NA_COLS)
    dcol = np.clip(kc[None, :] - qc[:, None] + NA_COLS - 1, 0, 2 * NA_COLS - 2)
    by_row = rpb.astype(F32)[:, drow, :]
    pick_col = jnp.asarray(dcol[None] == np.arange(n_dcol)[:, None, None], F32)
    vals = jnp.einsum('hpamd,dqk->hpaqmk', by_row, pick_col, precision=lax.Precision.HIGHEST)
    ok = row_ok[None, :, :, None, :, None] & col_ok[None, None, None, :, None, :]
    vals = jnp.where(jnp.asarray(ok), vals * LOG2E, NEG_INF)
    return vals.reshape(n_heads, 3, NA_TILE_ROWS * GRID_W, NA_WIN_ROWS * GRID_W)


def _na_kernel(q_ref, k_ref, v_ref, bias_ref, o_ref, *, ctx_len, rows):
    tq = NA_TILE_ROWS * GRID_W
    tk = NA_WIN_ROWS * GRID_W
    n_tiles = rows // NA_TILE_ROWS
    lc = ctx_len

    s = _dot_nt(q_ref[0, 0:lc, :], k_ref[0, 0:lc, :])
    (p,), inv = _softmax_parts([s])
    o_ref[0, 0:lc, :] = (_dot(p.astype(BF16), v_ref[0, 0:lc, :]) * inv).astype(o_ref.dtype)

    def tile(i, carry):
        qs = pl.multiple_of(lc + i * tq, tq)
        kr0 = jnp.clip(NA_TILE_ROWS * i - NA_ROWS // 2, 0, rows - NA_WIN_ROWS)
        ks = pl.multiple_of(lc + kr0 * GRID_W, NA_TILE_ROWS * GRID_W)
        pat = jnp.where(i == 0, 0, jnp.where(i == n_tiles - 1, 2, 1))
        q = q_ref[0, pl.ds(qs, tq), :]
        s_loc = _dot_nt(q, k_ref[0, pl.ds(ks, tk), :]) + bias_ref[0, pat]
        s_ctx = _dot_nt(q, k_ref[0, 0:lc, :])
        (p_loc, p_ctx), inv = _softmax_parts([s_loc, s_ctx])
        o = _dot(p_loc.astype(BF16), v_ref[0, pl.ds(ks, tk), :]) + _dot(p_ctx.astype(BF16), v_ref[0, 0:lc, :])
        o_ref[0, pl.ds(qs, tq), :] = (o * inv).astype(o_ref.dtype)
        return carry

    lax.fori_loop(0, n_tiles, tile, 0, unroll=4)


def neighbourhood_attention(qkv, bias, ctx_len, n_heads):
    b, p, _ = qkv.shape
    rows = (p - ctx_len) // GRID_W
    assert rows % NA_TILE_ROWS == 0 and rows >= NA_WIN_ROWS
    tq, tk = NA_TILE_ROWS * GRID_W, NA_WIN_ROWS * GRID_W
    assert ctx_len % 16 == 0 and ctx_len % tq == 0
    blk = lambda off: pl.BlockSpec((1, p, HEAD_DIM), lambda h, b: (b, 0, off + h))
    return pl.pallas_call(
        functools.partial(_na_kernel, ctx_len=ctx_len, rows=rows),
        grid=(n_heads, b),
        in_specs=[blk(0), blk(n_heads), blk(2 * n_heads),
                  pl.BlockSpec((1, 3, tq, tk), lambda h, b: (h, 0, 0, 0))],
        out_specs=blk(0),
        out_shape=jax.ShapeDtypeStruct((b, p, n_heads * HEAD_DIM), BF16),
        compiler_params=_cparams(("parallel", "parallel")),
        name="na_attention",
    )(qkv, qkv, qkv, bias)


def _swa_kernel(sink_ref, q_ref, k_ref, v_ref, o_ref, *, ctx_len, seq, group):
    lc = ctx_len
    kvh = pl.program_id(1)

    def stack_heads(q):
        return jnp.concatenate([q[:, g * HEAD_DIM:(g + 1) * HEAD_DIM] for g in range(group)], axis=0)

    def finish(parts_fn, n_q, pv_fn, store):
        ps_all, invs = [], []
        for g in range(group):
            sink = sink_ref[kvh * group + g]
            ps, inv = _softmax_parts(parts_fn(g), extra=sink)
            ps_all.append(ps)
            invs.append(inv)
        n_parts = len(ps_all[0])
        stacked = [jnp.concatenate([ps_all[g][k] for g in range(group)], axis=0).astype(BF16)
                   for k in range(n_parts)]
        o = pv_fn(stacked)
        store(jnp.concatenate([o[g * n_q:(g + 1) * n_q] * invs[g] for g in range(group)], axis=1))

    s_c = _dot_nt(stack_heads(q_ref[0, 0:lc, :]), k_ref[0, 0:lc, :])

    def store_ctx(o):
        o_ref[0, 0:lc, :] = o.astype(o_ref.dtype)

    finish(lambda g: [s_c[g * lc:(g + 1) * lc]], lc,
           lambda st: _dot(st[0], v_ref[0, 0:lc, :]), store_ctx)

    n_tiles = seq // SWA_TQ

    def tile(t, carry):
        q0 = t * SWA_TQ
        k0 = jnp.clip(q0 - SWA_WINDOW, 0, seq - SWA_TK)
        qs = pl.multiple_of(lc + q0, SWA_WINDOW)
        ks = pl.multiple_of(lc + k0, SWA_WINDOW)
        q4 = stack_heads(q_ref[0, pl.ds(qs, SWA_TQ), :])
        s_loc = _dot_nt(q4, k_ref[0, pl.ds(ks, SWA_TK), :])
        s_ctx = _dot_nt(q4, k_ref[0, 0:lc, :])
        dpos = (lax.broadcasted_iota(jnp.int32, (SWA_TQ, SWA_TK), 1)
                - lax.broadcasted_iota(jnp.int32, (SWA_TQ, SWA_TK), 0)) + (k0 - q0)
        valid = jnp.abs(dpos) <= SWA_WINDOW

        def parts(g):
            sl = slice(g * SWA_TQ, (g + 1) * SWA_TQ)
            return [jnp.where(valid, s_loc[sl], NEG_INF), s_ctx[sl]]

        def store(o):
            o_ref[0, pl.ds(qs, SWA_TQ), :] = o.astype(o_ref.dtype)

        finish(parts, SWA_TQ,
               lambda st: _dot(st[0], v_ref[0, pl.ds(ks, SWA_TK), :]) + _dot(st[1], v_ref[0, 0:lc, :]),
               store)
        return carry

    lax.fori_loop(0, n_tiles, tile, 0, unroll=2)


def window_attention(qkv, sink, ctx_len, n_heads, n_kv_heads):
    b, p, _ = qkv.shape
    seq = p - ctx_len
    group = n_heads // n_kv_heads
    assert seq % SWA_TQ == 0 and seq >= SWA_TK and ctx_len % SWA_WINDOW == 0
    kv = lambda off: pl.BlockSpec((1, p, HEAD_DIM), lambda b, h: (b, 0, off + h))
    qo = pl.BlockSpec((1, p, group * HEAD_DIM), lambda b, h: (b, 0, h))
    return pl.pallas_call(
        functools.partial(_swa_kernel, ctx_len=ctx_len, seq=seq, group=group),
        grid=(b, n_kv_heads),
        in_specs=[pl.BlockSpec(memory_space=pltpu.SMEM), qo, kv(n_heads), kv(n_heads + n_kv_heads)],
        out_specs=qo,
        out_shape=jax.ShapeDtypeStruct((b, p, n_heads * HEAD_DIM), BF16),
        compiler_params=_cparams(("parallel", "parallel")),
        name="swa_attention",
    )(sink.astype(F32) * LOG2E, qkv, qkv, qkv)


def _mla_down_kernel(x_ref, w_ref, gq_ref, gkv_ref, cos_ref, sin_ref, cq_ref, ckv_ref, kr_ref, *, q_rank, kv_rank):
    acc = _dot(x_ref[0], w_ref[...])

    def rms(x, g):
        ms = jnp.mean(x * x, axis=-1, keepdims=True)
        return x * lax.rsqrt(ms + NORM_EPS) * g

    cq_ref[0] = rms(acc[:, :q_rank], gq_ref[...]).astype(cq_ref.dtype)
    ckv_ref[0] = rms(acc[:, q_rank:q_rank + kv_rank], gkv_ref[...]).astype(ckv_ref.dtype)
    kr_ref[0] = _rope(acc[:, q_rank + kv_rank:], cos_ref[...], sin_ref[...]).astype(kr_ref.dtype)


def mla_down(x, w_down, gq, gkv, cos, sin, q_rank, kv_rank):
    b, p, k = x.shape
    n = w_down.shape[1]
    tm = 544 if p % 544 == 0 else _row_tile(p)
    row = lambda width: pl.BlockSpec((1, tm, width), lambda b, i: (b, i, 0))
    return pl.pallas_call(
        functools.partial(_mla_down_kernel, q_rank=q_rank, kv_rank=kv_rank),
        grid=(b, p // tm),
        in_specs=[row(k), pl.BlockSpec((k, n), lambda b, i: (0, 0)),
                  pl.BlockSpec((1, q_rank), lambda b, i: (0, 0)),
                  pl.BlockSpec((1, kv_rank), lambda b, i: (0, 0)),
                  pl.BlockSpec((tm, LANES), lambda b, i: (i, 0)),
                  pl.BlockSpec((tm, LANES), lambda b, i: (i, 0))],
        out_specs=[row(q_rank), row(kv_rank), row(LANES)],
        out_shape=[jax.ShapeDtypeStruct((b, p, q_rank), BF16),
                   jax.ShapeDtypeStruct((b, p, kv_rank), BF16),
                   jax.ShapeDtypeStruct((b, p, LANES), BF16)],
        compiler_params=_cparams(("parallel", "parallel")),
        name="mla_down",
    )(x, w_down, gq.reshape(1, q_rank), gkv.reshape(1, kv_rank), cos, sin)


def _mla_kernel(qn_ref, qr_ref, kn_ref, kr_ref, v_ref, o_ref, kcat_ref, vone_ref, *, ctx_len):
    lc = ctx_len
    p_all = kcat_ref.shape[0]
    kcat_ref[:, 0:LANES] = kn_ref[0]
    kcat_ref[:, LANES:2 * LANES] = kr_ref[0]
    vone_ref[:, 0:LANES] = v_ref[0]
    vone_ref[:, LANES:2 * LANES] = jnp.ones((p_all, LANES), BF16)

    def attend(qs, n_q, chunks):
        q = jnp.concatenate([qn_ref[0, pl.ds(qs, n_q), :], qr_ref[0, pl.ds(qs, n_q), :]], axis=1)
        m = jnp.full((n_q, 1), NEG_INF, F32)
        acc = jnp.zeros((n_q, 2 * LANES), F32)
        for c0, c1 in chunks:
            s = _dot_nt(q, kcat_ref[c0:c1, :])
            m_new = jnp.maximum(m, _row_reduce(jnp.maximum, jnp.max, [s]))
            p = jnp.exp2(s - m_new)
            acc = jnp.exp2(m - m_new) * acc + _dot(p.astype(BF16), vone_ref[c0:c1, :])
            m = m_new
        o_ref[0, pl.ds(qs, n_q), :] = (acc[:, :LANES] * (1.0 / acc[:, LANES:LANES + 1])).astype(o_ref.dtype)

    attend(0, lc, [(0, lc)])
    all_chunks = [(0, lc)] + [(c, c + MLA_TK) for c in range(lc, p_all, MLA_TK)]

    def tile(i, carry):
        attend(pl.multiple_of(lc + i * MLA_TQ, MLA_TQ), MLA_TQ, all_chunks)
        return carry

    lax.fori_loop(0, (p_all - lc) // MLA_TQ, tile, 0, unroll=2)


def latent_attention(q, kv, kr, ctx_len, n_heads):
    b, p, _ = q.shape
    assert (p - ctx_len) % MLA_TQ == 0 and (p - ctx_len) % MLA_TK == 0 and ctx_len % 16 == 0
    blk = lambda f: pl.BlockSpec((1, p, LANES), f)
    return pl.pallas_call(
        functools.partial(_mla_kernel, ctx_len=ctx_len),
        grid=(b, n_heads),
        in_specs=[blk(lambda b, h: (b, 0, h)), blk(lambda b, h: (b, 0, n_heads + h)),
                  blk(lambda b, h: (b, 0, 2 * h)), blk(lambda b, h: (b, 0, 0)),
                  blk(lambda b, h: (b, 0, 2 * h + 1))],
        out_specs=blk(lambda b, h: (b, 0, h)),
        out_shape=jax.ShapeDtypeStruct((b, p, n_heads * LANES), BF16),
        scratch_shapes=[pltpu.VMEM((p, 2 * LANES), BF16), pltpu.VMEM((p, 2 * LANES), BF16)],
        compiler_params=_cparams(("parallel", "parallel")),
        name="mla_attention",
    )(q, q, kv, kr, kv)


def gather_rows(src, idx):
    n = idx.shape[0]
    width = src.shape[1]
    win = GATHER_WINDOW
    info = plsc.get_sparse_core_info()
    n_workers = info.num_cores * info.num_subcores
    assert n % (n_workers * 2 * win) == 0
    per_worker = n // n_workers
    mesh = plsc.VectorSubcoreMesh(core_axis_name="core", subcore_axis_name="subcore")

    @functools.partial(
        pl.kernel, out_type=jax.ShapeDtypeStruct((n, width), src.dtype), mesh=mesh,
        scratch_types=[pltpu.VMEM((per_worker,), jnp.int32),
                       pltpu.VMEM((win, width), src.dtype), pltpu.VMEM((win, width), src.dtype),
                       pltpu.SemaphoreType.DMA, pltpu.SemaphoreType.DMA,
                       pltpu.SemaphoreType.DMA, pltpu.SemaphoreType.DMA],
        name="gather_rows")
    def gather(src_hbm, idx_hbm, out_hbm, idx_vmem, rows0, rows1, gsem0, gsem1, osem0, osem1):
        worker = lax.axis_index("subcore") * info.num_cores + lax.axis_index("core")
        base = worker * per_worker
        pltpu.sync_copy(idx_hbm.at[pl.ds(base, per_worker)], idx_vmem)

        def fetch(r, rows, sem):
            return pltpu.make_async_copy(src_hbm.at[idx_vmem.at[pl.ds(r, win)]], rows, sem)

        def flush(r, rows, sem):
            return pltpu.make_async_copy(rows, out_hbm.at[pl.ds(base + r, win)], sem)

        fetch(0, rows0, gsem0).start()

        @pl.loop(0, per_worker, step=2 * win)
        def _(r):
            fetch(r, rows0, gsem0).wait()

            @pl.when(r > 0)
            def _():
                flush(r - win, rows1, osem1).wait()

            fetch(r + win, rows1, gsem1).start()
            flush(r, rows0, osem0).start()
            fetch(r + win, rows1, gsem1).wait()
            flush(r, rows0, osem0).wait()

            @pl.when(r + 2 * win < per_worker)
            def _():
                fetch(r + 2 * win, rows0, gsem0).start()

            flush(r + win, rows1, osem1).start()

        flush(per_worker - win, rows1, osem1).wait()

    return gather(src, idx)


def scatter_rows(src, idx, n_out):
    n = idx.shape[0]
    n_src, width = src.shape
    win = GATHER_WINDOW
    info = plsc.get_sparse_core_info()
    n_workers = info.num_cores * info.num_subcores
    assert n % (n_workers * 2 * win) == 0
    per_worker = n // n_workers
    assert n_src % per_worker == 0
    n_steps = per_worker // win
    mesh = plsc.VectorSubcoreMesh(core_axis_name="core", subcore_axis_name="subcore")

    @functools.partial(
        pl.kernel, out_type=jax.ShapeDtypeStruct((n_out, width), src.dtype), mesh=mesh,
        scratch_types=[pltpu.VMEM((n_steps, win), jnp.int32),
                       pltpu.VMEM((win, width), src.dtype), pltpu.VMEM((win, width), src.dtype),
                       pltpu.SemaphoreType.DMA, pltpu.SemaphoreType.DMA,
                       pltpu.SemaphoreType.DMA, pltpu.SemaphoreType.DMA],
        name="scatter_rows")
    def scatter(src_hbm, idx_hbm, out_hbm, idx_vmem, rows0, rows1, lsem0, lsem1, ssem0, ssem1):
        worker = lax.axis_index("subcore") * info.num_cores + lax.axis_index("core")
        src_base = lax.rem(worker * per_worker, n_src)
        pltpu.sync_copy(idx_hbm.at[worker], idx_vmem)

        def load(j, rows, sem):
            return pltpu.make_async_copy(src_hbm.at[pl.ds(src_base + j * win, win)], rows, sem)

        def store(j, rows, sem):
            return pltpu.make_async_copy(rows, out_hbm.at[idx_vmem.at[j]], sem)

        load(0, rows0, lsem0).start()

        @pl.loop(0, n_steps, step=2)
        def _(j):
            load(j, rows0, lsem0).wait()

            @pl.when(j > 0)
            def _():
                store(j - 1, rows1, ssem1).wait()

            load(j + 1, rows1, lsem1).start()
            store(j, rows0, ssem0).start()
            load(j + 1, rows1, lsem1).wait()
            store(j, rows0, ssem0).wait()

            @pl.when(j + 2 < n_steps)
            def _():
                load(j + 2, rows0, lsem0).start()

            store(j + 1, rows1, ssem1).start()

        store(n_steps - 1, rows1, ssem1).wait()

    return scatter(src, idx.reshape(n_workers, n_steps, win))


def _expert_kernel(blk_e_ref, n_used_ref, n_valid_ref, x_ref, wg_ref, wu_ref, wd_ref, y_ref, wg_bf, wu_bf, wd_bf):
    i = pl.program_id(0)
    used = i < n_used_ref[0]
    new_expert = (i == 0) | (blk_e_ref[i] != blk_e_ref[jnp.maximum(i - 1, 0)])

    @pl.when(used & new_expert)
    def _():
        wg_bf[...] = wg_ref[0].astype(BF16)
        wu_bf[...] = wu_ref[0].astype(BF16)
        wd_bf[...] = wd_ref[0].astype(BF16)

    @pl.when(used)
    def _():
        row = lax.broadcasted_iota(jnp.int32, (x_ref.shape[0], 1), 0)
        x = _unpack_bf16_pairs(jnp.where(row < n_valid_ref[i], x_ref[...], 0)).astype(BF16)
        g = _dot(x, wg_bf[...])
        u = _dot(x, wu_bf[...])
        a = (g * jax.nn.sigmoid(g) * u).astype(BF16)
        y_ref[...] = _pack_bf16_pairs(_dot(a, wd_bf[...]))

    @pl.when(jnp.logical_not(used))
    def _():
        y_ref[...] = jnp.zeros_like(y_ref)


def expert_ffn(x_disp, blk_e, n_used, n_valid, w_gate, w_up, w_down, layer):
    rows, half = x_disp.shape
    d = 2 * half
    n_blk = rows // MOE_BLOCK
    de = w_gate.shape[3]
    grid_spec = pltpu.PrefetchScalarGridSpec(
        num_scalar_prefetch=3,
        grid=(n_blk,),
        in_specs=[pl.BlockSpec((MOE_BLOCK, half), lambda i, be, nu, nv: (i, 0)),
                  pl.BlockSpec((None, 1, d, de), lambda i, be, nu, nv: (layer, be[i], 0, 0)),
                  pl.BlockSpec((None, 1, d, de), lambda i, be, nu, nv: (layer, be[i], 0, 0)),
                  pl.BlockSpec((None, 1, de, d), lambda i, be, nu, nv: (layer, be[i], 0, 0))],
        out_specs=pl.BlockSpec((MOE_BLOCK, half), lambda i, be, nu, nv: (i, 0)),
        scratch_shapes=[pltpu.VMEM((d, de), BF16), pltpu.VMEM((d, de), BF16), pltpu.VMEM((de, d), BF16)],
    )
    return pl.pallas_call(
        _expert_kernel,
        grid_spec=grid_spec,
        out_shape=jax.ShapeDtypeStruct((rows, half), jnp.int32),
        compiler_params=_cparams(("arbitrary",)),
        name="expert_ffn",
    )(blk_e, n_used, n_valid, x_disp, w_gate, w_up, w_down)


def _moe_residual(h_ref, y0_ref, y1_ref, wt_ref, mod_ref, gate_idx):
    wt = wt_ref[0]
    moe = wt[:, 0:1] * _unpack_bf16_pairs(y0_ref[0, 0]) + wt[:, 1:2] * _unpack_bf16_pairs(y1_ref[0, 0])
    return h_ref[0] + mod_ref[0, gate_idx:gate_idx + 1, :] * moe


def _combine_next_kernel(h_ref, y0_ref, y1_ref, wt_ref, mod_ref, g_ref, nmod_ref, o_ref, hm_ref, *, gate_idx):
    h_new = _moe_residual(h_ref, y0_ref, y1_ref, wt_ref, mod_ref, gate_idx)
    o_ref[0] = h_new
    hm_ref[0] = _rms_mod(h_new, g_ref[...], nmod_ref[0, 0:1, :], nmod_ref[0, 1:2, :]).astype(hm_ref.dtype)


def _combine_final_kernel(h_ref, y0_ref, y1_ref, wt_ref, mod_ref, g_ref, o_ref, *, gate_idx):
    x = _moe_residual(h_ref, y0_ref, y1_ref, wt_ref, mod_ref, gate_idx)
    ms = jnp.mean(x * x, axis=-1, keepdims=True)
    o_ref[0] = x * lax.rsqrt(ms + NORM_EPS) * g_ref[...]


def combine(h, y_pairs, wt, modtab, gate_idx, next_g, next_modtab, ctx_len):
    b, p, d = h.shape
    last = next_modtab is None
    skip = ctx_len // ROW_TILE if last else 0
    row = pl.BlockSpec((1, ROW_TILE, d), lambda b, j: (b, j + skip, 0))
    mod = pl.BlockSpec((1, N_MOD, d), lambda b, j: (2 * b + jnp.minimum(j + skip, 1), 0, 0))
    in_specs = [row,
                pl.BlockSpec((1, 1, ROW_TILE, d // 2), lambda b, j: (0, b, j + skip, 0)),
                pl.BlockSpec((1, 1, ROW_TILE, d // 2), lambda b, j: (1, b, j + skip, 0)),
                pl.BlockSpec((1, ROW_TILE, LANES), lambda b, j: (b, j + skip, 0)),
                mod, pl.BlockSpec((1, d), lambda b, j: (0, 0))]
    args = [h, y_pairs, y_pairs, wt, modtab.reshape(b * 2, N_MOD, d), next_g.reshape(1, d)]
    out_row = pl.BlockSpec((1, ROW_TILE, d), lambda b, j: (b, j, 0))
    if last:
        return pl.pallas_call(
            functools.partial(_combine_final_kernel, gate_idx=gate_idx),
            grid=(b, (p - ctx_len) // ROW_TILE),
            in_specs=in_specs,
            out_specs=out_row,
            out_shape=jax.ShapeDtypeStruct((b, p - ctx_len, d), F32),
            compiler_params=_cparams(("parallel", "parallel")),
            name="moe_combine_final",
        )(*args)
    return pl.pallas_call(
        functools.partial(_combine_next_kernel, gate_idx=gate_idx),
        grid=(b, p // ROW_TILE),
        in_specs=in_specs + [mod],
        out_specs=[out_row, out_row],
        out_shape=[jax.ShapeDtypeStruct((b, p, d), F32), jax.ShapeDtypeStruct((b, p, d), BF16)],
        compiler_params=_cparams(("parallel", "parallel")),
        name="moe_combine",
    )(*args, next_modtab.reshape(b * 2, N_MOD, d))


def hier_moe(h, f, route, wt, counts, modtab, gate_idx, w_gate, w_up, w_down, layer, next_g, next_modtab, ctx_len):
    b, p, d = h.shape
    n_tok = b * p
    n_assign = 2 * n_tok
    eid_f = route[:, :, 0:2].reshape(n_assign)
    rank_f = route[:, :, 2:4].reshape(n_assign)
    cnt = counts[0, :N_EXPERTS].astype(jnp.int32)
    pcounts = (cnt + MOE_BLOCK - 1) // MOE_BLOCK * MOE_BLOCK
    pend = jnp.cumsum(pcounts)
    pstart = pend - pcounts
    dest = (pstart[eid_f] + rank_f).astype(jnp.int32)
    n_blk = -(-n_assign // MOE_BLOCK) + N_EXPERTS
    rows_total = n_blk * MOE_BLOCK
    blk_row0 = jnp.arange(n_blk, dtype=jnp.int32) * MOE_BLOCK
    blk_e = jnp.minimum(jnp.sum((pend[None, :] <= blk_row0[:, None]).astype(jnp.int32), axis=1), N_EXPERTS - 1)
    n_valid = jnp.clip(cnt[blk_e] - (blk_row0 - pstart[blk_e]), 0, MOE_BLOCK).astype(jnp.int32)
    n_used = (pend[-1] // MOE_BLOCK).astype(jnp.int32).reshape(1)

    dest_by_slot = jnp.concatenate([dest[0::2], dest[1::2]])
    x_disp = scatter_rows(f.reshape(n_tok, d // 2), dest_by_slot, rows_total)
    y = expert_ffn(x_disp, blk_e, n_used, n_valid, w_gate, w_up, w_down, layer)
    y_pairs = gather_rows(y, dest_by_slot).reshape(2, b, p, d // 2)
    return combine(h, y_pairs, wt, modtab, gate_idx, next_g, next_modtab, ctx_len)


def kernel(x, c, ctx, c_ctx, mod_w, mod_b, norm_mix_g, norm_ffn_g, router_grp_w, router_grp_b, router_exp_w, router_exp_b, exp_w_gate, exp_w_up, exp_w_down, l0_na_w_qkv, l0_na_rpb, l0_na_w_o, l1_swa_w_qkv, l1_swa_sink, l1_swa_w_o, l2_mla_w_dq, l2_mla_q_norm_g, l2_mla_w_uq, l2_mla_w_dkv, l2_mla_kv_norm_g, l2_mla_w_ukv, l2_mla_w_o, l3_na_w_qkv, l3_na_rpb, l3_na_w_o, final_norm_g):
    b, s, d = x.shape
    lc = ctx.shape[1]
    n_heads = d // HEAD_DIM
    n_kv_heads = n_heads // 4
    depth = mod_w.shape[0]
    rows = s // GRID_W

    h = jnp.concatenate([ctx, x], axis=1)
    modtabs = modulation_tables(c, c_ctx, mod_w, mod_b)

    def scale_q_cols(w):
        n_q = n_heads * HEAD_DIM
        q_scale = HEAD_DIM ** -0.5 * LOG2E
        return jnp.concatenate([w[:, :n_q] * q_scale, w[:, n_q:]], axis=1).astype(BF16)

    def na_mixer(hm, w_qkv, rpb):
        qkv = project(hm, scale_q_cols(w_qkv))
        return neighbourhood_attention(qkv, na_bias_table(rpb, rows), lc, n_heads)

    def swa_mixer(hm):
        cos, sin = rope_tables_full(s, lc)
        n_rope = (n_heads + n_kv_heads) * HEAD_DIM // 512
        qkv = project(hm, scale_q_cols(l1_swa_w_qkv), tn=512, rope=(cos, sin, 0, n_rope))
        return window_attention(qkv, l1_swa_sink, lc, n_heads, n_kv_heads)

    def mla_mixer(hm):
        q_rank = l2_mla_w_dq.shape[1]
        kv_rank = l2_mla_kv_norm_g.shape[0]
        cos, sin = rope_tables_mla(s, lc)
        w_down = jnp.concatenate([l2_mla_w_dq, l2_mla_w_dkv[:, :kv_rank],
                                  _spread_rope_cols(l2_mla_w_dkv[:, kv_rank:])], axis=1).astype(BF16)
        cq, ckv, kr = mla_down(hm, w_down, l2_mla_q_norm_g, l2_mla_kv_norm_g, cos, sin, q_rank, kv_rank)
        w_uq = l2_mla_w_uq.reshape(q_rank, n_heads, MLA_NOPE_DIM + MLA_ROPE_DIM)
        w_q_nope = w_uq[:, :, :MLA_NOPE_DIM].reshape(q_rank, n_heads * MLA_NOPE_DIM)
        w_q_rope = jnp.concatenate([_spread_rope_cols(w_uq[:, hh, MLA_NOPE_DIM:]) for hh in range(n_heads)], axis=1)
        q_scale = (MLA_NOPE_DIM + MLA_ROPE_DIM) ** -0.5 * LOG2E
        w_q = (jnp.concatenate([w_q_nope, w_q_rope], axis=1) * q_scale).astype(BF16)
        n_nope_tiles = n_heads * MLA_NOPE_DIM // 512
        q = project(cq, w_q, tn=512, rope=(cos, sin, n_nope_tiles, 2 * n_nope_tiles))
        kv = project(ckv, l2_mla_w_ukv.astype(BF16), tn=512)
        return latent_attention(q, kv, kr, lc, n_heads)

    hm = norm_modulate(h, norm_mix_g[0], modtabs[0], 0)
    for i in range(depth):
        modtab = modtabs[i]
        mixer = i % 3
        if mixer == 0:
            w_qkv, rpb, w_o = (l0_na_w_qkv, l0_na_rpb, l0_na_w_o) if i == 0 else (l3_na_w_qkv, l3_na_rpb, l3_na_w_o)
            y = na_mixer(hm, w_qkv, rpb)
        elif mixer == 1:
            y, w_o = swa_mixer(hm), l1_swa_w_o
        else:
            y, w_o = mla_mixer(hm), l2_mla_w_o
        h, f, route, wt, counts = attn_out_route(y, w_o.astype(BF16), h, modtab, norm_ffn_g[i], router_grp_w[i],
                                                 router_grp_b[i], router_exp_w[i], router_exp_b[i], lc)
        if i + 1 < depth:
            h, hm = hier_moe(h, f, route, wt, counts, modtab, 5, exp_w_gate, exp_w_up, exp_w_down, i,
                             norm_mix_g[i + 1], modtabs[i + 1], lc)
        else:
            return hier_moe(h, f, route, wt, counts, modtab, 5, exp_w_gate, exp_w_up, exp_w_down, i,
                            final_norm_g, None, lc)
```

```python
import functools

import numpy as np
import jax
import jax.numpy as jnp
from jax import lax
from jax.experimental import pallas as pl
from jax.experimental.pallas import tpu as pltpu
from jax.experimental.pallas import tpu_sc as plsc

GRID_W = 64
HEAD_DIM = 128
ROPE_BASE = 10000.0
NORM_EPS = 1e-6
NEG_INF = -1e30
N_MOD = 6

NA_ROWS = 8
NA_COLS = 16
NA_TILE_ROWS = 4
NA_WIN_ROWS = 12

SWA_WINDOW = 128
SWA_TQ = 256
SWA_TK = 512

MLA_NOPE_DIM = 128
MLA_ROPE_DIM = 64
MLA_TQ = 512
MLA_TK = 512
LOG2E = 1.4426950408889634

N_GROUPS = 4
EXPERTS_PER_GROUP = 8
N_EXPERTS = N_GROUPS * EXPERTS_PER_GROUP
MOE_BLOCK = 512
GATHER_WINDOW = 32

LANES = 128
ROW_TILE = 256
VMEM_LIMIT = 56 * 1024 * 1024

BF16 = jnp.bfloat16
F32 = jnp.float32


def _cparams(sem):
    return pltpu.CompilerParams(dimension_semantics=sem, vmem_limit_bytes=VMEM_LIMIT)


def _dot(a, b):
    return jnp.dot(a, b, preferred_element_type=F32)


def _dot_nt(a, b):
    return lax.dot_general(a, b, (((1,), (1,)), ((), ())), preferred_element_type=F32)


def _mod_kernel(x_ref, w_ref, b_ref, o_ref):
    x = x_ref[...]
    sx = (x * jax.nn.sigmoid(x)).astype(BF16)
    o_ref[0] = _dot(sx, w_ref[0].astype(BF16)) + b_ref[0]


def modulation_tables(c, c_ctx, mod_w, mod_b):
    depth, d, n_out = mod_w.shape
    b = c.shape[0]
    rows = 16
    xin = jnp.zeros((rows, d), F32).at[:b].set(c).at[b].set(c_ctx)
    tn = 1024
    out = pl.pallas_call(
        _mod_kernel,
        grid=(depth, n_out // tn),
        in_specs=[pl.BlockSpec((rows, d), lambda i, j: (0, 0)),
                  pl.BlockSpec((1, d, tn), lambda i, j: (i, 0, j)),
                  pl.BlockSpec((1, 1, tn), lambda i, j: (i, 0, j))],
        out_specs=pl.BlockSpec((1, rows, tn), lambda i, j: (i, 0, j)),
        out_shape=jax.ShapeDtypeStruct((depth, rows, n_out), F32),
        compiler_params=_cparams(("parallel", "parallel")),
        name="adaln_mod",
    )(xin, mod_w, mod_b.reshape(depth, 1, n_out))
    lat = out[:, :b].reshape(depth, b, 1, N_MOD, d)
    ctx = jnp.broadcast_to(out[:, b].reshape(depth, 1, 1, N_MOD, d), (depth, b, 1, N_MOD, d))
    return jnp.concatenate([ctx, lat], axis=2)


def _rms_mod(x, g, shift, scale):
    ms = jnp.mean(x * x, axis=-1, keepdims=True)
    y = x * lax.rsqrt(ms + NORM_EPS) * g
    return y * (1.0 + scale) + shift


def _norm_mod_kernel(h_ref, g_ref, mod_ref, o_ref, *, shift_idx):
    f = _rms_mod(h_ref[0], g_ref[...], mod_ref[0, shift_idx:shift_idx + 1, :],
                 mod_ref[0, shift_idx + 1:shift_idx + 2, :])
    o_ref[0] = f.astype(o_ref.dtype)


def _route(logits):
    lane = lax.broadcasted_iota(jnp.int32, logits.shape, 1).astype(F32)
    big = float(LANES)

    def first_lane(mask):
        return jnp.min(jnp.where(mask, lane, big), axis=-1, keepdims=True)

    in_grp = lane < N_GROUPS
    lg = jnp.where(in_grp, logits, NEG_INF)
    m_g = jnp.max(lg, axis=-1, keepdims=True)
    g_idx = first_lane(in_grp & (lg == m_g))
    g_w = 1.0 / jnp.sum(jnp.where(in_grp, jnp.exp(lg - m_g), 0.0), axis=-1, keepdims=True)
    e_lo = N_GROUPS + g_idx * EXPERTS_PER_GROUP
    in_e = (lane >= e_lo) & (lane < e_lo + EXPERTS_PER_GROUP)
    le = jnp.where(in_e, logits, NEG_INF)
    m1 = jnp.max(le, axis=-1, keepdims=True)
    e1 = first_lane(in_e & (le == m1))
    s_e = jnp.sum(jnp.where(in_e, jnp.exp(le - m1), 0.0), axis=-1, keepdims=True)
    in_e2 = in_e & (lane != e1)
    le2 = jnp.where(in_e2, logits, NEG_INF)
    m2 = jnp.max(le2, axis=-1, keepdims=True)
    e2 = first_lane(in_e2 & (le2 == m2))
    p1 = 1.0 / s_e
    p2 = jnp.exp(m2 - m1) / s_e
    den = p1 + p2
    return ((e1 - N_GROUPS).astype(jnp.int32), (e2 - N_GROUPS).astype(jnp.int32),
            g_w * p1 / den, g_w * p2 / den)


def _pack_bf16_pairs(x):
    n = x.shape[1] // 2
    xb = x.astype(BF16).astype(F32)
    hi = lax.bitcast_convert_type(xb[:, :n], jnp.int32)
    lo = lax.bitcast_convert_type(xb[:, n:], jnp.int32)
    return (hi & jnp.int32(-65536)) | lax.shift_right_logical(lo, jnp.int32(16))


def _unpack_bf16_pairs(w):
    hi = lax.bitcast_convert_type(w & jnp.int32(-65536), F32)
    lo = lax.bitcast_convert_type(lax.shift_left(w, jnp.int32(16)), F32)
    return jnp.concatenate([hi, lo], axis=1)


def _route_and_rank(f, wr_ref, br_ref, run_ref):
    f_hi = f.astype(BF16)
    f_lo = (f - f_hi.astype(F32)).astype(BF16)
    hi_terms = _dot(f_hi, wr_ref[...])
    logits = (hi_terms[:, :LANES] + _dot(f_lo, wr_ref[:, :LANES]) + hi_terms[:, LANES:]) + br_ref[...]
    e1, e2, w1, w2 = _route(logits)
    lane = lax.broadcasted_iota(jnp.int32, logits.shape, 1)
    pick1, pick2 = lane == e1, lane == e2
    chosen = (pick1 | pick2).astype(F32)
    n_rows = chosen.shape[0]
    earlier = (lax.broadcasted_iota(jnp.int32, (n_rows, n_rows), 1)
               < lax.broadcasted_iota(jnp.int32, (n_rows, n_rows), 0)).astype(BF16)
    before = run_ref[...] + _dot(earlier, chosen.astype(BF16))
    rank1 = jnp.sum(jnp.where(pick1, before, 0.0), axis=-1, keepdims=True).astype(jnp.int32)
    rank2 = jnp.sum(jnp.where(pick2, before, 0.0), axis=-1, keepdims=True).astype(jnp.int32)
    run_ref[...] = run_ref[...] + jnp.sum(chosen, axis=0, keepdims=True)
    route = jnp.where(lane == 0, e1, jnp.where(lane == 1, e2, jnp.where(lane == 2, rank1,
                                                                         jnp.where(lane == 3, rank2, 0))))
    return route, jnp.where(lane == 0, w1, jnp.where(lane == 1, w2, 0.0))


def _attn_out_route_kernel(y_ref, w_ref, h_ref, mod_ref, g_ref, wr_ref, br_ref,
                           ho_ref, f_ref, route_ref, wt_ref, cnt_ref, run_ref, *, ctx_len, tm, n_sub):
    first = (pl.program_id(0) == 0) & (pl.program_id(1) == 0)

    @pl.when(first)
    def _():
        run_ref[...] = jnp.zeros_like(run_ref)

    sub = tm // n_sub
    for s in range(n_sub):
        rows = slice(s * sub, (s + 1) * sub)
        acc = _dot(y_ref[0, rows, :], w_ref[...])
        pos = pl.program_id(1) * tm + s * sub + lax.broadcasted_iota(jnp.int32, (sub, 1), 0)
        is_ctx = pos < ctx_len

        def mod_row(k):
            return jnp.where(is_ctx, mod_ref[0, 0, k:k + 1, :], mod_ref[0, 1, k:k + 1, :])

        h_new = h_ref[0, rows, :] + mod_row(2) * acc
        ho_ref[0, rows, :] = h_new
        f = _rms_mod(h_new, g_ref[...], mod_row(3), mod_row(4))
        f_ref[0, rows, :] = _pack_bf16_pairs(f)
        route, wt = _route_and_rank(f, wr_ref, br_ref, run_ref)
        route_ref[0, rows, :] = route
        wt_ref[0, rows, :] = wt
    cnt_ref[...] = run_ref[...]


def _mod_spec(d):
    return pl.BlockSpec((1, N_MOD, d), lambda b, j: (2 * b + jnp.minimum(j, 1), 0, 0))


def norm_modulate(h, g, modtab, shift_idx):
    b, p, d = h.shape
    row = pl.BlockSpec((1, ROW_TILE, d), lambda b, j: (b, j, 0))
    return pl.pallas_call(
        functools.partial(_norm_mod_kernel, shift_idx=shift_idx),
        grid=(b, p // ROW_TILE),
        in_specs=[row, pl.BlockSpec((1, d), lambda b, j: (0, 0)), _mod_spec(d)],
        out_specs=row,
        out_shape=jax.ShapeDtypeStruct((b, p, d), BF16),
        compiler_params=_cparams(("parallel", "parallel")),
        name="norm_mod",
    )(h, g.reshape(1, d), modtab.reshape(b * 2, N_MOD, d))


def attn_out_route(y, w_o, h, modtab, g, w_grp, b_grp, w_rt, b_rt, ctx_len):
    b, p, d = h.shape
    n_r = N_GROUPS + N_EXPERTS
    wr = jnp.zeros((d, LANES), F32).at[:, :N_GROUPS].set(w_grp).at[:, N_GROUPS:n_r].set(w_rt)
    br = jnp.zeros((1, LANES), F32).at[0, :N_GROUPS].set(b_grp).at[0, N_GROUPS:n_r].set(b_rt)
    wr_hi = wr.astype(BF16)
    wr = jnp.concatenate([wr_hi, (wr - wr_hi.astype(F32)).astype(BF16)], axis=1)
    tm = 544 if p % 544 == 0 else ROW_TILE
    n_sub = 2
    assert (tm // n_sub) % 16 == 0
    row = lambda width: pl.BlockSpec((1, tm, width), lambda b, i: (b, i, 0))
    once = lambda shape: pl.BlockSpec(shape, lambda b, i: (0,) * len(shape), pipeline_mode=pl.Buffered(1))
    return pl.pallas_call(
        functools.partial(_attn_out_route_kernel, ctx_len=ctx_len, tm=tm, n_sub=n_sub),
        grid=(b, p // tm),
        in_specs=[row(d), once((d, d)), row(d),
                  pl.BlockSpec((1, 2, N_MOD, d), lambda b, i: (b, 0, 0, 0)),
                  once((1, d)), once((d, 2 * LANES)), once((1, LANES))],
        out_specs=[row(d), row(d // 2), row(LANES), row(LANES), pl.BlockSpec((1, LANES), lambda b, i: (0, 0))],
        out_shape=[jax.ShapeDtypeStruct((b, p, d), F32),
                   jax.ShapeDtypeStruct((b, p, d // 2), jnp.int32),
                   jax.ShapeDtypeStruct((b, p, LANES), jnp.int32),
                   jax.ShapeDtypeStruct((b, p, LANES), F32),
                   jax.ShapeDtypeStruct((1, LANES), F32)],
        scratch_shapes=[pltpu.VMEM((1, LANES), F32)],
        compiler_params=_cparams(("arbitrary", "arbitrary")),
        name="attn_out_route",
    )(y, w_o, h, modtab, g.reshape(1, d), wr, br)


def _rope(acc, cos, sin):
    n_blk = acc.shape[1] // LANES
    outs = []
    for c in range(n_blk):
        x = acc[:, c * LANES:(c + 1) * LANES]
        outs.append(x * cos + pltpu.roll(x, LANES // 2, 1) * sin)
    return outs[0] if n_blk == 1 else jnp.concatenate(outs, axis=1)


def _proj_kernel(*refs, rope_lo, rope_hi):
    x_ref, w_ref = refs[0], refs[1]
    o_ref = refs[-1]
    acc = _dot(x_ref[0], w_ref[...])
    if rope_hi > rope_lo:
        cos_ref, sin_ref = refs[2], refs[3]
        j = pl.program_id(2)
        roped = (j >= rope_lo) & (j < rope_hi)

        @pl.when(roped)
        def _():
            o_ref[0] = _rope(acc, cos_ref[...], sin_ref[...]).astype(o_ref.dtype)

        @pl.when(jnp.logical_not(roped))
        def _():
            o_ref[0] = acc.astype(o_ref.dtype)
    else:
        o_ref[0] = acc.astype(o_ref.dtype)


def _row_tile(p):
    for cand in (1088, 1024, 544, 512, 272, 256, 128, 64, 32, 16):
        if p % cand == 0:
            return cand
    raise ValueError(p)


def project(x, w, *, tn=512, out_dtype=BF16, rope=None):
    b, p, k = x.shape
    n = w.shape[1]
    tm = _row_tile(p)
    tn = min(tn, n)
    assert n % tn == 0
    in_specs = [pl.BlockSpec((1, tm, k), lambda b, i, j: (b, i, 0)),
                pl.BlockSpec((k, tn), lambda b, i, j: (0, j))]
    args = [x, w]
    kw = dict(rope_lo=0, rope_hi=0)
    if rope is not None:
        cos, sin, lo, hi = rope
        in_specs += [pl.BlockSpec((tm, LANES), lambda b, i, j: (i, 0)),
                     pl.BlockSpec((tm, LANES), lambda b, i, j: (i, 0))]
        args += [cos, sin]
        kw.update(rope_lo=lo, rope_hi=hi)
    return pl.pallas_call(
        functools.partial(_proj_kernel, **kw),
        grid=(b, p // tm, n // tn),
        in_specs=in_specs,
        out_specs=pl.BlockSpec((1, tm, tn), lambda b, i, j: (b, i, j)),
        out_shape=jax.ShapeDtypeStruct((b, p, n), out_dtype),
        compiler_params=_cparams(("parallel", "parallel", "arbitrary")),
        name="project",
    )(*args)


def _axial_cos_sin(n, rot_dim):
    t = jnp.arange(n, dtype=jnp.int32)
    row = (t // GRID_W).astype(F32)
    col = (t % GRID_W).astype(F32)
    n_freq = rot_dim // 4
    inv = ROPE_BASE ** (-jnp.arange(n_freq, dtype=F32) / n_freq)
    ang = jnp.concatenate([row[:, None] * inv, col[:, None] * inv], axis=-1)
    return jnp.cos(ang), jnp.sin(ang)


def rope_tables_full(s, ctx_len):
    c, sn = _axial_cos_sin(s, HEAD_DIM)
    cos = jnp.concatenate([c, c], axis=1)
    sin = jnp.concatenate([-sn, sn], axis=1)
    ident_c = jnp.ones((ctx_len, LANES), F32)
    ident_s = jnp.zeros((ctx_len, LANES), F32)
    return jnp.concatenate([ident_c, cos], axis=0), jnp.concatenate([ident_s, sin], axis=0)


def rope_tables_mla(s, ctx_len):
    c, sn = _axial_cos_sin(s, MLA_ROPE_DIM)
    one = jnp.ones_like(c)
    zero = jnp.zeros_like(c)
    cos = jnp.concatenate([c, one, c, one], axis=1)
    sin = jnp.concatenate([-sn, zero, sn, zero], axis=1)
    ident_c = jnp.ones((ctx_len, LANES), F32)
    ident_s = jnp.zeros((ctx_len, LANES), F32)
    return jnp.concatenate([ident_c, cos], axis=0), jnp.concatenate([ident_s, sin], axis=0)


def _spread_rope_cols(w_rope):
    k = w_rope.shape[0]
    half = MLA_ROPE_DIM // 2
    z = jnp.zeros((k, half), w_rope.dtype)
    return jnp.concatenate([w_rope[:, :half], z, w_rope[:, half:], z], axis=1)


def _softmax_parts(parts, extra=None):
    m = _row_reduce(jnp.maximum, jnp.max, parts)
    if extra is not None:
        m = jnp.maximum(m, extra)
    ps = [jnp.exp2(s - m) for s in parts]
    den = _row_reduce(jnp.add, jnp.sum, ps)
    if extra is not None:
        den = den + jnp.exp2(extra - m)
    return ps, 1.0 / den


def _row_reduce(combine, reduce, parts):
    blocks = [s[:, c:c + LANES] for s in parts for c in range(0, s.shape[1], LANES)]
    acc = blocks[0]
    for blk in blocks[1:]:
        acc = combine(acc, blk)
    return reduce(acc, axis=-1, keepdims=True)


def na_bias_table(rpb, rows):
    n_tiles = rows // NA_TILE_ROWS
    n_heads, _, n_dcol = rpb.shape
    drow, row_ok = [], []
    for tile in (0, 1, n_tiles - 1):
        kr0 = int(np.clip(NA_TILE_ROWS * tile - NA_ROWS // 2, 0, rows - NA_WIN_ROWS))
        r = NA_TILE_ROWS * tile + np.arange(NA_TILE_ROWS)
        r0 = np.clip(r - NA_ROWS // 2, 0, rows - NA_ROWS)
        krow = kr0 + np.arange(NA_WIN_ROWS)
        row_ok.append((krow[None, :] >= r0[:, None]) & (krow[None, :] < r0[:, None] + NA_ROWS))
        drow.append(np.clip(krow[None, :] - r[:, None] + NA_ROWS - 1, 0, 2 * NA_ROWS - 2))
    drow, row_ok = np.stack(drow), np.stack(row_ok)
    qc = np.arange(GRID_W)
    qcol0 = np.clip(qc - NA_COLS // 2, 0, GRID_W - NA_COLS)
    kc = np.arange(GRID_W)
    col_ok = (kc[None, :] >= qcol0[:, None]) & (kc[None, :] < qcol0[:, None] + NA_COLS)
    dcol = np.clip(kc[None, :] - qc[:, None] + NA_COLS - 1, 0, 2 * NA_COLS - 2)
    by_row = rpb.astype(F32)[:, drow, :]
    pick_col = jnp.asarray(dcol[None] == np.arange(n_dcol)[:, None, None], F32)
    vals = jnp.einsum('hpamd,dqk->hpaqmk', by_row, pick_col, precision=lax.Precision.HIGHEST)
    ok = row_ok[None, :, :, None, :, None] & col_ok[None, None, None, :, None, :]
    vals = jnp.where(jnp.asarray(ok), vals * LOG2E, NEG_INF)
    return vals.reshape(n_heads, 3, NA_TILE_ROWS * GRID_W, NA_WIN_ROWS * GRID_W)


def _na_kernel(q_ref, k_ref, v_ref, bias_ref, o_ref, *, ctx_len, rows):
    tq = NA_TILE_ROWS * GRID_W
    tk = NA_WIN_ROWS * GRID_W
    n_tiles = rows // NA_TILE_ROWS
    lc = ctx_len

    s = _dot_nt(q_ref[0, 0:lc, :], k_ref[0, 0:lc, :])
    (p,), inv = _softmax_parts([s])
    o_ref[0, 0:lc, :] = (_dot(p.astype(BF16), v_ref[0, 0:lc, :]) * inv).astype(o_ref.dtype)

    def tile(i, carry):
        qs = pl.multiple_of(lc + i * tq, tq)
        kr0 = jnp.clip(NA_TILE_ROWS * i - NA_ROWS // 2, 0, rows - NA_WIN_ROWS)
        ks = pl.multiple_of(lc + kr0 * GRID_W, NA_TILE_ROWS * GRID_W)
        pat = jnp.where(i == 0, 0, jnp.where(i == n_tiles - 1, 2, 1))
        q = q_ref[0, pl.ds(qs, tq), :]
        s_loc = _dot_nt(q, k_ref[0, pl.ds(ks, tk), :]) + bias_ref[0, pat]
        s_ctx = _dot_nt(q, k_ref[0, 0:lc, :])
        (p_loc, p_ctx), inv = _softmax_parts([s_loc, s_ctx])
        o = _dot(p_loc.astype(BF16), v_ref[0, pl.ds(ks, tk), :]) + _dot(p_ctx.astype(BF16), v_ref[0, 0:lc, :])
        o_ref[0, pl.ds(qs, tq), :] = (o * inv).astype(o_ref.dtype)
        return carry

    lax.fori_loop(0, n_tiles, tile, 0, unroll=4)


def neighbourhood_attention(qkv, bias, ctx_len, n_heads):
    b, p, _ = qkv.shape
    rows = (p - ctx_len) // GRID_W
    assert rows % NA_TILE_ROWS == 0 and rows >= NA_WIN_ROWS
    tq, tk = NA_TILE_ROWS * GRID_W, NA_WIN_ROWS * GRID_W
    assert ctx_len % 16 == 0 and ctx_len % tq == 0
    blk = lambda off: pl.BlockSpec((1, p, HEAD_DIM), lambda h, b: (b, 0, off + h))
    return pl.pallas_call(
        functools.partial(_na_kernel, ctx_len=ctx_len, rows=rows),
        grid=(n_heads, b),
        in_specs=[blk(0), blk(n_heads), blk(2 * n_heads),
                  pl.BlockSpec((1, 3, tq, tk), lambda h, b: (h, 0, 0, 0))],
        out_specs=blk(0),
        out_shape=jax.ShapeDtypeStruct((b, p, n_heads * HEAD_DIM), BF16),
        compiler_params=_cparams(("parallel", "parallel")),
        name="na_attention",
    )(qkv, qkv, qkv, bias)


def _swa_kernel(sink_ref, q_ref, k_ref, v_ref, o_ref, *, ctx_len, seq, group):
    lc = ctx_len
    kvh = pl.program_id(1)

    def stack_heads(q):
        return jnp.concatenate([q[:, g * HEAD_DIM:(g + 1) * HEAD_DIM] for g in range(group)], axis=0)

    def finish(parts_fn, n_q, pv_fn, store):
        ps_all, invs = [], []
        for g in range(group):
            sink = sink_ref[kvh * group + g]
            ps, inv = _softmax_parts(parts_fn(g), extra=sink)
            ps_all.append(ps)
            invs.append(inv)
        n_parts = len(ps_all[0])
        stacked = [jnp.concatenate([ps_all[g][k] for g in range(group)], axis=0).astype(BF16)
                   for k in range(n_parts)]
        o = pv_fn(stacked)
        store(jnp.concatenate([o[g * n_q:(g + 1) * n_q] * invs[g] for g in range(group)], axis=1))

    s_c = _dot_nt(stack_heads(q_ref[0, 0:lc, :]), k_ref[0, 0:lc, :])

    def store_ctx(o):
        o_ref[0, 0:lc, :] = o.astype(o_ref.dtype)

    finish(lambda g: [s_c[g * lc:(g + 1) * lc]], lc,
           lambda st: _dot(st[0], v_ref[0, 0:lc, :]), store_ctx)

    n_tiles = seq // SWA_TQ

    def tile(t, carry):
        q0 = t * SWA_TQ
        k0 = jnp.clip(q0 - SWA_WINDOW, 0, seq - SWA_TK)
        qs = pl.multiple_of(lc + q0, SWA_WINDOW)
        ks = pl.multiple_of(lc + k0, SWA_WINDOW)
        q4 = stack_heads(q_ref[0, pl.ds(qs, SWA_TQ), :])
        s_loc = _dot_nt(q4, k_ref[0, pl.ds(ks, SWA_TK), :])
        s_ctx = _dot_nt(q4, k_ref[0, 0:lc, :])
        dpos = (lax.broadcasted_iota(jnp.int32, (SWA_TQ, SWA_TK), 1)
                - lax.broadcasted_iota(jnp.int32, (SWA_TQ, SWA_TK), 0)) + (k0 - q0)
        valid = jnp.abs(dpos) <= SWA_WINDOW

        def parts(g):
            sl = slice(g * SWA_TQ, (g + 1) * SWA_TQ)
            return [jnp.where(valid, s_loc[sl], NEG_INF), s_ctx[sl]]

        def store(o):
            o_ref[0, pl.ds(qs, SWA_TQ), :] = o.astype(o_ref.dtype)

        finish(parts, SWA_TQ,
               lambda st: _dot(st[0], v_ref[0, pl.ds(ks, SWA_TK), :]) + _dot(st[1], v_ref[0, 0:lc, :]),
               store)
        return carry

    lax.fori_loop(0, n_tiles, tile, 0, unroll=2)


def window_attention(qkv, sink, ctx_len, n_heads, n_kv_heads):
    b, p, _ = qkv.shape
    seq = p - ctx_len
    group = n_heads // n_kv_heads
    assert seq % SWA_TQ == 0 and seq >= SWA_TK and ctx_len % SWA_WINDOW == 0
    kv = lambda off: pl.BlockSpec((1, p, HEAD_DIM), lambda b, h: (b, 0, off + h))
    qo = pl.BlockSpec((1, p, group * HEAD_DIM), lambda b, h: (b, 0, h))
    return pl.pallas_call(
        functools.partial(_swa_kernel, ctx_len=ctx_len, seq=seq, group=group),
        grid=(b, n_kv_heads),
        in_specs=[pl.BlockSpec(memory_space=pltpu.SMEM), qo, kv(n_heads), kv(n_heads + n_kv_heads)],
        out_specs=qo,
        out_shape=jax.ShapeDtypeStruct((b, p, n_heads * HEAD_DIM), BF16),
        compiler_params=_cparams(("parallel", "parallel")),
        name="swa_attention",
    )(sink.astype(F32) * LOG2E, qkv, qkv, qkv)


def _mla_down_kernel(x_ref, w_ref, gq_ref, gkv_ref, cos_ref, sin_ref, cq_ref, ckv_ref, kr_ref, *, q_rank, kv_rank):
    acc = _dot(x_ref[0], w_ref[...])

    def rms(x, g):
        ms = jnp.mean(x * x, axis=-1, keepdims=True)
        return x * lax.rsqrt(ms + NORM_EPS) * g

    cq_ref[0] = rms(acc[:, :q_rank], gq_ref[...]).astype(cq_ref.dtype)
    ckv_ref[0] = rms(acc[:, q_rank:q_rank + kv_rank], gkv_ref[...]).astype(ckv_ref.dtype)
    kr_ref[0] = _rope(acc[:, q_rank + kv_rank:], cos_ref[...], sin_ref[...]).astype(kr_ref.dtype)


def mla_down(x, w_down, gq, gkv, cos, sin, q_rank, kv_rank):
    b, p, k = x.shape
    n = w_down.shape[1]
    tm = 544 if p % 544 == 0 else _row_tile(p)
    row = lambda width: pl.BlockSpec((1, tm, width), lambda b, i: (b, i, 0))
    return pl.pallas_call(
        functools.partial(_mla_down_kernel, q_rank=q_rank, kv_rank=kv_rank),
        grid=(b, p // tm),
        in_specs=[row(k), pl.BlockSpec((k, n), lambda b, i: (0, 0)),
                  pl.BlockSpec((1, q_rank), lambda b, i: (0, 0)),
                  pl.BlockSpec((1, kv_rank), lambda b, i: (0, 0)),
                  pl.BlockSpec((tm, LANES), lambda b, i: (i, 0)),
                  pl.BlockSpec((tm, LANES), lambda b, i: (i, 0))],
        out_specs=[row(q_rank), row(kv_rank), row(LANES)],
        out_shape=[jax.ShapeDtypeStruct((b, p, q_rank), BF16),
                   jax.ShapeDtypeStruct((b, p, kv_rank), BF16),
                   jax.ShapeDtypeStruct((b, p, LANES), BF16)],
        compiler_params=_cparams(("parallel", "parallel")),
        name="mla_down",
    )(x, w_down, gq.reshape(1, q_rank), gkv.reshape(1, kv_rank), cos, sin)


def _mla_kernel(cq_ref, ckv_ref, kr_ref, wq_ref, wkv_ref, cos_ref, sin_ref, o_ref, qcat_ref, kcat_ref, vone_ref, *,
                ctx_len):
    lc = ctx_len
    p_all = kcat_ref.shape[0]
    kv = _dot(ckv_ref[0], wkv_ref[...])
    kcat_ref[:, 0:LANES] = kv[:, :LANES].astype(BF16)
    kcat_ref[:, LANES:2 * LANES] = kr_ref[0]
    vone_ref[:, 0:LANES] = kv[:, LANES:].astype(BF16)
    vone_ref[:, LANES:2 * LANES] = jnp.ones((p_all, LANES), BF16)
    q = _dot(cq_ref[0], wq_ref[...])
    qcat_ref[:, 0:LANES] = q[:, :LANES].astype(BF16)
    qcat_ref[:, LANES:2 * LANES] = _rope(q[:, LANES:], cos_ref[...], sin_ref[...]).astype(BF16)

    def attend(qs, n_q, chunks):
        q = qcat_ref[pl.ds(qs, n_q), :]
        m = jnp.full((n_q, 1), NEG_INF, F32)
        acc = jnp.zeros((n_q, 2 * LANES), F32)
        for c0, c1 in chunks:
            s = _dot_nt(q, kcat_ref[c0:c1, :])
            m_new = jnp.maximum(m, _row_reduce(jnp.maximum, jnp.max, [s]))
            p = jnp.exp2(s - m_new)
            acc = jnp.exp2(m - m_new) * acc + _dot(p.astype(BF16), vone_ref[c0:c1, :])
            m = m_new
        o_ref[0, pl.ds(qs, n_q), :] = (acc[:, :LANES] * (1.0 / acc[:, LANES:LANES + 1])).astype(o_ref.dtype)

    attend(0, lc, [(0, lc)])
    all_chunks = [(0, lc)] + [(c, c + MLA_TK) for c in range(lc, p_all, MLA_TK)]

    def tile(i, carry):
        attend(pl.multiple_of(lc + i * MLA_TQ, MLA_TQ), MLA_TQ, all_chunks)
        return carry

    lax.fori_loop(0, (p_all - lc) // MLA_TQ, tile, 0, unroll=2)


def latent_attention(cq, ckv, kr, w_q, w_kv, cos, sin, ctx_len, n_heads):
    b, p, q_rank = cq.shape
    kv_rank = ckv.shape[2]
    assert (p - ctx_len) % MLA_TQ == 0 and (p - ctx_len) % MLA_TK == 0 and ctx_len % 16 == 0
    per_batch = lambda width: pl.BlockSpec((1, p, width), lambda b, h: (b, 0, 0), pipeline_mode=pl.Buffered(1))
    table = pl.BlockSpec((p, LANES), lambda b, h: (0, 0), pipeline_mode=pl.Buffered(1))
    return pl.pallas_call(
        functools.partial(_mla_kernel, ctx_len=ctx_len),
        grid=(b, n_heads),
        in_specs=[per_batch(q_rank), per_batch(kv_rank), per_batch(LANES),
                  pl.BlockSpec((q_rank, 2 * LANES), lambda b, h: (0, h)),
                  pl.BlockSpec((kv_rank, 2 * LANES), lambda b, h: (0, h)),
                  table, table],
        out_specs=pl.BlockSpec((1, p, LANES), lambda b, h: (b, 0, h)),
        out_shape=jax.ShapeDtypeStruct((b, p, n_heads * LANES), BF16),
        scratch_shapes=[pltpu.VMEM((p, 2 * LANES), BF16), pltpu.VMEM((p, 2 * LANES), BF16),
                        pltpu.VMEM((p, 2 * LANES), BF16)],
        compiler_params=_cparams(("parallel", "parallel")),
        name="mla_attention",
    )(cq, ckv, kr, w_q, w_kv, cos, sin)


def gather_rows(src, idx):
    n = idx.shape[0]
    width = src.shape[1]
    win = GATHER_WINDOW
    info = plsc.get_sparse_core_info()
    n_workers = info.num_cores * info.num_subcores
    assert n % (n_workers * 2 * win) == 0
    per_worker = n // n_workers
    mesh = plsc.VectorSubcoreMesh(core_axis_name="core", subcore_axis_name="subcore")

    @functools.partial(
        pl.kernel, out_type=jax.ShapeDtypeStruct((n, width), src.dtype), mesh=mesh,
        scratch_types=[pltpu.VMEM((per_worker,), jnp.int32),
                       pltpu.VMEM((win, width), src.dtype), pltpu.VMEM((win, width), src.dtype),
                       pltpu.SemaphoreType.DMA, pltpu.SemaphoreType.DMA,
                       pltpu.SemaphoreType.DMA, pltpu.SemaphoreType.DMA],
        name="gather_rows")
    def gather(src_hbm, idx_hbm, out_hbm, idx_vmem, rows0, rows1, gsem0, gsem1, osem0, osem1):
        worker = lax.axis_index("subcore") * info.num_cores + lax.axis_index("core")
        base = worker * per_worker
        pltpu.sync_copy(idx_hbm.at[pl.ds(base, per_worker)], idx_vmem)

        def fetch(r, rows, sem):
            return pltpu.make_async_copy(src_hbm.at[idx_vmem.at[pl.ds(r, win)]], rows, sem)

        def flush(r, rows, sem):
            return pltpu.make_async_copy(rows, out_hbm.at[pl.ds(base + r, win)], sem)

        fetch(0, rows0, gsem0).start()

        @pl.loop(0, per_worker, step=2 * win)
        def _(r):
            fetch(r, rows0, gsem0).wait()

            @pl.when(r > 0)
            def _():
                flush(r - win, rows1, osem1).wait()

            fetch(r + win, rows1, gsem1).start()
            flush(r, rows0, osem0).start()
            fetch(r + win, rows1, gsem1).wait()
            flush(r, rows0, osem0).wait()

            @pl.when(r + 2 * win < per_worker)
            def _():
                fetch(r + 2 * win, rows0, gsem0).start()

            flush(r + win, rows1, osem1).start()

        flush(per_worker - win, rows1, osem1).wait()

    return gather(src, idx)


def scatter_rows(src, idx, n_out):
    n = idx.shape[0]
    n_src, width = src.shape
    win = GATHER_WINDOW
    info = plsc.get_sparse_core_info()
    n_workers = info.num_cores * info.num_subcores
    assert n % (n_workers * 2 * win) == 0
    per_worker = n // n_workers
    assert n_src % per_worker == 0
    n_steps = per_worker // win
    mesh = plsc.VectorSubcoreMesh(core_axis_name="core", subcore_axis_name="subcore")

    @functools.partial(
        pl.kernel, out_type=jax.ShapeDtypeStruct((n_out, width), src.dtype), mesh=mesh,
        scratch_types=[pltpu.VMEM((n_steps, win), jnp.int32),
                       pltpu.VMEM((win, width), src.dtype), pltpu.VMEM((win, width), src.dtype),
                       pltpu.SemaphoreType.DMA, pltpu.SemaphoreType.DMA,
                       pltpu.SemaphoreType.DMA, pltpu.SemaphoreType.DMA],
        name="scatter_rows")
    def scatter(src_hbm, idx_hbm, out_hbm, idx_vmem, rows0, rows1, lsem0, lsem1, ssem0, ssem1):
        worker = lax.axis_index("subcore") * info.num_cores + lax.axis_index("core")
        src_base = lax.rem(worker * per_worker, n_src)
        pltpu.sync_copy(idx_hbm.at[worker], idx_vmem)

        def load(j, rows, sem):
            return pltpu.make_async_copy(src_hbm.at[pl.ds(src_base + j * win, win)], rows, sem)

        def store(j, rows, sem):
            return pltpu.make_async_copy(rows, out_hbm.at[idx_vmem.at[j]], sem)

        load(0, rows0, lsem0).start()

        @pl.loop(0, n_steps, step=2)
        def _(j):
            load(j, rows0, lsem0).wait()

            @pl.when(j > 0)
            def _():
                store(j - 1, rows1, ssem1).wait()

            load(j + 1, rows1, lsem1).start()
            store(j, rows0, ssem0).start()
            load(j + 1, rows1, lsem1).wait()
            store(j, rows0, ssem0).wait()

            @pl.when(j + 2 < n_steps)
            def _():
                load(j + 2, rows0, lsem0).start()

            store(j + 1, rows1, ssem1).start()

        store(n_steps - 1, rows1, ssem1).wait()

    return scatter(src, idx.reshape(n_workers, n_steps, win))


def _expert_kernel(blk_e_ref, n_used_ref, n_valid_ref, x_ref, wg_ref, wu_ref, wd_ref, y_ref, wg_bf, wu_bf, wd_bf):
    i = pl.program_id(0)
    used = i < n_used_ref[0]
    new_expert = (i == 0) | (blk_e_ref[i] != blk_e_ref[jnp.maximum(i - 1, 0)])

    @pl.when(used & new_expert)
    def _():
        wg_bf[...] = wg_ref[0].astype(BF16)
        wu_bf[...] = wu_ref[0].astype(BF16)
        wd_bf[...] = wd_ref[0].astype(BF16)

    @pl.when(used)
    def _():
        row = lax.broadcasted_iota(jnp.int32, (x_ref.shape[0], 1), 0)
        x = _unpack_bf16_pairs(jnp.where(row < n_valid_ref[i], x_ref[...], 0)).astype(BF16)
        g = _dot(x, wg_bf[...])
        u = _dot(x, wu_bf[...])
        a = (g * jax.nn.sigmoid(g) * u).astype(BF16)
        y_ref[...] = _pack_bf16_pairs(_dot(a, wd_bf[...]))

    @pl.when(jnp.logical_not(used))
    def _():
        y_ref[...] = jnp.zeros_like(y_ref)


def expert_ffn(x_disp, blk_e, n_used, n_valid, w_gate, w_up, w_down, layer):
    rows, half = x_disp.shape
    d = 2 * half
    n_blk = rows // MOE_BLOCK
    de = w_gate.shape[3]
    grid_spec = pltpu.PrefetchScalarGridSpec(
        num_scalar_prefetch=3,
        grid=(n_blk,),
        in_specs=[pl.BlockSpec((MOE_BLOCK, half), lambda i, be, nu, nv: (i, 0)),
                  pl.BlockSpec((None, 1, d, de), lambda i, be, nu, nv: (layer, be[i], 0, 0)),
                  pl.BlockSpec((None, 1, d, de), lambda i, be, nu, nv: (layer, be[i], 0, 0)),
                  pl.BlockSpec((None, 1, de, d), lambda i, be, nu, nv: (layer, be[i], 0, 0))],
        out_specs=pl.BlockSpec((MOE_BLOCK, half), lambda i, be, nu, nv: (i, 0)),
        scratch_shapes=[pltpu.VMEM((d, de), BF16), pltpu.VMEM((d, de), BF16), pltpu.VMEM((de, d), BF16)],
    )
    return pl.pallas_call(
        _expert_kernel,
        grid_spec=grid_spec,
        out_shape=jax.ShapeDtypeStruct((rows, half), jnp.int32),
        compiler_params=_cparams(("arbitrary",)),
        name="expert_ffn",
    )(blk_e, n_used, n_valid, x_disp, w_gate, w_up, w_down)


def _moe_residual(h_ref, y0_ref, y1_ref, wt_ref, mod_ref, gate_idx):
    wt = wt_ref[0]
    moe = wt[:, 0:1] * _unpack_bf16_pairs(y0_ref[0, 0]) + wt[:, 1:2] * _unpack_bf16_pairs(y1_ref[0, 0])
    return h_ref[0] + mod_ref[0, gate_idx:gate_idx + 1, :] * moe


def _combine_next_kernel(h_ref, y0_ref, y1_ref, wt_ref, mod_ref, g_ref, nmod_ref, o_ref, hm_ref, *, gate_idx):
    h_new = _moe_residual(h_ref, y0_ref, y1_ref, wt_ref, mod_ref, gate_idx)
    o_ref[0] = h_new
    hm_ref[0] = _rms_mod(h_new, g_ref[...], nmod_ref[0, 0:1, :], nmod_ref[0, 1:2, :]).astype(hm_ref.dtype)


def _combine_final_kernel(h_ref, y0_ref, y1_ref, wt_ref, mod_ref, g_ref, o_ref, *, gate_idx):
    x = _moe_residual(h_ref, y0_ref, y1_ref, wt_ref, mod_ref, gate_idx)
    ms = jnp.mean(x * x, axis=-1, keepdims=True)
    o_ref[0] = x * lax.rsqrt(ms + NORM_EPS) * g_ref[...]


def combine(h, y_pairs, wt, modtab, gate_idx, next_g, next_modtab, ctx_len):
    b, p, d = h.shape
    last = next_modtab is None
    skip = ctx_len // ROW_TILE if last else 0
    row = pl.BlockSpec((1, ROW_TILE, d), lambda b, j: (b, j + skip, 0))
    mod = pl.BlockSpec((1, N_MOD, d), lambda b, j: (2 * b + jnp.minimum(j + skip, 1), 0, 0))
    in_specs = [row,
                pl.BlockSpec((1, 1, ROW_TILE, d // 2), lambda b, j: (0, b, j + skip, 0)),
                pl.BlockSpec((1, 1, ROW_TILE, d // 2), lambda b, j: (1, b, j + skip, 0)),
                pl.BlockSpec((1, ROW_TILE, LANES), lambda b, j: (b, j + skip, 0)),
                mod, pl.BlockSpec((1, d), lambda b, j: (0, 0))]
    args = [h, y_pairs, y_pairs, wt, modtab.reshape(b * 2, N_MOD, d), next_g.reshape(1, d)]
    out_row = pl.BlockSpec((1, ROW_TILE, d), lambda b, j: (b, j, 0))
    if last:
        return pl.pallas_call(
            functools.partial(_combine_final_kernel, gate_idx=gate_idx),
            grid=(b, (p - ctx_len) // ROW_TILE),
            in_specs=in_specs,
            out_specs=out_row,
            out_shape=jax.ShapeDtypeStruct((b, p - ctx_len, d), F32),
            compiler_params=_cparams(("parallel", "parallel")),
            name="moe_combine_final",
        )(*args)
    return pl.pallas_call(
        functools.partial(_combine_next_kernel, gate_idx=gate_idx),
        grid=(b, p // ROW_TILE),
        in_specs=in_specs + [mod],
        out_specs=[out_row, out_row],
        out_shape=[jax.ShapeDtypeStruct((b, p, d), F32), jax.ShapeDtypeStruct((b, p, d), BF16)],
        compiler_params=_cparams(("parallel", "parallel")),
        name="moe_combine",
    )(*args, next_modtab.reshape(b * 2, N_MOD, d))


def hier_moe(h, f, route, wt, counts, modtab, gate_idx, w_gate, w_up, w_down, layer, next_g, next_modtab, ctx_len):
    b, p, d = h.shape
    n_tok = b * p
    n_assign = 2 * n_tok
    eid_f = route[:, :, 0:2].reshape(n_assign)
    rank_f = route[:, :, 2:4].reshape(n_assign)
    cnt = counts[0, :N_EXPERTS].astype(jnp.int32)
    pcounts = (cnt + MOE_BLOCK - 1) // MOE_BLOCK * MOE_BLOCK
    pend = jnp.cumsum(pcounts)
    pstart = pend - pcounts
    dest = (pstart[eid_f] + rank_f).astype(jnp.int32)
    n_blk = -(-n_assign // MOE_BLOCK) + N_EXPERTS
    rows_total = n_blk * MOE_BLOCK
    blk_row0 = jnp.arange(n_blk, dtype=jnp.int32) * MOE_BLOCK
    blk_e = jnp.minimum(jnp.sum((pend[None, :] <= blk_row0[:, None]).astype(jnp.int32), axis=1), N_EXPERTS - 1)
    n_valid = jnp.clip(cnt[blk_e] - (blk_row0 - pstart[blk_e]), 0, MOE_BLOCK).astype(jnp.int32)
    n_used = (pend[-1] // MOE_BLOCK).astype(jnp.int32).reshape(1)

    dest_by_slot = jnp.concatenate([dest[0::2], dest[1::2]])
    x_disp = scatter_rows(f.reshape(n_tok, d // 2), dest_by_slot, rows_total)
    y = expert_ffn(x_disp, blk_e, n_used, n_valid, w_gate, w_up, w_down, layer)
    y_pairs = gather_rows(y, dest_by_slot).reshape(2, b, p, d // 2)
    return combine(h, y_pairs, wt, modtab, gate_idx, next_g, next_modtab, ctx_len)


def kernel(x, c, ctx, c_ctx, mod_w, mod_b, norm_mix_g, norm_ffn_g, router_grp_w, router_grp_b, router_exp_w, router_exp_b, exp_w_gate, exp_w_up, exp_w_down, l0_na_w_qkv, l0_na_rpb, l0_na_w_o, l1_swa_w_qkv, l1_swa_sink, l1_swa_w_o, l2_mla_w_dq, l2_mla_q_norm_g, l2_mla_w_uq, l2_mla_w_dkv, l2_mla_kv_norm_g, l2_mla_w_ukv, l2_mla_w_o, l3_na_w_qkv, l3_na_rpb, l3_na_w_o, final_norm_g):
    b, s, d = x.shape
    lc = ctx.shape[1]
    n_heads = d // HEAD_DIM
    n_kv_heads = n_heads // 4
    depth = mod_w.shape[0]
    rows = s // GRID_W

    h = jnp.concatenate([ctx, x], axis=1)
    modtabs = modulation_tables(c, c_ctx, mod_w, mod_b)

    def scale_q_cols(w):
        n_q = n_heads * HEAD_DIM
        q_scale = HEAD_DIM ** -0.5 * LOG2E
        return jnp.concatenate([w[:, :n_q] * q_scale, w[:, n_q:]], axis=1).astype(BF16)

    def na_mixer(hm, w_qkv, rpb):
        qkv = project(hm, scale_q_cols(w_qkv))
        return neighbourhood_attention(qkv, na_bias_table(rpb, rows), lc, n_heads)

    def swa_mixer(hm):
        cos, sin = rope_tables_full(s, lc)
        n_rope = (n_heads + n_kv_heads) * HEAD_DIM // 512
        qkv = project(hm, scale_q_cols(l1_swa_w_qkv), tn=512, rope=(cos, sin, 0, n_rope))
        return window_attention(qkv, l1_swa_sink, lc, n_heads, n_kv_heads)

    def mla_mixer(hm):
        q_rank = l2_mla_w_dq.shape[1]
        kv_rank = l2_mla_kv_norm_g.shape[0]
        cos, sin = rope_tables_mla(s, lc)
        w_down = jnp.concatenate([l2_mla_w_dq, l2_mla_w_dkv[:, :kv_rank],
                                  _spread_rope_cols(l2_mla_w_dkv[:, kv_rank:])], axis=1).astype(BF16)
        cq, ckv, kr = mla_down(hm, w_down, l2_mla_q_norm_g, l2_mla_kv_norm_g, cos, sin, q_rank, kv_rank)
        w_uq = l2_mla_w_uq.reshape(q_rank, n_heads, MLA_NOPE_DIM + MLA_ROPE_DIM)
        q_scale = (MLA_NOPE_DIM + MLA_ROPE_DIM) ** -0.5 * LOG2E
        w_q = jnp.concatenate([jnp.concatenate([w_uq[:, hh, :MLA_NOPE_DIM], _spread_rope_cols(w_uq[:, hh, MLA_NOPE_DIM:])],
                                               axis=1) for hh in range(n_heads)], axis=1)
        return latent_attention(cq, ckv, kr, (w_q * q_scale).astype(BF16), l2_mla_w_ukv.astype(BF16), cos, sin,
                                lc, n_heads)

    hm = norm_modulate(h, norm_mix_g[0], modtabs[0], 0)
    for i in range(depth):
        modtab = modtabs[i]
        mixer = i % 3
        if mixer == 0:
            w_qkv, rpb, w_o = (l0_na_w_qkv, l0_na_rpb, l0_na_w_o) if i == 0 else (l3_na_w_qkv, l3_na_rpb, l3_na_w_o)
            y = na_mixer(hm, w_qkv, rpb)
        elif mixer == 1:
            y, w_o = swa_mixer(hm), l1_swa_w_o
        else:
            y, w_o = mla_mixer(hm), l2_mla_w_o
        h, f, route, wt, counts = attn_out_route(y, w_o.astype(BF16), h, modtab, norm_ffn_g[i], router_grp_w[i],
                                                 router_grp_b[i], router_exp_w[i], router_exp_b[i], lc)
        if i + 1 < depth:
            h, hm = hier_moe(h, f, route, wt, counts, modtab, 5, exp_w_gate, exp_w_up, exp_w_down, i,
                             norm_mix_g[i + 1], modtabs[i + 1], lc)
        else:
            return hier_moe(h, f, route, wt, counts, modtab, 5, exp_w_gate, exp_w_up, exp_w_down, i,
                            final_norm_g, None, lc)
```

```python
import functools

import numpy as np
import jax
import jax.numpy as jnp
from jax import lax
from jax.experimental import pallas as pl
from jax.experimental.pallas import tpu as pltpu
from jax.experimental.pallas import tpu_sc as plsc

GRID_W = 64
HEAD_DIM = 128
ROPE_BASE = 10000.0
NORM_EPS = 1e-6
NEG_INF = -1e30
N_MOD = 6

NA_ROWS = 8
NA_COLS = 16
NA_TILE_ROWS = 4
NA_WIN_ROWS = 12

SWA_WINDOW = 128
SWA_TQ = 256
SWA_TK = 512

MLA_NOPE_DIM = 128
MLA_ROPE_DIM = 64
MLA_TQ = 512
MLA_TK = 512
LOG2E = 1.4426950408889634

N_GROUPS = 4
EXPERTS_PER_GROUP = 8
N_EXPERTS = N_GROUPS * EXPERTS_PER_GROUP
MOE_BLOCK = 512
GATHER_WINDOW = 32

LANES = 128
ROW_TILE = 256
VMEM_LIMIT = 56 * 1024 * 1024

BF16 = jnp.bfloat16
F32 = jnp.float32


def _cparams(sem):
    return pltpu.CompilerParams(dimension_semantics=sem, vmem_limit_bytes=VMEM_LIMIT)


def _dot(a, b):
    return jnp.dot(a, b, preferred_element_type=F32)


def _dot_nt(a, b):
    return lax.dot_general(a, b, (((1,), (1,)), ((), ())), preferred_element_type=F32)


def _mod_kernel(x_ref, w_ref, b_ref, o_ref):
    x = x_ref[...]
    sx = (x * jax.nn.sigmoid(x)).astype(BF16)
    o_ref[0] = _dot(sx, w_ref[0].astype(BF16)) + b_ref[0]


def modulation_tables(c, c_ctx, mod_w, mod_b):
    depth, d, n_out = mod_w.shape
    b = c.shape[0]
    rows = 16
    xin = jnp.zeros((rows, d), F32).at[:b].set(c).at[b].set(c_ctx)
    tn = 1024
    out = pl.pallas_call(
        _mod_kernel,
        grid=(depth, n_out // tn),
        in_specs=[pl.BlockSpec((rows, d), lambda i, j: (0, 0)),
                  pl.BlockSpec((1, d, tn), lambda i, j: (i, 0, j)),
                  pl.BlockSpec((1, 1, tn), lambda i, j: (i, 0, j))],
        out_specs=pl.BlockSpec((1, rows, tn), lambda i, j: (i, 0, j)),
        out_shape=jax.ShapeDtypeStruct((depth, rows, n_out), F32),
        compiler_params=_cparams(("parallel", "parallel")),
        name="adaln_mod",
    )(xin, mod_w, mod_b.reshape(depth, 1, n_out))
    lat = out[:, :b].reshape(depth, b, 1, N_MOD, d)
    ctx = jnp.broadcast_to(out[:, b].reshape(depth, 1, 1, N_MOD, d), (depth, b, 1, N_MOD, d))
    return jnp.concatenate([ctx, lat], axis=2)


def _rms_mod(x, g, shift, scale):
    ms = jnp.mean(x * x, axis=-1, keepdims=True)
    y = x * lax.rsqrt(ms + NORM_EPS) * g
    return y * (1.0 + scale) + shift


def _norm_mod_kernel(h_ref, g_ref, mod_ref, o_ref, *, shift_idx):
    f = _rms_mod(h_ref[0], g_ref[...], mod_ref[0, shift_idx:shift_idx + 1, :],
                 mod_ref[0, shift_idx + 1:shift_idx + 2, :])
    o_ref[0] = f.astype(o_ref.dtype)


def _route(logits):
    lane = lax.broadcasted_iota(jnp.int32, logits.shape, 1).astype(F32)
    big = float(LANES)

    def first_lane(mask):
        return jnp.min(jnp.where(mask, lane, big), axis=-1, keepdims=True)

    in_grp = lane < N_GROUPS
    lg = jnp.where(in_grp, logits, NEG_INF)
    m_g = jnp.max(lg, axis=-1, keepdims=True)
    g_idx = first_lane(in_grp & (lg == m_g))
    g_w = 1.0 / jnp.sum(jnp.where(in_grp, jnp.exp(lg - m_g), 0.0), axis=-1, keepdims=True)
    e_lo = N_GROUPS + g_idx * EXPERTS_PER_GROUP
    in_e = (lane >= e_lo) & (lane < e_lo + EXPERTS_PER_GROUP)
    le = jnp.where(in_e, logits, NEG_INF)
    m1 = jnp.max(le, axis=-1, keepdims=True)
    e1 = first_lane(in_e & (le == m1))
    s_e = jnp.sum(jnp.where(in_e, jnp.exp(le - m1), 0.0), axis=-1, keepdims=True)
    in_e2 = in_e & (lane != e1)
    le2 = jnp.where(in_e2, logits, NEG_INF)
    m2 = jnp.max(le2, axis=-1, keepdims=True)
    e2 = first_lane(in_e2 & (le2 == m2))
    p1 = 1.0 / s_e
    p2 = jnp.exp(m2 - m1) / s_e
    den = p1 + p2
    return ((e1 - N_GROUPS).astype(jnp.int32), (e2 - N_GROUPS).astype(jnp.int32),
            g_w * p1 / den, g_w * p2 / den)


def _pack_bf16_pairs(x):
    n = x.shape[1] // 2
    xb = x.astype(BF16).astype(F32)
    hi = lax.bitcast_convert_type(xb[:, :n], jnp.int32)
    lo = lax.bitcast_convert_type(xb[:, n:], jnp.int32)
    return (hi & jnp.int32(-65536)) | lax.shift_right_logical(lo, jnp.int32(16))


def _unpack_bf16_pairs(w):
    hi = lax.bitcast_convert_type(w & jnp.int32(-65536), F32)
    lo = lax.bitcast_convert_type(lax.shift_left(w, jnp.int32(16)), F32)
    return jnp.concatenate([hi, lo], axis=1)


def _route_and_rank(f, wr_ref, br_ref, run_ref):
    f_hi = f.astype(BF16)
    f_lo = (f - f_hi.astype(F32)).astype(BF16)
    hi_terms = _dot(f_hi, wr_ref[...])
    logits = (hi_terms[:, :LANES] + _dot(f_lo, wr_ref[:, :LANES]) + hi_terms[:, LANES:]) + br_ref[...]
    e1, e2, w1, w2 = _route(logits)
    lane = lax.broadcasted_iota(jnp.int32, logits.shape, 1)
    pick1, pick2 = lane == e1, lane == e2
    chosen = (pick1 | pick2).astype(F32)
    n_rows = chosen.shape[0]
    earlier = (lax.broadcasted_iota(jnp.int32, (n_rows, n_rows), 1)
               < lax.broadcasted_iota(jnp.int32, (n_rows, n_rows), 0)).astype(BF16)
    before = run_ref[...] + _dot(earlier, chosen.astype(BF16))
    rank1 = jnp.sum(jnp.where(pick1, before, 0.0), axis=-1, keepdims=True).astype(jnp.int32)
    rank2 = jnp.sum(jnp.where(pick2, before, 0.0), axis=-1, keepdims=True).astype(jnp.int32)
    run_ref[...] = run_ref[...] + jnp.sum(chosen, axis=0, keepdims=True)
    route = jnp.where(lane == 0, e1, jnp.where(lane == 1, e2, jnp.where(lane == 2, rank1,
                                                                         jnp.where(lane == 3, rank2, 0))))
    return route, jnp.where(lane == 0, w1, jnp.where(lane == 1, w2, 0.0))


def _attn_out_route_kernel(y_ref, w_ref, h_ref, mod_ref, g_ref, wr_ref, br_ref,
                           ho_ref, f_ref, route_ref, wt_ref, cnt_ref, run_ref, *, ctx_len, tm, n_sub):
    first = (pl.program_id(0) == 0) & (pl.program_id(1) == 0)

    @pl.when(first)
    def _():
        run_ref[...] = jnp.zeros_like(run_ref)

    sub = tm // n_sub
    for s in range(n_sub):
        rows = slice(s * sub, (s + 1) * sub)
        acc = _dot(y_ref[0, rows, :], w_ref[...])
        pos = pl.program_id(1) * tm + s * sub + lax.broadcasted_iota(jnp.int32, (sub, 1), 0)
        is_ctx = pos < ctx_len

        def mod_row(k):
            return jnp.where(is_ctx, mod_ref[0, 0, k:k + 1, :], mod_ref[0, 1, k:k + 1, :])

        h_new = h_ref[0, rows, :] + mod_row(2) * acc
        ho_ref[0, rows, :] = h_new
        f = _rms_mod(h_new, g_ref[...], mod_row(3), mod_row(4))
        f_ref[0, rows, :] = _pack_bf16_pairs(f)
        route, wt = _route_and_rank(f, wr_ref, br_ref, run_ref)
        route_ref[0, rows, :] = route
        wt_ref[0, rows, :] = wt
    cnt_ref[...] = run_ref[...]


def _mod_spec(d):
    return pl.BlockSpec((1, N_MOD, d), lambda b, j: (2 * b + jnp.minimum(j, 1), 0, 0))


def norm_modulate(h, g, modtab, shift_idx):
    b, p, d = h.shape
    row = pl.BlockSpec((1, ROW_TILE, d), lambda b, j: (b, j, 0))
    return pl.pallas_call(
        functools.partial(_norm_mod_kernel, shift_idx=shift_idx),
        grid=(b, p // ROW_TILE),
        in_specs=[row, pl.BlockSpec((1, d), lambda b, j: (0, 0)), _mod_spec(d)],
        out_specs=row,
        out_shape=jax.ShapeDtypeStruct((b, p, d), BF16),
        compiler_params=_cparams(("parallel", "parallel")),
        name="norm_mod",
    )(h, g.reshape(1, d), modtab.reshape(b * 2, N_MOD, d))


def attn_out_route(y, w_o, h, modtab, g, w_grp, b_grp, w_rt, b_rt, ctx_len):
    b, p, d = h.shape
    n_r = N_GROUPS + N_EXPERTS
    wr = jnp.zeros((d, LANES), F32).at[:, :N_GROUPS].set(w_grp).at[:, N_GROUPS:n_r].set(w_rt)
    br = jnp.zeros((1, LANES), F32).at[0, :N_GROUPS].set(b_grp).at[0, N_GROUPS:n_r].set(b_rt)
    wr_hi = wr.astype(BF16)
    wr = jnp.concatenate([wr_hi, (wr - wr_hi.astype(F32)).astype(BF16)], axis=1)
    tm = 544 if p % 544 == 0 else ROW_TILE
    n_sub = 2
    assert (tm // n_sub) % 16 == 0
    row = lambda width: pl.BlockSpec((1, tm, width), lambda b, i: (b, i, 0))
    once = lambda shape: pl.BlockSpec(shape, lambda b, i: (0,) * len(shape), pipeline_mode=pl.Buffered(1))
    return pl.pallas_call(
        functools.partial(_attn_out_route_kernel, ctx_len=ctx_len, tm=tm, n_sub=n_sub),
        grid=(b, p // tm),
        in_specs=[row(d), once((d, d)), row(d),
                  pl.BlockSpec((1, 2, N_MOD, d), lambda b, i: (b, 0, 0, 0)),
                  once((1, d)), once((d, 2 * LANES)), once((1, LANES))],
        out_specs=[row(d), row(d // 2), row(LANES), row(LANES), pl.BlockSpec((1, LANES), lambda b, i: (0, 0))],
        out_shape=[jax.ShapeDtypeStruct((b, p, d), F32),
                   jax.ShapeDtypeStruct((b, p, d // 2), jnp.int32),
                   jax.ShapeDtypeStruct((b, p, LANES), jnp.int32),
                   jax.ShapeDtypeStruct((b, p, LANES), F32),
                   jax.ShapeDtypeStruct((1, LANES), F32)],
        scratch_shapes=[pltpu.VMEM((1, LANES), F32)],
        compiler_params=_cparams(("arbitrary", "arbitrary")),
        name="attn_out_route",
    )(y, w_o, h, modtab, g.reshape(1, d), wr, br)


def _rope(acc, cos, sin):
    n_blk = acc.shape[1] // LANES
    outs = []
    for c in range(n_blk):
        x = acc[:, c * LANES:(c + 1) * LANES]
        outs.append(x * cos + pltpu.roll(x, LANES // 2, 1) * sin)
    return outs[0] if n_blk == 1 else jnp.concatenate(outs, axis=1)


def _proj_kernel(*refs, rope_lo, rope_hi):
    x_ref, w_ref = refs[0], refs[1]
    o_ref = refs[-1]
    acc = _dot(x_ref[0], w_ref[...])
    if rope_hi > rope_lo:
        cos_ref, sin_ref = refs[2], refs[3]
        j = pl.program_id(2)
        roped = (j >= rope_lo) & (j < rope_hi)

        @pl.when(roped)
        def _():
            o_ref[0] = _rope(acc, cos_ref[...], sin_ref[...]).astype(o_ref.dtype)

        @pl.when(jnp.logical_not(roped))
        def _():
            o_ref[0] = acc.astype(o_ref.dtype)
    else:
        o_ref[0] = acc.astype(o_ref.dtype)


def _row_tile(p):
    for cand in (1088, 1024, 544, 512, 272, 256, 128, 64, 32, 16):
        if p % cand == 0:
            return cand
    raise ValueError(p)


def project(x, w, *, tn=512, out_dtype=BF16, rope=None):
    b, p, k = x.shape
    n = w.shape[1]
    tm = _row_tile(p)
    tn = min(tn, n)
    assert n % tn == 0
    in_specs = [pl.BlockSpec((1, tm, k), lambda b, i, j: (b, i, 0)),
                pl.BlockSpec((k, tn), lambda b, i, j: (0, j))]
    args = [x, w]
    kw = dict(rope_lo=0, rope_hi=0)
    if rope is not None:
        cos, sin, lo, hi = rope
        in_specs += [pl.BlockSpec((tm, LANES), lambda b, i, j: (i, 0)),
                     pl.BlockSpec((tm, LANES), lambda b, i, j: (i, 0))]
        args += [cos, sin]
        kw.update(rope_lo=lo, rope_hi=hi)
    return pl.pallas_call(
        functools.partial(_proj_kernel, **kw),
        grid=(b, p // tm, n // tn),
        in_specs=in_specs,
        out_specs=pl.BlockSpec((1, tm, tn), lambda b, i, j: (b, i, j)),
        out_shape=jax.ShapeDtypeStruct((b, p, n), out_dtype),
        compiler_params=_cparams(("parallel", "parallel", "arbitrary")),
        name="project",
    )(*args)


def _axial_cos_sin(n, rot_dim):
    t = jnp.arange(n, dtype=jnp.int32)
    row = (t // GRID_W).astype(F32)
    col = (t % GRID_W).astype(F32)
    n_freq = rot_dim // 4
    inv = ROPE_BASE ** (-jnp.arange(n_freq, dtype=F32) / n_freq)
    ang = jnp.concatenate([row[:, None] * inv, col[:, None] * inv], axis=-1)
    return jnp.cos(ang), jnp.sin(ang)


def rope_tables_full(s, ctx_len):
    c, sn = _axial_cos_sin(s, HEAD_DIM)
    cos = jnp.concatenate([c, c], axis=1)
    sin = jnp.concatenate([-sn, sn], axis=1)
    ident_c = jnp.ones((ctx_len, LANES), F32)
    ident_s = jnp.zeros((ctx_len, LANES), F32)
    return jnp.concatenate([ident_c, cos], axis=0), jnp.concatenate([ident_s, sin], axis=0)


def rope_tables_mla(s, ctx_len):
    c, sn = _axial_cos_sin(s, MLA_ROPE_DIM)
    one = jnp.ones_like(c)
    zero = jnp.zeros_like(c)
    cos = jnp.concatenate([c, one, c, one], axis=1)
    sin = jnp.concatenate([-sn, zero, sn, zero], axis=1)
    ident_c = jnp.ones((ctx_len, LANES), F32)
    ident_s = jnp.zeros((ctx_len, LANES), F32)
    return jnp.concatenate([ident_c, cos], axis=0), jnp.concatenate([ident_s, sin], axis=0)


def _spread_rope_cols(w_rope):
    k = w_rope.shape[0]
    half = MLA_ROPE_DIM // 2
    z = jnp.zeros((k, half), w_rope.dtype)
    return jnp.concatenate([w_rope[:, :half], z, w_rope[:, half:], z], axis=1)


def _softmax_probs(parts, extra=None):
    m = _row_reduce(jnp.maximum, jnp.max, parts)
    if extra is not None:
        m = jnp.maximum(m, extra)
    return [jnp.exp2((s - m).astype(BF16)) for s in parts], m


def _with_ones(v_ref, vone_ref):
    vone_ref[:, 0:LANES] = v_ref[0]
    vone_ref[:, LANES:2 * LANES] = jnp.ones((vone_ref.shape[0], LANES), BF16)


def _normalise(acc, extra_den=None):
    den = acc[:, LANES:LANES + 1]
    if extra_den is not None:
        den = den + extra_den
    return acc[:, :LANES] * (1.0 / den)


def _row_reduce(combine, reduce, parts):
    blocks = [s[:, c:c + LANES] for s in parts for c in range(0, s.shape[1], LANES)]
    acc = blocks[0]
    for blk in blocks[1:]:
        acc = combine(acc, blk)
    return reduce(acc, axis=-1, keepdims=True)


def na_bias_table(rpb, rows):
    n_tiles = rows // NA_TILE_ROWS
    n_heads, _, n_dcol = rpb.shape
    drow, row_ok = [], []
    for tile in (0, 1, n_tiles - 1):
        kr0 = int(np.clip(NA_TILE_ROWS * tile - NA_ROWS // 2, 0, rows - NA_WIN_ROWS))
        r = NA_TILE_ROWS * tile + np.arange(NA_TILE_ROWS)
        r0 = np.clip(r - NA_ROWS // 2, 0, rows - NA_ROWS)
        krow = kr0 + np.arange(NA_WIN_ROWS)
        row_ok.append((krow[None, :] >= r0[:, None]) & (krow[None, :] < r0[:, None] + NA_ROWS))
        drow.append(np.clip(krow[None, :] - r[:, None] + NA_ROWS - 1, 0, 2 * NA_ROWS - 2))
    drow, row_ok = np.stack(drow), np.stack(row_ok)
    qc = np.arange(GRID_W)
    qcol0 = np.clip(qc - NA_COLS // 2, 0, GRID_W - NA_COLS)
    kc = np.arange(GRID_W)
    col_ok = (kc[None, :] >= qcol0[:, None]) & (kc[None, :] < qcol0[:, None] + NA_COLS)
    dcol = np.clip(kc[None, :] - qc[:, None] + NA_COLS - 1, 0, 2 * NA_COLS - 2)
    pick_col = jnp.asarray(dcol[None] == np.arange(n_dcol)[:, None, None], F32)
    slabs = jnp.einsum('hrd,dqk->hrqk', rpb.astype(F32), pick_col, precision=lax.Precision.HIGHEST)
    slabs = jnp.where(jnp.asarray(col_ok)[None, None], slabs * LOG2E, NEG_INF)
    masked = jnp.full((n_heads, GRID_W, GRID_W), NEG_INF, F32)
    pats = []
    for p in range(3):
        per_row = [jnp.stack([slabs[:, drow[p, a, m]] if row_ok[p, a, m] else masked
                              for m in range(NA_WIN_ROWS)], axis=2)
                   for a in range(NA_TILE_ROWS)]
        pats.append(jnp.stack(per_row, axis=1))
    vals = jnp.stack(pats, axis=1)
    return vals.reshape(n_heads, 3, NA_TILE_ROWS * GRID_W, NA_WIN_ROWS * GRID_W)


def _na_kernel(q_ref, k_ref, v_ref, bias_ref, o_ref, vone_ref, *, ctx_len, rows):
    tq = NA_TILE_ROWS * GRID_W
    tk = NA_WIN_ROWS * GRID_W
    n_tiles = rows // NA_TILE_ROWS
    lc = ctx_len
    _with_ones(v_ref, vone_ref)

    s = _dot_nt(q_ref[0, 0:lc, :], k_ref[0, 0:lc, :])
    (p,), _ = _softmax_probs([s])
    o_ref[0, 0:lc, :] = _normalise(_dot(p, vone_ref[0:lc, :])).astype(o_ref.dtype)

    def tile(i, carry):
        qs = pl.multiple_of(lc + i * tq, tq)
        kr0 = jnp.clip(NA_TILE_ROWS * i - NA_ROWS // 2, 0, rows - NA_WIN_ROWS)
        ks = pl.multiple_of(lc + kr0 * GRID_W, NA_TILE_ROWS * GRID_W)
        pat = jnp.where(i == 0, 0, jnp.where(i == n_tiles - 1, 2, 1))
        q = q_ref[0, pl.ds(qs, tq), :]
        s_loc = _dot_nt(q, k_ref[0, pl.ds(ks, tk), :]) + bias_ref[0, pat]
        s_ctx = _dot_nt(q, k_ref[0, 0:lc, :])
        (p_loc, p_ctx), _ = _softmax_probs([s_loc, s_ctx])
        acc = _dot(p_loc, vone_ref[pl.ds(ks, tk), :]) + _dot(p_ctx, vone_ref[0:lc, :])
        o_ref[0, pl.ds(qs, tq), :] = _normalise(acc).astype(o_ref.dtype)
        return carry

    lax.fori_loop(0, n_tiles, tile, 0, unroll=4)


def neighbourhood_attention(qkv, bias, ctx_len, n_heads):
    b, p, _ = qkv.shape
    rows = (p - ctx_len) // GRID_W
    assert rows % NA_TILE_ROWS == 0 and rows >= NA_WIN_ROWS
    tq, tk = NA_TILE_ROWS * GRID_W, NA_WIN_ROWS * GRID_W
    assert ctx_len % 16 == 0 and ctx_len % tq == 0
    blk = lambda off: pl.BlockSpec((1, p, HEAD_DIM), lambda h, b: (b, 0, off + h))
    return pl.pallas_call(
        functools.partial(_na_kernel, ctx_len=ctx_len, rows=rows),
        grid=(n_heads, b),
        in_specs=[blk(0), blk(n_heads), blk(2 * n_heads),
                  pl.BlockSpec((1, 3, tq, tk), lambda h, b: (h, 0, 0, 0))],
        out_specs=blk(0),
        out_shape=jax.ShapeDtypeStruct((b, p, n_heads * HEAD_DIM), BF16),
        scratch_shapes=[pltpu.VMEM((p, 2 * LANES), BF16)],
        compiler_params=_cparams(("parallel", "parallel")),
        name="na_attention",
    )(qkv, qkv, qkv, bias)


def _swa_kernel(sink_ref, q_ref, k_ref, v_ref, o_ref, vone_ref, *, ctx_len, seq, group):
    lc = ctx_len
    kvh = pl.program_id(1)
    _with_ones(v_ref, vone_ref)

    def stack_heads(q):
        return jnp.concatenate([q[:, g * HEAD_DIM:(g + 1) * HEAD_DIM] for g in range(group)], axis=0)

    def finish(parts_fn, n_q, pv_fn, store):
        ps_all, sink_terms = [], []
        for g in range(group):
            sink = sink_ref[kvh * group + g]
            ps, m = _softmax_probs(parts_fn(g), extra=sink)
            ps_all.append(ps)
            sink_terms.append(jnp.exp2(sink - m))
        n_parts = len(ps_all[0])
        stacked = [jnp.concatenate([ps_all[g][k] for g in range(group)], axis=0) for k in range(n_parts)]
        acc = pv_fn(stacked)
        store(jnp.concatenate([_normalise(acc[g * n_q:(g + 1) * n_q], sink_terms[g]) for g in range(group)], axis=1))

    s_c = _dot_nt(stack_heads(q_ref[0, 0:lc, :]), k_ref[0, 0:lc, :])

    def store_ctx(o):
        o_ref[0, 0:lc, :] = o.astype(o_ref.dtype)

    finish(lambda g: [s_c[g * lc:(g + 1) * lc]], lc,
           lambda st: _dot(st[0], vone_ref[0:lc, :]), store_ctx)

    n_tiles = seq // SWA_TQ

    def tile(t, carry):
        q0 = t * SWA_TQ
        k0 = jnp.clip(q0 - SWA_WINDOW, 0, seq - SWA_TK)
        qs = pl.multiple_of(lc + q0, SWA_WINDOW)
        ks = pl.multiple_of(lc + k0, SWA_WINDOW)
        q4 = stack_heads(q_ref[0, pl.ds(qs, SWA_TQ), :])
        s_loc = _dot_nt(q4, k_ref[0, pl.ds(ks, SWA_TK), :])
        s_ctx = _dot_nt(q4, k_ref[0, 0:lc, :])
        dpos = (lax.broadcasted_iota(jnp.int32, (SWA_TQ, SWA_TK), 1)
                - lax.broadcasted_iota(jnp.int32, (SWA_TQ, SWA_TK), 0)) + (k0 - q0)
        valid = jnp.abs(dpos) <= SWA_WINDOW

        def parts(g):
            sl = slice(g * SWA_TQ, (g + 1) * SWA_TQ)
            return [jnp.where(valid, s_loc[sl], NEG_INF), s_ctx[sl]]

        def store(o):
            o_ref[0, pl.ds(qs, SWA_TQ), :] = o.astype(o_ref.dtype)

        finish(parts, SWA_TQ,
               lambda st: _dot(st[0], vone_ref[pl.ds(ks, SWA_TK), :]) + _dot(st[1], vone_ref[0:lc, :]),
               store)
        return carry

    lax.fori_loop(0, n_tiles, tile, 0, unroll=2)


def window_attention(qkv, sink, ctx_len, n_heads, n_kv_heads):
    b, p, _ = qkv.shape
    seq = p - ctx_len
    group = n_heads // n_kv_heads
    assert seq % SWA_TQ == 0 and seq >= SWA_TK and ctx_len % SWA_WINDOW == 0
    kv = lambda off: pl.BlockSpec((1, p, HEAD_DIM), lambda b, h: (b, 0, off + h))
    qo = pl.BlockSpec((1, p, group * HEAD_DIM), lambda b, h: (b, 0, h))
    return pl.pallas_call(
        functools.partial(_swa_kernel, ctx_len=ctx_len, seq=seq, group=group),
        grid=(b, n_kv_heads),
        in_specs=[pl.BlockSpec(memory_space=pltpu.SMEM), qo, kv(n_heads), kv(n_heads + n_kv_heads)],
        out_specs=qo,
        out_shape=jax.ShapeDtypeStruct((b, p, n_heads * HEAD_DIM), BF16),
        scratch_shapes=[pltpu.VMEM((p, 2 * LANES), BF16)],
        compiler_params=_cparams(("parallel", "parallel")),
        name="swa_attention",
    )(sink.astype(F32) * LOG2E, qkv, qkv, qkv)


def _mla_down_kernel(x_ref, w_ref, gq_ref, gkv_ref, cos_ref, sin_ref, cq_ref, ckv_ref, kr_ref, *, q_rank, kv_rank):
    acc = _dot(x_ref[0], w_ref[...])

    def rms(x, g):
        ms = jnp.mean(x * x, axis=-1, keepdims=True)
        return x * lax.rsqrt(ms + NORM_EPS) * g

    cq_ref[0] = rms(acc[:, :q_rank], gq_ref[...]).astype(cq_ref.dtype)
    ckv_ref[0] = rms(acc[:, q_rank:q_rank + kv_rank], gkv_ref[...]).astype(ckv_ref.dtype)
    kr_ref[0] = _rope(acc[:, q_rank + kv_rank:], cos_ref[...], sin_ref[...]).astype(kr_ref.dtype)


def mla_down(x, w_down, gq, gkv, cos, sin, q_rank, kv_rank):
    b, p, k = x.shape
    n = w_down.shape[1]
    tm = 544 if p % 544 == 0 else _row_tile(p)
    row = lambda width: pl.BlockSpec((1, tm, width), lambda b, i: (b, i, 0))
    return pl.pallas_call(
        functools.partial(_mla_down_kernel, q_rank=q_rank, kv_rank=kv_rank),
        grid=(b, p // tm),
        in_specs=[row(k), pl.BlockSpec((k, n), lambda b, i: (0, 0)),
                  pl.BlockSpec((1, q_rank), lambda b, i: (0, 0)),
                  pl.BlockSpec((1, kv_rank), lambda b, i: (0, 0)),
                  pl.BlockSpec((tm, LANES), lambda b, i: (i, 0)),
                  pl.BlockSpec((tm, LANES), lambda b, i: (i, 0))],
        out_specs=[row(q_rank), row(kv_rank), row(LANES)],
        out_shape=[jax.ShapeDtypeStruct((b, p, q_rank), BF16),
                   jax.ShapeDtypeStruct((b, p, kv_rank), BF16),
                   jax.ShapeDtypeStruct((b, p, LANES), BF16)],
        compiler_params=_cparams(("parallel", "parallel")),
        name="mla_down",
    )(x, w_down, gq.reshape(1, q_rank), gkv.reshape(1, kv_rank), cos, sin)


def _mla_kernel(cq_ref, ckv_ref, kr_ref, wq_ref, wkv_ref, cos_ref, sin_ref, o_ref, qcat_ref, kcat_ref, vone_ref, *,
                ctx_len):
    lc = ctx_len
    p_all = kcat_ref.shape[0]
    kv = _dot(ckv_ref[0], wkv_ref[...])
    kcat_ref[:, 0:LANES] = kv[:, :LANES].astype(BF16)
    kcat_ref[:, LANES:2 * LANES] = kr_ref[0]
    vone_ref[:, 0:LANES] = kv[:, LANES:].astype(BF16)
    vone_ref[:, LANES:2 * LANES] = jnp.ones((p_all, LANES), BF16)
    q = _dot(cq_ref[0], wq_ref[...])
    qcat_ref[:, 0:LANES] = q[:, :LANES].astype(BF16)
    qcat_ref[:, LANES:2 * LANES] = _rope(q[:, LANES:], cos_ref[...], sin_ref[...]).astype(BF16)

    def attend(qs, n_q, chunks):
        q = qcat_ref[pl.ds(qs, n_q), :]
        m = jnp.full((n_q, 1), NEG_INF, F32)
        acc = jnp.zeros((n_q, 2 * LANES), F32)
        for c0, c1 in chunks:
            s = _dot_nt(q, kcat_ref[c0:c1, :])
            m_new = jnp.maximum(m, _row_reduce(jnp.maximum, jnp.max, [s]))
            p = jnp.exp2((s - m_new).astype(BF16))
            acc = jnp.exp2(m - m_new) * acc + _dot(p, vone_ref[c0:c1, :])
            m = m_new
        o_ref[0, pl.ds(qs, n_q), :] = (acc[:, :LANES] * (1.0 / acc[:, LANES:LANES + 1])).astype(o_ref.dtype)

    attend(0, lc, [(0, lc)])
    all_chunks = [(0, lc)] + [(c, c + MLA_TK) for c in range(lc, p_all, MLA_TK)]

    def tile(i, carry):
        attend(pl.multiple_of(lc + i * MLA_TQ, MLA_TQ), MLA_TQ, all_chunks)
        return carry

    lax.fori_loop(0, (p_all - lc) // MLA_TQ, tile, 0, unroll=2)


def latent_attention(cq, ckv, kr, w_q, w_kv, cos, sin, ctx_len, n_heads):
    b, p, q_rank = cq.shape
    kv_rank = ckv.shape[2]
    assert (p - ctx_len) % MLA_TQ == 0 and (p - ctx_len) % MLA_TK == 0 and ctx_len % 16 == 0
    per_batch = lambda width: pl.BlockSpec((1, p, width), lambda b, h: (b, 0, 0), pipeline_mode=pl.Buffered(1))
    table = pl.BlockSpec((p, LANES), lambda b, h: (0, 0), pipeline_mode=pl.Buffered(1))
    return pl.pallas_call(
        functools.partial(_mla_kernel, ctx_len=ctx_len),
        grid=(b, n_heads),
        in_specs=[per_batch(q_rank), per_batch(kv_rank), per_batch(LANES),
                  pl.BlockSpec((q_rank, 2 * LANES), lambda b, h: (0, h)),
                  pl.BlockSpec((kv_rank, 2 * LANES), lambda b, h: (0, h)),
                  table, table],
        out_specs=pl.BlockSpec((1, p, LANES), lambda b, h: (b, 0, h)),
        out_shape=jax.ShapeDtypeStruct((b, p, n_heads * LANES), BF16),
        scratch_shapes=[pltpu.VMEM((p, 2 * LANES), BF16), pltpu.VMEM((p, 2 * LANES), BF16),
                        pltpu.VMEM((p, 2 * LANES), BF16)],
        compiler_params=_cparams(("parallel", "parallel")),
        name="mla_attention",
    )(cq, ckv, kr, w_q, w_kv, cos, sin)


def gather_rows(src, idx):
    n = idx.shape[0]
    width = src.shape[1]
    win = GATHER_WINDOW
    info = plsc.get_sparse_core_info()
    n_workers = info.num_cores * info.num_subcores
    assert n % (n_workers * 2 * win) == 0
    per_worker = n // n_workers
    mesh = plsc.VectorSubcoreMesh(core_axis_name="core", subcore_axis_name="subcore")

    @functools.partial(
        pl.kernel, out_type=jax.ShapeDtypeStruct((n, width), src.dtype), mesh=mesh,
        scratch_types=[pltpu.VMEM((per_worker,), jnp.int32),
                       pltpu.VMEM((win, width), src.dtype), pltpu.VMEM((win, width), src.dtype),
                       pltpu.SemaphoreType.DMA, pltpu.SemaphoreType.DMA,
                       pltpu.SemaphoreType.DMA, pltpu.SemaphoreType.DMA],
        name="gather_rows")
    def gather(src_hbm, idx_hbm, out_hbm, idx_vmem, rows0, rows1, gsem0, gsem1, osem0, osem1):
        worker = lax.axis_index("subcore") * info.num_cores + lax.axis_index("core")
        base = worker * per_worker
        pltpu.sync_copy(idx_hbm.at[pl.ds(base, per_worker)], idx_vmem)

        def fetch(r, rows, sem):
            return pltpu.make_async_copy(src_hbm.at[idx_vmem.at[pl.ds(r, win)]], rows, sem)

        def flush(r, rows, sem):
            return pltpu.make_async_copy(rows, out_hbm.at[pl.ds(base + r, win)], sem)

        fetch(0, rows0, gsem0).start()

        @pl.loop(0, per_worker, step=2 * win)
        def _(r):
            fetch(r, rows0, gsem0).wait()

            @pl.when(r > 0)
            def _():
                flush(r - win, rows1, osem1).wait()

            fetch(r + win, rows1, gsem1).start()
            flush(r, rows0, osem0).start()
            fetch(r + win, rows1, gsem1).wait()
            flush(r, rows0, osem0).wait()

            @pl.when(r + 2 * win < per_worker)
            def _():
                fetch(r + 2 * win, rows0, gsem0).start()

            flush(r + win, rows1, osem1).start()

        flush(per_worker - win, rows1, osem1).wait()

    return gather(src, idx)


def scatter_rows(src, idx, n_out):
    n = idx.shape[0]
    n_src, width = src.shape
    win = GATHER_WINDOW
    info = plsc.get_sparse_core_info()
    n_workers = info.num_cores * info.num_subcores
    assert n % (n_workers * 2 * win) == 0
    per_worker = n // n_workers
    assert n_src % per_worker == 0
    n_steps = per_worker // win
    mesh = plsc.VectorSubcoreMesh(core_axis_name="core", subcore_axis_name="subcore")

    @functools.partial(
        pl.kernel, out_type=jax.ShapeDtypeStruct((n_out, width), src.dtype), mesh=mesh,
        scratch_types=[pltpu.VMEM((n_steps, win), jnp.int32),
                       pltpu.VMEM((win, width), src.dtype), pltpu.VMEM((win, width), src.dtype),
                       pltpu.SemaphoreType.DMA, pltpu.SemaphoreType.DMA,
                       pltpu.SemaphoreType.DMA, pltpu.SemaphoreType.DMA],
        name="scatter_rows")
    def scatter(src_hbm, idx_hbm, out_hbm, idx_vmem, rows0, rows1, lsem0, lsem1, ssem0, ssem1):
        worker = lax.axis_index("subcore") * info.num_cores + lax.axis_index("core")
        src_base = lax.rem(worker * per_worker, n_src)
        pltpu.sync_copy(idx_hbm.at[worker], idx_vmem)

        def load(j, rows, sem):
            return pltpu.make_async_copy(src_hbm.at[pl.ds(src_base + j * win, win)], rows, sem)

        def store(j, rows, sem):
            return pltpu.make_async_copy(rows, out_hbm.at[idx_vmem.at[j]], sem)

        load(0, rows0, lsem0).start()

        @pl.loop(0, n_steps, step=2)
        def _(j):
            load(j, rows0, lsem0).wait()

            @pl.when(j > 0)
            def _():
                store(j - 1, rows1, ssem1).wait()

            load(j + 1, rows1, lsem1).start()
            store(j, rows0, ssem0).start()
            load(j + 1, rows1, lsem1).wait()
            store(j, rows0, ssem0).wait()

            @pl.when(j + 2 < n_steps)
            def _():
                load(j + 2, rows0, lsem0).start()

            store(j + 1, rows1, ssem1).start()

        store(n_steps - 1, rows1, ssem1).wait()

    return scatter(src, idx.reshape(n_workers, n_steps, win))


def _expert_kernel(blk_e_ref, n_used_ref, n_valid_ref, x_ref, wg_ref, wu_ref, wd_ref, y_ref, wg_bf, wu_bf, wd_bf):
    i = pl.program_id(0)
    used = i < n_used_ref[0]
    new_expert = (i == 0) | (blk_e_ref[i] != blk_e_ref[jnp.maximum(i - 1, 0)])

    @pl.when(used & new_expert)
    def _():
        wg_bf[...] = wg_ref[0].astype(BF16)
        wu_bf[...] = wu_ref[0].astype(BF16)
        wd_bf[...] = wd_ref[0].astype(BF16)

    @pl.when(used)
    def _():
        row = lax.broadcasted_iota(jnp.int32, (x_ref.shape[0], 1), 0)
        x = _unpack_bf16_pairs(jnp.where(row < n_valid_ref[i], x_ref[...], 0)).astype(BF16)
        g = _dot(x, wg_bf[...])
        u = _dot(x, wu_bf[...])
        a = (g * jax.nn.sigmoid(g) * u).astype(BF16)
        y_ref[...] = _pack_bf16_pairs(_dot(a, wd_bf[...]))

    @pl.when(jnp.logical_not(used))
    def _():
        y_ref[...] = jnp.zeros_like(y_ref)


def expert_ffn(x_disp, blk_e, n_used, n_valid, w_gate, w_up, w_down, layer):
    rows, half = x_disp.shape
    d = 2 * half
    n_blk = rows // MOE_BLOCK
    de = w_gate.shape[3]
    grid_spec = pltpu.PrefetchScalarGridSpec(
        num_scalar_prefetch=3,
        grid=(n_blk,),
        in_specs=[pl.BlockSpec((MOE_BLOCK, half), lambda i, be, nu, nv: (i, 0)),
                  pl.BlockSpec((None, 1, d, de), lambda i, be, nu, nv: (layer, be[i], 0, 0)),
                  pl.BlockSpec((None, 1, d, de), lambda i, be, nu, nv: (layer, be[i], 0, 0)),
                  pl.BlockSpec((None, 1, de, d), lambda i, be, nu, nv: (layer, be[i], 0, 0))],
        out_specs=pl.BlockSpec((MOE_BLOCK, half), lambda i, be, nu, nv: (i, 0)),
        scratch_shapes=[pltpu.VMEM((d, de), BF16), pltpu.VMEM((d, de), BF16), pltpu.VMEM((de, d), BF16)],
    )
    return pl.pallas_call(
        _expert_kernel,
        grid_spec=grid_spec,
        out_shape=jax.ShapeDtypeStruct((rows, half), jnp.int32),
        compiler_params=_cparams(("arbitrary",)),
        name="expert_ffn",
    )(blk_e, n_used, n_valid, x_disp, w_gate, w_up, w_down)


def _moe_residual(h_ref, y0_ref, y1_ref, wt_ref, mod_ref, gate_idx):
    wt = wt_ref[0]
    moe = wt[:, 0:1] * _unpack_bf16_pairs(y0_ref[0, 0]) + wt[:, 1:2] * _unpack_bf16_pairs(y1_ref[0, 0])
    return h_ref[0] + mod_ref[0, gate_idx:gate_idx + 1, :] * moe


def _combine_next_kernel(h_ref, y0_ref, y1_ref, wt_ref, mod_ref, g_ref, nmod_ref, o_ref, hm_ref, *, gate_idx):
    h_new = _moe_residual(h_ref, y0_ref, y1_ref, wt_ref, mod_ref, gate_idx)
    o_ref[0] = h_new
    hm_ref[0] = _rms_mod(h_new, g_ref[...], nmod_ref[0, 0:1, :], nmod_ref[0, 1:2, :]).astype(hm_ref.dtype)


def _combine_final_kernel(h_ref, y0_ref, y1_ref, wt_ref, mod_ref, g_ref, o_ref, *, gate_idx):
    x = _moe_residual(h_ref, y0_ref, y1_ref, wt_ref, mod_ref, gate_idx)
    ms = jnp.mean(x * x, axis=-1, keepdims=True)
    o_ref[0] = x * lax.rsqrt(ms + NORM_EPS) * g_ref[...]


def combine(h, y_pairs, wt, modtab, gate_idx, next_g, next_modtab, ctx_len):
    b, p, d = h.shape
    last = next_modtab is None
    skip = ctx_len // ROW_TILE if last else 0
    row = pl.BlockSpec((1, ROW_TILE, d), lambda b, j: (b, j + skip, 0))
    mod = pl.BlockSpec((1, N_MOD, d), lambda b, j: (2 * b + jnp.minimum(j + skip, 1), 0, 0))
    in_specs = [row,
                pl.BlockSpec((1, 1, ROW_TILE, d // 2), lambda b, j: (0, b, j + skip, 0)),
                pl.BlockSpec((1, 1, ROW_TILE, d // 2), lambda b, j: (1, b, j + skip, 0)),
                pl.BlockSpec((1, ROW_TILE, LANES), lambda b, j: (b, j + skip, 0)),
                mod, pl.BlockSpec((1, d), lambda b, j: (0, 0))]
    args = [h, y_pairs, y_pairs, wt, modtab.reshape(b * 2, N_MOD, d), next_g.reshape(1, d)]
    out_row = pl.BlockSpec((1, ROW_TILE, d), lambda b, j: (b, j, 0))
    if last:
        return pl.pallas_call(
            functools.partial(_combine_final_kernel, gate_idx=gate_idx),
            grid=(b, (p - ctx_len) // ROW_TILE),
            in_specs=in_specs,
            out_specs=out_row,
            out_shape=jax.ShapeDtypeStruct((b, p - ctx_len, d), F32),
            compiler_params=_cparams(("parallel", "parallel")),
            name="moe_combine_final",
        )(*args)
    return pl.pallas_call(
        functools.partial(_combine_next_kernel, gate_idx=gate_idx),
        grid=(b, p // ROW_TILE),
        in_specs=in_specs + [mod],
        out_specs=[out_row, out_row],
        out_shape=[jax.ShapeDtypeStruct((b, p, d), F32), jax.ShapeDtypeStruct((b, p, d), BF16)],
        compiler_params=_cparams(("parallel", "parallel")),
        name="moe_combine",
    )(*args, next_modtab.reshape(b * 2, N_MOD, d))


def hier_moe(h, f, route, wt, counts, modtab, gate_idx, w_gate, w_up, w_down, layer, next_g, next_modtab, ctx_len):
    b, p, d = h.shape
    n_tok = b * p
    n_assign = 2 * n_tok
    e1, e2, rank1, rank2 = route.reshape(n_tok, LANES)[:, :4].T
    cnt = counts[0, :N_EXPERTS].astype(jnp.int32)
    pcounts = (cnt + MOE_BLOCK - 1) // MOE_BLOCK * MOE_BLOCK
    pend = jnp.cumsum(pcounts)
    pstart = pend - pcounts
    dest_by_slot = jnp.concatenate([pstart[e1] + rank1, pstart[e2] + rank2]).astype(jnp.int32)
    n_blk = -(-n_assign // MOE_BLOCK) + N_EXPERTS
    rows_total = n_blk * MOE_BLOCK
    blk_row0 = jnp.arange(n_blk, dtype=jnp.int32) * MOE_BLOCK
    blk_e = jnp.minimum(jnp.sum((pend[None, :] <= blk_row0[:, None]).astype(jnp.int32), axis=1), N_EXPERTS - 1)
    n_valid = jnp.clip(cnt[blk_e] - (blk_row0 - pstart[blk_e]), 0, MOE_BLOCK).astype(jnp.int32)
    n_used = (pend[-1] // MOE_BLOCK).astype(jnp.int32).reshape(1)

    x_disp = scatter_rows(f.reshape(n_tok, d // 2), dest_by_slot, rows_total)
    y = expert_ffn(x_disp, blk_e, n_used, n_valid, w_gate, w_up, w_down, layer)
    y_pairs = gather_rows(y, dest_by_slot).reshape(2, b, p, d // 2)
    return combine(h, y_pairs, wt, modtab, gate_idx, next_g, next_modtab, ctx_len)


def kernel(x, c, ctx, c_ctx, mod_w, mod_b, norm_mix_g, norm_ffn_g, router_grp_w, router_grp_b, router_exp_w, router_exp_b, exp_w_gate, exp_w_up, exp_w_down, l0_na_w_qkv, l0_na_rpb, l0_na_w_o, l1_swa_w_qkv, l1_swa_sink, l1_swa_w_o, l2_mla_w_dq, l2_mla_q_norm_g, l2_mla_w_uq, l2_mla_w_dkv, l2_mla_kv_norm_g, l2_mla_w_ukv, l2_mla_w_o, l3_na_w_qkv, l3_na_rpb, l3_na_w_o, final_norm_g):
    b, s, d = x.shape
    lc = ctx.shape[1]
    n_heads = d // HEAD_DIM
    n_kv_heads = n_heads // 4
    depth = mod_w.shape[0]
    rows = s // GRID_W

    h = jnp.concatenate([ctx, x], axis=1)
    modtabs = modulation_tables(c, c_ctx, mod_w, mod_b)

    def scale_q_cols(w):
        n_q = n_heads * HEAD_DIM
        q_scale = HEAD_DIM ** -0.5 * LOG2E
        return jnp.concatenate([w[:, :n_q] * q_scale, w[:, n_q:]], axis=1).astype(BF16)

    def na_mixer(hm, w_qkv, rpb):
        qkv = project(hm, scale_q_cols(w_qkv))
        return neighbourhood_attention(qkv, na_bias_table(rpb, rows), lc, n_heads)

    def swa_mixer(hm):
        cos, sin = rope_tables_full(s, lc)
        n_rope = (n_heads + n_kv_heads) * HEAD_DIM // 512
        qkv = project(hm, scale_q_cols(l1_swa_w_qkv), tn=512, rope=(cos, sin, 0, n_rope))
        return window_attention(qkv, l1_swa_sink, lc, n_heads, n_kv_heads)

    def mla_mixer(hm):
        q_rank = l2_mla_w_dq.shape[1]
        kv_rank = l2_mla_kv_norm_g.shape[0]
        cos, sin = rope_tables_mla(s, lc)
        w_down = jnp.concatenate([l2_mla_w_dq, l2_mla_w_dkv[:, :kv_rank],
                                  _spread_rope_cols(l2_mla_w_dkv[:, kv_rank:])], axis=1).astype(BF16)
        cq, ckv, kr = mla_down(hm, w_down, l2_mla_q_norm_g, l2_mla_kv_norm_g, cos, sin, q_rank, kv_rank)
        w_uq = l2_mla_w_uq.reshape(q_rank, n_heads, MLA_NOPE_DIM + MLA_ROPE_DIM)
        q_scale = (MLA_NOPE_DIM + MLA_ROPE_DIM) ** -0.5 * LOG2E
        w_q = jnp.concatenate([jnp.concatenate([w_uq[:, hh, :MLA_NOPE_DIM], _spread_rope_cols(w_uq[:, hh, MLA_NOPE_DIM:])],
                                               axis=1) for hh in range(n_heads)], axis=1)
        return latent_attention(cq, ckv, kr, (w_q * q_scale).astype(BF16), l2_mla_w_ukv.astype(BF16), cos, sin,
                                lc, n_heads)

    hm = norm_modulate(h, norm_mix_g[0], modtabs[0], 0)
    for i in range(depth):
        modtab = modtabs[i]
        mixer = i % 3
        if mixer == 0:
            w_qkv, rpb, w_o = (l0_na_w_qkv, l0_na_rpb, l0_na_w_o) if i == 0 else (l3_na_w_qkv, l3_na_rpb, l3_na_w_o)
            y = na_mixer(hm, w_qkv, rpb)
        elif mixer == 1:
            y, w_o = swa_mixer(hm), l1_swa_w_o
        else:
            y, w_o = mla_mixer(hm), l2_mla_w_o
        h, f, route, wt, counts = attn_out_route(y, w_o.astype(BF16), h, modtab, norm_ffn_g[i], router_grp_w[i],
                                                 router_grp_b[i], router_exp_w[i], router_exp_b[i], lc)
        if i + 1 < depth:
            h, hm = hier_moe(h, f, route, wt, counts, modtab, 5, exp_w_gate, exp_w_up, exp_w_down, i,
                             norm_mix_g[i + 1], modtabs[i + 1], lc)
        else:
            return hier_moe(h, f, route, wt, counts, modtab, 5, exp_w_gate, exp_w_up, exp_w_down, i,
                            final_norm_g, None, lc)
```

```python
import functools

import numpy as np
import jax
import jax.numpy as jnp
from jax import lax
from jax.experimental import pallas as pl
from jax.experimental.pallas import tpu as pltpu
from jax.experimental.pallas import tpu_sc as plsc

GRID_W = 64
HEAD_DIM = 128
ROPE_BASE = 10000.0
NORM_EPS = 1e-6
NEG_INF = -1e30
N_MOD = 6

NA_ROWS = 8
NA_COLS = 16
NA_TILE_ROWS = 4
NA_WIN_ROWS = 12

SWA_WINDOW = 128
SWA_TQ = 256
SWA_TK = 512

MLA_NOPE_DIM = 128
MLA_ROPE_DIM = 64
MLA_TQ = 512
MLA_TK = 512
LOG2E = 1.4426950408889634

N_GROUPS = 4
EXPERTS_PER_GROUP = 8
N_EXPERTS = N_GROUPS * EXPERTS_PER_GROUP
MOE_BLOCK = 512
GATHER_WINDOW = 32

LANES = 128
ROW_TILE = 256
VMEM_LIMIT = 56 * 1024 * 1024

BF16 = jnp.bfloat16
F32 = jnp.float32


def _cparams(sem):
    return pltpu.CompilerParams(dimension_semantics=sem, vmem_limit_bytes=VMEM_LIMIT)


def _dot(a, b):
    return jnp.dot(a, b, preferred_element_type=F32)


def _dot_nt(a, b):
    return lax.dot_general(a, b, (((1,), (1,)), ((), ())), preferred_element_type=F32)


def _mod_kernel(x_ref, w_ref, b_ref, o_ref):
    x = x_ref[...]
    sx = (x * jax.nn.sigmoid(x)).astype(BF16)
    o_ref[0] = _dot(sx, w_ref[0].astype(BF16)) + b_ref[0]


def modulation_tables(c, c_ctx, mod_w, mod_b):
    depth, d, n_out = mod_w.shape
    b = c.shape[0]
    rows = 16
    xin = jnp.zeros((rows, d), F32).at[:b].set(c).at[b].set(c_ctx)
    tn = 1024
    out = pl.pallas_call(
        _mod_kernel,
        grid=(depth, n_out // tn),
        in_specs=[pl.BlockSpec((rows, d), lambda i, j: (0, 0)),
                  pl.BlockSpec((1, d, tn), lambda i, j: (i, 0, j)),
                  pl.BlockSpec((1, 1, tn), lambda i, j: (i, 0, j))],
        out_specs=pl.BlockSpec((1, rows, tn), lambda i, j: (i, 0, j)),
        out_shape=jax.ShapeDtypeStruct((depth, rows, n_out), F32),
        compiler_params=_cparams(("parallel", "parallel")),
        name="adaln_mod",
    )(xin, mod_w, mod_b.reshape(depth, 1, n_out))
    lat = out[:, :b].reshape(depth, b, 1, N_MOD, d)
    ctx = jnp.broadcast_to(out[:, b].reshape(depth, 1, 1, N_MOD, d), (depth, b, 1, N_MOD, d))
    return jnp.concatenate([ctx, lat], axis=2)


def _rms_mod(x, g, shift, scale):
    ms = jnp.mean(x * x, axis=-1, keepdims=True)
    y = x * lax.rsqrt(ms + NORM_EPS) * g
    return y * (1.0 + scale) + shift


def _join_norm_mod_kernel(ctx_ref, x_ref, g_ref, mod_ref, h_ref, hm_ref, *, ctx_tiles):
    j = pl.program_id(1)

    def emit(src_ref):
        h_ref[0] = src_ref[0]
        hm_ref[0] = _rms_mod(src_ref[0], g_ref[...], mod_ref[0, 0:1, :], mod_ref[0, 1:2, :]).astype(hm_ref.dtype)

    @pl.when(j < ctx_tiles)
    def _():
        emit(ctx_ref)

    @pl.when(j >= ctx_tiles)
    def _():
        emit(x_ref)


def _route(logits):
    lane = lax.broadcasted_iota(jnp.int32, logits.shape, 1).astype(F32)
    big = float(LANES)

    def first_lane(mask):
        return jnp.min(jnp.where(mask, lane, big), axis=-1, keepdims=True)

    in_grp = lane < N_GROUPS
    lg = jnp.where(in_grp, logits, NEG_INF)
    m_g = jnp.max(lg, axis=-1, keepdims=True)
    g_idx = first_lane(in_grp & (lg == m_g))
    g_w = 1.0 / jnp.sum(jnp.where(in_grp, jnp.exp(lg - m_g), 0.0), axis=-1, keepdims=True)
    e_lo = N_GROUPS + g_idx * EXPERTS_PER_GROUP
    in_e = (lane >= e_lo) & (lane < e_lo + EXPERTS_PER_GROUP)
    le = jnp.where(in_e, logits, NEG_INF)
    m1 = jnp.max(le, axis=-1, keepdims=True)
    e1 = first_lane(in_e & (le == m1))
    s_e = jnp.sum(jnp.where(in_e, jnp.exp(le - m1), 0.0), axis=-1, keepdims=True)
    in_e2 = in_e & (lane != e1)
    le2 = jnp.where(in_e2, logits, NEG_INF)
    m2 = jnp.max(le2, axis=-1, keepdims=True)
    e2 = first_lane(in_e2 & (le2 == m2))
    p1 = 1.0 / s_e
    p2 = jnp.exp(m2 - m1) / s_e
    den = p1 + p2
    return ((e1 - N_GROUPS).astype(jnp.int32), (e2 - N_GROUPS).astype(jnp.int32),
            g_w * p1 / den, g_w * p2 / den)


def _pack_bf16_pairs(x):
    n = x.shape[1] // 2
    xb = x.astype(BF16).astype(F32)
    hi = lax.bitcast_convert_type(xb[:, :n], jnp.int32)
    lo = lax.bitcast_convert_type(xb[:, n:], jnp.int32)
    return (hi & jnp.int32(-65536)) | lax.shift_right_logical(lo, jnp.int32(16))


def _unpack_bf16_pairs(w):
    hi = lax.bitcast_convert_type(w & jnp.int32(-65536), F32)
    lo = lax.bitcast_convert_type(lax.shift_left(w, jnp.int32(16)), F32)
    return jnp.concatenate([hi, lo], axis=1)


def _route_and_rank(f, wr_ref, br_ref, run_ref):
    f_hi = f.astype(BF16)
    f_lo = (f - f_hi.astype(F32)).astype(BF16)
    hi_terms = _dot(f_hi, wr_ref[...])
    logits = (hi_terms[:, :LANES] + _dot(f_lo, wr_ref[:, :LANES]) + hi_terms[:, LANES:]) + br_ref[...]
    e1, e2, w1, w2 = _route(logits)
    lane = lax.broadcasted_iota(jnp.int32, logits.shape, 1)
    pick1, pick2 = lane == e1, lane == e2
    chosen = (pick1 | pick2).astype(F32)
    n_rows = chosen.shape[0]
    earlier = (lax.broadcasted_iota(jnp.int32, (n_rows, n_rows), 1)
               < lax.broadcasted_iota(jnp.int32, (n_rows, n_rows), 0)).astype(BF16)
    before = run_ref[...] + _dot(earlier, chosen.astype(BF16))
    rank1 = jnp.sum(jnp.where(pick1, before, 0.0), axis=-1, keepdims=True).astype(jnp.int32)
    rank2 = jnp.sum(jnp.where(pick2, before, 0.0), axis=-1, keepdims=True).astype(jnp.int32)
    run_ref[...] = run_ref[...] + jnp.sum(chosen, axis=0, keepdims=True)
    route = jnp.where(lane == 0, e1, jnp.where(lane == 1, e2, jnp.where(lane == 2, rank1,
                                                                         jnp.where(lane == 3, rank2, 0))))
    return route, jnp.where(lane == 0, w1, jnp.where(lane == 1, w2, 0.0))


def _attn_out_route_kernel(y_ref, w_ref, h_ref, mod_ref, g_ref, wr_ref, br_ref,
                           ho_ref, f_ref, route_ref, wt_ref, cnt_ref, run_ref, *, ctx_len, tm, n_sub):
    first = (pl.program_id(0) == 0) & (pl.program_id(1) == 0)

    @pl.when(first)
    def _():
        run_ref[...] = jnp.zeros_like(run_ref)

    sub = tm // n_sub
    for s in range(n_sub):
        rows = slice(s * sub, (s + 1) * sub)
        acc = _dot(y_ref[0, rows, :], w_ref[...])
        pos = pl.program_id(1) * tm + s * sub + lax.broadcasted_iota(jnp.int32, (sub, 1), 0)
        is_ctx = pos < ctx_len

        def mod_row(k):
            return jnp.where(is_ctx, mod_ref[0, 0, k:k + 1, :], mod_ref[0, 1, k:k + 1, :])

        h_new = h_ref[0, rows, :] + mod_row(2) * acc
        ho_ref[0, rows, :] = h_new
        f = _rms_mod(h_new, g_ref[...], mod_row(3), mod_row(4))
        f_ref[0, rows, :] = _pack_bf16_pairs(f)
        route, wt = _route_and_rank(f, wr_ref, br_ref, run_ref)
        route_ref[0, rows, :] = route
        wt_ref[0, rows, :] = wt
    cnt_ref[...] = run_ref[...]


def _mod_spec(d):
    return pl.BlockSpec((1, N_MOD, d), lambda b, j: (2 * b + jnp.minimum(j, 1), 0, 0))


def join_norm_modulate(ctx, x, g, modtab):
    b, lc, d = ctx.shape
    p = lc + x.shape[1]
    assert lc == ROW_TILE
    ctx_tiles = lc // ROW_TILE
    row = pl.BlockSpec((1, ROW_TILE, d), lambda b, j: (b, j, 0))
    return pl.pallas_call(
        functools.partial(_join_norm_mod_kernel, ctx_tiles=ctx_tiles),
        grid=(b, p // ROW_TILE),
        in_specs=[pl.BlockSpec((1, ROW_TILE, d), lambda b, j: (b, jnp.minimum(j, ctx_tiles - 1), 0)),
                  pl.BlockSpec((1, ROW_TILE, d), lambda b, j: (b, jnp.maximum(j - ctx_tiles, 0), 0)),
                  pl.BlockSpec((1, d), lambda b, j: (0, 0)), _mod_spec(d)],
        out_specs=[row, row],
        out_shape=[jax.ShapeDtypeStruct((b, p, d), F32), jax.ShapeDtypeStruct((b, p, d), BF16)],
        compiler_params=_cparams(("parallel", "parallel")),
        name="join_norm_mod",
    )(ctx, x, g.reshape(1, d), modtab.reshape(b * 2, N_MOD, d))


def attn_out_route(y, w_o, h, modtab, g, w_grp, b_grp, w_rt, b_rt, ctx_len):
    b, p, d = h.shape
    n_r = N_GROUPS + N_EXPERTS
    wr = jnp.zeros((d, LANES), F32).at[:, :N_GROUPS].set(w_grp).at[:, N_GROUPS:n_r].set(w_rt)
    br = jnp.zeros((1, LANES), F32).at[0, :N_GROUPS].set(b_grp).at[0, N_GROUPS:n_r].set(b_rt)
    wr_hi = wr.astype(BF16)
    wr = jnp.concatenate([wr_hi, (wr - wr_hi.astype(F32)).astype(BF16)], axis=1)
    tm = 544 if p % 544 == 0 else ROW_TILE
    n_sub = 2
    assert (tm // n_sub) % 16 == 0
    row = lambda width: pl.BlockSpec((1, tm, width), lambda b, i: (b, i, 0))
    once = lambda shape: pl.BlockSpec(shape, lambda b, i: (0,) * len(shape), pipeline_mode=pl.Buffered(1))
    return pl.pallas_call(
        functools.partial(_attn_out_route_kernel, ctx_len=ctx_len, tm=tm, n_sub=n_sub),
        grid=(b, p // tm),
        in_specs=[row(d), once((d, d)), row(d),
                  pl.BlockSpec((1, 2, N_MOD, d), lambda b, i: (b, 0, 0, 0)),
                  once((1, d)), once((d, 2 * LANES)), once((1, LANES))],
        out_specs=[row(d), row(d // 2), row(LANES), row(LANES), pl.BlockSpec((1, LANES), lambda b, i: (0, 0))],
        out_shape=[jax.ShapeDtypeStruct((b, p, d), F32),
                   jax.ShapeDtypeStruct((b, p, d // 2), jnp.int32),
                   jax.ShapeDtypeStruct((b, p, LANES), jnp.int32),
                   jax.ShapeDtypeStruct((b, p, LANES), F32),
                   jax.ShapeDtypeStruct((1, LANES), F32)],
        scratch_shapes=[pltpu.VMEM((1, LANES), F32)],
        compiler_params=_cparams(("arbitrary", "arbitrary")),
        name="attn_out_route",
    )(y, w_o, h, modtab, g.reshape(1, d), wr, br)


def _rope(acc, cos, sin):
    n_blk = acc.shape[1] // LANES
    outs = []
    for c in range(n_blk):
        x = acc[:, c * LANES:(c + 1) * LANES]
        outs.append(x * cos + pltpu.roll(x, LANES // 2, 1) * sin)
    return outs[0] if n_blk == 1 else jnp.concatenate(outs, axis=1)


def _proj_kernel(*refs, rope_lo, rope_hi):
    x_ref, w_ref = refs[0], refs[1]
    o_ref = refs[-1]
    acc = _dot(x_ref[0], w_ref[...])
    if rope_hi > rope_lo:
        cos_ref, sin_ref = refs[2], refs[3]
        j = pl.program_id(2)
        roped = (j >= rope_lo) & (j < rope_hi)

        @pl.when(roped)
        def _():
            o_ref[0] = _rope(acc, cos_ref[...], sin_ref[...]).astype(o_ref.dtype)

        @pl.when(jnp.logical_not(roped))
        def _():
            o_ref[0] = acc.astype(o_ref.dtype)
    else:
        o_ref[0] = acc.astype(o_ref.dtype)


def _row_tile(p):
    for cand in (1088, 1024, 544, 512, 272, 256, 128, 64, 32, 16):
        if p % cand == 0:
            return cand
    raise ValueError(p)


def project(x, w, *, tn=512, out_dtype=BF16, rope=None):
    b, p, k = x.shape
    n = w.shape[1]
    tm = _row_tile(p)
    tn = min(tn, n)
    assert n % tn == 0
    in_specs = [pl.BlockSpec((1, tm, k), lambda b, i, j: (b, i, 0)),
                pl.BlockSpec((k, tn), lambda b, i, j: (0, j))]
    args = [x, w]
    kw = dict(rope_lo=0, rope_hi=0)
    if rope is not None:
        cos, sin, lo, hi = rope
        in_specs += [pl.BlockSpec((tm, LANES), lambda b, i, j: (i, 0)),
                     pl.BlockSpec((tm, LANES), lambda b, i, j: (i, 0))]
        args += [cos, sin]
        kw.update(rope_lo=lo, rope_hi=hi)
    return pl.pallas_call(
        functools.partial(_proj_kernel, **kw),
        grid=(b, p // tm, n // tn),
        in_specs=in_specs,
        out_specs=pl.BlockSpec((1, tm, tn), lambda b, i, j: (b, i, j)),
        out_shape=jax.ShapeDtypeStruct((b, p, n), out_dtype),
        compiler_params=_cparams(("parallel", "parallel", "arbitrary")),
        name="project",
    )(*args)


def _axial_cos_sin(n, rot_dim):
    t = jnp.arange(n, dtype=jnp.int32)
    row = (t // GRID_W).astype(F32)
    col = (t % GRID_W).astype(F32)
    n_freq = rot_dim // 4
    inv = ROPE_BASE ** (-jnp.arange(n_freq, dtype=F32) / n_freq)
    ang = jnp.concatenate([row[:, None] * inv, col[:, None] * inv], axis=-1)
    return jnp.cos(ang), jnp.sin(ang)


def rope_tables_full(s, ctx_len):
    c, sn = _axial_cos_sin(s, HEAD_DIM)
    cos = jnp.concatenate([c, c], axis=1)
    sin = jnp.concatenate([-sn, sn], axis=1)
    ident_c = jnp.ones((ctx_len, LANES), F32)
    ident_s = jnp.zeros((ctx_len, LANES), F32)
    return jnp.concatenate([ident_c, cos], axis=0), jnp.concatenate([ident_s, sin], axis=0)


def rope_tables_mla(s, ctx_len):
    c, sn = _axial_cos_sin(s, MLA_ROPE_DIM)
    one = jnp.ones_like(c)
    zero = jnp.zeros_like(c)
    cos = jnp.concatenate([c, one, c, one], axis=1)
    sin = jnp.concatenate([-sn, zero, sn, zero], axis=1)
    ident_c = jnp.ones((ctx_len, LANES), F32)
    ident_s = jnp.zeros((ctx_len, LANES), F32)
    return jnp.concatenate([ident_c, cos], axis=0), jnp.concatenate([ident_s, sin], axis=0)


def _spread_rope_cols(w_rope):
    k = w_rope.shape[0]
    half = MLA_ROPE_DIM // 2
    z = jnp.zeros((k, half), w_rope.dtype)
    return jnp.concatenate([w_rope[:, :half], z, w_rope[:, half:], z], axis=1)


def _softmax_probs(parts, extra=None):
    m = _row_reduce(jnp.maximum, jnp.max, parts)
    if extra is not None:
        m = jnp.maximum(m, extra)
    return [jnp.exp2((s - m).astype(BF16)) for s in parts], m


def _with_ones(v_ref, vone_ref):
    vone_ref[:, 0:LANES] = v_ref[0]
    vone_ref[:, LANES:2 * LANES] = jnp.ones((vone_ref.shape[0], LANES), BF16)


def _normalise(acc, extra_den=None):
    den = acc[:, LANES:LANES + 1]
    if extra_den is not None:
        den = den + extra_den
    return acc[:, :LANES] * (1.0 / den)


def _row_reduce(combine, reduce, parts):
    blocks = [s[:, c:c + LANES] for s in parts for c in range(0, s.shape[1], LANES)]
    acc = blocks[0]
    for blk in blocks[1:]:
        acc = combine(acc, blk)
    return reduce(acc, axis=-1, keepdims=True)


def na_bias_table(rpb, rows):
    n_tiles = rows // NA_TILE_ROWS
    n_heads, _, n_dcol = rpb.shape
    drow, row_ok = [], []
    for tile in (0, 1, n_tiles - 1):
        kr0 = int(np.clip(NA_TILE_ROWS * tile - NA_ROWS // 2, 0, rows - NA_WIN_ROWS))
        r = NA_TILE_ROWS * tile + np.arange(NA_TILE_ROWS)
        r0 = np.clip(r - NA_ROWS // 2, 0, rows - NA_ROWS)
        krow = kr0 + np.arange(NA_WIN_ROWS)
        row_ok.append((krow[None, :] >= r0[:, None]) & (krow[None, :] < r0[:, None] + NA_ROWS))
        drow.append(np.clip(krow[None, :] - r[:, None] + NA_ROWS - 1, 0, 2 * NA_ROWS - 2))
    drow, row_ok = np.stack(drow), np.stack(row_ok)
    qc = np.arange(GRID_W)
    qcol0 = np.clip(qc - NA_COLS // 2, 0, GRID_W - NA_COLS)
    kc = np.arange(GRID_W)
    col_ok = (kc[None, :] >= qcol0[:, None]) & (kc[None, :] < qcol0[:, None] + NA_COLS)
    dcol = np.clip(kc[None, :] - qc[:, None] + NA_COLS - 1, 0, 2 * NA_COLS - 2)
    pick_col = jnp.asarray(dcol[None] == np.arange(n_dcol)[:, None, None], F32)
    slabs = jnp.einsum('hrd,dqk->hrqk', rpb.astype(F32), pick_col, precision=lax.Precision.HIGHEST)
    slabs = jnp.where(jnp.asarray(col_ok)[None, None], slabs * LOG2E, NEG_INF)
    masked = jnp.full((n_heads, GRID_W, GRID_W), NEG_INF, F32)
    pats = []
    for p in range(3):
        per_row = [jnp.stack([slabs[:, drow[p, a, m]] if row_ok[p, a, m] else masked
                              for m in range(NA_WIN_ROWS)], axis=2)
                   for a in range(NA_TILE_ROWS)]
        pats.append(jnp.stack(per_row, axis=1))
    vals = jnp.stack(pats, axis=1)
    return vals.reshape(n_heads, 3, NA_TILE_ROWS * GRID_W, NA_WIN_ROWS * GRID_W)


def _na_kernel(q_ref, k_ref, v_ref, bias_ref, o_ref, vone_ref, *, ctx_len, rows):
    tq = NA_TILE_ROWS * GRID_W
    tk = NA_WIN_ROWS * GRID_W
    n_tiles = rows // NA_TILE_ROWS
    lc = ctx_len
    _with_ones(v_ref, vone_ref)

    s = _dot_nt(q_ref[0, 0:lc, :], k_ref[0, 0:lc, :])
    (p,), _ = _softmax_probs([s])
    o_ref[0, 0:lc, :] = _normalise(_dot(p, vone_ref[0:lc, :])).astype(o_ref.dtype)

    def tile(i, carry):
        qs = pl.multiple_of(lc + i * tq, tq)
        kr0 = jnp.clip(NA_TILE_ROWS * i - NA_ROWS // 2, 0, rows - NA_WIN_ROWS)
        ks = pl.multiple_of(lc + kr0 * GRID_W, NA_TILE_ROWS * GRID_W)
        pat = jnp.where(i == 0, 0, jnp.where(i == n_tiles - 1, 2, 1))
        q = q_ref[0, pl.ds(qs, tq), :]
        s_loc = _dot_nt(q, k_ref[0, pl.ds(ks, tk), :]) + bias_ref[0, pat]
        s_ctx = _dot_nt(q, k_ref[0, 0:lc, :])
        (p_loc, p_ctx), _ = _softmax_probs([s_loc, s_ctx])
        acc = _dot(p_loc, vone_ref[pl.ds(ks, tk), :]) + _dot(p_ctx, vone_ref[0:lc, :])
        o_ref[0, pl.ds(qs, tq), :] = _normalise(acc).astype(o_ref.dtype)
        return carry

    lax.fori_loop(0, n_tiles, tile, 0, unroll=4)


def neighbourhood_attention(qkv, bias, ctx_len, n_heads):
    b, p, _ = qkv.shape
    rows = (p - ctx_len) // GRID_W
    assert rows % NA_TILE_ROWS == 0 and rows >= NA_WIN_ROWS
    tq, tk = NA_TILE_ROWS * GRID_W, NA_WIN_ROWS * GRID_W
    assert ctx_len % 16 == 0 and ctx_len % tq == 0
    blk = lambda off: pl.BlockSpec((1, p, HEAD_DIM), lambda h, b: (b, 0, off + h))
    return pl.pallas_call(
        functools.partial(_na_kernel, ctx_len=ctx_len, rows=rows),
        grid=(n_heads, b),
        in_specs=[blk(0), blk(n_heads), blk(2 * n_heads),
                  pl.BlockSpec((1, 3, tq, tk), lambda h, b: (h, 0, 0, 0))],
        out_specs=blk(0),
        out_shape=jax.ShapeDtypeStruct((b, p, n_heads * HEAD_DIM), BF16),
        scratch_shapes=[pltpu.VMEM((p, 2 * LANES), BF16)],
        compiler_params=_cparams(("parallel", "parallel")),
        name="na_attention",
    )(qkv, qkv, qkv, bias)


def _swa_kernel(sink_ref, q_ref, k_ref, v_ref, o_ref, vone_ref, *, ctx_len, seq, group):
    lc = ctx_len
    kvh = pl.program_id(1)
    _with_ones(v_ref, vone_ref)

    def stack_heads(q):
        return jnp.concatenate([q[:, g * HEAD_DIM:(g + 1) * HEAD_DIM] for g in range(group)], axis=0)

    def finish(parts_fn, n_q, pv_fn, store):
        ps_all, sink_terms = [], []
        for g in range(group):
            sink = sink_ref[kvh * group + g]
            ps, m = _softmax_probs(parts_fn(g), extra=sink)
            ps_all.append(ps)
            sink_terms.append(jnp.exp2(sink - m))
        n_parts = len(ps_all[0])
        stacked = [jnp.concatenate([ps_all[g][k] for g in range(group)], axis=0) for k in range(n_parts)]
        acc = pv_fn(stacked)
        store(jnp.concatenate([_normalise(acc[g * n_q:(g + 1) * n_q], sink_terms[g]) for g in range(group)], axis=1))

    s_c = _dot_nt(stack_heads(q_ref[0, 0:lc, :]), k_ref[0, 0:lc, :])

    def store_ctx(o):
        o_ref[0, 0:lc, :] = o.astype(o_ref.dtype)

    finish(lambda g: [s_c[g * lc:(g + 1) * lc]], lc,
           lambda st: _dot(st[0], vone_ref[0:lc, :]), store_ctx)

    n_tiles = seq // SWA_TQ

    def tile(t, carry):
        q0 = t * SWA_TQ
        k0 = jnp.clip(q0 - SWA_WINDOW, 0, seq - SWA_TK)
        qs = pl.multiple_of(lc + q0, SWA_WINDOW)
        ks = pl.multiple_of(lc + k0, SWA_WINDOW)
        q4 = stack_heads(q_ref[0, pl.ds(qs, SWA_TQ), :])
        s_loc = _dot_nt(q4, k_ref[0, pl.ds(ks, SWA_TK), :])
        s_ctx = _dot_nt(q4, k_ref[0, 0:lc, :])
        dpos = (lax.broadcasted_iota(jnp.int32, (SWA_TQ, SWA_TK), 1)
                - lax.broadcasted_iota(jnp.int32, (SWA_TQ, SWA_TK), 0)) + (k0 - q0)
        valid = jnp.abs(dpos) <= SWA_WINDOW

        def parts(g):
            sl = slice(g * SWA_TQ, (g + 1) * SWA_TQ)
            return [jnp.where(valid, s_loc[sl], NEG_INF), s_ctx[sl]]

        def store(o):
            o_ref[0, pl.ds(qs, SWA_TQ), :] = o.astype(o_ref.dtype)

        finish(parts, SWA_TQ,
               lambda st: _dot(st[0], vone_ref[pl.ds(ks, SWA_TK), :]) + _dot(st[1], vone_ref[0:lc, :]),
               store)
        return carry

    lax.fori_loop(0, n_tiles, tile, 0, unroll=2)


def window_attention(qkv, sink, ctx_len, n_heads, n_kv_heads):
    b, p, _ = qkv.shape
    seq = p - ctx_len
    group = n_heads // n_kv_heads
    assert seq % SWA_TQ == 0 and seq >= SWA_TK and ctx_len % SWA_WINDOW == 0
    kv = lambda off: pl.BlockSpec((1, p, HEAD_DIM), lambda b, h: (b, 0, off + h))
    qo = pl.BlockSpec((1, p, group * HEAD_DIM), lambda b, h: (b, 0, h))
    return pl.pallas_call(
        functools.partial(_swa_kernel, ctx_len=ctx_len, seq=seq, group=group),
        grid=(b, n_kv_heads),
        in_specs=[pl.BlockSpec(memory_space=pltpu.SMEM), qo, kv(n_heads), kv(n_heads + n_kv_heads)],
        out_specs=qo,
        out_shape=jax.ShapeDtypeStruct((b, p, n_heads * HEAD_DIM), BF16),
        scratch_shapes=[pltpu.VMEM((p, 2 * LANES), BF16)],
        compiler_params=_cparams(("parallel", "parallel")),
        name="swa_attention",
    )(sink.astype(F32) * LOG2E, qkv, qkv, qkv)


def _mla_down_kernel(x_ref, w_ref, gq_ref, gkv_ref, cos_ref, sin_ref, cq_ref, ckv_ref, kr_ref, *, q_rank, kv_rank):
    acc = _dot(x_ref[0], w_ref[...])

    def rms(x, g):
        ms = jnp.mean(x * x, axis=-1, keepdims=True)
        return x * lax.rsqrt(ms + NORM_EPS) * g

    cq_ref[0] = rms(acc[:, :q_rank], gq_ref[...]).astype(cq_ref.dtype)
    ckv_ref[0] = rms(acc[:, q_rank:q_rank + kv_rank], gkv_ref[...]).astype(ckv_ref.dtype)
    kr_ref[0] = _rope(acc[:, q_rank + kv_rank:], cos_ref[...], sin_ref[...]).astype(kr_ref.dtype)


def mla_down(x, w_down, gq, gkv, cos, sin, q_rank, kv_rank):
    b, p, k = x.shape
    n = w_down.shape[1]
    tm = 544 if p % 544 == 0 else _row_tile(p)
    row = lambda width: pl.BlockSpec((1, tm, width), lambda b, i: (b, i, 0))
    return pl.pallas_call(
        functools.partial(_mla_down_kernel, q_rank=q_rank, kv_rank=kv_rank),
        grid=(b, p // tm),
        in_specs=[row(k), pl.BlockSpec((k, n), lambda b, i: (0, 0)),
                  pl.BlockSpec((1, q_rank), lambda b, i: (0, 0)),
                  pl.BlockSpec((1, kv_rank), lambda b, i: (0, 0)),
                  pl.BlockSpec((tm, LANES), lambda b, i: (i, 0)),
                  pl.BlockSpec((tm, LANES), lambda b, i: (i, 0))],
        out_specs=[row(q_rank), row(kv_rank), row(LANES)],
        out_shape=[jax.ShapeDtypeStruct((b, p, q_rank), BF16),
                   jax.ShapeDtypeStruct((b, p, kv_rank), BF16),
                   jax.ShapeDtypeStruct((b, p, LANES), BF16)],
        compiler_params=_cparams(("parallel", "parallel")),
        name="mla_down",
    )(x, w_down, gq.reshape(1, q_rank), gkv.reshape(1, kv_rank), cos, sin)


def _mla_kernel(cq_ref, ckv_ref, kr_ref, wq_ref, wkv_ref, cos_ref, sin_ref, o_ref, qcat_ref, kcat_ref, vone_ref, *,
                ctx_len):
    lc = ctx_len
    p_all = kcat_ref.shape[0]
    kv = _dot(ckv_ref[0], wkv_ref[...])
    kcat_ref[:, 0:LANES] = kv[:, :LANES].astype(BF16)
    kcat_ref[:, LANES:2 * LANES] = kr_ref[0]
    vone_ref[:, 0:LANES] = kv[:, LANES:].astype(BF16)
    vone_ref[:, LANES:2 * LANES] = jnp.ones((p_all, LANES), BF16)
    q = _dot(cq_ref[0], wq_ref[...])
    qcat_ref[:, 0:LANES] = q[:, :LANES].astype(BF16)
    qcat_ref[:, LANES:2 * LANES] = _rope(q[:, LANES:], cos_ref[...], sin_ref[...]).astype(BF16)

    def attend(qs, n_q, chunks):
        q = qcat_ref[pl.ds(qs, n_q), :]
        m = jnp.full((n_q, 1), NEG_INF, F32)
        acc = jnp.zeros((n_q, 2 * LANES), F32)
        for c0, c1 in chunks:
            s = _dot_nt(q, kcat_ref[c0:c1, :])
            m_new = jnp.maximum(m, _row_reduce(jnp.maximum, jnp.max, [s]))
            p = jnp.exp2((s - m_new).astype(BF16))
            acc = jnp.exp2(m - m_new) * acc + _dot(p, vone_ref[c0:c1, :])
            m = m_new
        o_ref[0, pl.ds(qs, n_q), :] = (acc[:, :LANES] * (1.0 / acc[:, LANES:LANES + 1])).astype(o_ref.dtype)

    attend(0, lc, [(0, lc)])
    all_chunks = [(0, lc)] + [(c, c + MLA_TK) for c in range(lc, p_all, MLA_TK)]

    def tile(i, carry):
        attend(pl.multiple_of(lc + i * MLA_TQ, MLA_TQ), MLA_TQ, all_chunks)
        return carry

    lax.fori_loop(0, (p_all - lc) // MLA_TQ, tile, 0, unroll=2)


def latent_attention(cq, ckv, kr, w_q, w_kv, cos, sin, ctx_len, n_heads):
    b, p, q_rank = cq.shape
    kv_rank = ckv.shape[2]
    assert (p - ctx_len) % MLA_TQ == 0 and (p - ctx_len) % MLA_TK == 0 and ctx_len % 16 == 0
    per_batch = lambda width: pl.BlockSpec((1, p, width), lambda b, h: (b, 0, 0), pipeline_mode=pl.Buffered(1))
    table = pl.BlockSpec((p, LANES), lambda b, h: (0, 0), pipeline_mode=pl.Buffered(1))
    return pl.pallas_call(
        functools.partial(_mla_kernel, ctx_len=ctx_len),
        grid=(b, n_heads),
        in_specs=[per_batch(q_rank), per_batch(kv_rank), per_batch(LANES),
                  pl.BlockSpec((q_rank, 2 * LANES), lambda b, h: (0, h)),
                  pl.BlockSpec((kv_rank, 2 * LANES), lambda b, h: (0, h)),
                  table, table],
        out_specs=pl.BlockSpec((1, p, LANES), lambda b, h: (b, 0, h)),
        out_shape=jax.ShapeDtypeStruct((b, p, n_heads * LANES), BF16),
        scratch_shapes=[pltpu.VMEM((p, 2 * LANES), BF16), pltpu.VMEM((p, 2 * LANES), BF16),
                        pltpu.VMEM((p, 2 * LANES), BF16)],
        compiler_params=_cparams(("parallel", "parallel")),
        name="mla_attention",
    )(cq, ckv, kr, w_q, w_kv, cos, sin)


def gather_rows(src, idx):
    n = idx.shape[0]
    width = src.shape[1]
    win = GATHER_WINDOW
    info = plsc.get_sparse_core_info()
    n_workers = info.num_cores * info.num_subcores
    assert n % (n_workers * 2 * win) == 0
    per_worker = n // n_workers
    mesh = plsc.VectorSubcoreMesh(core_axis_name="core", subcore_axis_name="subcore")

    @functools.partial(
        pl.kernel, out_type=jax.ShapeDtypeStruct((n, width), src.dtype), mesh=mesh,
        scratch_types=[pltpu.VMEM((per_worker,), jnp.int32),
                       pltpu.VMEM((win, width), src.dtype), pltpu.VMEM((win, width), src.dtype),
                       pltpu.SemaphoreType.DMA, pltpu.SemaphoreType.DMA,
                       pltpu.SemaphoreType.DMA, pltpu.SemaphoreType.DMA],
        name="gather_rows")
    def gather(src_hbm, idx_hbm, out_hbm, idx_vmem, rows0, rows1, gsem0, gsem1, osem0, osem1):
        worker = lax.axis_index("subcore") * info.num_cores + lax.axis_index("core")
        base = worker * per_worker
        pltpu.sync_copy(idx_hbm.at[pl.ds(base, per_worker)], idx_vmem)

        def fetch(r, rows, sem):
            return pltpu.make_async_copy(src_hbm.at[idx_vmem.at[pl.ds(r, win)]], rows, sem)

        def flush(r, rows, sem):
            return pltpu.make_async_copy(rows, out_hbm.at[pl.ds(base + r, win)], sem)

        fetch(0, rows0, gsem0).start()

        @pl.loop(0, per_worker, step=2 * win)
        def _(r):
            fetch(r, rows0, gsem0).wait()

            @pl.when(r > 0)
            def _():
                flush(r - win, rows1, osem1).wait()

            fetch(r + win, rows1, gsem1).start()
            flush(r, rows0, osem0).start()
            fetch(r + win, rows1, gsem1).wait()
            flush(r, rows0, osem0).wait()

            @pl.when(r + 2 * win < per_worker)
            def _():
                fetch(r + 2 * win, rows0, gsem0).start()

            flush(r + win, rows1, osem1).start()

        flush(per_worker - win, rows1, osem1).wait()

    return gather(src, idx)


def scatter_rows(src, idx, n_out):
    n = idx.shape[0]
    n_src, width = src.shape
    win = GATHER_WINDOW
    info = plsc.get_sparse_core_info()
    n_workers = info.num_cores * info.num_subcores
    assert n % (n_workers * 2 * win) == 0
    per_worker = n // n_workers
    assert n_src % per_worker == 0
    n_steps = per_worker // win
    mesh = plsc.VectorSubcoreMesh(core_axis_name="core", subcore_axis_name="subcore")

    @functools.partial(
        pl.kernel, out_type=jax.ShapeDtypeStruct((n_out, width), src.dtype), mesh=mesh,
        scratch_types=[pltpu.VMEM((n_steps, win), jnp.int32),
                       pltpu.VMEM((win, width), src.dtype), pltpu.VMEM((win, width), src.dtype),
                       pltpu.SemaphoreType.DMA, pltpu.SemaphoreType.DMA,
                       pltpu.SemaphoreType.DMA, pltpu.SemaphoreType.DMA],
        name="scatter_rows")
    def scatter(src_hbm, idx_hbm, out_hbm, idx_vmem, rows0, rows1, lsem0, lsem1, ssem0, ssem1):
        worker = lax.axis_index("subcore") * info.num_cores + lax.axis_index("core")
        src_base = lax.rem(worker * per_worker, n_src)
        pltpu.sync_copy(idx_hbm.at[worker], idx_vmem)

        def load(j, rows, sem):
            return pltpu.make_async_copy(src_hbm.at[pl.ds(src_base + j * win, win)], rows, sem)

        def store(j, rows, sem):
            return pltpu.make_async_copy(rows, out_hbm.at[idx_vmem.at[j]], sem)

        load(0, rows0, lsem0).start()

        @pl.loop(0, n_steps, step=2)
        def _(j):
            load(j, rows0, lsem0).wait()

            @pl.when(j > 0)
            def _():
                store(j - 1, rows1, ssem1).wait()

            load(j + 1, rows1, lsem1).start()
            store(j, rows0, ssem0).start()
            load(j + 1, rows1, lsem1).wait()
            store(j, rows0, ssem0).wait()

            @pl.when(j + 2 < n_steps)
            def _():
                load(j + 2, rows0, lsem0).start()

            store(j + 1, rows1, ssem1).start()

        store(n_steps - 1, rows1, ssem1).wait()

    return scatter(src, idx.reshape(n_workers, n_steps, win))


def _expert_kernel(blk_e_ref, n_used_ref, n_valid_ref, x_ref, wg_ref, wu_ref, wd_ref, y_ref, wg_bf, wu_bf, wd_bf):
    i = pl.program_id(0)
    used = i < n_used_ref[0]
    new_expert = (i == 0) | (blk_e_ref[i] != blk_e_ref[jnp.maximum(i - 1, 0)])

    @pl.when(used & new_expert)
    def _():
        wg_bf[...] = wg_ref[0].astype(BF16)
        wu_bf[...] = wu_ref[0].astype(BF16)
        wd_bf[...] = wd_ref[0].astype(BF16)

    n_valid = n_valid_ref[i]
    half = x_ref.shape[0] // 2

    def ffn(n_rows):
        row = lax.broadcasted_iota(jnp.int32, (n_rows, 1), 0)
        x = _unpack_bf16_pairs(jnp.where(row < n_valid, x_ref[0:n_rows, :], 0)).astype(BF16)
        g = _dot(x, wg_bf[...])
        u = _dot(x, wu_bf[...])
        a = (g * jax.nn.sigmoid(g) * u).astype(BF16)
        y_ref[0:n_rows, :] = _pack_bf16_pairs(_dot(a, wd_bf[...]))

    @pl.when(used & (n_valid > half))
    def _():
        ffn(2 * half)

    @pl.when(used & (n_valid <= half))
    def _():
        ffn(half)
        y_ref[half:, :] = jnp.zeros((half, y_ref.shape[1]), y_ref.dtype)

    @pl.when(jnp.logical_not(used))
    def _():
        y_ref[...] = jnp.zeros_like(y_ref)


def expert_ffn(x_disp, blk_e, n_used, n_valid, w_gate, w_up, w_down, layer):
    rows, half = x_disp.shape
    d = 2 * half
    n_blk = rows // MOE_BLOCK
    de = w_gate.shape[3]
    grid_spec = pltpu.PrefetchScalarGridSpec(
        num_scalar_prefetch=3,
        grid=(n_blk,),
        in_specs=[pl.BlockSpec((MOE_BLOCK, half), lambda i, be, nu, nv: (i, 0)),
                  pl.BlockSpec((None, 1, d, de), lambda i, be, nu, nv: (layer, be[i], 0, 0)),
                  pl.BlockSpec((None, 1, d, de), lambda i, be, nu, nv: (layer, be[i], 0, 0)),
                  pl.BlockSpec((None, 1, de, d), lambda i, be, nu, nv: (layer, be[i], 0, 0))],
        out_specs=pl.BlockSpec((MOE_BLOCK, half), lambda i, be, nu, nv: (i, 0)),
        scratch_shapes=[pltpu.VMEM((d, de), BF16), pltpu.VMEM((d, de), BF16), pltpu.VMEM((de, d), BF16)],
    )
    return pl.pallas_call(
        _expert_kernel,
        grid_spec=grid_spec,
        out_shape=jax.ShapeDtypeStruct((rows, half), jnp.int32),
        compiler_params=_cparams(("arbitrary",)),
        name="expert_ffn",
    )(blk_e, n_used, n_valid, x_disp, w_gate, w_up, w_down)


def _moe_residual(h_ref, y0_ref, y1_ref, wt_ref, mod_ref, gate_idx):
    wt = wt_ref[0]
    moe = wt[:, 0:1] * _unpack_bf16_pairs(y0_ref[0, 0]) + wt[:, 1:2] * _unpack_bf16_pairs(y1_ref[0, 0])
    return h_ref[0] + mod_ref[0, gate_idx:gate_idx + 1, :] * moe


def _combine_next_kernel(h_ref, y0_ref, y1_ref, wt_ref, mod_ref, g_ref, nmod_ref, o_ref, hm_ref, *, gate_idx):
    h_new = _moe_residual(h_ref, y0_ref, y1_ref, wt_ref, mod_ref, gate_idx)
    o_ref[0] = h_new
    hm_ref[0] = _rms_mod(h_new, g_ref[...], nmod_ref[0, 0:1, :], nmod_ref[0, 1:2, :]).astype(hm_ref.dtype)


def _combine_final_kernel(h_ref, y0_ref, y1_ref, wt_ref, mod_ref, g_ref, o_ref, *, gate_idx):
    x = _moe_residual(h_ref, y0_ref, y1_ref, wt_ref, mod_ref, gate_idx)
    ms = jnp.mean(x * x, axis=-1, keepdims=True)
    o_ref[0] = x * lax.rsqrt(ms + NORM_EPS) * g_ref[...]


def combine(h, y_pairs, wt, modtab, gate_idx, next_g, next_modtab, ctx_len):
    b, p, d = h.shape
    last = next_modtab is None
    skip = ctx_len // ROW_TILE if last else 0
    row = pl.BlockSpec((1, ROW_TILE, d), lambda b, j: (b, j + skip, 0))
    mod = pl.BlockSpec((1, N_MOD, d), lambda b, j: (2 * b + jnp.minimum(j + skip, 1), 0, 0))
    in_specs = [row,
                pl.BlockSpec((1, 1, ROW_TILE, d // 2), lambda b, j: (0, b, j + skip, 0)),
                pl.BlockSpec((1, 1, ROW_TILE, d // 2), lambda b, j: (1, b, j + skip, 0)),
                pl.BlockSpec((1, ROW_TILE, LANES), lambda b, j: (b, j + skip, 0)),
                mod, pl.BlockSpec((1, d), lambda b, j: (0, 0))]
    args = [h, y_pairs, y_pairs, wt, modtab.reshape(b * 2, N_MOD, d), next_g.reshape(1, d)]
    out_row = pl.BlockSpec((1, ROW_TILE, d), lambda b, j: (b, j, 0))
    if last:
        return pl.pallas_call(
            functools.partial(_combine_final_kernel, gate_idx=gate_idx),
            grid=(b, (p - ctx_len) // ROW_TILE),
            in_specs=in_specs,
            out_specs=out_row,
            out_shape=jax.ShapeDtypeStruct((b, p - ctx_len, d), F32),
            compiler_params=_cparams(("parallel", "parallel")),
            name="moe_combine_final",
        )(*args)
    return pl.pallas_call(
        functools.partial(_combine_next_kernel, gate_idx=gate_idx),
        grid=(b, p // ROW_TILE),
        in_specs=in_specs + [mod],
        out_specs=[out_row, out_row],
        out_shape=[jax.ShapeDtypeStruct((b, p, d), F32), jax.ShapeDtypeStruct((b, p, d), BF16)],
        compiler_params=_cparams(("parallel", "parallel")),
        name="moe_combine",
    )(*args, next_modtab.reshape(b * 2, N_MOD, d))


def hier_moe(h, f, route, wt, counts, modtab, gate_idx, w_gate, w_up, w_down, layer, next_g, next_modtab, ctx_len):
    b, p, d = h.shape
    n_tok = b * p
    n_assign = 2 * n_tok
    e1, e2, rank1, rank2 = route.reshape(n_tok, LANES)[:, :4].T
    cnt = counts[0, :N_EXPERTS].astype(jnp.int32)
    pcounts = (cnt + MOE_BLOCK - 1) // MOE_BLOCK * MOE_BLOCK
    pend = jnp.cumsum(pcounts)
    pstart = pend - pcounts
    dest_by_slot = jnp.concatenate([pstart[e1] + rank1, pstart[e2] + rank2]).astype(jnp.int32)
    n_blk = -(-n_assign // MOE_BLOCK) + N_EXPERTS
    rows_total = n_blk * MOE_BLOCK
    blk_row0 = jnp.arange(n_blk, dtype=jnp.int32) * MOE_BLOCK
    blk_e = jnp.minimum(jnp.sum((pend[None, :] <= blk_row0[:, None]).astype(jnp.int32), axis=1), N_EXPERTS - 1)
    n_valid = jnp.clip(cnt[blk_e] - (blk_row0 - pstart[blk_e]), 0, MOE_BLOCK).astype(jnp.int32)
    n_used = (pend[-1] // MOE_BLOCK).astype(jnp.int32).reshape(1)

    x_disp = scatter_rows(f.reshape(n_tok, d // 2), dest_by_slot, rows_total)
    y = expert_ffn(x_disp, blk_e, n_used, n_valid, w_gate, w_up, w_down, layer)
    y_pairs = gather_rows(y, dest_by_slot).reshape(2, b, p, d // 2)
    return combine(h, y_pairs, wt, modtab, gate_idx, next_g, next_modtab, ctx_len)


def kernel(x, c, ctx, c_ctx, mod_w, mod_b, norm_mix_g, norm_ffn_g, router_grp_w, router_grp_b, router_exp_w, router_exp_b, exp_w_gate, exp_w_up, exp_w_down, l0_na_w_qkv, l0_na_rpb, l0_na_w_o, l1_swa_w_qkv, l1_swa_sink, l1_swa_w_o, l2_mla_w_dq, l2_mla_q_norm_g, l2_mla_w_uq, l2_mla_w_dkv, l2_mla_kv_norm_g, l2_mla_w_ukv, l2_mla_w_o, l3_na_w_qkv, l3_na_rpb, l3_na_w_o, final_norm_g):
    b, s, d = x.shape
    lc = ctx.shape[1]
    n_heads = d // HEAD_DIM
    n_kv_heads = n_heads // 4
    depth = mod_w.shape[0]
    rows = s // GRID_W

    modtabs = modulation_tables(c, c_ctx, mod_w, mod_b)

    def scale_q_cols(w):
        n_q = n_heads * HEAD_DIM
        q_scale = HEAD_DIM ** -0.5 * LOG2E
        return jnp.concatenate([w[:, :n_q] * q_scale, w[:, n_q:]], axis=1).astype(BF16)

    def na_mixer(hm, w_qkv, rpb):
        qkv = project(hm, scale_q_cols(w_qkv))
        return neighbourhood_attention(qkv, na_bias_table(rpb, rows), lc, n_heads)

    def swa_mixer(hm):
        cos, sin = rope_tables_full(s, lc)
        n_rope = (n_heads + n_kv_heads) * HEAD_DIM // 512
        qkv = project(hm, scale_q_cols(l1_swa_w_qkv), tn=512, rope=(cos, sin, 0, n_rope))
        return window_attention(qkv, l1_swa_sink, lc, n_heads, n_kv_heads)

    def mla_mixer(hm):
        q_rank = l2_mla_w_dq.shape[1]
        kv_rank = l2_mla_kv_norm_g.shape[0]
        cos, sin = rope_tables_mla(s, lc)
        w_down = jnp.concatenate([l2_mla_w_dq, l2_mla_w_dkv[:, :kv_rank],
                                  _spread_rope_cols(l2_mla_w_dkv[:, kv_rank:])], axis=1).astype(BF16)
        cq, ckv, kr = mla_down(hm, w_down, l2_mla_q_norm_g, l2_mla_kv_norm_g, cos, sin, q_rank, kv_rank)
        w_uq = l2_mla_w_uq.reshape(q_rank, n_heads, MLA_NOPE_DIM + MLA_ROPE_DIM)
        q_scale = (MLA_NOPE_DIM + MLA_ROPE_DIM) ** -0.5 * LOG2E
        half = MLA_ROPE_DIM // 2
        gap = jnp.zeros((q_rank, n_heads, half), l2_mla_w_uq.dtype)
        w_q = jnp.concatenate([w_uq[:, :, :MLA_NOPE_DIM], w_uq[:, :, MLA_NOPE_DIM:MLA_NOPE_DIM + half], gap,
                               w_uq[:, :, MLA_NOPE_DIM + half:], gap], axis=2).reshape(q_rank, n_heads * 2 * LANES)
        return latent_attention(cq, ckv, kr, (w_q * q_scale).astype(BF16), l2_mla_w_ukv.astype(BF16), cos, sin,
                                lc, n_heads)

    h, hm = join_norm_modulate(ctx, x, norm_mix_g[0], modtabs[0])
    for i in range(depth):
        modtab = modtabs[i]
        mixer = i % 3
        if mixer == 0:
            w_qkv, rpb, w_o = (l0_na_w_qkv, l0_na_rpb, l0_na_w_o) if i == 0 else (l3_na_w_qkv, l3_na_rpb, l3_na_w_o)
            y = na_mixer(hm, w_qkv, rpb)
        elif mixer == 1:
            y, w_o = swa_mixer(hm), l1_swa_w_o
        else:
            y, w_o = mla_mixer(hm), l2_mla_w_o
        h, f, route, wt, counts = attn_out_route(y, w_o.astype(BF16), h, modtab, norm_ffn_g[i], router_grp_w[i],
                                                 router_grp_b[i], router_exp_w[i], router_exp_b[i], lc)
        if i + 1 < depth:
            h, hm = hier_moe(h, f, route, wt, counts, modtab, 5, exp_w_gate, exp_w_up, exp_w_down, i,
                             norm_mix_g[i + 1], modtabs[i + 1], lc)
        else:
            return hier_moe(h, f, route, wt, counts, modtab, 5, exp_w_gate, exp_w_up, exp_w_down, i,
                            final_norm_g, None, lc)
```

```python
import functools

import numpy as np
import jax
import jax.numpy as jnp
from jax import lax
from jax.experimental import pallas as pl
from jax.experimental.pallas import tpu as pltpu
from jax.experimental.pallas import tpu_sc as plsc

GRID_W = 64
HEAD_DIM = 128
ROPE_BASE = 10000.0
NORM_EPS = 1e-6
NEG_INF = -1e30
N_MOD = 6

NA_ROWS = 8
NA_COLS = 16
NA_TILE_ROWS = 4
NA_WIN_ROWS = 12

SWA_WINDOW = 128
SWA_TQ = 256
SWA_TK = 512

MLA_NOPE_DIM = 128
MLA_ROPE_DIM = 64
MLA_TQ = 512
MLA_TK = 512
LOG2E = 1.4426950408889634

N_GROUPS = 4
EXPERTS_PER_GROUP = 8
N_EXPERTS = N_GROUPS * EXPERTS_PER_GROUP
MOE_BLOCK = 512
GATHER_WINDOW = 32

LANES = 128
ROW_TILE = 256
VMEM_LIMIT = 56 * 1024 * 1024

BF16 = jnp.bfloat16
F32 = jnp.float32


def _cparams(sem):
    return pltpu.CompilerParams(dimension_semantics=sem, vmem_limit_bytes=VMEM_LIMIT)


def _dot(a, b):
    return jnp.dot(a, b, preferred_element_type=F32)


def _transposed(k_rows):
    return k_rows.astype(F32).T.astype(BF16)


def _mod_kernel(x_ref, w_ref, b_ref, o_ref):
    x = x_ref[...]
    sx = (x * jax.nn.sigmoid(x)).astype(BF16)
    o_ref[0] = _dot(sx, w_ref[0].astype(BF16)) + b_ref[0]


def modulation_tables(c, c_ctx, mod_w, mod_b):
    depth, d, n_out = mod_w.shape
    b = c.shape[0]
    rows = 16
    xin = jnp.zeros((rows, d), F32).at[:b].set(c).at[b].set(c_ctx)
    tn = 1024
    out = pl.pallas_call(
        _mod_kernel,
        grid=(depth, n_out // tn),
        in_specs=[pl.BlockSpec((rows, d), lambda i, j: (0, 0)),
                  pl.BlockSpec((1, d, tn), lambda i, j: (i, 0, j)),
                  pl.BlockSpec((1, 1, tn), lambda i, j: (i, 0, j))],
        out_specs=pl.BlockSpec((1, rows, tn), lambda i, j: (i, 0, j)),
        out_shape=jax.ShapeDtypeStruct((depth, rows, n_out), F32),
        compiler_params=_cparams(("parallel", "parallel")),
        name="adaln_mod",
    )(xin, mod_w, mod_b.reshape(depth, 1, n_out))
    lat = out[:, :b].reshape(depth, b, 1, N_MOD, d)
    ctx = jnp.broadcast_to(out[:, b].reshape(depth, 1, 1, N_MOD, d), (depth, b, 1, N_MOD, d))
    return jnp.concatenate([ctx, lat], axis=2)


def _rms_mod(x, g, shift, scale):
    ms = jnp.mean(x * x, axis=-1, keepdims=True)
    y = x * lax.rsqrt(ms + NORM_EPS) * g
    return y * (1.0 + scale) + shift


def _join_norm_mod_kernel(ctx_ref, x_ref, g_ref, mod_ref, h_ref, hm_ref, *, ctx_tiles):
    j = pl.program_id(1)

    def emit(src_ref):
        h_ref[0] = src_ref[0]
        hm_ref[0] = _rms_mod(src_ref[0], g_ref[...], mod_ref[0, 0:1, :], mod_ref[0, 1:2, :]).astype(hm_ref.dtype)

    @pl.when(j < ctx_tiles)
    def _():
        emit(ctx_ref)

    @pl.when(j >= ctx_tiles)
    def _():
        emit(x_ref)


def _route(logits):
    lane = lax.broadcasted_iota(jnp.int32, logits.shape, 1).astype(F32)
    big = float(LANES)

    def first_lane(mask):
        return jnp.min(jnp.where(mask, lane, big), axis=-1, keepdims=True)

    in_grp = lane < N_GROUPS
    lg = jnp.where(in_grp, logits, NEG_INF)
    m_g = jnp.max(lg, axis=-1, keepdims=True)
    g_idx = first_lane(in_grp & (lg == m_g))
    g_w = 1.0 / jnp.sum(jnp.where(in_grp, jnp.exp(lg - m_g), 0.0), axis=-1, keepdims=True)
    e_lo = N_GROUPS + g_idx * EXPERTS_PER_GROUP
    in_e = (lane >= e_lo) & (lane < e_lo + EXPERTS_PER_GROUP)
    le = jnp.where(in_e, logits, NEG_INF)
    m1 = jnp.max(le, axis=-1, keepdims=True)
    e1 = first_lane(in_e & (le == m1))
    s_e = jnp.sum(jnp.where(in_e, jnp.exp(le - m1), 0.0), axis=-1, keepdims=True)
    in_e2 = in_e & (lane != e1)
    le2 = jnp.where(in_e2, logits, NEG_INF)
    m2 = jnp.max(le2, axis=-1, keepdims=True)
    e2 = first_lane(in_e2 & (le2 == m2))
    p1 = 1.0 / s_e
    p2 = jnp.exp(m2 - m1) / s_e
    den = p1 + p2
    return ((e1 - N_GROUPS).astype(jnp.int32), (e2 - N_GROUPS).astype(jnp.int32),
            g_w * p1 / den, g_w * p2 / den)


def _pack_bf16_pairs(x):
    n = x.shape[1] // 2
    xb = x.astype(BF16).astype(F32)
    hi = lax.bitcast_convert_type(xb[:, :n], jnp.int32)
    lo = lax.bitcast_convert_type(xb[:, n:], jnp.int32)
    return (hi & jnp.int32(-65536)) | lax.shift_right_logical(lo, jnp.int32(16))


def _unpack_bf16_pairs(w):
    hi = lax.bitcast_convert_type(w & jnp.int32(-65536), F32)
    lo = lax.bitcast_convert_type(lax.shift_left(w, jnp.int32(16)), F32)
    return jnp.concatenate([hi, lo], axis=1)


def _route_and_rank(f, wr_ref, br_ref, run_ref):
    f_hi = f.astype(BF16)
    f_lo = (f - f_hi.astype(F32)).astype(BF16)
    hi_terms = _dot(f_hi, wr_ref[...])
    logits = (hi_terms[:, :LANES] + _dot(f_lo, wr_ref[:, :LANES]) + hi_terms[:, LANES:]) + br_ref[...]
    e1, e2, w1, w2 = _route(logits)
    lane = lax.broadcasted_iota(jnp.int32, logits.shape, 1)
    pick1, pick2 = lane == e1, lane == e2
    chosen = (pick1 | pick2).astype(F32)
    n_rows = chosen.shape[0]
    earlier = (lax.broadcasted_iota(jnp.int32, (n_rows, n_rows), 1)
               < lax.broadcasted_iota(jnp.int32, (n_rows, n_rows), 0)).astype(BF16)
    before = run_ref[...] + _dot(earlier, chosen.astype(BF16))
    rank1 = jnp.sum(jnp.where(pick1, before, 0.0), axis=-1, keepdims=True).astype(jnp.int32)
    rank2 = jnp.sum(jnp.where(pick2, before, 0.0), axis=-1, keepdims=True).astype(jnp.int32)
    run_ref[...] = run_ref[...] + jnp.sum(chosen, axis=0, keepdims=True)
    route = jnp.where(lane == 0, e1, jnp.where(lane == 1, e2, jnp.where(lane == 2, rank1,
                                                                         jnp.where(lane == 3, rank2, 0))))
    return route, jnp.where(lane == 0, w1, jnp.where(lane == 1, w2, 0.0))


def _attn_out_route_kernel(y_ref, w_ref, h_ref, mod_ref, g_ref, wr_ref, br_ref,
                           ho_ref, f_ref, route_ref, wt_ref, cnt_ref, run_ref, *, ctx_len, tm, n_sub):
    first = (pl.program_id(0) == 0) & (pl.program_id(1) == 0)

    @pl.when(first)
    def _():
        run_ref[...] = jnp.zeros_like(run_ref)

    sub = tm // n_sub
    for s in range(n_sub):
        rows = slice(s * sub, (s + 1) * sub)
        acc = _dot(y_ref[0, rows, :], w_ref[...])
        pos = pl.program_id(1) * tm + s * sub + lax.broadcasted_iota(jnp.int32, (sub, 1), 0)
        is_ctx = pos < ctx_len

        def mod_row(k):
            return jnp.where(is_ctx, mod_ref[0, 0, k:k + 1, :], mod_ref[0, 1, k:k + 1, :])

        h_new = h_ref[0, rows, :] + mod_row(2) * acc
        ho_ref[0, rows, :] = h_new
        f = _rms_mod(h_new, g_ref[...], mod_row(3), mod_row(4))
        f_ref[0, rows, :] = _pack_bf16_pairs(f)
        route, wt = _route_and_rank(f, wr_ref, br_ref, run_ref)
        route_ref[0, rows, :] = route
        wt_ref[0, rows, :] = wt
    cnt_ref[...] = run_ref[...]


def _mod_spec(d):
    return pl.BlockSpec((1, N_MOD, d), lambda b, j: (2 * b + jnp.minimum(j, 1), 0, 0))


def join_norm_modulate(ctx, x, g, modtab):
    b, lc, d = ctx.shape
    p = lc + x.shape[1]
    assert lc == ROW_TILE
    ctx_tiles = lc // ROW_TILE
    row = pl.BlockSpec((1, ROW_TILE, d), lambda b, j: (b, j, 0))
    return pl.pallas_call(
        functools.partial(_join_norm_mod_kernel, ctx_tiles=ctx_tiles),
        grid=(b, p // ROW_TILE),
        in_specs=[pl.BlockSpec((1, ROW_TILE, d), lambda b, j: (b, jnp.minimum(j, ctx_tiles - 1), 0)),
                  pl.BlockSpec((1, ROW_TILE, d), lambda b, j: (b, jnp.maximum(j - ctx_tiles, 0), 0)),
                  pl.BlockSpec((1, d), lambda b, j: (0, 0)), _mod_spec(d)],
        out_specs=[row, row],
        out_shape=[jax.ShapeDtypeStruct((b, p, d), F32), jax.ShapeDtypeStruct((b, p, d), BF16)],
        compiler_params=_cparams(("parallel", "parallel")),
        name="join_norm_mod",
    )(ctx, x, g.reshape(1, d), modtab.reshape(b * 2, N_MOD, d))


def attn_out_route(y, w_o, h, modtab, g, w_grp, b_grp, w_rt, b_rt, ctx_len):
    b, p, d = h.shape
    n_r = N_GROUPS + N_EXPERTS
    wr = jnp.zeros((d, LANES), F32).at[:, :N_GROUPS].set(w_grp).at[:, N_GROUPS:n_r].set(w_rt)
    br = jnp.zeros((1, LANES), F32).at[0, :N_GROUPS].set(b_grp).at[0, N_GROUPS:n_r].set(b_rt)
    wr_hi = wr.astype(BF16)
    wr = jnp.concatenate([wr_hi, (wr - wr_hi.astype(F32)).astype(BF16)], axis=1)
    tm = 544 if p % 544 == 0 else ROW_TILE
    n_sub = 2
    assert (tm // n_sub) % 16 == 0
    row = lambda width: pl.BlockSpec((1, tm, width), lambda b, i: (b, i, 0))
    once = lambda shape: pl.BlockSpec(shape, lambda b, i: (0,) * len(shape), pipeline_mode=pl.Buffered(1))
    return pl.pallas_call(
        functools.partial(_attn_out_route_kernel, ctx_len=ctx_len, tm=tm, n_sub=n_sub),
        grid=(b, p // tm),
        in_specs=[row(d), once((d, d)), row(d),
                  pl.BlockSpec((1, 2, N_MOD, d), lambda b, i: (b, 0, 0, 0)),
                  once((1, d)), once((d, 2 * LANES)), once((1, LANES))],
        out_specs=[row(d), row(d // 2), row(LANES), row(LANES), pl.BlockSpec((1, LANES), lambda b, i: (0, 0))],
        out_shape=[jax.ShapeDtypeStruct((b, p, d), F32),
                   jax.ShapeDtypeStruct((b, p, d // 2), jnp.int32),
                   jax.ShapeDtypeStruct((b, p, LANES), jnp.int32),
                   jax.ShapeDtypeStruct((b, p, LANES), F32),
                   jax.ShapeDtypeStruct((1, LANES), F32)],
        scratch_shapes=[pltpu.VMEM((1, LANES), F32)],
        compiler_params=_cparams(("arbitrary", "arbitrary")),
        name="attn_out_route",
    )(y, w_o, h, modtab, g.reshape(1, d), wr, br)


def _rope(acc, cos, sin):
    n_blk = acc.shape[1] // LANES
    outs = []
    for c in range(n_blk):
        x = acc[:, c * LANES:(c + 1) * LANES]
        outs.append(x * cos + pltpu.roll(x, LANES // 2, 1) * sin)
    return outs[0] if n_blk == 1 else jnp.concatenate(outs, axis=1)


def _proj_kernel(*refs, rope_lo, rope_hi):
    x_ref, w_ref = refs[0], refs[1]
    o_ref = refs[-1]
    acc = _dot(x_ref[0], w_ref[...])
    if rope_hi > rope_lo:
        cos_ref, sin_ref = refs[2], refs[3]
        j = pl.program_id(2)
        roped = (j >= rope_lo) & (j < rope_hi)

        @pl.when(roped)
        def _():
            o_ref[0] = _rope(acc, cos_ref[...], sin_ref[...]).astype(o_ref.dtype)

        @pl.when(jnp.logical_not(roped))
        def _():
            o_ref[0] = acc.astype(o_ref.dtype)
    else:
        o_ref[0] = acc.astype(o_ref.dtype)


def _row_tile(p):
    for cand in (1088, 1024, 544, 512, 272, 256, 128, 64, 32, 16):
        if p % cand == 0:
            return cand
    raise ValueError(p)


def project(x, w, *, tn=512, out_dtype=BF16, rope=None):
    b, p, k = x.shape
    n = w.shape[1]
    tm = _row_tile(p)
    tn = min(tn, n)
    assert n % tn == 0
    in_specs = [pl.BlockSpec((1, tm, k), lambda b, i, j: (b, i, 0)),
                pl.BlockSpec((k, tn), lambda b, i, j: (0, j))]
    args = [x, w]
    kw = dict(rope_lo=0, rope_hi=0)
    if rope is not None:
        cos, sin, lo, hi = rope
        in_specs += [pl.BlockSpec((tm, LANES), lambda b, i, j: (i, 0)),
                     pl.BlockSpec((tm, LANES), lambda b, i, j: (i, 0))]
        args += [cos, sin]
        kw.update(rope_lo=lo, rope_hi=hi)
    return pl.pallas_call(
        functools.partial(_proj_kernel, **kw),
        grid=(b, p // tm, n // tn),
        in_specs=in_specs,
        out_specs=pl.BlockSpec((1, tm, tn), lambda b, i, j: (b, i, j)),
        out_shape=jax.ShapeDtypeStruct((b, p, n), out_dtype),
        compiler_params=_cparams(("parallel", "parallel", "arbitrary")),
        name="project",
    )(*args)


def _axial_cos_sin(n, rot_dim):
    t = jnp.arange(n, dtype=jnp.int32)
    row = (t // GRID_W).astype(F32)
    col = (t % GRID_W).astype(F32)
    n_freq = rot_dim // 4
    inv = ROPE_BASE ** (-jnp.arange(n_freq, dtype=F32) / n_freq)
    ang = jnp.concatenate([row[:, None] * inv, col[:, None] * inv], axis=-1)
    return jnp.cos(ang), jnp.sin(ang)


def rope_tables_full(s, ctx_len):
    c, sn = _axial_cos_sin(s, HEAD_DIM)
    cos = jnp.concatenate([c, c], axis=1)
    sin = jnp.concatenate([-sn, sn], axis=1)
    ident_c = jnp.ones((ctx_len, LANES), F32)
    ident_s = jnp.zeros((ctx_len, LANES), F32)
    return jnp.concatenate([ident_c, cos], axis=0), jnp.concatenate([ident_s, sin], axis=0)


def rope_tables_mla(s, ctx_len):
    c, sn = _axial_cos_sin(s, MLA_ROPE_DIM)
    one = jnp.ones_like(c)
    zero = jnp.zeros_like(c)
    cos = jnp.concatenate([c, one, c, one], axis=1)
    sin = jnp.concatenate([-sn, zero, sn, zero], axis=1)
    ident_c = jnp.ones((ctx_len, LANES), F32)
    ident_s = jnp.zeros((ctx_len, LANES), F32)
    return jnp.concatenate([ident_c, cos], axis=0), jnp.concatenate([ident_s, sin], axis=0)


def _spread_rope_cols(w_rope):
    k = w_rope.shape[0]
    half = MLA_ROPE_DIM // 2
    z = jnp.zeros((k, half), w_rope.dtype)
    return jnp.concatenate([w_rope[:, :half], z, w_rope[:, half:], z], axis=1)


def _softmax_probs(parts, extra=None):
    m = _row_reduce(jnp.maximum, jnp.max, parts)
    if extra is not None:
        m = jnp.maximum(m, extra)
    return [jnp.exp2((s - m).astype(BF16)) for s in parts], m


def _with_ones(v_ref, vone_ref):
    vone_ref[:, 0:LANES] = v_ref[0]
    vone_ref[:, LANES:2 * LANES] = jnp.ones((vone_ref.shape[0], LANES), BF16)


def _normalise(acc, extra_den=None):
    den = acc[:, LANES:LANES + 1]
    if extra_den is not None:
        den = den + extra_den
    return acc[:, :LANES] * (1.0 / den)


def _row_reduce(combine, reduce, parts):
    blocks = [s[:, c:c + LANES] for s in parts for c in range(0, s.shape[1], LANES)]
    acc = blocks[0]
    for blk in blocks[1:]:
        acc = combine(acc, blk)
    return reduce(acc, axis=-1, keepdims=True)


def na_bias_table(rpb, rows):
    n_tiles = rows // NA_TILE_ROWS
    n_heads, _, n_dcol = rpb.shape
    drow, row_ok = [], []
    for tile in (0, 1, n_tiles - 1):
        kr0 = int(np.clip(NA_TILE_ROWS * tile - NA_ROWS // 2, 0, rows - NA_WIN_ROWS))
        r = NA_TILE_ROWS * tile + np.arange(NA_TILE_ROWS)
        r0 = np.clip(r - NA_ROWS // 2, 0, rows - NA_ROWS)
        krow = kr0 + np.arange(NA_WIN_ROWS)
        row_ok.append((krow[None, :] >= r0[:, None]) & (krow[None, :] < r0[:, None] + NA_ROWS))
        drow.append(np.clip(krow[None, :] - r[:, None] + NA_ROWS - 1, 0, 2 * NA_ROWS - 2))
    drow, row_ok = np.stack(drow), np.stack(row_ok)
    qc = np.arange(GRID_W)
    qcol0 = np.clip(qc - NA_COLS // 2, 0, GRID_W - NA_COLS)
    kc = np.arange(GRID_W)
    col_ok = (kc[None, :] >= qcol0[:, None]) & (kc[None, :] < qcol0[:, None] + NA_COLS)
    dcol = np.clip(kc[None, :] - qc[:, None] + NA_COLS - 1, 0, 2 * NA_COLS - 2)
    pick_col = jnp.asarray(dcol[None] == np.arange(n_dcol)[:, None, None], F32)
    slabs = jnp.einsum('hrd,dqk->hrqk', rpb.astype(F32), pick_col, precision=lax.Precision.HIGHEST)
    slabs = jnp.where(jnp.asarray(col_ok)[None, None], slabs * LOG2E, NEG_INF)
    masked = jnp.full((n_heads, GRID_W, GRID_W), NEG_INF, F32)
    pats = []
    for p in range(3):
        per_row = [jnp.stack([slabs[:, drow[p, a, m]] if row_ok[p, a, m] else masked
                              for m in range(NA_WIN_ROWS)], axis=2)
                   for a in range(NA_TILE_ROWS)]
        pats.append(jnp.stack(per_row, axis=1))
    vals = jnp.stack(pats, axis=1)
    return vals.reshape(n_heads, 3, NA_TILE_ROWS * GRID_W, NA_WIN_ROWS * GRID_W)


def _na_kernel(q_ref, k_ref, v_ref, bias_ref, o_ref, vone_ref, kt_ref, *, ctx_len, rows):
    tq = NA_TILE_ROWS * GRID_W
    tk = NA_WIN_ROWS * GRID_W
    n_tiles = rows // NA_TILE_ROWS
    lc = ctx_len
    _with_ones(v_ref, vone_ref)
    kt_ref[...] = _transposed(k_ref[0])

    s = _dot(q_ref[0, 0:lc, :], kt_ref[:, 0:lc])
    (p,), _ = _softmax_probs([s])
    o_ref[0, 0:lc, :] = _normalise(_dot(p, vone_ref[0:lc, :])).astype(o_ref.dtype)

    def tile(i, carry):
        qs = pl.multiple_of(lc + i * tq, tq)
        kr0 = jnp.clip(NA_TILE_ROWS * i - NA_ROWS // 2, 0, rows - NA_WIN_ROWS)
        ks = pl.multiple_of(lc + kr0 * GRID_W, NA_TILE_ROWS * GRID_W)
        pat = jnp.where(i == 0, 0, jnp.where(i == n_tiles - 1, 2, 1))
        q = q_ref[0, pl.ds(qs, tq), :]
        s_loc = _dot(q, kt_ref[:, pl.ds(ks, tk)]) + bias_ref[0, pat]
        s_ctx = _dot(q, kt_ref[:, 0:lc])
        (p_loc, p_ctx), _ = _softmax_probs([s_loc, s_ctx])
        acc = _dot(p_loc, vone_ref[pl.ds(ks, tk), :]) + _dot(p_ctx, vone_ref[0:lc, :])
        o_ref[0, pl.ds(qs, tq), :] = _normalise(acc).astype(o_ref.dtype)
        return carry

    lax.fori_loop(0, n_tiles, tile, 0, unroll=4)


def neighbourhood_attention(qkv, bias, ctx_len, n_heads):
    b, p, _ = qkv.shape
    rows = (p - ctx_len) // GRID_W
    assert rows % NA_TILE_ROWS == 0 and rows >= NA_WIN_ROWS
    tq, tk = NA_TILE_ROWS * GRID_W, NA_WIN_ROWS * GRID_W
    assert ctx_len % 16 == 0 and ctx_len % tq == 0
    blk = lambda off: pl.BlockSpec((1, p, HEAD_DIM), lambda h, b: (b, 0, off + h))
    return pl.pallas_call(
        functools.partial(_na_kernel, ctx_len=ctx_len, rows=rows),
        grid=(n_heads, b),
        in_specs=[blk(0), blk(n_heads), blk(2 * n_heads),
                  pl.BlockSpec((1, 3, tq, tk), lambda h, b: (h, 0, 0, 0))],
        out_specs=blk(0),
        out_shape=jax.ShapeDtypeStruct((b, p, n_heads * HEAD_DIM), BF16),
        scratch_shapes=[pltpu.VMEM((p, 2 * LANES), BF16), pltpu.VMEM((HEAD_DIM, p), BF16)],
        compiler_params=_cparams(("parallel", "parallel")),
        name="na_attention",
    )(qkv, qkv, qkv, bias)


def _swa_kernel(sink_ref, q_ref, k_ref, v_ref, o_ref, vone_ref, kt_ref, *, ctx_len, seq, group):
    lc = ctx_len
    kvh = pl.program_id(1)
    _with_ones(v_ref, vone_ref)
    kt_ref[...] = _transposed(k_ref[0])

    def stack_heads(q):
        return jnp.concatenate([q[:, g * HEAD_DIM:(g + 1) * HEAD_DIM] for g in range(group)], axis=0)

    def finish(parts_fn, n_q, pv_fn, store):
        ps_all, sink_terms = [], []
        for g in range(group):
            sink = sink_ref[kvh * group + g]
            ps, m = _softmax_probs(parts_fn(g), extra=sink)
            ps_all.append(ps)
            sink_terms.append(jnp.exp2(sink - m))
        n_parts = len(ps_all[0])
        stacked = [jnp.concatenate([ps_all[g][k] for g in range(group)], axis=0) for k in range(n_parts)]
        acc = pv_fn(stacked)
        store(jnp.concatenate([_normalise(acc[g * n_q:(g + 1) * n_q], sink_terms[g]) for g in range(group)], axis=1))

    s_c = _dot(stack_heads(q_ref[0, 0:lc, :]), kt_ref[:, 0:lc])

    def store_ctx(o):
        o_ref[0, 0:lc, :] = o.astype(o_ref.dtype)

    finish(lambda g: [s_c[g * lc:(g + 1) * lc]], lc,
           lambda st: _dot(st[0], vone_ref[0:lc, :]), store_ctx)

    n_tiles = seq // SWA_TQ

    def tile(t, carry):
        q0 = t * SWA_TQ
        k0 = jnp.clip(q0 - SWA_WINDOW, 0, seq - SWA_TK)
        qs = pl.multiple_of(lc + q0, SWA_WINDOW)
        ks = pl.multiple_of(lc + k0, SWA_WINDOW)
        q4 = stack_heads(q_ref[0, pl.ds(qs, SWA_TQ), :])
        s_loc = _dot(q4, kt_ref[:, pl.ds(ks, SWA_TK)])
        s_ctx = _dot(q4, kt_ref[:, 0:lc])
        dpos = (lax.broadcasted_iota(jnp.int32, (SWA_TQ, SWA_TK), 1)
                - lax.broadcasted_iota(jnp.int32, (SWA_TQ, SWA_TK), 0)) + (k0 - q0)
        valid = jnp.abs(dpos) <= SWA_WINDOW

        def parts(g):
            sl = slice(g * SWA_TQ, (g + 1) * SWA_TQ)
            return [jnp.where(valid, s_loc[sl], NEG_INF), s_ctx[sl]]

        def store(o):
            o_ref[0, pl.ds(qs, SWA_TQ), :] = o.astype(o_ref.dtype)

        finish(parts, SWA_TQ,
               lambda st: _dot(st[0], vone_ref[pl.ds(ks, SWA_TK), :]) + _dot(st[1], vone_ref[0:lc, :]),
               store)
        return carry

    lax.fori_loop(0, n_tiles, tile, 0, unroll=2)


def window_attention(qkv, sink, ctx_len, n_heads, n_kv_heads):
    b, p, _ = qkv.shape
    seq = p - ctx_len
    group = n_heads // n_kv_heads
    assert seq % SWA_TQ == 0 and seq >= SWA_TK and ctx_len % SWA_WINDOW == 0
    kv = lambda off: pl.BlockSpec((1, p, HEAD_DIM), lambda b, h: (b, 0, off + h))
    qo = pl.BlockSpec((1, p, group * HEAD_DIM), lambda b, h: (b, 0, h))
    return pl.pallas_call(
        functools.partial(_swa_kernel, ctx_len=ctx_len, seq=seq, group=group),
        grid=(b, n_kv_heads),
        in_specs=[pl.BlockSpec(memory_space=pltpu.SMEM), qo, kv(n_heads), kv(n_heads + n_kv_heads)],
        out_specs=qo,
        out_shape=jax.ShapeDtypeStruct((b, p, n_heads * HEAD_DIM), BF16),
        scratch_shapes=[pltpu.VMEM((p, 2 * LANES), BF16), pltpu.VMEM((HEAD_DIM, p), BF16)],
        compiler_params=_cparams(("parallel", "parallel")),
        name="swa_attention",
    )(sink.astype(F32) * LOG2E, qkv, qkv, qkv)


def _mla_down_kernel(x_ref, w_ref, gq_ref, gkv_ref, cos_ref, sin_ref, cq_ref, ckv_ref, kr_ref, *, q_rank, kv_rank):
    acc = _dot(x_ref[0], w_ref[...])

    def rms(x, g):
        ms = jnp.mean(x * x, axis=-1, keepdims=True)
        return x * lax.rsqrt(ms + NORM_EPS) * g

    cq_ref[0] = rms(acc[:, :q_rank], gq_ref[...]).astype(cq_ref.dtype)
    ckv_ref[0] = rms(acc[:, q_rank:q_rank + kv_rank], gkv_ref[...]).astype(ckv_ref.dtype)
    kr_ref[0] = _rope(acc[:, q_rank + kv_rank:], cos_ref[...], sin_ref[...]).astype(kr_ref.dtype)


def mla_down(x, w_down, gq, gkv, cos, sin, q_rank, kv_rank):
    b, p, k = x.shape
    n = w_down.shape[1]
    tm = 544 if p % 544 == 0 else _row_tile(p)
    row = lambda width: pl.BlockSpec((1, tm, width), lambda b, i: (b, i, 0))
    return pl.pallas_call(
        functools.partial(_mla_down_kernel, q_rank=q_rank, kv_rank=kv_rank),
        grid=(b, p // tm),
        in_specs=[row(k), pl.BlockSpec((k, n), lambda b, i: (0, 0)),
                  pl.BlockSpec((1, q_rank), lambda b, i: (0, 0)),
                  pl.BlockSpec((1, kv_rank), lambda b, i: (0, 0)),
                  pl.BlockSpec((tm, LANES), lambda b, i: (i, 0)),
                  pl.BlockSpec((tm, LANES), lambda b, i: (i, 0))],
        out_specs=[row(q_rank), row(kv_rank), row(LANES)],
        out_shape=[jax.ShapeDtypeStruct((b, p, q_rank), BF16),
                   jax.ShapeDtypeStruct((b, p, kv_rank), BF16),
                   jax.ShapeDtypeStruct((b, p, LANES), BF16)],
        compiler_params=_cparams(("parallel", "parallel")),
        name="mla_down",
    )(x, w_down, gq.reshape(1, q_rank), gkv.reshape(1, kv_rank), cos, sin)


def _mla_kernel(cq_ref, ckv_ref, kr_ref, wq_ref, wkv_ref, cos_ref, sin_ref, o_ref, qcat_ref, kcat_ref, vone_ref, *,
                ctx_len):
    lc = ctx_len
    p_all = vone_ref.shape[0]
    kv = _dot(ckv_ref[0], wkv_ref[...])
    kcat_ref[0:LANES, :] = kv[:, :LANES].T.astype(BF16)
    kcat_ref[LANES:2 * LANES, :] = _transposed(kr_ref[0])
    vone_ref[:, 0:LANES] = kv[:, LANES:].astype(BF16)
    vone_ref[:, LANES:2 * LANES] = jnp.ones((p_all, LANES), BF16)
    q = _dot(cq_ref[0], wq_ref[...])
    qcat_ref[:, 0:LANES] = q[:, :LANES].astype(BF16)
    qcat_ref[:, LANES:2 * LANES] = _rope(q[:, LANES:], cos_ref[...], sin_ref[...]).astype(BF16)

    def attend(qs, n_q, chunks):
        q = qcat_ref[pl.ds(qs, n_q), :]
        m = jnp.full((n_q, 1), NEG_INF, F32)
        acc = jnp.zeros((n_q, 2 * LANES), F32)
        for c0, c1 in chunks:
            s = _dot(q, kcat_ref[:, c0:c1])
            m_new = jnp.maximum(m, _row_reduce(jnp.maximum, jnp.max, [s]))
            p = jnp.exp2((s - m_new).astype(BF16))
            acc = jnp.exp2(m - m_new) * acc + _dot(p, vone_ref[c0:c1, :])
            m = m_new
        o_ref[0, pl.ds(qs, n_q), :] = (acc[:, :LANES] * (1.0 / acc[:, LANES:LANES + 1])).astype(o_ref.dtype)

    attend(0, lc, [(0, lc)])
    all_chunks = [(0, lc)] + [(c, c + MLA_TK) for c in range(lc, p_all, MLA_TK)]

    def tile(i, carry):
        attend(pl.multiple_of(lc + i * MLA_TQ, MLA_TQ), MLA_TQ, all_chunks)
        return carry

    lax.fori_loop(0, (p_all - lc) // MLA_TQ, tile, 0, unroll=2)


def latent_attention(cq, ckv, kr, w_q, w_kv, cos, sin, ctx_len, n_heads):
    b, p, q_rank = cq.shape
    kv_rank = ckv.shape[2]
    assert (p - ctx_len) % MLA_TQ == 0 and (p - ctx_len) % MLA_TK == 0 and ctx_len % 16 == 0
    per_batch = lambda width: pl.BlockSpec((1, p, width), lambda b, h: (b, 0, 0), pipeline_mode=pl.Buffered(1))
    table = pl.BlockSpec((p, LANES), lambda b, h: (0, 0), pipeline_mode=pl.Buffered(1))
    return pl.pallas_call(
        functools.partial(_mla_kernel, ctx_len=ctx_len),
        grid=(b, n_heads),
        in_specs=[per_batch(q_rank), per_batch(kv_rank), per_batch(LANES),
                  pl.BlockSpec((q_rank, 2 * LANES), lambda b, h: (0, h)),
                  pl.BlockSpec((kv_rank, 2 * LANES), lambda b, h: (0, h)),
                  table, table],
        out_specs=pl.BlockSpec((1, p, LANES), lambda b, h: (b, 0, h)),
        out_shape=jax.ShapeDtypeStruct((b, p, n_heads * LANES), BF16),
        scratch_shapes=[pltpu.VMEM((p, 2 * LANES), BF16), pltpu.VMEM((2 * LANES, p), BF16),
                        pltpu.VMEM((p, 2 * LANES), BF16)],
        compiler_params=_cparams(("parallel", "parallel")),
        name="mla_attention",
    )(cq, ckv, kr, w_q, w_kv, cos, sin)


def gather_rows(src, idx):
    n = idx.shape[0]
    width = src.shape[1]
    win = GATHER_WINDOW
    info = plsc.get_sparse_core_info()
    n_workers = info.num_cores * info.num_subcores
    assert n % (n_workers * 2 * win) == 0
    per_worker = n // n_workers
    mesh = plsc.VectorSubcoreMesh(core_axis_name="core", subcore_axis_name="subcore")

    @functools.partial(
        pl.kernel, out_type=jax.ShapeDtypeStruct((n, width), src.dtype), mesh=mesh,
        scratch_types=[pltpu.VMEM((per_worker,), jnp.int32),
                       pltpu.VMEM((win, width), src.dtype), pltpu.VMEM((win, width), src.dtype),
                       pltpu.SemaphoreType.DMA, pltpu.SemaphoreType.DMA,
                       pltpu.SemaphoreType.DMA, pltpu.SemaphoreType.DMA],
        name="gather_rows")
    def gather(src_hbm, idx_hbm, out_hbm, idx_vmem, rows0, rows1, gsem0, gsem1, osem0, osem1):
        worker = lax.axis_index("subcore") * info.num_cores + lax.axis_index("core")
        base = worker * per_worker
        pltpu.sync_copy(idx_hbm.at[pl.ds(base, per_worker)], idx_vmem)

        def fetch(r, rows, sem):
            return pltpu.make_async_copy(src_hbm.at[idx_vmem.at[pl.ds(r, win)]], rows, sem)

        def flush(r, rows, sem):
            return pltpu.make_async_copy(rows, out_hbm.at[pl.ds(base + r, win)], sem)

        fetch(0, rows0, gsem0).start()

        @pl.loop(0, per_worker, step=2 * win)
        def _(r):
            fetch(r, rows0, gsem0).wait()

            @pl.when(r > 0)
            def _():
                flush(r - win, rows1, osem1).wait()

            fetch(r + win, rows1, gsem1).start()
            flush(r, rows0, osem0).start()
            fetch(r + win, rows1, gsem1).wait()
            flush(r, rows0, osem0).wait()

            @pl.when(r + 2 * win < per_worker)
            def _():
                fetch(r + 2 * win, rows0, gsem0).start()

            flush(r + win, rows1, osem1).start()

        flush(per_worker - win, rows1, osem1).wait()

    return gather(src, idx)


def scatter_rows(src, idx, n_out):
    n = idx.shape[0]
    n_src, width = src.shape
    win = GATHER_WINDOW
    info = plsc.get_sparse_core_info()
    n_workers = info.num_cores * info.num_subcores
    assert n % (n_workers * 2 * win) == 0
    per_worker = n // n_workers
    assert n_src % per_worker == 0
    n_steps = per_worker // win
    mesh = plsc.VectorSubcoreMesh(core_axis_name="core", subcore_axis_name="subcore")

    @functools.partial(
        pl.kernel, out_type=jax.ShapeDtypeStruct((n_out, width), src.dtype), mesh=mesh,
        scratch_types=[pltpu.VMEM((n_steps, win), jnp.int32),
                       pltpu.VMEM((win, width), src.dtype), pltpu.VMEM((win, width), src.dtype),
                       pltpu.SemaphoreType.DMA, pltpu.SemaphoreType.DMA,
                       pltpu.SemaphoreType.DMA, pltpu.SemaphoreType.DMA],
        name="scatter_rows")
    def scatter(src_hbm, idx_hbm, out_hbm, idx_vmem, rows0, rows1, lsem0, lsem1, ssem0, ssem1):
        worker = lax.axis_index("subcore") * info.num_cores + lax.axis_index("core")
        src_base = lax.rem(worker * per_worker, n_src)
        pltpu.sync_copy(idx_hbm.at[worker], idx_vmem)

        def load(j, rows, sem):
            return pltpu.make_async_copy(src_hbm.at[pl.ds(src_base + j * win, win)], rows, sem)

        def store(j, rows, sem):
            return pltpu.make_async_copy(rows, out_hbm.at[idx_vmem.at[j]], sem)

        load(0, rows0, lsem0).start()

        @pl.loop(0, n_steps, step=2)
        def _(j):
            load(j, rows0, lsem0).wait()

            @pl.when(j > 0)
            def _():
                store(j - 1, rows1, ssem1).wait()

            load(j + 1, rows1, lsem1).start()
            store(j, rows0, ssem0).start()
            load(j + 1, rows1, lsem1).wait()
            store(j, rows0, ssem0).wait()

            @pl.when(j + 2 < n_steps)
            def _():
                load(j + 2, rows0, lsem0).start()

            store(j + 1, rows1, ssem1).start()

        store(n_steps - 1, rows1, ssem1).wait()

    return scatter(src, idx.reshape(n_workers, n_steps, win))


def _expert_kernel(blk_e_ref, n_used_ref, n_valid_ref, x_ref, wg_ref, wu_ref, wd_ref, y_ref, wg_bf, wu_bf, wd_bf):
    i = pl.program_id(0)
    used = i < n_used_ref[0]
    new_expert = (i == 0) | (blk_e_ref[i] != blk_e_ref[jnp.maximum(i - 1, 0)])

    @pl.when(used & new_expert)
    def _():
        wg_bf[...] = wg_ref[0].astype(BF16)
        wu_bf[...] = wu_ref[0].astype(BF16)
        wd_bf[...] = wd_ref[0].astype(BF16)

    @pl.when(used)
    def _():
        row = lax.broadcasted_iota(jnp.int32, (x_ref.shape[0], 1), 0)
        x = _unpack_bf16_pairs(jnp.where(row < n_valid_ref[i], x_ref[...], 0)).astype(BF16)
        g = _dot(x, wg_bf[...])
        u = _dot(x, wu_bf[...])
        a = (g * jax.nn.sigmoid(g) * u).astype(BF16)
        y_ref[...] = _pack_bf16_pairs(_dot(a, wd_bf[...]))

    @pl.when(jnp.logical_not(used))
    def _():
        y_ref[...] = jnp.zeros_like(y_ref)


def expert_ffn(x_disp, blk_e, n_used, n_valid, w_gate, w_up, w_down, layer):
    rows, half = x_disp.shape
    d = 2 * half
    n_blk = rows // MOE_BLOCK
    de = w_gate.shape[3]
    grid_spec = pltpu.PrefetchScalarGridSpec(
        num_scalar_prefetch=3,
        grid=(n_blk,),
        in_specs=[pl.BlockSpec((MOE_BLOCK, half), lambda i, be, nu, nv: (i, 0)),
                  pl.BlockSpec((None, 1, d, de), lambda i, be, nu, nv: (layer, be[i], 0, 0)),
                  pl.BlockSpec((None, 1, d, de), lambda i, be, nu, nv: (layer, be[i], 0, 0)),
                  pl.BlockSpec((None, 1, de, d), lambda i, be, nu, nv: (layer, be[i], 0, 0))],
        out_specs=pl.BlockSpec((MOE_BLOCK, half), lambda i, be, nu, nv: (i, 0)),
        scratch_shapes=[pltpu.VMEM((d, de), BF16), pltpu.VMEM((d, de), BF16), pltpu.VMEM((de, d), BF16)],
    )
    return pl.pallas_call(
        _expert_kernel,
        grid_spec=grid_spec,
        out_shape=jax.ShapeDtypeStruct((rows, half), jnp.int32),
        compiler_params=_cparams(("arbitrary",)),
        name="expert_ffn",
    )(blk_e, n_used, n_valid, x_disp, w_gate, w_up, w_down)


def _moe_residual(h_ref, y0_ref, y1_ref, wt_ref, mod_ref, gate_idx):
    wt = wt_ref[0]
    moe = wt[:, 0:1] * _unpack_bf16_pairs(y0_ref[0, 0]) + wt[:, 1:2] * _unpack_bf16_pairs(y1_ref[0, 0])
    return h_ref[0] + mod_ref[0, gate_idx:gate_idx + 1, :] * moe


def _combine_next_kernel(h_ref, y0_ref, y1_ref, wt_ref, mod_ref, g_ref, nmod_ref, o_ref, hm_ref, *, gate_idx):
    h_new = _moe_residual(h_ref, y0_ref, y1_ref, wt_ref, mod_ref, gate_idx)
    o_ref[0] = h_new
    hm_ref[0] = _rms_mod(h_new, g_ref[...], nmod_ref[0, 0:1, :], nmod_ref[0, 1:2, :]).astype(hm_ref.dtype)


def _combine_final_kernel(h_ref, y0_ref, y1_ref, wt_ref, mod_ref, g_ref, o_ref, *, gate_idx):
    x = _moe_residual(h_ref, y0_ref, y1_ref, wt_ref, mod_ref, gate_idx)
    ms = jnp.mean(x * x, axis=-1, keepdims=True)
    o_ref[0] = x * lax.rsqrt(ms + NORM_EPS) * g_ref[...]


def combine(h, y_pairs, wt, modtab, gate_idx, next_g, next_modtab, ctx_len):
    b, p, d = h.shape
    last = next_modtab is None
    skip = ctx_len // ROW_TILE if last else 0
    row = pl.BlockSpec((1, ROW_TILE, d), lambda b, j: (b, j + skip, 0))
    mod = pl.BlockSpec((1, N_MOD, d), lambda b, j: (2 * b + jnp.minimum(j + skip, 1), 0, 0))
    in_specs = [row,
                pl.BlockSpec((1, 1, ROW_TILE, d // 2), lambda b, j: (0, b, j + skip, 0)),
                pl.BlockSpec((1, 1, ROW_TILE, d // 2), lambda b, j: (1, b, j + skip, 0)),
                pl.BlockSpec((1, ROW_TILE, LANES), lambda b, j: (b, j + skip, 0)),
                mod, pl.BlockSpec((1, d), lambda b, j: (0, 0))]
    args = [h, y_pairs, y_pairs, wt, modtab.reshape(b * 2, N_MOD, d), next_g.reshape(1, d)]
    out_row = pl.BlockSpec((1, ROW_TILE, d), lambda b, j: (b, j, 0))
    if last:
        return pl.pallas_call(
            functools.partial(_combine_final_kernel, gate_idx=gate_idx),
            grid=(b, (p - ctx_len) // ROW_TILE),
            in_specs=in_specs,
            out_specs=out_row,
            out_shape=jax.ShapeDtypeStruct((b, p - ctx_len, d), F32),
            compiler_params=_cparams(("parallel", "parallel")),
            name="moe_combine_final",
        )(*args)
    return pl.pallas_call(
        functools.partial(_combine_next_kernel, gate_idx=gate_idx),
        grid=(b, p // ROW_TILE),
        in_specs=in_specs + [mod],
        out_specs=[out_row, out_row],
        out_shape=[jax.ShapeDtypeStruct((b, p, d), F32), jax.ShapeDtypeStruct((b, p, d), BF16)],
        compiler_params=_cparams(("parallel", "parallel")),
        name="moe_combine",
    )(*args, next_modtab.reshape(b * 2, N_MOD, d))


def hier_moe(h, f, route, wt, counts, modtab, gate_idx, w_gate, w_up, w_down, layer, next_g, next_modtab, ctx_len):
    b, p, d = h.shape
    n_tok = b * p
    n_assign = 2 * n_tok
    e1, e2, rank1, rank2 = route.reshape(n_tok, LANES)[:, :4].T
    cnt = counts[0, :N_EXPERTS].astype(jnp.int32)
    pcounts = (cnt + MOE_BLOCK - 1) // MOE_BLOCK * MOE_BLOCK
    pend = jnp.cumsum(pcounts)
    pstart = pend - pcounts
    dest_by_slot = jnp.concatenate([pstart[e1] + rank1, pstart[e2] + rank2]).astype(jnp.int32)
    n_blk = -(-n_assign // MOE_BLOCK) + N_EXPERTS
    rows_total = n_blk * MOE_BLOCK
    blk_row0 = jnp.arange(n_blk, dtype=jnp.int32) * MOE_BLOCK
    blk_e = jnp.minimum(jnp.sum((pend[None, :] <= blk_row0[:, None]).astype(jnp.int32), axis=1), N_EXPERTS - 1)
    n_valid = jnp.clip(cnt[blk_e] - (blk_row0 - pstart[blk_e]), 0, MOE_BLOCK).astype(jnp.int32)
    n_used = (pend[-1] // MOE_BLOCK).astype(jnp.int32).reshape(1)

    x_disp = scatter_rows(f.reshape(n_tok, d // 2), dest_by_slot, rows_total)
    y = expert_ffn(x_disp, blk_e, n_used, n_valid, w_gate, w_up, w_down, layer)
    y_pairs = gather_rows(y, dest_by_slot).reshape(2, b, p, d // 2)
    return combine(h, y_pairs, wt, modtab, gate_idx, next_g, next_modtab, ctx_len)


def kernel(x, c, ctx, c_ctx, mod_w, mod_b, norm_mix_g, norm_ffn_g, router_grp_w, router_grp_b, router_exp_w, router_exp_b, exp_w_gate, exp_w_up, exp_w_down, l0_na_w_qkv, l0_na_rpb, l0_na_w_o, l1_swa_w_qkv, l1_swa_sink, l1_swa_w_o, l2_mla_w_dq, l2_mla_q_norm_g, l2_mla_w_uq, l2_mla_w_dkv, l2_mla_kv_norm_g, l2_mla_w_ukv, l2_mla_w_o, l3_na_w_qkv, l3_na_rpb, l3_na_w_o, final_norm_g):
    b, s, d = x.shape
    lc = ctx.shape[1]
    n_heads = d // HEAD_DIM
    n_kv_heads = n_heads // 4
    depth = mod_w.shape[0]
    rows = s // GRID_W

    modtabs = modulation_tables(c, c_ctx, mod_w, mod_b)

    def scale_q_cols(w):
        n_q = n_heads * HEAD_DIM
        q_scale = HEAD_DIM ** -0.5 * LOG2E
        return jnp.concatenate([w[:, :n_q] * q_scale, w[:, n_q:]], axis=1).astype(BF16)

    def na_mixer(hm, w_qkv, rpb):
        qkv = project(hm, scale_q_cols(w_qkv))
        return neighbourhood_attention(qkv, na_bias_table(rpb, rows), lc, n_heads)

    def swa_mixer(hm):
        cos, sin = rope_tables_full(s, lc)
        n_rope = (n_heads + n_kv_heads) * HEAD_DIM // 512
        qkv = project(hm, scale_q_cols(l1_swa_w_qkv), tn=512, rope=(cos, sin, 0, n_rope))
        return window_attention(qkv, l1_swa_sink, lc, n_heads, n_kv_heads)

    def mla_mixer(hm):
        q_rank = l2_mla_w_dq.shape[1]
        kv_rank = l2_mla_kv_norm_g.shape[0]
        cos, sin = rope_tables_mla(s, lc)
        w_down = jnp.concatenate([l2_mla_w_dq, l2_mla_w_dkv[:, :kv_rank],
                                  _spread_rope_cols(l2_mla_w_dkv[:, kv_rank:])], axis=1).astype(BF16)
        cq, ckv, kr = mla_down(hm, w_down, l2_mla_q_norm_g, l2_mla_kv_norm_g, cos, sin, q_rank, kv_rank)
        w_uq = l2_mla_w_uq.reshape(q_rank, n_heads, MLA_NOPE_DIM + MLA_ROPE_DIM)
        q_scale = (MLA_NOPE_DIM + MLA_ROPE_DIM) ** -0.5 * LOG2E
        half = MLA_ROPE_DIM // 2
        gap = jnp.zeros((q_rank, n_heads, half), l2_mla_w_uq.dtype)
        w_q = jnp.concatenate([w_uq[:, :, :MLA_NOPE_DIM], w_uq[:, :, MLA_NOPE_DIM:MLA_NOPE_DIM + half], gap,
                               w_uq[:, :, MLA_NOPE_DIM + half:], gap], axis=2).reshape(q_rank, n_heads * 2 * LANES)
        return latent_attention(cq, ckv, kr, (w_q * q_scale).astype(BF16), l2_mla_w_ukv.astype(BF16), cos, sin,
                                lc, n_heads)

    h, hm = join_norm_modulate(ctx, x, norm_mix_g[0], modtabs[0])
    for i in range(depth):
        modtab = modtabs[i]
        mixer = i % 3
        if mixer == 0:
            w_qkv, rpb, w_o = (l0_na_w_qkv, l0_na_rpb, l0_na_w_o) if i == 0 else (l3_na_w_qkv, l3_na_rpb, l3_na_w_o)
            y = na_mixer(hm, w_qkv, rpb)
        elif mixer == 1:
            y, w_o = swa_mixer(hm), l1_swa_w_o
        else:
            y, w_o = mla_mixer(hm), l2_mla_w_o
        h, f, route, wt, counts = attn_out_route(y, w_o.astype(BF16), h, modtab, norm_ffn_g[i], router_grp_w[i],
                                                 router_grp_b[i], router_exp_w[i], router_exp_b[i], lc)
        if i + 1 < depth:
            h, hm = hier_moe(h, f, route, wt, counts, modtab, 5, exp_w_gate, exp_w_up, exp_w_down, i,
                             norm_mix_g[i + 1], modtabs[i + 1], lc)
        else:
            return hier_moe(h, f, route, wt, counts, modtab, 5, exp_w_gate, exp_w_up, exp_w_down, i,
                            final_norm_g, None, lc)
```

```python
import functools

import numpy as np
import jax
import jax.numpy as jnp
from jax import lax
from jax.experimental import pallas as pl
from jax.experimental.pallas import tpu as pltpu
from jax.experimental.pallas import tpu_sc as plsc

GRID_W = 64
HEAD_DIM = 128
ROPE_BASE = 10000.0
NORM_EPS = 1e-6
NEG_INF = -1e30
N_MOD = 6

NA_ROWS = 8
NA_COLS = 16
NA_TILE_ROWS = 4
NA_WIN_ROWS = 12

SWA_WINDOW = 128
SWA_TQ = 256
SWA_TK = 512

MLA_NOPE_DIM = 128
MLA_ROPE_DIM = 64
MLA_TQ = 512
MLA_TK = 1024
LOG2E = 1.4426950408889634

N_GROUPS = 4
EXPERTS_PER_GROUP = 8
N_EXPERTS = N_GROUPS * EXPERTS_PER_GROUP
MOE_BLOCK = 512
GATHER_WINDOW = 32

LANES = 128
ROW_TILE = 256
VMEM_LIMIT = 56 * 1024 * 1024

BF16 = jnp.bfloat16
F32 = jnp.float32


def _cparams(sem):
    return pltpu.CompilerParams(dimension_semantics=sem, vmem_limit_bytes=VMEM_LIMIT)


def _dot(a, b):
    return jnp.dot(a, b, preferred_element_type=F32)


def _dot_nt(a, b):
    return lax.dot_general(a, b, (((1,), (1,)), ((), ())), preferred_element_type=F32)


def _mod_kernel(x_ref, w_ref, b_ref, o_ref):
    x = x_ref[...]
    sx = (x * jax.nn.sigmoid(x)).astype(BF16)
    o_ref[0] = _dot(sx, w_ref[0].astype(BF16)) + b_ref[0]


def modulation_tables(c, c_ctx, mod_w, mod_b):
    depth, d, n_out = mod_w.shape
    b = c.shape[0]
    rows = 16
    xin = jnp.zeros((rows, d), F32).at[:b].set(c).at[b].set(c_ctx)
    tn = 1024
    out = pl.pallas_call(
        _mod_kernel,
        grid=(depth, n_out // tn),
        in_specs=[pl.BlockSpec((rows, d), lambda i, j: (0, 0)),
                  pl.BlockSpec((1, d, tn), lambda i, j: (i, 0, j)),
                  pl.BlockSpec((1, 1, tn), lambda i, j: (i, 0, j))],
        out_specs=pl.BlockSpec((1, rows, tn), lambda i, j: (i, 0, j)),
        out_shape=jax.ShapeDtypeStruct((depth, rows, n_out), F32),
        compiler_params=_cparams(("parallel", "parallel")),
        name="adaln_mod",
    )(xin, mod_w, mod_b.reshape(depth, 1, n_out))
    lat = out[:, :b].reshape(depth, b, 1, N_MOD, d)
    ctx = jnp.broadcast_to(out[:, b].reshape(depth, 1, 1, N_MOD, d), (depth, b, 1, N_MOD, d))
    return jnp.concatenate([ctx, lat], axis=2)


def _rms_mod(x, g, shift, scale):
    ms = jnp.mean(x * x, axis=-1, keepdims=True)
    y = x * lax.rsqrt(ms + NORM_EPS) * g
    return y * (1.0 + scale) + shift


def _join_norm_mod_kernel(ctx_ref, x_ref, g_ref, mod_ref, h_ref, hm_ref, *, ctx_tiles):
    j = pl.program_id(1)

    def emit(src_ref):
        h_ref[0] = src_ref[0]
        hm_ref[0] = _rms_mod(src_ref[0], g_ref[...], mod_ref[0, 0:1, :], mod_ref[0, 1:2, :]).astype(hm_ref.dtype)

    @pl.when(j < ctx_tiles)
    def _():
        emit(ctx_ref)

    @pl.when(j >= ctx_tiles)
    def _():
        emit(x_ref)


def _route(logits):
    lane = lax.broadcasted_iota(jnp.int32, logits.shape, 1).astype(F32)
    big = float(LANES)

    def first_lane(mask):
        return jnp.min(jnp.where(mask, lane, big), axis=-1, keepdims=True)

    in_grp = lane < N_GROUPS
    lg = jnp.where(in_grp, logits, NEG_INF)
    m_g = jnp.max(lg, axis=-1, keepdims=True)
    g_idx = first_lane(in_grp & (lg == m_g))
    g_w = 1.0 / jnp.sum(jnp.where(in_grp, jnp.exp(lg - m_g), 0.0), axis=-1, keepdims=True)
    e_lo = N_GROUPS + g_idx * EXPERTS_PER_GROUP
    in_e = (lane >= e_lo) & (lane < e_lo + EXPERTS_PER_GROUP)
    le = jnp.where(in_e, logits, NEG_INF)
    m1 = jnp.max(le, axis=-1, keepdims=True)
    e1 = first_lane(in_e & (le == m1))
    s_e = jnp.sum(jnp.where(in_e, jnp.exp(le - m1), 0.0), axis=-1, keepdims=True)
    in_e2 = in_e & (lane != e1)
    le2 = jnp.where(in_e2, logits, NEG_INF)
    m2 = jnp.max(le2, axis=-1, keepdims=True)
    e2 = first_lane(in_e2 & (le2 == m2))
    p1 = 1.0 / s_e
    p2 = jnp.exp(m2 - m1) / s_e
    den = p1 + p2
    return ((e1 - N_GROUPS).astype(jnp.int32), (e2 - N_GROUPS).astype(jnp.int32),
            g_w * p1 / den, g_w * p2 / den)


def _pack_bf16_pairs(x):
    n = x.shape[1] // 2
    xb = x.astype(BF16).astype(F32)
    hi = lax.bitcast_convert_type(xb[:, :n], jnp.int32)
    lo = lax.bitcast_convert_type(xb[:, n:], jnp.int32)
    return (hi & jnp.int32(-65536)) | lax.shift_right_logical(lo, jnp.int32(16))


def _unpack_bf16_pairs(w):
    hi = lax.bitcast_convert_type(w & jnp.int32(-65536), F32)
    lo = lax.bitcast_convert_type(lax.shift_left(w, jnp.int32(16)), F32)
    return jnp.concatenate([hi, lo], axis=1)


def _route_and_rank(f, wr_ref, br_ref, run_ref):
    f_hi = f.astype(BF16)
    f_lo = (f - f_hi.astype(F32)).astype(BF16)
    hi_terms = _dot(f_hi, wr_ref[...])
    logits = (hi_terms[:, :LANES] + _dot(f_lo, wr_ref[:, :LANES]) + hi_terms[:, LANES:]) + br_ref[...]
    e1, e2, w1, w2 = _route(logits)
    lane = lax.broadcasted_iota(jnp.int32, logits.shape, 1)
    pick1, pick2 = lane == e1, lane == e2
    chosen = (pick1 | pick2).astype(F32)
    n_rows = chosen.shape[0]
    earlier = (lax.broadcasted_iota(jnp.int32, (n_rows, n_rows), 1)
               < lax.broadcasted_iota(jnp.int32, (n_rows, n_rows), 0)).astype(BF16)
    before = run_ref[...] + _dot(earlier, chosen.astype(BF16))
    rank1 = jnp.sum(jnp.where(pick1, before, 0.0), axis=-1, keepdims=True).astype(jnp.int32)
    rank2 = jnp.sum(jnp.where(pick2, before, 0.0), axis=-1, keepdims=True).astype(jnp.int32)
    run_ref[...] = run_ref[...] + jnp.sum(chosen, axis=0, keepdims=True)
    route = jnp.where(lane == 0, e1, jnp.where(lane == 1, e2, jnp.where(lane == 2, rank1,
                                                                         jnp.where(lane == 3, rank2, 0))))
    return route, jnp.where(lane == 0, w1, jnp.where(lane == 1, w2, 0.0))


def _attn_out_route_kernel(y_ref, w_ref, h_ref, mod_ref, g_ref, wr_ref, br_ref,
                           ho_ref, f_ref, route_ref, wt_ref, cnt_ref, run_ref, *, ctx_len, tm, n_sub):
    first = (pl.program_id(0) == 0) & (pl.program_id(1) == 0)

    @pl.when(first)
    def _():
        run_ref[...] = jnp.zeros_like(run_ref)

    sub = tm // n_sub
    for s in range(n_sub):
        rows = slice(s * sub, (s + 1) * sub)
        acc = _dot(y_ref[0, rows, :], w_ref[...])
        pos = pl.program_id(1) * tm + s * sub + lax.broadcasted_iota(jnp.int32, (sub, 1), 0)
        is_ctx = pos < ctx_len

        def mod_row(k):
            return jnp.where(is_ctx, mod_ref[0, 0, k:k + 1, :], mod_ref[0, 1, k:k + 1, :])

        h_new = h_ref[0, rows, :] + mod_row(2) * acc
        ho_ref[0, rows, :] = h_new
        f = _rms_mod(h_new, g_ref[...], mod_row(3), mod_row(4))
        f_ref[0, rows, :] = _pack_bf16_pairs(f)
        route, wt = _route_and_rank(f, wr_ref, br_ref, run_ref)
        route_ref[0, rows, :] = route
        wt_ref[0, rows, :] = wt
    cnt_ref[...] = run_ref[...]


def _mod_spec(d):
    return pl.BlockSpec((1, N_MOD, d), lambda b, j: (2 * b + jnp.minimum(j, 1), 0, 0))


def join_norm_modulate(ctx, x, g, modtab):
    b, lc, d = ctx.shape
    p = lc + x.shape[1]
    assert lc == ROW_TILE
    ctx_tiles = lc // ROW_TILE
    row = pl.BlockSpec((1, ROW_TILE, d), lambda b, j: (b, j, 0))
    return pl.pallas_call(
        functools.partial(_join_norm_mod_kernel, ctx_tiles=ctx_tiles),
        grid=(b, p // ROW_TILE),
        in_specs=[pl.BlockSpec((1, ROW_TILE, d), lambda b, j: (b, jnp.minimum(j, ctx_tiles - 1), 0)),
                  pl.BlockSpec((1, ROW_TILE, d), lambda b, j: (b, jnp.maximum(j - ctx_tiles, 0), 0)),
                  pl.BlockSpec((1, d), lambda b, j: (0, 0)), _mod_spec(d)],
        out_specs=[row, row],
        out_shape=[jax.ShapeDtypeStruct((b, p, d), F32), jax.ShapeDtypeStruct((b, p, d), BF16)],
        compiler_params=_cparams(("parallel", "parallel")),
        name="join_norm_mod",
    )(ctx, x, g.reshape(1, d), modtab.reshape(b * 2, N_MOD, d))


def attn_out_route(y, w_o, h, modtab, g, w_grp, b_grp, w_rt, b_rt, ctx_len):
    b, p, d = h.shape
    n_r = N_GROUPS + N_EXPERTS
    wr = jnp.zeros((d, LANES), F32).at[:, :N_GROUPS].set(w_grp).at[:, N_GROUPS:n_r].set(w_rt)
    br = jnp.zeros((1, LANES), F32).at[0, :N_GROUPS].set(b_grp).at[0, N_GROUPS:n_r].set(b_rt)
    wr_hi = wr.astype(BF16)
    wr = jnp.concatenate([wr_hi, (wr - wr_hi.astype(F32)).astype(BF16)], axis=1)
    tm = 544 if p % 544 == 0 else ROW_TILE
    n_sub = 2
    assert (tm // n_sub) % 16 == 0
    row = lambda width: pl.BlockSpec((1, tm, width), lambda b, i: (b, i, 0))
    once = lambda shape: pl.BlockSpec(shape, lambda b, i: (0,) * len(shape), pipeline_mode=pl.Buffered(1))
    return pl.pallas_call(
        functools.partial(_attn_out_route_kernel, ctx_len=ctx_len, tm=tm, n_sub=n_sub),
        grid=(b, p // tm),
        in_specs=[row(d), once((d, d)), row(d),
                  pl.BlockSpec((1, 2, N_MOD, d), lambda b, i: (b, 0, 0, 0)),
                  once((1, d)), once((d, 2 * LANES)), once((1, LANES))],
        out_specs=[row(d), row(d // 2), row(LANES), row(LANES), pl.BlockSpec((1, LANES), lambda b, i: (0, 0))],
        out_shape=[jax.ShapeDtypeStruct((b, p, d), F32),
                   jax.ShapeDtypeStruct((b, p, d // 2), jnp.int32),
                   jax.ShapeDtypeStruct((b, p, LANES), jnp.int32),
                   jax.ShapeDtypeStruct((b, p, LANES), F32),
                   jax.ShapeDtypeStruct((1, LANES), F32)],
        scratch_shapes=[pltpu.VMEM((1, LANES), F32)],
        compiler_params=_cparams(("arbitrary", "arbitrary")),
        name="attn_out_route",
    )(y, w_o, h, modtab, g.reshape(1, d), wr, br)


def _rope(acc, cos, sin):
    n_blk = acc.shape[1] // LANES
    outs = []
    for c in range(n_blk):
        x = acc[:, c * LANES:(c + 1) * LANES]
        outs.append(x * cos + pltpu.roll(x, LANES // 2, 1) * sin)
    return outs[0] if n_blk == 1 else jnp.concatenate(outs, axis=1)


def _proj_kernel(*refs, rope_lo, rope_hi):
    x_ref, w_ref = refs[0], refs[1]
    o_ref = refs[-1]
    acc = _dot(x_ref[0], w_ref[...])
    if rope_hi > rope_lo:
        cos_ref, sin_ref = refs[2], refs[3]
        j = pl.program_id(2)
        roped = (j >= rope_lo) & (j < rope_hi)

        @pl.when(roped)
        def _():
            o_ref[0] = _rope(acc, cos_ref[...], sin_ref[...]).astype(o_ref.dtype)

        @pl.when(jnp.logical_not(roped))
        def _():
            o_ref[0] = acc.astype(o_ref.dtype)
    else:
        o_ref[0] = acc.astype(o_ref.dtype)


def _row_tile(p):
    for cand in (1088, 1024, 544, 512, 272, 256, 128, 64, 32, 16):
        if p % cand == 0:
            return cand
    raise ValueError(p)


def project(x, w, *, tn=512, out_dtype=BF16, rope=None):
    b, p, k = x.shape
    n = w.shape[1]
    tm = _row_tile(p)
    tn = min(tn, n)
    assert n % tn == 0
    in_specs = [pl.BlockSpec((1, tm, k), lambda b, i, j: (b, i, 0)),
                pl.BlockSpec((k, tn), lambda b, i, j: (0, j))]
    args = [x, w]
    kw = dict(rope_lo=0, rope_hi=0)
    if rope is not None:
        cos, sin, lo, hi = rope
        in_specs += [pl.BlockSpec((tm, LANES), lambda b, i, j: (i, 0)),
                     pl.BlockSpec((tm, LANES), lambda b, i, j: (i, 0))]
        args += [cos, sin]
        kw.update(rope_lo=lo, rope_hi=hi)
    return pl.pallas_call(
        functools.partial(_proj_kernel, **kw),
        grid=(b, p // tm, n // tn),
        in_specs=in_specs,
        out_specs=pl.BlockSpec((1, tm, tn), lambda b, i, j: (b, i, j)),
        out_shape=jax.ShapeDtypeStruct((b, p, n), out_dtype),
        compiler_params=_cparams(("parallel", "parallel", "arbitrary")),
        name="project",
    )(*args)


def _axial_cos_sin(n, rot_dim):
    t = jnp.arange(n, dtype=jnp.int32)
    row = (t // GRID_W).astype(F32)
    col = (t % GRID_W).astype(F32)
    n_freq = rot_dim // 4
    inv = ROPE_BASE ** (-jnp.arange(n_freq, dtype=F32) / n_freq)
    ang = jnp.concatenate([row[:, None] * inv, col[:, None] * inv], axis=-1)
    return jnp.cos(ang), jnp.sin(ang)


def rope_tables_full(s, ctx_len):
    c, sn = _axial_cos_sin(s, HEAD_DIM)
    cos = jnp.concatenate([c, c], axis=1)
    sin = jnp.concatenate([-sn, sn], axis=1)
    ident_c = jnp.ones((ctx_len, LANES), F32)
    ident_s = jnp.zeros((ctx_len, LANES), F32)
    return jnp.concatenate([ident_c, cos], axis=0), jnp.concatenate([ident_s, sin], axis=0)


def rope_tables_mla(s, ctx_len):
    c, sn = _axial_cos_sin(s, MLA_ROPE_DIM)
    one = jnp.ones_like(c)
    zero = jnp.zeros_like(c)
    cos = jnp.concatenate([c, one, c, one], axis=1)
    sin = jnp.concatenate([-sn, zero, sn, zero], axis=1)
    ident_c = jnp.ones((ctx_len, LANES), F32)
    ident_s = jnp.zeros((ctx_len, LANES), F32)
    return jnp.concatenate([ident_c, cos], axis=0), jnp.concatenate([ident_s, sin], axis=0)


def _spread_rope_cols(w_rope):
    k = w_rope.shape[0]
    half = MLA_ROPE_DIM // 2
    z = jnp.zeros((k, half), w_rope.dtype)
    return jnp.concatenate([w_rope[:, :half], z, w_rope[:, half:], z], axis=1)


def _softmax_probs(parts, extra=None):
    m = _row_reduce(jnp.maximum, jnp.max, parts)
    if extra is not None:
        m = jnp.maximum(m, extra)
    return [jnp.exp2((s - m).astype(BF16)) for s in parts], m


def _with_ones(v_ref, vone_ref):
    vone_ref[:, 0:LANES] = v_ref[0]
    vone_ref[:, LANES:2 * LANES] = jnp.ones((vone_ref.shape[0], LANES), BF16)


def _normalise(acc, extra_den=None):
    den = acc[:, LANES:LANES + 1]
    if extra_den is not None:
        den = den + extra_den
    return acc[:, :LANES] * (1.0 / den)


def _row_reduce(combine, reduce, parts):
    blocks = [s[:, c:c + LANES] for s in parts for c in range(0, s.shape[1], LANES)]
    acc = blocks[0]
    for blk in blocks[1:]:
        acc = combine(acc, blk)
    return reduce(acc, axis=-1, keepdims=True)


def na_bias_table(rpb, rows):
    n_tiles = rows // NA_TILE_ROWS
    n_heads, _, n_dcol = rpb.shape
    drow, row_ok = [], []
    for tile in (0, 1, n_tiles - 1):
        kr0 = int(np.clip(NA_TILE_ROWS * tile - NA_ROWS // 2, 0, rows - NA_WIN_ROWS))
        r = NA_TILE_ROWS * tile + np.arange(NA_TILE_ROWS)
        r0 = np.clip(r - NA_ROWS // 2, 0, rows - NA_ROWS)
        krow = kr0 + np.arange(NA_WIN_ROWS)
        row_ok.append((krow[None, :] >= r0[:, None]) & (krow[None, :] < r0[:, None] + NA_ROWS))
        drow.append(np.clip(krow[None, :] - r[:, None] + NA_ROWS - 1, 0, 2 * NA_ROWS - 2))
    drow, row_ok = np.stack(drow), np.stack(row_ok)
    qc = np.arange(GRID_W)
    qcol0 = np.clip(qc - NA_COLS // 2, 0, GRID_W - NA_COLS)
    kc = np.arange(GRID_W)
    col_ok = (kc[None, :] >= qcol0[:, None]) & (kc[None, :] < qcol0[:, None] + NA_COLS)
    dcol = np.clip(kc[None, :] - qc[:, None] + NA_COLS - 1, 0, 2 * NA_COLS - 2)
    pick_col = jnp.asarray(dcol[None] == np.arange(n_dcol)[:, None, None], F32)
    slabs = jnp.einsum('hrd,dqk->hrqk', rpb.astype(F32), pick_col, precision=lax.Precision.HIGHEST)
    slabs = jnp.where(jnp.asarray(col_ok)[None, None], slabs * LOG2E, NEG_INF)
    masked = jnp.full((n_heads, GRID_W, GRID_W), NEG_INF, F32)
    pats = []
    for p in range(3):
        per_row = [jnp.stack([slabs[:, drow[p, a, m]] if row_ok[p, a, m] else masked
                              for m in range(NA_WIN_ROWS)], axis=2)
                   for a in range(NA_TILE_ROWS)]
        pats.append(jnp.stack(per_row, axis=1))
    vals = jnp.stack(pats, axis=1)
    return vals.reshape(n_heads, 3, NA_TILE_ROWS * GRID_W, NA_WIN_ROWS * GRID_W)


def _na_kernel(q_ref, k_ref, v_ref, bias_ref, o_ref, vone_ref, *, ctx_len, rows):
    tq = NA_TILE_ROWS * GRID_W
    tk = NA_WIN_ROWS * GRID_W
    n_tiles = rows // NA_TILE_ROWS
    lc = ctx_len
    _with_ones(v_ref, vone_ref)

    s = _dot_nt(q_ref[0, 0:lc, :], k_ref[0, 0:lc, :])
    (p,), _ = _softmax_probs([s])
    o_ref[0, 0:lc, :] = _normalise(_dot(p, vone_ref[0:lc, :])).astype(o_ref.dtype)

    def tile(i, carry):
        qs = pl.multiple_of(lc + i * tq, tq)
        kr0 = jnp.clip(NA_TILE_ROWS * i - NA_ROWS // 2, 0, rows - NA_WIN_ROWS)
        ks = pl.multiple_of(lc + kr0 * GRID_W, NA_TILE_ROWS * GRID_W)
        pat = jnp.where(i == 0, 0, jnp.where(i == n_tiles - 1, 2, 1))
        q = q_ref[0, pl.ds(qs, tq), :]
        s_loc = _dot_nt(q, k_ref[0, pl.ds(ks, tk), :]) + bias_ref[0, pat]
        s_ctx = _dot_nt(q, k_ref[0, 0:lc, :])
        (p_loc, p_ctx), _ = _softmax_probs([s_loc, s_ctx])
        acc = _dot(p_loc, vone_ref[pl.ds(ks, tk), :]) + _dot(p_ctx, vone_ref[0:lc, :])
        o_ref[0, pl.ds(qs, tq), :] = _normalise(acc).astype(o_ref.dtype)
        return carry

    lax.fori_loop(0, n_tiles, tile, 0, unroll=4)


def neighbourhood_attention(qkv, bias, ctx_len, n_heads):
    b, p, _ = qkv.shape
    rows = (p - ctx_len) // GRID_W
    assert rows % NA_TILE_ROWS == 0 and rows >= NA_WIN_ROWS
    tq, tk = NA_TILE_ROWS * GRID_W, NA_WIN_ROWS * GRID_W
    assert ctx_len % 16 == 0 and ctx_len % tq == 0
    blk = lambda off: pl.BlockSpec((1, p, HEAD_DIM), lambda h, b: (b, 0, off + h))
    return pl.pallas_call(
        functools.partial(_na_kernel, ctx_len=ctx_len, rows=rows),
        grid=(n_heads, b),
        in_specs=[blk(0), blk(n_heads), blk(2 * n_heads),
                  pl.BlockSpec((1, 3, tq, tk), lambda h, b: (h, 0, 0, 0))],
        out_specs=blk(0),
        out_shape=jax.ShapeDtypeStruct((b, p, n_heads * HEAD_DIM), BF16),
        scratch_shapes=[pltpu.VMEM((p, 2 * LANES), BF16)],
        compiler_params=_cparams(("parallel", "parallel")),
        name="na_attention",
    )(qkv, qkv, qkv, bias)


def _swa_kernel(sink_ref, q_ref, k_ref, v_ref, o_ref, vone_ref, *, ctx_len, seq, group):
    lc = ctx_len
    kvh = pl.program_id(1)
    _with_ones(v_ref, vone_ref)

    def stack_heads(q):
        return jnp.concatenate([q[:, g * HEAD_DIM:(g + 1) * HEAD_DIM] for g in range(group)], axis=0)

    def finish(parts_fn, n_q, pv_fn, store):
        ps_all, sink_terms = [], []
        for g in range(group):
            sink = sink_ref[kvh * group + g]
            ps, m = _softmax_probs(parts_fn(g), extra=sink)
            ps_all.append(ps)
            sink_terms.append(jnp.exp2(sink - m))
        n_parts = len(ps_all[0])
        stacked = [jnp.concatenate([ps_all[g][k] for g in range(group)], axis=0) for k in range(n_parts)]
        acc = pv_fn(stacked)
        store(jnp.concatenate([_normalise(acc[g * n_q:(g + 1) * n_q], sink_terms[g]) for g in range(group)], axis=1))

    s_c = _dot_nt(stack_heads(q_ref[0, 0:lc, :]), k_ref[0, 0:lc, :])

    def store_ctx(o):
        o_ref[0, 0:lc, :] = o.astype(o_ref.dtype)

    finish(lambda g: [s_c[g * lc:(g + 1) * lc]], lc,
           lambda st: _dot(st[0], vone_ref[0:lc, :]), store_ctx)

    n_tiles = seq // SWA_TQ

    def tile(t, carry):
        q0 = t * SWA_TQ
        k0 = jnp.clip(q0 - SWA_WINDOW, 0, seq - SWA_TK)
        qs = pl.multiple_of(lc + q0, SWA_WINDOW)
        ks = pl.multiple_of(lc + k0, SWA_WINDOW)
        q4 = stack_heads(q_ref[0, pl.ds(qs, SWA_TQ), :])
        s_loc = _dot_nt(q4, k_ref[0, pl.ds(ks, SWA_TK), :])
        s_ctx = _dot_nt(q4, k_ref[0, 0:lc, :])
        dpos = (lax.broadcasted_iota(jnp.int32, (SWA_TQ, SWA_TK), 1)
                - lax.broadcasted_iota(jnp.int32, (SWA_TQ, SWA_TK), 0)) + (k0 - q0)
        valid = jnp.abs(dpos) <= SWA_WINDOW

        def parts(g):
            sl = slice(g * SWA_TQ, (g + 1) * SWA_TQ)
            return [jnp.where(valid, s_loc[sl], NEG_INF), s_ctx[sl]]

        def store(o):
            o_ref[0, pl.ds(qs, SWA_TQ), :] = o.astype(o_ref.dtype)

        finish(parts, SWA_TQ,
               lambda st: _dot(st[0], vone_ref[pl.ds(ks, SWA_TK), :]) + _dot(st[1], vone_ref[0:lc, :]),
               store)
        return carry

    lax.fori_loop(0, n_tiles, tile, 0, unroll=2)


def window_attention(qkv, sink, ctx_len, n_heads, n_kv_heads):
    b, p, _ = qkv.shape
    seq = p - ctx_len
    group = n_heads // n_kv_heads
    assert seq % SWA_TQ == 0 and seq >= SWA_TK and ctx_len % SWA_WINDOW == 0
    kv = lambda off: pl.BlockSpec((1, p, HEAD_DIM), lambda b, h: (b, 0, off + h))
    qo = pl.BlockSpec((1, p, group * HEAD_DIM), lambda b, h: (b, 0, h))
    return pl.pallas_call(
        functools.partial(_swa_kernel, ctx_len=ctx_len, seq=seq, group=group),
        grid=(b, n_kv_heads),
        in_specs=[pl.BlockSpec(memory_space=pltpu.SMEM), qo, kv(n_heads), kv(n_heads + n_kv_heads)],
        out_specs=qo,
        out_shape=jax.ShapeDtypeStruct((b, p, n_heads * HEAD_DIM), BF16),
        scratch_shapes=[pltpu.VMEM((p, 2 * LANES), BF16)],
        compiler_params=_cparams(("parallel", "parallel")),
        name="swa_attention",
    )(sink.astype(F32) * LOG2E, qkv, qkv, qkv)


def _mla_down_kernel(x_ref, w_ref, gq_ref, gkv_ref, cos_ref, sin_ref, cq_ref, ckv_ref, kr_ref, *, q_rank, kv_rank):
    acc = _dot(x_ref[0], w_ref[...])

    def rms(x, g):
        ms = jnp.mean(x * x, axis=-1, keepdims=True)
        return x * lax.rsqrt(ms + NORM_EPS) * g

    cq_ref[0] = rms(acc[:, :q_rank], gq_ref[...]).astype(cq_ref.dtype)
    ckv_ref[0] = rms(acc[:, q_rank:q_rank + kv_rank], gkv_ref[...]).astype(ckv_ref.dtype)
    kr_ref[0] = _rope(acc[:, q_rank + kv_rank:], cos_ref[...], sin_ref[...]).astype(kr_ref.dtype)


def mla_down(x, w_down, gq, gkv, cos, sin, q_rank, kv_rank):
    b, p, k = x.shape
    n = w_down.shape[1]
    tm = 544 if p % 544 == 0 else _row_tile(p)
    row = lambda width: pl.BlockSpec((1, tm, width), lambda b, i: (b, i, 0))
    return pl.pallas_call(
        functools.partial(_mla_down_kernel, q_rank=q_rank, kv_rank=kv_rank),
        grid=(b, p // tm),
        in_specs=[row(k), pl.BlockSpec((k, n), lambda b, i: (0, 0)),
                  pl.BlockSpec((1, q_rank), lambda b, i: (0, 0)),
                  pl.BlockSpec((1, kv_rank), lambda b, i: (0, 0)),
                  pl.BlockSpec((tm, LANES), lambda b, i: (i, 0)),
                  pl.BlockSpec((tm, LANES), lambda b, i: (i, 0))],
        out_specs=[row(q_rank), row(kv_rank), row(LANES)],
        out_shape=[jax.ShapeDtypeStruct((b, p, q_rank), BF16),
                   jax.ShapeDtypeStruct((b, p, kv_rank), BF16),
                   jax.ShapeDtypeStruct((b, p, LANES), BF16)],
        compiler_params=_cparams(("parallel", "parallel")),
        name="mla_down",
    )(x, w_down, gq.reshape(1, q_rank), gkv.reshape(1, kv_rank), cos, sin)


def _mla_kernel(cq_ref, ckv_ref, kr_ref, wq_ref, wkv_ref, cos_ref, sin_ref, o_ref, qcat_ref, kcat_ref, vone_ref, *,
                ctx_len):
    lc = ctx_len
    p_all = vone_ref.shape[0]
    kv = _dot(ckv_ref[0], wkv_ref[...])
    kcat_ref[0:LANES, :] = kv[:, :LANES].T.astype(BF16)
    kcat_ref[LANES:2 * LANES, :] = kr_ref[0].astype(F32).T.astype(BF16)
    vone_ref[:, 0:LANES] = kv[:, LANES:].astype(BF16)
    vone_ref[:, LANES:2 * LANES] = jnp.ones((p_all, LANES), BF16)
    q = _dot(cq_ref[0], wq_ref[...])
    qcat_ref[:, 0:LANES] = q[:, :LANES].astype(BF16)
    qcat_ref[:, LANES:2 * LANES] = _rope(q[:, LANES:], cos_ref[...], sin_ref[...]).astype(BF16)

    def attend(qs, n_q, chunks):
        q = qcat_ref[pl.ds(qs, n_q), :]
        m = jnp.full((n_q, 1), NEG_INF, F32)
        acc = jnp.zeros((n_q, 2 * LANES), F32)
        for c0, c1 in chunks:
            s = _dot(q, kcat_ref[:, c0:c1])
            m_new = jnp.maximum(m, _row_reduce(jnp.maximum, jnp.max, [s]))
            p = jnp.exp2((s - m_new).astype(BF16))
            acc = jnp.exp2(m - m_new) * acc + _dot(p, vone_ref[c0:c1, :])
            m = m_new
        o_ref[0, pl.ds(qs, n_q), :] = (acc[:, :LANES] * (1.0 / acc[:, LANES:LANES + 1])).astype(o_ref.dtype)

    attend(0, lc, [(0, lc)])
    all_chunks = [(0, lc)] + [(c, c + MLA_TK) for c in range(lc, p_all, MLA_TK)]

    def tile(i, carry):
        attend(pl.multiple_of(lc + i * MLA_TQ, MLA_TQ), MLA_TQ, all_chunks)
        return carry

    lax.fori_loop(0, (p_all - lc) // MLA_TQ, tile, 0, unroll=2)


def latent_attention(cq, ckv, kr, w_q, w_kv, cos, sin, ctx_len, n_heads):
    b, p, q_rank = cq.shape
    kv_rank = ckv.shape[2]
    assert (p - ctx_len) % MLA_TQ == 0 and (p - ctx_len) % MLA_TK == 0 and ctx_len % 16 == 0
    per_batch = lambda width: pl.BlockSpec((1, p, width), lambda b, h: (b, 0, 0), pipeline_mode=pl.Buffered(1))
    table = pl.BlockSpec((p, LANES), lambda b, h: (0, 0), pipeline_mode=pl.Buffered(1))
    return pl.pallas_call(
        functools.partial(_mla_kernel, ctx_len=ctx_len),
        grid=(b, n_heads),
        in_specs=[per_batch(q_rank), per_batch(kv_rank), per_batch(LANES),
                  pl.BlockSpec((q_rank, 2 * LANES), lambda b, h: (0, h)),
                  pl.BlockSpec((kv_rank, 2 * LANES), lambda b, h: (0, h)),
                  table, table],
        out_specs=pl.BlockSpec((1, p, LANES), lambda b, h: (b, 0, h)),
        out_shape=jax.ShapeDtypeStruct((b, p, n_heads * LANES), BF16),
        scratch_shapes=[pltpu.VMEM((p, 2 * LANES), BF16), pltpu.VMEM((2 * LANES, p), BF16),
                        pltpu.VMEM((p, 2 * LANES), BF16)],
        compiler_params=_cparams(("parallel", "parallel")),
        name="mla_attention",
    )(cq, ckv, kr, w_q, w_kv, cos, sin)


def gather_rows(src, idx):
    n = idx.shape[0]
    width = src.shape[1]
    win = GATHER_WINDOW
    info = plsc.get_sparse_core_info()
    n_workers = info.num_cores * info.num_subcores
    assert n % (n_workers * 2 * win) == 0
    per_worker = n // n_workers
    mesh = plsc.VectorSubcoreMesh(core_axis_name="core", subcore_axis_name="subcore")

    @functools.partial(
        pl.kernel, out_type=jax.ShapeDtypeStruct((n, width), src.dtype), mesh=mesh,
        scratch_types=[pltpu.VMEM((per_worker,), jnp.int32),
                       pltpu.VMEM((win, width), src.dtype), pltpu.VMEM((win, width), src.dtype),
                       pltpu.SemaphoreType.DMA, pltpu.SemaphoreType.DMA,
                       pltpu.SemaphoreType.DMA, pltpu.SemaphoreType.DMA],
        name="gather_rows")
    def gather(src_hbm, idx_hbm, out_hbm, idx_vmem, rows0, rows1, gsem0, gsem1, osem0, osem1):
        worker = lax.axis_index("subcore") * info.num_cores + lax.axis_index("core")
        base = worker * per_worker
        pltpu.sync_copy(idx_hbm.at[pl.ds(base, per_worker)], idx_vmem)

        def fetch(r, rows, sem):
            return pltpu.make_async_copy(src_hbm.at[idx_vmem.at[pl.ds(r, win)]], rows, sem)

        def flush(r, rows, sem):
            return pltpu.make_async_copy(rows, out_hbm.at[pl.ds(base + r, win)], sem)

        fetch(0, rows0, gsem0).start()

        @pl.loop(0, per_worker, step=2 * win)
        def _(r):
            fetch(r, rows0, gsem0).wait()

            @pl.when(r > 0)
            def _():
                flush(r - win, rows1, osem1).wait()

            fetch(r + win, rows1, gsem1).start()
            flush(r, rows0, osem0).start()
            fetch(r + win, rows1, gsem1).wait()
            flush(r, rows0, osem0).wait()

            @pl.when(r + 2 * win < per_worker)
            def _():
                fetch(r + 2 * win, rows0, gsem0).start()

            flush(r + win, rows1, osem1).start()

        flush(per_worker - win, rows1, osem1).wait()

    return gather(src, idx)


def scatter_rows(src, idx, n_out):
    n = idx.shape[0]
    n_src, width = src.shape
    win = GATHER_WINDOW
    info = plsc.get_sparse_core_info()
    n_workers = info.num_cores * info.num_subcores
    assert n % (n_workers * 2 * win) == 0
    per_worker = n // n_workers
    assert n_src % per_worker == 0
    n_steps = per_worker // win
    mesh = plsc.VectorSubcoreMesh(core_axis_name="core", subcore_axis_name="subcore")

    @functools.partial(
        pl.kernel, out_type=jax.ShapeDtypeStruct((n_out, width), src.dtype), mesh=mesh,
        scratch_types=[pltpu.VMEM((n_steps, win), jnp.int32),
                       pltpu.VMEM((win, width), src.dtype), pltpu.VMEM((win, width), src.dtype),
                       pltpu.SemaphoreType.DMA, pltpu.SemaphoreType.DMA,
                       pltpu.SemaphoreType.DMA, pltpu.SemaphoreType.DMA],
        name="scatter_rows")
    def scatter(src_hbm, idx_hbm, out_hbm, idx_vmem, rows0, rows1, lsem0, lsem1, ssem0, ssem1):
        worker = lax.axis_index("subcore") * info.num_cores + lax.axis_index("core")
        src_base = lax.rem(worker * per_worker, n_src)
        pltpu.sync_copy(idx_hbm.at[worker], idx_vmem)

        def load(j, rows, sem):
            return pltpu.make_async_copy(src_hbm.at[pl.ds(src_base + j * win, win)], rows, sem)

        def store(j, rows, sem):
            return pltpu.make_async_copy(rows, out_hbm.at[idx_vmem.at[j]], sem)

        load(0, rows0, lsem0).start()

        @pl.loop(0, n_steps, step=2)
        def _(j):
            load(j, rows0, lsem0).wait()

            @pl.when(j > 0)
            def _():
                store(j - 1, rows1, ssem1).wait()

            load(j + 1, rows1, lsem1).start()
            store(j, rows0, ssem0).start()
            load(j + 1, rows1, lsem1).wait()
            store(j, rows0, ssem0).wait()

            @pl.when(j + 2 < n_steps)
            def _():
                load(j + 2, rows0, lsem0).start()

            store(j + 1, rows1, ssem1).start()

        store(n_steps - 1, rows1, ssem1).wait()

    return scatter(src, idx.reshape(n_workers, n_steps, win))


def _expert_kernel(blk_e_ref, n_used_ref, n_valid_ref, x_ref, wg_ref, wu_ref, wd_ref, y_ref, wg_bf, wu_bf, wd_bf):
    i = pl.program_id(0)
    used = i < n_used_ref[0]
    new_expert = (i == 0) | (blk_e_ref[i] != blk_e_ref[jnp.maximum(i - 1, 0)])

    @pl.when(used & new_expert)
    def _():
        wg_bf[...] = wg_ref[0].astype(BF16)
        wu_bf[...] = wu_ref[0].astype(BF16)
        wd_bf[...] = wd_ref[0].astype(BF16)

    @pl.when(used)
    def _():
        row = lax.broadcasted_iota(jnp.int32, (x_ref.shape[0], 1), 0)
        x = _unpack_bf16_pairs(jnp.where(row < n_valid_ref[i], x_ref[...], 0)).astype(BF16)
        g = _dot(x, wg_bf[...])
        u = _dot(x, wu_bf[...])
        a = (g * jax.nn.sigmoid(g) * u).astype(BF16)
        y_ref[...] = _pack_bf16_pairs(_dot(a, wd_bf[...]))

    @pl.when(jnp.logical_not(used))
    def _():
        y_ref[...] = jnp.zeros_like(y_ref)


def expert_ffn(x_disp, blk_e, n_used, n_valid, w_gate, w_up, w_down, layer):
    rows, half = x_disp.shape
    d = 2 * half
    n_blk = rows // MOE_BLOCK
    de = w_gate.shape[3]
    grid_spec = pltpu.PrefetchScalarGridSpec(
        num_scalar_prefetch=3,
        grid=(n_blk,),
        in_specs=[pl.BlockSpec((MOE_BLOCK, half), lambda i, be, nu, nv: (i, 0)),
                  pl.BlockSpec((None, 1, d, de), lambda i, be, nu, nv: (layer, be[i], 0, 0)),
                  pl.BlockSpec((None, 1, d, de), lambda i, be, nu, nv: (layer, be[i], 0, 0)),
                  pl.BlockSpec((None, 1, de, d), lambda i, be, nu, nv: (layer, be[i], 0, 0))],
        out_specs=pl.BlockSpec((MOE_BLOCK, half), lambda i, be, nu, nv: (i, 0)),
        scratch_shapes=[pltpu.VMEM((d, de), BF16), pltpu.VMEM((d, de), BF16), pltpu.VMEM((de, d), BF16)],
    )
    return pl.pallas_call(
        _expert_kernel,
        grid_spec=grid_spec,
        out_shape=jax.ShapeDtypeStruct((rows, half), jnp.int32),
        compiler_params=_cparams(("arbitrary",)),
        name="expert_ffn",
    )(blk_e, n_used, n_valid, x_disp, w_gate, w_up, w_down)


def _moe_residual(h_ref, y0_ref, y1_ref, wt_ref, mod_ref, gate_idx):
    wt = wt_ref[0]
    moe = wt[:, 0:1] * _unpack_bf16_pairs(y0_ref[0, 0]) + wt[:, 1:2] * _unpack_bf16_pairs(y1_ref[0, 0])
    return h_ref[0] + mod_ref[0, gate_idx:gate_idx + 1, :] * moe


def _combine_next_kernel(h_ref, y0_ref, y1_ref, wt_ref, mod_ref, g_ref, nmod_ref, o_ref, hm_ref, *, gate_idx):
    h_new = _moe_residual(h_ref, y0_ref, y1_ref, wt_ref, mod_ref, gate_idx)
    o_ref[0] = h_new
    hm_ref[0] = _rms_mod(h_new, g_ref[...], nmod_ref[0, 0:1, :], nmod_ref[0, 1:2, :]).astype(hm_ref.dtype)


def _combine_final_kernel(h_ref, y0_ref, y1_ref, wt_ref, mod_ref, g_ref, o_ref, *, gate_idx):
    x = _moe_residual(h_ref, y0_ref, y1_ref, wt_ref, mod_ref, gate_idx)
    ms = jnp.mean(x * x, axis=-1, keepdims=True)
    o_ref[0] = x * lax.rsqrt(ms + NORM_EPS) * g_ref[...]


def combine(h, y_pairs, wt, modtab, gate_idx, next_g, next_modtab, ctx_len):
    b, p, d = h.shape
    last = next_modtab is None
    skip = ctx_len // ROW_TILE if last else 0
    row = pl.BlockSpec((1, ROW_TILE, d), lambda b, j: (b, j + skip, 0))
    mod = pl.BlockSpec((1, N_MOD, d), lambda b, j: (2 * b + jnp.minimum(j + skip, 1), 0, 0))
    in_specs = [row,
                pl.BlockSpec((1, 1, ROW_TILE, d // 2), lambda b, j: (0, b, j + skip, 0)),
                pl.BlockSpec((1, 1, ROW_TILE, d // 2), lambda b, j: (1, b, j + skip, 0)),
                pl.BlockSpec((1, ROW_TILE, LANES), lambda b, j: (b, j + skip, 0)),
                mod, pl.BlockSpec((1, d), lambda b, j: (0, 0))]
    args = [h, y_pairs, y_pairs, wt, modtab.reshape(b * 2, N_MOD, d), next_g.reshape(1, d)]
    out_row = pl.BlockSpec((1, ROW_TILE, d), lambda b, j: (b, j, 0))
    if last:
        return pl.pallas_call(
            functools.partial(_combine_final_kernel, gate_idx=gate_idx),
            grid=(b, (p - ctx_len) // ROW_TILE),
            in_specs=in_specs,
            out_specs=out_row,
            out_shape=jax.ShapeDtypeStruct((b, p - ctx_len, d), F32),
            compiler_params=_cparams(("parallel", "parallel")),
            name="moe_combine_final",
        )(*args)
    return pl.pallas_call(
        functools.partial(_combine_next_kernel, gate_idx=gate_idx),
        grid=(b, p // ROW_TILE),
        in_specs=in_specs + [mod],
        out_specs=[out_row, out_row],
        out_shape=[jax.ShapeDtypeStruct((b, p, d), F32), jax.ShapeDtypeStruct((b, p, d), BF16)],
        compiler_params=_cparams(("parallel", "parallel")),
        name="moe_combine",
    )(*args, next_modtab.reshape(b * 2, N_MOD, d))


def hier_moe(h, f, route, wt, counts, modtab, gate_idx, w_gate, w_up, w_down, layer, next_g, next_modtab, ctx_len):
    b, p, d = h.shape
    n_tok = b * p
    n_assign = 2 * n_tok
    e1, e2, rank1, rank2 = route.reshape(n_tok, LANES)[:, :4].T
    cnt = counts[0, :N_EXPERTS].astype(jnp.int32)
    pcounts = (cnt + MOE_BLOCK - 1) // MOE_BLOCK * MOE_BLOCK
    pend = jnp.cumsum(pcounts)
    pstart = pend - pcounts
    dest_by_slot = jnp.concatenate([pstart[e1] + rank1, pstart[e2] + rank2]).astype(jnp.int32)
    n_blk = -(-n_assign // MOE_BLOCK) + N_EXPERTS
    rows_total = n_blk * MOE_BLOCK
    blk_row0 = jnp.arange(n_blk, dtype=jnp.int32) * MOE_BLOCK
    blk_e = jnp.minimum(jnp.sum((pend[None, :] <= blk_row0[:, None]).astype(jnp.int32), axis=1), N_EXPERTS - 1)
    n_valid = jnp.clip(cnt[blk_e] - (blk_row0 - pstart[blk_e]), 0, MOE_BLOCK).astype(jnp.int32)
    n_used = (pend[-1] // MOE_BLOCK).astype(jnp.int32).reshape(1)

    x_disp = scatter_rows(f.reshape(n_tok, d // 2), dest_by_slot, rows_total)
    y = expert_ffn(x_disp, blk_e, n_used, n_valid, w_gate, w_up, w_down, layer)
    y_pairs = gather_rows(y, dest_by_slot).reshape(2, b, p, d // 2)
    return combine(h, y_pairs, wt, modtab, gate_idx, next_g, next_modtab, ctx_len)


def kernel(x, c, ctx, c_ctx, mod_w, mod_b, norm_mix_g, norm_ffn_g, router_grp_w, router_grp_b, router_exp_w, router_exp_b, exp_w_gate, exp_w_up, exp_w_down, l0_na_w_qkv, l0_na_rpb, l0_na_w_o, l1_swa_w_qkv, l1_swa_sink, l1_swa_w_o, l2_mla_w_dq, l2_mla_q_norm_g, l2_mla_w_uq, l2_mla_w_dkv, l2_mla_kv_norm_g, l2_mla_w_ukv, l2_mla_w_o, l3_na_w_qkv, l3_na_rpb, l3_na_w_o, final_norm_g):
    b, s, d = x.shape
    lc = ctx.shape[1]
    n_heads = d // HEAD_DIM
    n_kv_heads = n_heads // 4
    depth = mod_w.shape[0]
    rows = s // GRID_W

    modtabs = modulation_tables(c, c_ctx, mod_w, mod_b)

    def scale_q_cols(w):
        n_q = n_heads * HEAD_DIM
        q_scale = HEAD_DIM ** -0.5 * LOG2E
        return jnp.concatenate([w[:, :n_q] * q_scale, w[:, n_q:]], axis=1).astype(BF16)

    def na_mixer(hm, w_qkv, rpb):
        qkv = project(hm, scale_q_cols(w_qkv))
        return neighbourhood_attention(qkv, na_bias_table(rpb, rows), lc, n_heads)

    def swa_mixer(hm):
        cos, sin = rope_tables_full(s, lc)
        n_rope = (n_heads + n_kv_heads) * HEAD_DIM // 512
        qkv = project(hm, scale_q_cols(l1_swa_w_qkv), tn=512, rope=(cos, sin, 0, n_rope))
        return window_attention(qkv, l1_swa_sink, lc, n_heads, n_kv_heads)

    def mla_mixer(hm):
        q_rank = l2_mla_w_dq.shape[1]
        kv_rank = l2_mla_kv_norm_g.shape[0]
        cos, sin = rope_tables_mla(s, lc)
        w_down = jnp.concatenate([l2_mla_w_dq, l2_mla_w_dkv[:, :kv_rank],
                                  _spread_rope_cols(l2_mla_w_dkv[:, kv_rank:])], axis=1).astype(BF16)
        cq, ckv, kr = mla_down(hm, w_down, l2_mla_q_norm_g, l2_mla_kv_norm_g, cos, sin, q_rank, kv_rank)
        w_uq = l2_mla_w_uq.reshape(q_rank, n_heads, MLA_NOPE_DIM + MLA_ROPE_DIM)
        q_scale = (MLA_NOPE_DIM + MLA_ROPE_DIM) ** -0.5 * LOG2E
        half = MLA_ROPE_DIM // 2
        gap = jnp.zeros((q_rank, n_heads, half), l2_mla_w_uq.dtype)
        w_q = jnp.concatenate([w_uq[:, :, :MLA_NOPE_DIM], w_uq[:, :, MLA_NOPE_DIM:MLA_NOPE_DIM + half], gap,
                               w_uq[:, :, MLA_NOPE_DIM + half:], gap], axis=2).reshape(q_rank, n_heads * 2 * LANES)
        return latent_attention(cq, ckv, kr, (w_q * q_scale).astype(BF16), l2_mla_w_ukv.astype(BF16), cos, sin,
                                lc, n_heads)

    h, hm = join_norm_modulate(ctx, x, norm_mix_g[0], modtabs[0])
    for i in range(depth):
        modtab = modtabs[i]
        mixer = i % 3
        if mixer == 0:
            w_qkv, rpb, w_o = (l0_na_w_qkv, l0_na_rpb, l0_na_w_o) if i == 0 else (l3_na_w_qkv, l3_na_rpb, l3_na_w_o)
            y = na_mixer(hm, w_qkv, rpb)
        elif mixer == 1:
            y, w_o = swa_mixer(hm), l1_swa_w_o
        else:
            y, w_o = mla_mixer(hm), l2_mla_w_o
        h, f, route, wt, counts = attn_out_route(y, w_o.astype(BF16), h, modtab, norm_ffn_g[i], router_grp_w[i],
                                                 router_grp_b[i], router_exp_w[i], router_exp_b[i], lc)
        if i + 1 < depth:
            h, hm = hier_moe(h, f, route, wt, counts, modtab, 5, exp_w_gate, exp_w_up, exp_w_down, i,
                             norm_mix_g[i + 1], modtabs[i + 1], lc)
        else:
            return hier_moe(h, f, route, wt, counts, modtab, 5, exp_w_gate, exp_w_up, exp_w_down, i,
                            final_norm_g, None, lc)
```

```python
import functools

import numpy as np
import jax
import jax.numpy as jnp
from jax import lax
from jax.experimental import pallas as pl
from jax.experimental.pallas import tpu as pltpu
from jax.experimental.pallas import tpu_sc as plsc

GRID_W = 64
HEAD_DIM = 128
ROPE_BASE = 10000.0
NORM_EPS = 1e-6
NEG_INF = -1e30
N_MOD = 6

NA_ROWS = 8
NA_COLS = 16
NA_TILE_ROWS = 4
NA_WIN_ROWS = 12

SWA_WINDOW = 128
SWA_TQ = 256
SWA_TK = 512

MLA_NOPE_DIM = 128
MLA_ROPE_DIM = 64
MLA_TQ = 512
MLA_TK = 512
LOG2E = 1.4426950408889634

N_GROUPS = 4
EXPERTS_PER_GROUP = 8
N_EXPERTS = N_GROUPS * EXPERTS_PER_GROUP
MOE_BLOCK = 512
GATHER_WINDOW = 32

LANES = 128
ROW_TILE = 256
VMEM_LIMIT = 56 * 1024 * 1024

BF16 = jnp.bfloat16
F32 = jnp.float32


def _cparams(sem):
    return pltpu.CompilerParams(dimension_semantics=sem, vmem_limit_bytes=VMEM_LIMIT)


def _dot(a, b):
    return jnp.dot(a, b, preferred_element_type=F32)


def _dot_nt(a, b):
    return lax.dot_general(a, b, (((1,), (1,)), ((), ())), preferred_element_type=F32)


def _mod_kernel(x_ref, w_ref, b_ref, o_ref):
    x = x_ref[...]
    sx = (x * jax.nn.sigmoid(x)).astype(BF16)
    o_ref[0] = _dot(sx, w_ref[0].astype(BF16)) + b_ref[0]


def modulation_tables(c, c_ctx, mod_w, mod_b):
    depth, d, n_out = mod_w.shape
    b = c.shape[0]
    rows = 16
    xin = jnp.zeros((rows, d), F32).at[:b].set(c).at[b].set(c_ctx)
    tn = 1024
    out = pl.pallas_call(
        _mod_kernel,
        grid=(depth, n_out // tn),
        in_specs=[pl.BlockSpec((rows, d), lambda i, j: (0, 0)),
                  pl.BlockSpec((1, d, tn), lambda i, j: (i, 0, j)),
                  pl.BlockSpec((1, 1, tn), lambda i, j: (i, 0, j))],
        out_specs=pl.BlockSpec((1, rows, tn), lambda i, j: (i, 0, j)),
        out_shape=jax.ShapeDtypeStruct((depth, rows, n_out), F32),
        compiler_params=_cparams(("parallel", "parallel")),
        name="adaln_mod",
    )(xin, mod_w, mod_b.reshape(depth, 1, n_out))
    lat = out[:, :b].reshape(depth, b, 1, N_MOD, d)
    ctx = jnp.broadcast_to(out[:, b].reshape(depth, 1, 1, N_MOD, d), (depth, b, 1, N_MOD, d))
    return jnp.concatenate([ctx, lat], axis=2)


def _rms_mod(x, g, shift, scale):
    ms = jnp.mean(x * x, axis=-1, keepdims=True)
    y = x * lax.rsqrt(ms + NORM_EPS) * g
    return y * (1.0 + scale) + shift


def _join_norm_mod_kernel(ctx_ref, x_ref, g_ref, mod_ref, h_ref, hm_ref, *, ctx_tiles):
    j = pl.program_id(1)

    def emit(src_ref):
        h_ref[0] = src_ref[0]
        hm_ref[0] = _rms_mod(src_ref[0], g_ref[...], mod_ref[0, 0:1, :], mod_ref[0, 1:2, :]).astype(hm_ref.dtype)

    @pl.when(j < ctx_tiles)
    def _():
        emit(ctx_ref)

    @pl.when(j >= ctx_tiles)
    def _():
        emit(x_ref)


def _route(logits):
    lane = lax.broadcasted_iota(jnp.int32, logits.shape, 1).astype(F32)
    big = float(LANES)

    def first_lane(mask):
        return jnp.min(jnp.where(mask, lane, big), axis=-1, keepdims=True)

    in_grp = lane < N_GROUPS
    lg = jnp.where(in_grp, logits, NEG_INF)
    m_g = jnp.max(lg, axis=-1, keepdims=True)
    g_idx = first_lane(in_grp & (lg == m_g))
    g_w = 1.0 / jnp.sum(jnp.where(in_grp, jnp.exp(lg - m_g), 0.0), axis=-1, keepdims=True)
    e_lo = N_GROUPS + g_idx * EXPERTS_PER_GROUP
    in_e = (lane >= e_lo) & (lane < e_lo + EXPERTS_PER_GROUP)
    le = jnp.where(in_e, logits, NEG_INF)
    m1 = jnp.max(le, axis=-1, keepdims=True)
    e1 = first_lane(in_e & (le == m1))
    s_e = jnp.sum(jnp.where(in_e, jnp.exp(le - m1), 0.0), axis=-1, keepdims=True)
    in_e2 = in_e & (lane != e1)
    le2 = jnp.where(in_e2, logits, NEG_INF)
    m2 = jnp.max(le2, axis=-1, keepdims=True)
    e2 = first_lane(in_e2 & (le2 == m2))
    p1 = 1.0 / s_e
    p2 = jnp.exp(m2 - m1) / s_e
    den = p1 + p2
    return ((e1 - N_GROUPS).astype(jnp.int32), (e2 - N_GROUPS).astype(jnp.int32),
            g_w * p1 / den, g_w * p2 / den)


def _pack_bf16_pairs(x):
    n = x.shape[1] // 2
    xb = x.astype(BF16).astype(F32)
    hi = lax.bitcast_convert_type(xb[:, :n], jnp.int32)
    lo = lax.bitcast_convert_type(xb[:, n:], jnp.int32)
    return (hi & jnp.int32(-65536)) | lax.shift_right_logical(lo, jnp.int32(16))


def _unpack_bf16_pairs(w):
    hi = lax.bitcast_convert_type(w & jnp.int32(-65536), F32)
    lo = lax.bitcast_convert_type(lax.shift_left(w, jnp.int32(16)), F32)
    return jnp.concatenate([hi, lo], axis=1)


def _route_and_rank(f, wr_ref, br_ref, run_ref):
    f_hi = f.astype(BF16)
    f_lo = (f - f_hi.astype(F32)).astype(BF16)
    hi_terms = _dot(f_hi, wr_ref[...])
    logits = (hi_terms[:, :LANES] + _dot(f_lo, wr_ref[:, :LANES]) + hi_terms[:, LANES:]) + br_ref[...]
    e1, e2, w1, w2 = _route(logits)
    lane = lax.broadcasted_iota(jnp.int32, logits.shape, 1)
    pick1, pick2 = lane == e1, lane == e2
    chosen = (pick1 | pick2).astype(F32)
    n_rows = chosen.shape[0]
    earlier = (lax.broadcasted_iota(jnp.int32, (n_rows, n_rows), 1)
               < lax.broadcasted_iota(jnp.int32, (n_rows, n_rows), 0)).astype(BF16)
    before = run_ref[...] + _dot(earlier, chosen.astype(BF16))
    rank1 = jnp.sum(jnp.where(pick1, before, 0.0), axis=-1, keepdims=True).astype(jnp.int32)
    rank2 = jnp.sum(jnp.where(pick2, before, 0.0), axis=-1, keepdims=True).astype(jnp.int32)
    run_ref[...] = run_ref[...] + jnp.sum(chosen, axis=0, keepdims=True)
    route = jnp.where(lane == 0, e1, jnp.where(lane == 1, e2, jnp.where(lane == 2, rank1,
                                                                         jnp.where(lane == 3, rank2, 0))))
    return route, jnp.where(lane == 0, w1, jnp.where(lane == 1, w2, 0.0))


def _attn_out_route_kernel(y_ref, w_ref, h_ref, mod_ref, g_ref, wr_ref, br_ref,
                           ho_ref, f_ref, route_ref, wt_ref, cnt_ref, run_ref, *, ctx_len, tm, n_sub):
    first = (pl.program_id(0) == 0) & (pl.program_id(1) == 0)

    @pl.when(first)
    def _():
        run_ref[...] = jnp.zeros_like(run_ref)

    sub = tm // n_sub
    for s in range(n_sub):
        rows = slice(s * sub, (s + 1) * sub)
        acc = _dot(y_ref[0, rows, :], w_ref[...])
        pos = pl.program_id(1) * tm + s * sub + lax.broadcasted_iota(jnp.int32, (sub, 1), 0)
        is_ctx = pos < ctx_len

        def mod_row(k):
            return jnp.where(is_ctx, mod_ref[0, 0, k:k + 1, :], mod_ref[0, 1, k:k + 1, :])

        h_new = h_ref[0, rows, :] + mod_row(2) * acc
        ho_ref[0, rows, :] = h_new
        f = _rms_mod(h_new, g_ref[...], mod_row(3), mod_row(4))
        f_ref[0, rows, :] = _pack_bf16_pairs(f)
        route, wt = _route_and_rank(f, wr_ref, br_ref, run_ref)
        route_ref[0, rows, :] = route
        wt_ref[0, rows, :] = wt
    cnt_ref[...] = run_ref[...]


def _mod_spec(d):
    return pl.BlockSpec((1, N_MOD, d), lambda b, j: (2 * b + jnp.minimum(j, 1), 0, 0))


def join_norm_modulate(ctx, x, g, modtab):
    b, lc, d = ctx.shape
    p = lc + x.shape[1]
    assert lc == ROW_TILE
    ctx_tiles = lc // ROW_TILE
    row = pl.BlockSpec((1, ROW_TILE, d), lambda b, j: (b, j, 0))
    return pl.pallas_call(
        functools.partial(_join_norm_mod_kernel, ctx_tiles=ctx_tiles),
        grid=(b, p // ROW_TILE),
        in_specs=[pl.BlockSpec((1, ROW_TILE, d), lambda b, j: (b, jnp.minimum(j, ctx_tiles - 1), 0)),
                  pl.BlockSpec((1, ROW_TILE, d), lambda b, j: (b, jnp.maximum(j - ctx_tiles, 0), 0)),
                  pl.BlockSpec((1, d), lambda b, j: (0, 0)), _mod_spec(d)],
        out_specs=[row, row],
        out_shape=[jax.ShapeDtypeStruct((b, p, d), F32), jax.ShapeDtypeStruct((b, p, d), BF16)],
        compiler_params=_cparams(("parallel", "parallel")),
        name="join_norm_mod",
    )(ctx, x, g.reshape(1, d), modtab.reshape(b * 2, N_MOD, d))


def attn_out_route(y, w_o, h, modtab, g, w_grp, b_grp, w_rt, b_rt, ctx_len, n_sub=2):
    b, p, d = h.shape
    n_r = N_GROUPS + N_EXPERTS
    wr = jnp.zeros((d, LANES), F32).at[:, :N_GROUPS].set(w_grp).at[:, N_GROUPS:n_r].set(w_rt)
    br = jnp.zeros((1, LANES), F32).at[0, :N_GROUPS].set(b_grp).at[0, N_GROUPS:n_r].set(b_rt)
    wr_hi = wr.astype(BF16)
    wr = jnp.concatenate([wr_hi, (wr - wr_hi.astype(F32)).astype(BF16)], axis=1)
    tm = 544 if p % 544 == 0 else ROW_TILE
    assert (tm // n_sub) % 16 == 0
    row = lambda width: pl.BlockSpec((1, tm, width), lambda b, i: (b, i, 0))
    once = lambda shape: pl.BlockSpec(shape, lambda b, i: (0,) * len(shape), pipeline_mode=pl.Buffered(1))
    return pl.pallas_call(
        functools.partial(_attn_out_route_kernel, ctx_len=ctx_len, tm=tm, n_sub=n_sub),
        grid=(b, p // tm),
        in_specs=[row(d), once((d, d)), row(d),
                  pl.BlockSpec((1, 2, N_MOD, d), lambda b, i: (b, 0, 0, 0)),
                  once((1, d)), once((d, 2 * LANES)), once((1, LANES))],
        out_specs=[row(d), row(d // 2), row(LANES), row(LANES), pl.BlockSpec((1, LANES), lambda b, i: (0, 0))],
        out_shape=[jax.ShapeDtypeStruct((b, p, d), F32),
                   jax.ShapeDtypeStruct((b, p, d // 2), jnp.int32),
                   jax.ShapeDtypeStruct((b, p, LANES), jnp.int32),
                   jax.ShapeDtypeStruct((b, p, LANES), F32),
                   jax.ShapeDtypeStruct((1, LANES), F32)],
        scratch_shapes=[pltpu.VMEM((1, LANES), F32)],
        compiler_params=_cparams(("arbitrary", "arbitrary")),
        name="attn_out_route",
    )(y, w_o, h, modtab, g.reshape(1, d), wr, br)


def _rope(acc, cos, sin):
    n_blk = acc.shape[1] // LANES
    outs = []
    for c in range(n_blk):
        x = acc[:, c * LANES:(c + 1) * LANES]
        outs.append(x * cos + pltpu.roll(x, LANES // 2, 1) * sin)
    return outs[0] if n_blk == 1 else jnp.concatenate(outs, axis=1)


def _proj_kernel(*refs, rope_lo, rope_hi):
    x_ref, w_ref = refs[0], refs[1]
    o_ref = refs[-1]
    acc = _dot(x_ref[0], w_ref[...])
    if rope_hi > rope_lo:
        cos_ref, sin_ref = refs[2], refs[3]
        j = pl.program_id(2)
        roped = (j >= rope_lo) & (j < rope_hi)

        @pl.when(roped)
        def _():
            o_ref[0] = _rope(acc, cos_ref[...], sin_ref[...]).astype(o_ref.dtype)

        @pl.when(jnp.logical_not(roped))
        def _():
            o_ref[0] = acc.astype(o_ref.dtype)
    else:
        o_ref[0] = acc.astype(o_ref.dtype)


def _row_tile(p):
    for cand in (1088, 1024, 544, 512, 272, 256, 128, 64, 32, 16):
        if p % cand == 0:
            return cand
    raise ValueError(p)


def project(x, w, *, tn=512, out_dtype=BF16, rope=None):
    b, p, k = x.shape
    n = w.shape[1]
    tm = _row_tile(p)
    tn = min(tn, n)
    assert n % tn == 0
    in_specs = [pl.BlockSpec((1, tm, k), lambda b, i, j: (b, i, 0)),
                pl.BlockSpec((k, tn), lambda b, i, j: (0, j))]
    args = [x, w]
    kw = dict(rope_lo=0, rope_hi=0)
    if rope is not None:
        cos, sin, lo, hi = rope
        in_specs += [pl.BlockSpec((tm, LANES), lambda b, i, j: (i, 0)),
                     pl.BlockSpec((tm, LANES), lambda b, i, j: (i, 0))]
        args += [cos, sin]
        kw.update(rope_lo=lo, rope_hi=hi)
    return pl.pallas_call(
        functools.partial(_proj_kernel, **kw),
        grid=(b, p // tm, n // tn),
        in_specs=in_specs,
        out_specs=pl.BlockSpec((1, tm, tn), lambda b, i, j: (b, i, j)),
        out_shape=jax.ShapeDtypeStruct((b, p, n), out_dtype),
        compiler_params=_cparams(("parallel", "parallel", "arbitrary")),
        name="project",
    )(*args)


def _axial_cos_sin(n, rot_dim):
    t = jnp.arange(n, dtype=jnp.int32)
    row = (t // GRID_W).astype(F32)
    col = (t % GRID_W).astype(F32)
    n_freq = rot_dim // 4
    inv = ROPE_BASE ** (-jnp.arange(n_freq, dtype=F32) / n_freq)
    ang = jnp.concatenate([row[:, None] * inv, col[:, None] * inv], axis=-1)
    return jnp.cos(ang), jnp.sin(ang)


def rope_tables_full(s, ctx_len):
    c, sn = _axial_cos_sin(s, HEAD_DIM)
    cos = jnp.concatenate([c, c], axis=1)
    sin = jnp.concatenate([-sn, sn], axis=1)
    ident_c = jnp.ones((ctx_len, LANES), F32)
    ident_s = jnp.zeros((ctx_len, LANES), F32)
    return jnp.concatenate([ident_c, cos], axis=0), jnp.concatenate([ident_s, sin], axis=0)


def rope_tables_mla(s, ctx_len):
    c, sn = _axial_cos_sin(s, MLA_ROPE_DIM)
    one = jnp.ones_like(c)
    zero = jnp.zeros_like(c)
    cos = jnp.concatenate([c, one, c, one], axis=1)
    sin = jnp.concatenate([-sn, zero, sn, zero], axis=1)
    ident_c = jnp.ones((ctx_len, LANES), F32)
    ident_s = jnp.zeros((ctx_len, LANES), F32)
    return jnp.concatenate([ident_c, cos], axis=0), jnp.concatenate([ident_s, sin], axis=0)


def _spread_rope_cols(w_rope):
    k = w_rope.shape[0]
    half = MLA_ROPE_DIM // 2
    z = jnp.zeros((k, half), w_rope.dtype)
    return jnp.concatenate([w_rope[:, :half], z, w_rope[:, half:], z], axis=1)


def _softmax_probs(parts, extra=None):
    m = _row_reduce(jnp.maximum, jnp.max, parts)
    if extra is not None:
        m = jnp.maximum(m, extra)
    return [jnp.exp2((s - m).astype(BF16)) for s in parts], m


def _with_ones(v_ref, vone_ref):
    vone_ref[:, 0:LANES] = v_ref[0]
    vone_ref[:, LANES:2 * LANES] = jnp.ones((vone_ref.shape[0], LANES), BF16)


def _normalise(acc, extra_den=None):
    den = acc[:, LANES:LANES + 1]
    if extra_den is not None:
        den = den + extra_den
    return acc[:, :LANES] * (1.0 / den)


def _row_reduce(combine, reduce, parts):
    blocks = [s[:, c:c + LANES] for s in parts for c in range(0, s.shape[1], LANES)]
    acc = blocks[0]
    for blk in blocks[1:]:
        acc = combine(acc, blk)
    return reduce(acc, axis=-1, keepdims=True)


def na_bias_table(rpb, rows):
    n_tiles = rows // NA_TILE_ROWS
    n_heads, _, n_dcol = rpb.shape
    drow, row_ok = [], []
    for tile in (0, 1, n_tiles - 1):
        kr0 = int(np.clip(NA_TILE_ROWS * tile - NA_ROWS // 2, 0, rows - NA_WIN_ROWS))
        r = NA_TILE_ROWS * tile + np.arange(NA_TILE_ROWS)
        r0 = np.clip(r - NA_ROWS // 2, 0, rows - NA_ROWS)
        krow = kr0 + np.arange(NA_WIN_ROWS)
        row_ok.append((krow[None, :] >= r0[:, None]) & (krow[None, :] < r0[:, None] + NA_ROWS))
        drow.append(np.clip(krow[None, :] - r[:, None] + NA_ROWS - 1, 0, 2 * NA_ROWS - 2))
    drow, row_ok = np.stack(drow), np.stack(row_ok)
    qc = np.arange(GRID_W)
    qcol0 = np.clip(qc - NA_COLS // 2, 0, GRID_W - NA_COLS)
    kc = np.arange(GRID_W)
    col_ok = (kc[None, :] >= qcol0[:, None]) & (kc[None, :] < qcol0[:, None] + NA_COLS)
    dcol = np.clip(kc[None, :] - qc[:, None] + NA_COLS - 1, 0, 2 * NA_COLS - 2)
    pick_col = jnp.asarray(dcol[None] == np.arange(n_dcol)[:, None, None], F32)
    slabs = jnp.einsum('hrd,dqk->hrqk', rpb.astype(F32), pick_col, precision=lax.Precision.HIGHEST)
    slabs = jnp.where(jnp.asarray(col_ok)[None, None], slabs * LOG2E, NEG_INF)
    masked = jnp.full((n_heads, GRID_W, GRID_W), NEG_INF, F32)
    pats = []
    for p in range(3):
        per_row = [jnp.stack([slabs[:, drow[p, a, m]] if row_ok[p, a, m] else masked
                              for m in range(NA_WIN_ROWS)], axis=2)
                   for a in range(NA_TILE_ROWS)]
        pats.append(jnp.stack(per_row, axis=1))
    vals = jnp.stack(pats, axis=1)
    return vals.reshape(n_heads, 3, NA_TILE_ROWS * GRID_W, NA_WIN_ROWS * GRID_W)


def _na_kernel(q_ref, k_ref, v_ref, bias_ref, o_ref, vone_ref, *, ctx_len, rows, unroll):
    tq = NA_TILE_ROWS * GRID_W
    tk = NA_WIN_ROWS * GRID_W
    n_tiles = rows // NA_TILE_ROWS
    lc = ctx_len
    _with_ones(v_ref, vone_ref)

    s = _dot_nt(q_ref[0, 0:lc, :], k_ref[0, 0:lc, :])
    (p,), _ = _softmax_probs([s])
    o_ref[0, 0:lc, :] = _normalise(_dot(p, vone_ref[0:lc, :])).astype(o_ref.dtype)

    def tile(i, carry):
        qs = pl.multiple_of(lc + i * tq, tq)
        kr0 = jnp.clip(NA_TILE_ROWS * i - NA_ROWS // 2, 0, rows - NA_WIN_ROWS)
        ks = pl.multiple_of(lc + kr0 * GRID_W, NA_TILE_ROWS * GRID_W)
        pat = jnp.where(i == 0, 0, jnp.where(i == n_tiles - 1, 2, 1))
        q = q_ref[0, pl.ds(qs, tq), :]
        s_loc = _dot_nt(q, k_ref[0, pl.ds(ks, tk), :]) + bias_ref[0, pat]
        s_ctx = _dot_nt(q, k_ref[0, 0:lc, :])
        (p_loc, p_ctx), _ = _softmax_probs([s_loc, s_ctx])
        acc = _dot(p_loc, vone_ref[pl.ds(ks, tk), :]) + _dot(p_ctx, vone_ref[0:lc, :])
        o_ref[0, pl.ds(qs, tq), :] = _normalise(acc).astype(o_ref.dtype)
        return carry

    lax.fori_loop(0, n_tiles, tile, 0, unroll=unroll)


def neighbourhood_attention(qkv, bias, ctx_len, n_heads, unroll=4):
    b, p, _ = qkv.shape
    rows = (p - ctx_len) // GRID_W
    assert rows % NA_TILE_ROWS == 0 and rows >= NA_WIN_ROWS
    tq, tk = NA_TILE_ROWS * GRID_W, NA_WIN_ROWS * GRID_W
    assert ctx_len % 16 == 0 and ctx_len % tq == 0
    blk = lambda off: pl.BlockSpec((1, p, HEAD_DIM), lambda h, b: (b, 0, off + h))
    return pl.pallas_call(
        functools.partial(_na_kernel, ctx_len=ctx_len, rows=rows, unroll=unroll),
        grid=(n_heads, b),
        in_specs=[blk(0), blk(n_heads), blk(2 * n_heads),
                  pl.BlockSpec((1, 3, tq, tk), lambda h, b: (h, 0, 0, 0))],
        out_specs=blk(0),
        out_shape=jax.ShapeDtypeStruct((b, p, n_heads * HEAD_DIM), BF16),
        scratch_shapes=[pltpu.VMEM((p, 2 * LANES), BF16)],
        compiler_params=_cparams(("parallel", "parallel")),
        name="na_attention",
    )(qkv, qkv, qkv, bias)


def _swa_kernel(sink_ref, q_ref, k_ref, v_ref, o_ref, vone_ref, *, ctx_len, seq, group):
    lc = ctx_len
    kvh = pl.program_id(1)
    _with_ones(v_ref, vone_ref)

    def stack_heads(q):
        return jnp.concatenate([q[:, g * HEAD_DIM:(g + 1) * HEAD_DIM] for g in range(group)], axis=0)

    def finish(parts_fn, n_q, pv_fn, store):
        ps_all, sink_terms = [], []
        for g in range(group):
            sink = sink_ref[kvh * group + g]
            ps, m = _softmax_probs(parts_fn(g), extra=sink)
            ps_all.append(ps)
            sink_terms.append(jnp.exp2(sink - m))
        n_parts = len(ps_all[0])
        stacked = [jnp.concatenate([ps_all[g][k] for g in range(group)], axis=0) for k in range(n_parts)]
        acc = pv_fn(stacked)
        store(jnp.concatenate([_normalise(acc[g * n_q:(g + 1) * n_q], sink_terms[g]) for g in range(group)], axis=1))

    s_c = _dot_nt(stack_heads(q_ref[0, 0:lc, :]), k_ref[0, 0:lc, :])

    def store_ctx(o):
        o_ref[0, 0:lc, :] = o.astype(o_ref.dtype)

    finish(lambda g: [s_c[g * lc:(g + 1) * lc]], lc,
           lambda st: _dot(st[0], vone_ref[0:lc, :]), store_ctx)

    n_tiles = seq // SWA_TQ

    def tile(t, carry):
        q0 = t * SWA_TQ
        k0 = jnp.clip(q0 - SWA_WINDOW, 0, seq - SWA_TK)
        qs = pl.multiple_of(lc + q0, SWA_WINDOW)
        ks = pl.multiple_of(lc + k0, SWA_WINDOW)
        q4 = stack_heads(q_ref[0, pl.ds(qs, SWA_TQ), :])
        s_loc = _dot_nt(q4, k_ref[0, pl.ds(ks, SWA_TK), :])
        s_ctx = _dot_nt(q4, k_ref[0, 0:lc, :])
        dpos = (lax.broadcasted_iota(jnp.int32, (SWA_TQ, SWA_TK), 1)
                - lax.broadcasted_iota(jnp.int32, (SWA_TQ, SWA_TK), 0)) + (k0 - q0)
        valid = jnp.abs(dpos) <= SWA_WINDOW

        def parts(g):
            sl = slice(g * SWA_TQ, (g + 1) * SWA_TQ)
            return [jnp.where(valid, s_loc[sl], NEG_INF), s_ctx[sl]]

        def store(o):
            o_ref[0, pl.ds(qs, SWA_TQ), :] = o.astype(o_ref.dtype)

        finish(parts, SWA_TQ,
               lambda st: _dot(st[0], vone_ref[pl.ds(ks, SWA_TK), :]) + _dot(st[1], vone_ref[0:lc, :]),
               store)
        return carry

    lax.fori_loop(0, n_tiles, tile, 0, unroll=4)


def window_attention(qkv, sink, ctx_len, n_heads, n_kv_heads):
    b, p, _ = qkv.shape
    seq = p - ctx_len
    group = n_heads // n_kv_heads
    assert seq % SWA_TQ == 0 and seq >= SWA_TK and ctx_len % SWA_WINDOW == 0
    kv = lambda off: pl.BlockSpec((1, p, HEAD_DIM), lambda b, h: (b, 0, off + h))
    qo = pl.BlockSpec((1, p, group * HEAD_DIM), lambda b, h: (b, 0, h))
    return pl.pallas_call(
        functools.partial(_swa_kernel, ctx_len=ctx_len, seq=seq, group=group),
        grid=(b, n_kv_heads),
        in_specs=[pl.BlockSpec(memory_space=pltpu.SMEM), qo, kv(n_heads), kv(n_heads + n_kv_heads)],
        out_specs=qo,
        out_shape=jax.ShapeDtypeStruct((b, p, n_heads * HEAD_DIM), BF16),
        scratch_shapes=[pltpu.VMEM((p, 2 * LANES), BF16)],
        compiler_params=_cparams(("parallel", "parallel")),
        name="swa_attention",
    )(sink.astype(F32) * LOG2E, qkv, qkv, qkv)


def _mla_down_kernel(x_ref, w_ref, gq_ref, gkv_ref, cos_ref, sin_ref, cq_ref, ckv_ref, kr_ref, *, q_rank, kv_rank):
    acc = _dot(x_ref[0], w_ref[...])

    def rms(x, g):
        ms = jnp.mean(x * x, axis=-1, keepdims=True)
        return x * lax.rsqrt(ms + NORM_EPS) * g

    cq_ref[0] = rms(acc[:, :q_rank], gq_ref[...]).astype(cq_ref.dtype)
    ckv_ref[0] = rms(acc[:, q_rank:q_rank + kv_rank], gkv_ref[...]).astype(ckv_ref.dtype)
    kr_ref[0] = _rope(acc[:, q_rank + kv_rank:], cos_ref[...], sin_ref[...]).astype(kr_ref.dtype)


def mla_down(x, w_down, gq, gkv, cos, sin, q_rank, kv_rank):
    b, p, k = x.shape
    n = w_down.shape[1]
    tm = 544 if p % 544 == 0 else _row_tile(p)
    row = lambda width: pl.BlockSpec((1, tm, width), lambda b, i: (b, i, 0))
    return pl.pallas_call(
        functools.partial(_mla_down_kernel, q_rank=q_rank, kv_rank=kv_rank),
        grid=(b, p // tm),
        in_specs=[row(k), pl.BlockSpec((k, n), lambda b, i: (0, 0)),
                  pl.BlockSpec((1, q_rank), lambda b, i: (0, 0)),
                  pl.BlockSpec((1, kv_rank), lambda b, i: (0, 0)),
                  pl.BlockSpec((tm, LANES), lambda b, i: (i, 0)),
                  pl.BlockSpec((tm, LANES), lambda b, i: (i, 0))],
        out_specs=[row(q_rank), row(kv_rank), row(LANES)],
        out_shape=[jax.ShapeDtypeStruct((b, p, q_rank), BF16),
                   jax.ShapeDtypeStruct((b, p, kv_rank), BF16),
                   jax.ShapeDtypeStruct((b, p, LANES), BF16)],
        compiler_params=_cparams(("parallel", "parallel")),
        name="mla_down",
    )(x, w_down, gq.reshape(1, q_rank), gkv.reshape(1, kv_rank), cos, sin)


def _mla_kernel(cq_ref, ckv_ref, kr_ref, wq_ref, wkv_ref, cos_ref, sin_ref, o_ref, qcat_ref, kcat_ref, vone_ref, *,
                ctx_len):
    lc = ctx_len
    p_all = vone_ref.shape[0]
    kv = _dot(ckv_ref[0], wkv_ref[...])
    kcat_ref[0:LANES, :] = kv[:, :LANES].T.astype(BF16)
    kcat_ref[LANES:2 * LANES, :] = kr_ref[0].astype(F32).T.astype(BF16)
    vone_ref[:, 0:LANES] = kv[:, LANES:].astype(BF16)
    vone_ref[:, LANES:2 * LANES] = jnp.ones((p_all, LANES), BF16)
    q = _dot(cq_ref[0], wq_ref[...])
    qcat_ref[:, 0:LANES] = q[:, :LANES].astype(BF16)
    qcat_ref[:, LANES:2 * LANES] = _rope(q[:, LANES:], cos_ref[...], sin_ref[...]).astype(BF16)

    def attend(qs, n_q, chunks):
        q = qcat_ref[pl.ds(qs, n_q), :]
        m = jnp.full((n_q, 1), NEG_INF, F32)
        acc = jnp.zeros((n_q, 2 * LANES), F32)
        for c0, c1 in chunks:
            s = _dot(q, kcat_ref[:, c0:c1])
            m_new = jnp.maximum(m, _row_reduce(jnp.maximum, jnp.max, [s]))
            p = jnp.exp2((s - m_new).astype(BF16))
            acc = jnp.exp2(m - m_new) * acc + _dot(p, vone_ref[c0:c1, :])
            m = m_new
        o_ref[0, pl.ds(qs, n_q), :] = (acc[:, :LANES] * (1.0 / acc[:, LANES:LANES + 1])).astype(o_ref.dtype)

    attend(0, lc, [(0, lc)])
    all_chunks = [(0, lc)] + [(c, c + MLA_TK) for c in range(lc, p_all, MLA_TK)]

    def tile(i, carry):
        attend(pl.multiple_of(lc + i * MLA_TQ, MLA_TQ), MLA_TQ, all_chunks)
        return carry

    lax.fori_loop(0, (p_all - lc) // MLA_TQ, tile, 0, unroll=2)


def latent_attention(cq, ckv, kr, w_q, w_kv, cos, sin, ctx_len, n_heads):
    b, p, q_rank = cq.shape
    kv_rank = ckv.shape[2]
    assert (p - ctx_len) % MLA_TQ == 0 and (p - ctx_len) % MLA_TK == 0 and ctx_len % 16 == 0
    per_batch = lambda width: pl.BlockSpec((1, p, width), lambda b, h: (b, 0, 0), pipeline_mode=pl.Buffered(1))
    table = pl.BlockSpec((p, LANES), lambda b, h: (0, 0), pipeline_mode=pl.Buffered(1))
    return pl.pallas_call(
        functools.partial(_mla_kernel, ctx_len=ctx_len),
        grid=(b, n_heads),
        in_specs=[per_batch(q_rank), per_batch(kv_rank), per_batch(LANES),
                  pl.BlockSpec((q_rank, 2 * LANES), lambda b, h: (0, h)),
                  pl.BlockSpec((kv_rank, 2 * LANES), lambda b, h: (0, h)),
                  table, table],
        out_specs=pl.BlockSpec((1, p, LANES), lambda b, h: (b, 0, h)),
        out_shape=jax.ShapeDtypeStruct((b, p, n_heads * LANES), BF16),
        scratch_shapes=[pltpu.VMEM((p, 2 * LANES), BF16), pltpu.VMEM((2 * LANES, p), BF16),
                        pltpu.VMEM((p, 2 * LANES), BF16)],
        compiler_params=_cparams(("parallel", "parallel")),
        name="mla_attention",
    )(cq, ckv, kr, w_q, w_kv, cos, sin)


def gather_rows(src, idx):
    n = idx.shape[0]
    width = src.shape[1]
    win = GATHER_WINDOW
    info = plsc.get_sparse_core_info()
    n_workers = info.num_cores * info.num_subcores
    assert n % (n_workers * 2 * win) == 0
    per_worker = n // n_workers
    mesh = plsc.VectorSubcoreMesh(core_axis_name="core", subcore_axis_name="subcore")

    @functools.partial(
        pl.kernel, out_type=jax.ShapeDtypeStruct((n, width), src.dtype), mesh=mesh,
        scratch_types=[pltpu.VMEM((per_worker,), jnp.int32),
                       pltpu.VMEM((win, width), src.dtype), pltpu.VMEM((win, width), src.dtype),
                       pltpu.SemaphoreType.DMA, pltpu.SemaphoreType.DMA,
                       pltpu.SemaphoreType.DMA, pltpu.SemaphoreType.DMA],
        name="gather_rows")
    def gather(src_hbm, idx_hbm, out_hbm, idx_vmem, rows0, rows1, gsem0, gsem1, osem0, osem1):
        worker = lax.axis_index("subcore") * info.num_cores + lax.axis_index("core")
        base = worker * per_worker
        pltpu.sync_copy(idx_hbm.at[pl.ds(base, per_worker)], idx_vmem)

        def fetch(r, rows, sem):
            return pltpu.make_async_copy(src_hbm.at[idx_vmem.at[pl.ds(r, win)]], rows, sem)

        def flush(r, rows, sem):
            return pltpu.make_async_copy(rows, out_hbm.at[pl.ds(base + r, win)], sem)

        fetch(0, rows0, gsem0).start()

        @pl.loop(0, per_worker, step=2 * win)
        def _(r):
            fetch(r, rows0, gsem0).wait()

            @pl.when(r > 0)
            def _():
                flush(r - win, rows1, osem1).wait()

            fetch(r + win, rows1, gsem1).start()
            flush(r, rows0, osem0).start()
            fetch(r + win, rows1, gsem1).wait()
            flush(r, rows0, osem0).wait()

            @pl.when(r + 2 * win < per_worker)
            def _():
                fetch(r + 2 * win, rows0, gsem0).start()

            flush(r + win, rows1, osem1).start()

        flush(per_worker - win, rows1, osem1).wait()

    return gather(src, idx)


def scatter_rows(src, idx, n_out):
    n = idx.shape[0]
    n_src, width = src.shape
    win = GATHER_WINDOW
    info = plsc.get_sparse_core_info()
    n_workers = info.num_cores * info.num_subcores
    assert n % (n_workers * 2 * win) == 0
    per_worker = n // n_workers
    assert n_src % per_worker == 0
    n_steps = per_worker // win
    mesh = plsc.VectorSubcoreMesh(core_axis_name="core", subcore_axis_name="subcore")

    @functools.partial(
        pl.kernel, out_type=jax.ShapeDtypeStruct((n_out, width), src.dtype), mesh=mesh,
        scratch_types=[pltpu.VMEM((n_steps, win), jnp.int32),
                       pltpu.VMEM((win, width), src.dtype), pltpu.VMEM((win, width), src.dtype),
                       pltpu.SemaphoreType.DMA, pltpu.SemaphoreType.DMA,
                       pltpu.SemaphoreType.DMA, pltpu.SemaphoreType.DMA],
        name="scatter_rows")
    def scatter(src_hbm, idx_hbm, out_hbm, idx_vmem, rows0, rows1, lsem0, lsem1, ssem0, ssem1):
        worker = lax.axis_index("subcore") * info.num_cores + lax.axis_index("core")
        src_base = lax.rem(worker * per_worker, n_src)
        pltpu.sync_copy(idx_hbm.at[worker], idx_vmem)

        def load(j, rows, sem):
            return pltpu.make_async_copy(src_hbm.at[pl.ds(src_base + j * win, win)], rows, sem)

        def store(j, rows, sem):
            return pltpu.make_async_copy(rows, out_hbm.at[idx_vmem.at[j]], sem)

        load(0, rows0, lsem0).start()

        @pl.loop(0, n_steps, step=2)
        def _(j):
            load(j, rows0, lsem0).wait()

            @pl.when(j > 0)
            def _():
                store(j - 1, rows1, ssem1).wait()

            load(j + 1, rows1, lsem1).start()
            store(j, rows0, ssem0).start()
            load(j + 1, rows1, lsem1).wait()
            store(j, rows0, ssem0).wait()

            @pl.when(j + 2 < n_steps)
            def _():
                load(j + 2, rows0, lsem0).start()

            store(j + 1, rows1, ssem1).start()

        store(n_steps - 1, rows1, ssem1).wait()

    return scatter(src, idx.reshape(n_workers, n_steps, win))


def _expert_kernel(blk_e_ref, n_used_ref, n_valid_ref, x_ref, wg_ref, wu_ref, wd_ref, y_ref, wg_bf, wu_bf, wd_bf):
    i = pl.program_id(0)
    used = i < n_used_ref[0]
    new_expert = (i == 0) | (blk_e_ref[i] != blk_e_ref[jnp.maximum(i - 1, 0)])

    @pl.when(used & new_expert)
    def _():
        wg_bf[...] = wg_ref[0].astype(BF16)
        wu_bf[...] = wu_ref[0].astype(BF16)
        wd_bf[...] = wd_ref[0].astype(BF16)

    @pl.when(used)
    def _():
        row = lax.broadcasted_iota(jnp.int32, (x_ref.shape[0], 1), 0)
        x = _unpack_bf16_pairs(jnp.where(row < n_valid_ref[i], x_ref[...], 0)).astype(BF16)
        g = _dot(x, wg_bf[...])
        u = _dot(x, wu_bf[...])
        a = (g * jax.nn.sigmoid(g) * u).astype(BF16)
        y_ref[...] = _pack_bf16_pairs(_dot(a, wd_bf[...]))

    @pl.when(jnp.logical_not(used))
    def _():
        y_ref[...] = jnp.zeros_like(y_ref)


def expert_ffn(x_disp, blk_e, n_used, n_valid, w_gate, w_up, w_down, layer):
    rows, half = x_disp.shape
    d = 2 * half
    n_blk = rows // MOE_BLOCK
    de = w_gate.shape[3]
    grid_spec = pltpu.PrefetchScalarGridSpec(
        num_scalar_prefetch=3,
        grid=(n_blk,),
        in_specs=[pl.BlockSpec((MOE_BLOCK, half), lambda i, be, nu, nv: (i, 0)),
                  pl.BlockSpec((None, 1, d, de), lambda i, be, nu, nv: (layer, be[i], 0, 0)),
                  pl.BlockSpec((None, 1, d, de), lambda i, be, nu, nv: (layer, be[i], 0, 0)),
                  pl.BlockSpec((None, 1, de, d), lambda i, be, nu, nv: (layer, be[i], 0, 0))],
        out_specs=pl.BlockSpec((MOE_BLOCK, half), lambda i, be, nu, nv: (i, 0)),
        scratch_shapes=[pltpu.VMEM((d, de), BF16), pltpu.VMEM((d, de), BF16), pltpu.VMEM((de, d), BF16)],
    )
    return pl.pallas_call(
        _expert_kernel,
        grid_spec=grid_spec,
        out_shape=jax.ShapeDtypeStruct((rows, half), jnp.int32),
        compiler_params=_cparams(("arbitrary",)),
        name="expert_ffn",
    )(blk_e, n_used, n_valid, x_disp, w_gate, w_up, w_down)


def _moe_residual(h_ref, y0_ref, y1_ref, wt_ref, mod_ref, gate_idx):
    wt = wt_ref[0]
    moe = wt[:, 0:1] * _unpack_bf16_pairs(y0_ref[0, 0]) + wt[:, 1:2] * _unpack_bf16_pairs(y1_ref[0, 0])
    return h_ref[0] + mod_ref[0, gate_idx:gate_idx + 1, :] * moe


def _combine_next_kernel(h_ref, y0_ref, y1_ref, wt_ref, mod_ref, g_ref, nmod_ref, o_ref, hm_ref, *, gate_idx):
    h_new = _moe_residual(h_ref, y0_ref, y1_ref, wt_ref, mod_ref, gate_idx)
    o_ref[0] = h_new
    hm_ref[0] = _rms_mod(h_new, g_ref[...], nmod_ref[0, 0:1, :], nmod_ref[0, 1:2, :]).astype(hm_ref.dtype)


def _combine_final_kernel(h_ref, y0_ref, y1_ref, wt_ref, mod_ref, g_ref, o_ref, *, gate_idx):
    x = _moe_residual(h_ref, y0_ref, y1_ref, wt_ref, mod_ref, gate_idx)
    ms = jnp.mean(x * x, axis=-1, keepdims=True)
    o_ref[0] = x * lax.rsqrt(ms + NORM_EPS) * g_ref[...]


def combine(h, y_pairs, wt, modtab, gate_idx, next_g, next_modtab, ctx_len):
    b, p, d = h.shape
    last = next_modtab is None
    skip = ctx_len // ROW_TILE if last else 0
    row = pl.BlockSpec((1, ROW_TILE, d), lambda b, j: (b, j + skip, 0))
    mod = pl.BlockSpec((1, N_MOD, d), lambda b, j: (2 * b + jnp.minimum(j + skip, 1), 0, 0))
    in_specs = [row,
                pl.BlockSpec((1, 1, ROW_TILE, d // 2), lambda b, j: (0, b, j + skip, 0)),
                pl.BlockSpec((1, 1, ROW_TILE, d // 2), lambda b, j: (1, b, j + skip, 0)),
                pl.BlockSpec((1, ROW_TILE, LANES), lambda b, j: (b, j + skip, 0)),
                mod, pl.BlockSpec((1, d), lambda b, j: (0, 0))]
    args = [h, y_pairs, y_pairs, wt, modtab.reshape(b * 2, N_MOD, d), next_g.reshape(1, d)]
    out_row = pl.BlockSpec((1, ROW_TILE, d), lambda b, j: (b, j, 0))
    if last:
        return pl.pallas_call(
            functools.partial(_combine_final_kernel, gate_idx=gate_idx),
            grid=(b, (p - ctx_len) // ROW_TILE),
            in_specs=in_specs,
            out_specs=out_row,
            out_shape=jax.ShapeDtypeStruct((b, p - ctx_len, d), F32),
            compiler_params=_cparams(("parallel", "parallel")),
            name="moe_combine_final",
        )(*args)
    return pl.pallas_call(
        functools.partial(_combine_next_kernel, gate_idx=gate_idx),
        grid=(b, p // ROW_TILE),
        in_specs=in_specs + [mod],
        out_specs=[out_row, out_row],
        out_shape=[jax.ShapeDtypeStruct((b, p, d), F32), jax.ShapeDtypeStruct((b, p, d), BF16)],
        compiler_params=_cparams(("parallel", "parallel")),
        name="moe_combine",
    )(*args, next_modtab.reshape(b * 2, N_MOD, d))


def hier_moe(h, f, route, wt, counts, modtab, gate_idx, w_gate, w_up, w_down, layer, next_g, next_modtab, ctx_len):
    b, p, d = h.shape
    n_tok = b * p
    n_assign = 2 * n_tok
    e1, e2, rank1, rank2 = route.reshape(n_tok, LANES)[:, :4].T
    cnt = counts[0, :N_EXPERTS].astype(jnp.int32)
    pcounts = (cnt + MOE_BLOCK - 1) // MOE_BLOCK * MOE_BLOCK
    pend = jnp.cumsum(pcounts)
    pstart = pend - pcounts
    dest_by_slot = jnp.concatenate([pstart[e1] + rank1, pstart[e2] + rank2]).astype(jnp.int32)
    n_blk = -(-n_assign // MOE_BLOCK) + N_EXPERTS
    rows_total = n_blk * MOE_BLOCK
    blk_row0 = jnp.arange(n_blk, dtype=jnp.int32) * MOE_BLOCK
    blk_e = jnp.minimum(jnp.sum((pend[None, :] <= blk_row0[:, None]).astype(jnp.int32), axis=1), N_EXPERTS - 1)
    n_valid = jnp.clip(cnt[blk_e] - (blk_row0 - pstart[blk_e]), 0, MOE_BLOCK).astype(jnp.int32)
    n_used = (pend[-1] // MOE_BLOCK).astype(jnp.int32).reshape(1)

    x_disp = scatter_rows(f.reshape(n_tok, d // 2), dest_by_slot, rows_total)
    y = expert_ffn(x_disp, blk_e, n_used, n_valid, w_gate, w_up, w_down, layer)
    y_pairs = gather_rows(y, dest_by_slot).reshape(2, b, p, d // 2)
    return combine(h, y_pairs, wt, modtab, gate_idx, next_g, next_modtab, ctx_len)


def kernel(x, c, ctx, c_ctx, mod_w, mod_b, norm_mix_g, norm_ffn_g, router_grp_w, router_grp_b, router_exp_w, router_exp_b, exp_w_gate, exp_w_up, exp_w_down, l0_na_w_qkv, l0_na_rpb, l0_na_w_o, l1_swa_w_qkv, l1_swa_sink, l1_swa_w_o, l2_mla_w_dq, l2_mla_q_norm_g, l2_mla_w_uq, l2_mla_w_dkv, l2_mla_kv_norm_g, l2_mla_w_ukv, l2_mla_w_o, l3_na_w_qkv, l3_na_rpb, l3_na_w_o, final_norm_g):
    b, s, d = x.shape
    lc = ctx.shape[1]
    n_heads = d // HEAD_DIM
    n_kv_heads = n_heads // 4
    depth = mod_w.shape[0]
    rows = s // GRID_W

    modtabs = modulation_tables(c, c_ctx, mod_w, mod_b)

    def scale_q_cols(w):
        n_q = n_heads * HEAD_DIM
        q_scale = HEAD_DIM ** -0.5 * LOG2E
        return jnp.concatenate([w[:, :n_q] * q_scale, w[:, n_q:]], axis=1).astype(BF16)

    def na_mixer(hm, w_qkv, rpb, tn, unroll):
        qkv = project(hm, scale_q_cols(w_qkv), tn=tn)
        return neighbourhood_attention(qkv, na_bias_table(rpb, rows), lc, n_heads, unroll)

    def swa_mixer(hm):
        cos, sin = rope_tables_full(s, lc)
        n_rope = (n_heads + n_kv_heads) * HEAD_DIM // 512
        qkv = project(hm, scale_q_cols(l1_swa_w_qkv), tn=512, rope=(cos, sin, 0, n_rope))
        return window_attention(qkv, l1_swa_sink, lc, n_heads, n_kv_heads)

    def mla_mixer(hm):
        q_rank = l2_mla_w_dq.shape[1]
        kv_rank = l2_mla_kv_norm_g.shape[0]
        cos, sin = rope_tables_mla(s, lc)
        w_down = jnp.concatenate([l2_mla_w_dq, l2_mla_w_dkv[:, :kv_rank],
                                  _spread_rope_cols(l2_mla_w_dkv[:, kv_rank:])], axis=1).astype(BF16)
        cq, ckv, kr = mla_down(hm, w_down, l2_mla_q_norm_g, l2_mla_kv_norm_g, cos, sin, q_rank, kv_rank)
        w_uq = l2_mla_w_uq.reshape(q_rank, n_heads, MLA_NOPE_DIM + MLA_ROPE_DIM)
        q_scale = (MLA_NOPE_DIM + MLA_ROPE_DIM) ** -0.5 * LOG2E
        half = MLA_ROPE_DIM // 2
        gap = jnp.zeros((q_rank, n_heads, half), l2_mla_w_uq.dtype)
        w_q = jnp.concatenate([w_uq[:, :, :MLA_NOPE_DIM], w_uq[:, :, MLA_NOPE_DIM:MLA_NOPE_DIM + half], gap,
                               w_uq[:, :, MLA_NOPE_DIM + half:], gap], axis=2).reshape(q_rank, n_heads * 2 * LANES)
        return latent_attention(cq, ckv, kr, (w_q * q_scale).astype(BF16), l2_mla_w_ukv.astype(BF16), cos, sin,
                                lc, n_heads)

    h, hm = join_norm_modulate(ctx, x, norm_mix_g[0], modtabs[0])
    for i in range(depth):
        modtab = modtabs[i]
        mixer = i % 3
        if mixer == 0:
            w_qkv, rpb, w_o = (l0_na_w_qkv, l0_na_rpb, l0_na_w_o) if i == 0 else (l3_na_w_qkv, l3_na_rpb, l3_na_w_o)
            y = na_mixer(hm, w_qkv, rpb, *((512, 4) if i == 0 else (1024, 8)))
        elif mixer == 1:
            y, w_o = swa_mixer(hm), l1_swa_w_o
        else:
            y, w_o = mla_mixer(hm), l2_mla_w_o
        h, f, route, wt, counts = attn_out_route(y, w_o.astype(BF16), h, modtab, norm_ffn_g[i], router_grp_w[i],
                                                 router_grp_b[i], router_exp_w[i], router_exp_b[i], lc,
                                                 n_sub=2 if i < 2 else 1)
        if i + 1 < depth:
            h, hm = hier_moe(h, f, route, wt, counts, modtab, 5, exp_w_gate, exp_w_up, exp_w_down, i,
                             norm_mix_g[i + 1], modtabs[i + 1], lc)
        else:
            return hier_moe(h, f, route, wt, counts, modtab, 5, exp_w_gate, exp_w_up, exp_w_down, i,
                            final_norm_g, None, lc)
```

```python
import functools

import numpy as np
import jax
import jax.numpy as jnp
from jax import lax
from jax.experimental import pallas as pl
from jax.experimental.pallas import tpu as pltpu
from jax.experimental.pallas import tpu_sc as plsc

GRID_W = 64
HEAD_DIM = 128
ROPE_BASE = 10000.0
NORM_EPS = 1e-6
NEG_INF = -1e30
N_MOD = 6

NA_ROWS = 8
NA_COLS = 16
NA_TILE_ROWS = 4
NA_WIN_ROWS = 12

SWA_WINDOW = 128
SWA_TQ = 256
SWA_TK = 512

MLA_NOPE_DIM = 128
MLA_ROPE_DIM = 64
MLA_TQ = 512
MLA_TK = 512
LOG2E = 1.4426950408889634

N_GROUPS = 4
EXPERTS_PER_GROUP = 8
N_EXPERTS = N_GROUPS * EXPERTS_PER_GROUP
MOE_BLOCK = 512
GATHER_WINDOW = 32

LANES = 128
ROW_TILE = 256
VMEM_LIMIT = 56 * 1024 * 1024

BF16 = jnp.bfloat16
F32 = jnp.float32


def _cparams(sem):
    return pltpu.CompilerParams(dimension_semantics=sem, vmem_limit_bytes=VMEM_LIMIT)


def _dot(a, b):
    return jnp.dot(a, b, preferred_element_type=F32)


def _dot_nt(a, b):
    return lax.dot_general(a, b, (((1,), (1,)), ((), ())), preferred_element_type=F32)


def _mod_kernel(x_ref, w_ref, b_ref, o_ref):
    x = x_ref[...]
    sx = (x * jax.nn.sigmoid(x)).astype(BF16)
    o_ref[0] = _dot(sx, w_ref[0].astype(BF16)) + b_ref[0]


def modulation_tables(c, c_ctx, mod_w, mod_b):
    depth, d, n_out = mod_w.shape
    b = c.shape[0]
    rows = 16
    xin = jnp.zeros((rows, d), F32).at[:b].set(c).at[b].set(c_ctx)
    tn = 1024
    out = pl.pallas_call(
        _mod_kernel,
        grid=(depth, n_out // tn),
        in_specs=[pl.BlockSpec((rows, d), lambda i, j: (0, 0)),
                  pl.BlockSpec((1, d, tn), lambda i, j: (i, 0, j)),
                  pl.BlockSpec((1, 1, tn), lambda i, j: (i, 0, j))],
        out_specs=pl.BlockSpec((1, rows, tn), lambda i, j: (i, 0, j)),
        out_shape=jax.ShapeDtypeStruct((depth, rows, n_out), F32),
        compiler_params=_cparams(("parallel", "parallel")),
        name="adaln_mod",
    )(xin, mod_w, mod_b.reshape(depth, 1, n_out))
    lat = out[:, :b].reshape(depth, b, 1, N_MOD, d)
    ctx = jnp.broadcast_to(out[:, b].reshape(depth, 1, 1, N_MOD, d), (depth, b, 1, N_MOD, d))
    return jnp.concatenate([ctx, lat], axis=2)


def _rms_mod(x, g, shift, scale):
    ms = jnp.mean(x * x, axis=-1, keepdims=True)
    y = x * lax.rsqrt(ms + NORM_EPS) * g
    return y * (1.0 + scale) + shift


def _join_norm_mod_kernel(ctx_ref, x_ref, g_ref, mod_ref, h_ref, hm_ref, *, ctx_tiles):
    j = pl.program_id(1)

    def emit(src_ref):
        h_ref[0] = src_ref[0]
        hm_ref[0] = _rms_mod(src_ref[0], g_ref[...], mod_ref[0, 0:1, :], mod_ref[0, 1:2, :]).astype(hm_ref.dtype)

    @pl.when(j < ctx_tiles)
    def _():
        emit(ctx_ref)

    @pl.when(j >= ctx_tiles)
    def _():
        emit(x_ref)


def _route(logits):
    lane = lax.broadcasted_iota(jnp.int32, logits.shape, 1).astype(F32)
    big = float(LANES)

    def first_lane(mask):
        return jnp.min(jnp.where(mask, lane, big), axis=-1, keepdims=True)

    in_grp = lane < N_GROUPS
    lg = jnp.where(in_grp, logits, NEG_INF)
    m_g = jnp.max(lg, axis=-1, keepdims=True)
    g_idx = first_lane(in_grp & (lg == m_g))
    g_w = 1.0 / jnp.sum(jnp.where(in_grp, jnp.exp(lg - m_g), 0.0), axis=-1, keepdims=True)
    e_lo = N_GROUPS + g_idx * EXPERTS_PER_GROUP
    in_e = (lane >= e_lo) & (lane < e_lo + EXPERTS_PER_GROUP)
    le = jnp.where(in_e, logits, NEG_INF)
    m1 = jnp.max(le, axis=-1, keepdims=True)
    e1 = first_lane(in_e & (le == m1))
    s_e = jnp.sum(jnp.where(in_e, jnp.exp(le - m1), 0.0), axis=-1, keepdims=True)
    in_e2 = in_e & (lane != e1)
    le2 = jnp.where(in_e2, logits, NEG_INF)
    m2 = jnp.max(le2, axis=-1, keepdims=True)
    e2 = first_lane(in_e2 & (le2 == m2))
    p1 = 1.0 / s_e
    p2 = jnp.exp(m2 - m1) / s_e
    den = p1 + p2
    return ((e1 - N_GROUPS).astype(jnp.int32), (e2 - N_GROUPS).astype(jnp.int32),
            g_w * p1 / den, g_w * p2 / den)


def _pack_bf16_pairs(x):
    n = x.shape[1] // 2
    xb = x.astype(BF16).astype(F32)
    hi = lax.bitcast_convert_type(xb[:, :n], jnp.int32)
    lo = lax.bitcast_convert_type(xb[:, n:], jnp.int32)
    return (hi & jnp.int32(-65536)) | lax.shift_right_logical(lo, jnp.int32(16))


def _unpack_bf16_pairs(w):
    hi = lax.bitcast_convert_type(w & jnp.int32(-65536), F32)
    lo = lax.bitcast_convert_type(lax.shift_left(w, jnp.int32(16)), F32)
    return jnp.concatenate([hi, lo], axis=1)


def _route_and_rank(f, wr_ref, br_ref, run_ref):
    f_hi = f.astype(BF16)
    f_lo = (f - f_hi.astype(F32)).astype(BF16)
    hi_terms = _dot(f_hi, wr_ref[...])
    logits = (hi_terms[:, :LANES] + _dot(f_lo, wr_ref[:, :LANES]) + hi_terms[:, LANES:]) + br_ref[...]
    e1, e2, w1, w2 = _route(logits)
    lane = lax.broadcasted_iota(jnp.int32, logits.shape, 1)
    pick1, pick2 = lane == e1, lane == e2
    chosen = (pick1 | pick2).astype(F32)
    n_rows = chosen.shape[0]
    earlier = (lax.broadcasted_iota(jnp.int32, (n_rows, n_rows), 1)
               < lax.broadcasted_iota(jnp.int32, (n_rows, n_rows), 0)).astype(BF16)
    before = run_ref[...] + _dot(earlier, chosen.astype(BF16))
    rank1 = jnp.sum(jnp.where(pick1, before, 0.0), axis=-1, keepdims=True).astype(jnp.int32)
    rank2 = jnp.sum(jnp.where(pick2, before, 0.0), axis=-1, keepdims=True).astype(jnp.int32)
    run_ref[...] = run_ref[...] + jnp.sum(chosen, axis=0, keepdims=True)
    route = jnp.where(lane == 0, e1, jnp.where(lane == 1, e2, jnp.where(lane == 2, rank1,
                                                                         jnp.where(lane == 3, rank2, 0))))
    return route, jnp.where(lane == 0, w1, jnp.where(lane == 1, w2, 0.0))


def _attn_out_route_kernel(y_ref, w_ref, h_ref, mod_ref, g_ref, wr_ref, br_ref,
                           ho_ref, f_ref, route_ref, wt_ref, cnt_ref, run_ref, *, ctx_len, tm, n_sub):
    first = (pl.program_id(0) == 0) & (pl.program_id(1) == 0)

    @pl.when(first)
    def _():
        run_ref[...] = jnp.zeros_like(run_ref)

    sub = tm // n_sub
    for s in range(n_sub):
        rows = slice(s * sub, (s + 1) * sub)
        acc = _dot(y_ref[0, rows, :], w_ref[...])
        pos = pl.program_id(1) * tm + s * sub + lax.broadcasted_iota(jnp.int32, (sub, 1), 0)
        is_ctx = pos < ctx_len

        def mod_row(k):
            return jnp.where(is_ctx, mod_ref[0, 0, k:k + 1, :], mod_ref[0, 1, k:k + 1, :])

        h_new = h_ref[0, rows, :] + mod_row(2) * acc
        ho_ref[0, rows, :] = h_new
        f = _rms_mod(h_new, g_ref[...], mod_row(3), mod_row(4))
        f_ref[0, rows, :] = _pack_bf16_pairs(f)
        route, wt = _route_and_rank(f, wr_ref, br_ref, run_ref)
        route_ref[0, rows, :] = route
        wt_ref[0, rows, :] = wt
    cnt_ref[...] = run_ref[...]


def _mod_spec(d):
    return pl.BlockSpec((1, N_MOD, d), lambda b, j: (2 * b + jnp.minimum(j, 1), 0, 0))


def join_norm_modulate(ctx, x, g, modtab):
    b, lc, d = ctx.shape
    p = lc + x.shape[1]
    assert lc == ROW_TILE
    ctx_tiles = lc // ROW_TILE
    row = pl.BlockSpec((1, ROW_TILE, d), lambda b, j: (b, j, 0))
    return pl.pallas_call(
        functools.partial(_join_norm_mod_kernel, ctx_tiles=ctx_tiles),
        grid=(b, p // ROW_TILE),
        in_specs=[pl.BlockSpec((1, ROW_TILE, d), lambda b, j: (b, jnp.minimum(j, ctx_tiles - 1), 0)),
                  pl.BlockSpec((1, ROW_TILE, d), lambda b, j: (b, jnp.maximum(j - ctx_tiles, 0), 0)),
                  pl.BlockSpec((1, d), lambda b, j: (0, 0)), _mod_spec(d)],
        out_specs=[row, row],
        out_shape=[jax.ShapeDtypeStruct((b, p, d), F32), jax.ShapeDtypeStruct((b, p, d), BF16)],
        compiler_params=_cparams(("parallel", "parallel")),
        name="join_norm_mod",
    )(ctx, x, g.reshape(1, d), modtab.reshape(b * 2, N_MOD, d))


def attn_out_route(y, w_o, h, modtab, g, w_grp, b_grp, w_rt, b_rt, ctx_len, n_sub=2):
    b, p, d = h.shape
    n_r = N_GROUPS + N_EXPERTS
    wr = jnp.zeros((d, LANES), F32).at[:, :N_GROUPS].set(w_grp).at[:, N_GROUPS:n_r].set(w_rt)
    br = jnp.zeros((1, LANES), F32).at[0, :N_GROUPS].set(b_grp).at[0, N_GROUPS:n_r].set(b_rt)
    wr_hi = wr.astype(BF16)
    wr = jnp.concatenate([wr_hi, (wr - wr_hi.astype(F32)).astype(BF16)], axis=1)
    tm = 544 if p % 544 == 0 else ROW_TILE
    assert (tm // n_sub) % 16 == 0
    row = lambda width: pl.BlockSpec((1, tm, width), lambda b, i: (b, i, 0))
    once = lambda shape: pl.BlockSpec(shape, lambda b, i: (0,) * len(shape), pipeline_mode=pl.Buffered(1))
    return pl.pallas_call(
        functools.partial(_attn_out_route_kernel, ctx_len=ctx_len, tm=tm, n_sub=n_sub),
        grid=(b, p // tm),
        in_specs=[row(d), once((d, d)), row(d),
                  pl.BlockSpec((1, 2, N_MOD, d), lambda b, i: (b, 0, 0, 0)),
                  once((1, d)), once((d, 2 * LANES)), once((1, LANES))],
        out_specs=[row(d), row(d // 2), row(LANES), row(LANES), pl.BlockSpec((1, LANES), lambda b, i: (0, 0))],
        out_shape=[jax.ShapeDtypeStruct((b, p, d), F32),
                   jax.ShapeDtypeStruct((b, p, d // 2), jnp.int32),
                   jax.ShapeDtypeStruct((b, p, LANES), jnp.int32),
                   jax.ShapeDtypeStruct((b, p, LANES), F32),
                   jax.ShapeDtypeStruct((1, LANES), F32)],
        scratch_shapes=[pltpu.VMEM((1, LANES), F32)],
        compiler_params=_cparams(("arbitrary", "arbitrary")),
        name="attn_out_route",
    )(y, w_o, h, modtab, g.reshape(1, d), wr, br)


def _rope(acc, cos, sin):
    n_blk = acc.shape[1] // LANES
    outs = []
    for c in range(n_blk):
        x = acc[:, c * LANES:(c + 1) * LANES]
        outs.append(x * cos + pltpu.roll(x, LANES // 2, 1) * sin)
    return outs[0] if n_blk == 1 else jnp.concatenate(outs, axis=1)


def _proj_kernel(*refs, rope_lo, rope_hi):
    x_ref, w_ref = refs[0], refs[1]
    o_ref = refs[-1]
    acc = _dot(x_ref[0], w_ref[...])
    if rope_hi > rope_lo:
        cos_ref, sin_ref = refs[2], refs[3]
        j = pl.program_id(2)
        roped = (j >= rope_lo) & (j < rope_hi)

        @pl.when(roped)
        def _():
            o_ref[0] = _rope(acc, cos_ref[...], sin_ref[...]).astype(o_ref.dtype)

        @pl.when(jnp.logical_not(roped))
        def _():
            o_ref[0] = acc.astype(o_ref.dtype)
    else:
        o_ref[0] = acc.astype(o_ref.dtype)


def _row_tile(p):
    for cand in (1088, 1024, 544, 512, 272, 256, 128, 64, 32, 16):
        if p % cand == 0:
            return cand
    raise ValueError(p)


def project(x, w, *, tn=512, out_dtype=BF16, rope=None):
    b, p, k = x.shape
    n = w.shape[1]
    tm = _row_tile(p)
    tn = min(tn, n)
    assert n % tn == 0
    in_specs = [pl.BlockSpec((1, tm, k), lambda b, i, j: (b, i, 0)),
                pl.BlockSpec((k, tn), lambda b, i, j: (0, j))]
    args = [x, w]
    kw = dict(rope_lo=0, rope_hi=0)
    if rope is not None:
        cos, sin, lo, hi = rope
        in_specs += [pl.BlockSpec((tm, LANES), lambda b, i, j: (i, 0)),
                     pl.BlockSpec((tm, LANES), lambda b, i, j: (i, 0))]
        args += [cos, sin]
        kw.update(rope_lo=lo, rope_hi=hi)
    return pl.pallas_call(
        functools.partial(_proj_kernel, **kw),
        grid=(b, p // tm, n // tn),
        in_specs=in_specs,
        out_specs=pl.BlockSpec((1, tm, tn), lambda b, i, j: (b, i, j)),
        out_shape=jax.ShapeDtypeStruct((b, p, n), out_dtype),
        compiler_params=_cparams(("parallel", "parallel", "arbitrary")),
        name="project",
    )(*args)


def _axial_cos_sin(n, rot_dim):
    t = jnp.arange(n, dtype=jnp.int32)
    row = (t // GRID_W).astype(F32)
    col = (t % GRID_W).astype(F32)
    n_freq = rot_dim // 4
    inv = ROPE_BASE ** (-jnp.arange(n_freq, dtype=F32) / n_freq)
    ang = jnp.concatenate([row[:, None] * inv, col[:, None] * inv], axis=-1)
    return jnp.cos(ang), jnp.sin(ang)


def rope_tables_full(s, ctx_len):
    c, sn = _axial_cos_sin(s, HEAD_DIM)
    cos = jnp.concatenate([c, c], axis=1)
    sin = jnp.concatenate([-sn, sn], axis=1)
    ident_c = jnp.ones((ctx_len, LANES), F32)
    ident_s = jnp.zeros((ctx_len, LANES), F32)
    return jnp.concatenate([ident_c, cos], axis=0), jnp.concatenate([ident_s, sin], axis=0)


def rope_tables_mla(s, ctx_len):
    c, sn = _axial_cos_sin(s, MLA_ROPE_DIM)
    one = jnp.ones_like(c)
    zero = jnp.zeros_like(c)
    cos = jnp.concatenate([c, one, c, one], axis=1)
    sin = jnp.concatenate([-sn, zero, sn, zero], axis=1)
    ident_c = jnp.ones((ctx_len, LANES), F32)
    ident_s = jnp.zeros((ctx_len, LANES), F32)
    return jnp.concatenate([ident_c, cos], axis=0), jnp.concatenate([ident_s, sin], axis=0)


def _spread_rope_cols(w_rope):
    k = w_rope.shape[0]
    half = MLA_ROPE_DIM // 2
    z = jnp.zeros((k, half), w_rope.dtype)
    return jnp.concatenate([w_rope[:, :half], z, w_rope[:, half:], z], axis=1)


def _softmax_probs(parts, extra=None):
    m = _row_reduce(jnp.maximum, jnp.max, parts)
    if extra is not None:
        m = jnp.maximum(m, extra)
    return [jnp.exp2((s - m).astype(BF16)) for s in parts], m


def _with_ones(v_ref, vone_ref):
    vone_ref[:, 0:LANES] = v_ref[0]
    vone_ref[:, LANES:2 * LANES] = jnp.ones((vone_ref.shape[0], LANES), BF16)


def _normalise(acc, extra_den=None):
    den = acc[:, LANES:LANES + 1]
    if extra_den is not None:
        den = den + extra_den
    return acc[:, :LANES] * (1.0 / den)


def _row_reduce(combine, reduce, parts):
    blocks = [s[:, c:c + LANES] for s in parts for c in range(0, s.shape[1], LANES)]
    acc = blocks[0]
    for blk in blocks[1:]:
        acc = combine(acc, blk)
    return reduce(acc, axis=-1, keepdims=True)


def na_bias_table(rpb, rows):
    n_tiles = rows // NA_TILE_ROWS
    n_heads, _, n_dcol = rpb.shape
    drow, row_ok = [], []
    for tile in (0, 1, n_tiles - 1):
        kr0 = int(np.clip(NA_TILE_ROWS * tile - NA_ROWS // 2, 0, rows - NA_WIN_ROWS))
        r = NA_TILE_ROWS * tile + np.arange(NA_TILE_ROWS)
        r0 = np.clip(r - NA_ROWS // 2, 0, rows - NA_ROWS)
        krow = kr0 + np.arange(NA_WIN_ROWS)
        row_ok.append((krow[None, :] >= r0[:, None]) & (krow[None, :] < r0[:, None] + NA_ROWS))
        drow.append(np.clip(krow[None, :] - r[:, None] + NA_ROWS - 1, 0, 2 * NA_ROWS - 2))
    drow, row_ok = np.stack(drow), np.stack(row_ok)
    qc = np.arange(GRID_W)
    qcol0 = np.clip(qc - NA_COLS // 2, 0, GRID_W - NA_COLS)
    kc = np.arange(GRID_W)
    col_ok = (kc[None, :] >= qcol0[:, None]) & (kc[None, :] < qcol0[:, None] + NA_COLS)
    dcol = np.clip(kc[None, :] - qc[:, None] + NA_COLS - 1, 0, 2 * NA_COLS - 2)
    pick_col = jnp.asarray(dcol[None] == np.arange(n_dcol)[:, None, None], F32)
    slabs = jnp.einsum('hrd,dqk->hrqk', rpb.astype(F32), pick_col, precision=lax.Precision.HIGHEST)
    slabs = jnp.where(jnp.asarray(col_ok)[None, None], slabs * LOG2E, NEG_INF)
    masked = jnp.full((n_heads, GRID_W, GRID_W), NEG_INF, F32)
    pats = []
    for p in range(3):
        per_row = [jnp.stack([slabs[:, drow[p, a, m]] if row_ok[p, a, m] else masked
                              for m in range(NA_WIN_ROWS)], axis=2)
                   for a in range(NA_TILE_ROWS)]
        pats.append(jnp.stack(per_row, axis=1))
    vals = jnp.stack(pats, axis=1)
    return vals.reshape(n_heads, 3, NA_TILE_ROWS * GRID_W, NA_WIN_ROWS * GRID_W)


def _na_kernel(q_ref, k_ref, v_ref, bias_ref, o_ref, vone_ref, *, ctx_len, rows, unroll):
    tq = NA_TILE_ROWS * GRID_W
    tk = NA_WIN_ROWS * GRID_W
    n_tiles = rows // NA_TILE_ROWS
    lc = ctx_len
    _with_ones(v_ref, vone_ref)

    s = _dot_nt(q_ref[0, 0:lc, :], k_ref[0, 0:lc, :])
    (p,), _ = _softmax_probs([s])
    o_ref[0, 0:lc, :] = _normalise(_dot(p, vone_ref[0:lc, :])).astype(o_ref.dtype)

    def tile(i, carry):
        qs = pl.multiple_of(lc + i * tq, tq)
        kr0 = jnp.clip(NA_TILE_ROWS * i - NA_ROWS // 2, 0, rows - NA_WIN_ROWS)
        ks = pl.multiple_of(lc + kr0 * GRID_W, NA_TILE_ROWS * GRID_W)
        pat = jnp.where(i == 0, 0, jnp.where(i == n_tiles - 1, 2, 1))
        q = q_ref[0, pl.ds(qs, tq), :]
        s_loc = _dot_nt(q, k_ref[0, pl.ds(ks, tk), :]) + bias_ref[0, pat]
        s_ctx = _dot_nt(q, k_ref[0, 0:lc, :])
        (p_loc, p_ctx), _ = _softmax_probs([s_loc, s_ctx])
        acc = _dot(p_loc, vone_ref[pl.ds(ks, tk), :]) + _dot(p_ctx, vone_ref[0:lc, :])
        o_ref[0, pl.ds(qs, tq), :] = _normalise(acc).astype(o_ref.dtype)
        return carry

    lax.fori_loop(0, n_tiles, tile, 0, unroll=unroll)


def neighbourhood_attention(qkv, bias, ctx_len, n_heads, unroll=4):
    b, p, _ = qkv.shape
    rows = (p - ctx_len) // GRID_W
    assert rows % NA_TILE_ROWS == 0 and rows >= NA_WIN_ROWS
    tq, tk = NA_TILE_ROWS * GRID_W, NA_WIN_ROWS * GRID_W
    assert ctx_len % 16 == 0 and ctx_len % tq == 0
    blk = lambda off: pl.BlockSpec((1, p, HEAD_DIM), lambda h, b: (b, 0, off + h))
    return pl.pallas_call(
        functools.partial(_na_kernel, ctx_len=ctx_len, rows=rows, unroll=unroll),
        grid=(n_heads, b),
        in_specs=[blk(0), blk(n_heads), blk(2 * n_heads),
                  pl.BlockSpec((1, 3, tq, tk), lambda h, b: (h, 0, 0, 0))],
        out_specs=blk(0),
        out_shape=jax.ShapeDtypeStruct((b, p, n_heads * HEAD_DIM), BF16),
        scratch_shapes=[pltpu.VMEM((p, 2 * LANES), BF16)],
        compiler_params=_cparams(("parallel", "parallel")),
        name="na_attention",
    )(qkv, qkv, qkv, bias)


def _swa_kernel(sink_ref, q_ref, k_ref, v_ref, o_ref, vone_ref, *, ctx_len, seq, group):
    lc = ctx_len
    kvh = pl.program_id(1)
    _with_ones(v_ref, vone_ref)

    def stack_heads(q):
        return jnp.concatenate([q[:, g * HEAD_DIM:(g + 1) * HEAD_DIM] for g in range(group)], axis=0)

    def finish(parts_fn, n_q, pv_fn, store):
        ps_all, sink_terms = [], []
        for g in range(group):
            sink = sink_ref[kvh * group + g]
            ps, m = _softmax_probs(parts_fn(g), extra=sink)
            ps_all.append(ps)
            sink_terms.append(jnp.exp2(sink - m))
        n_parts = len(ps_all[0])
        stacked = [jnp.concatenate([ps_all[g][k] for g in range(group)], axis=0) for k in range(n_parts)]
        acc = pv_fn(stacked)
        store(jnp.concatenate([_normalise(acc[g * n_q:(g + 1) * n_q], sink_terms[g]) for g in range(group)], axis=1))

    s_c = _dot_nt(stack_heads(q_ref[0, 0:lc, :]), k_ref[0, 0:lc, :])

    def store_ctx(o):
        o_ref[0, 0:lc, :] = o.astype(o_ref.dtype)

    finish(lambda g: [s_c[g * lc:(g + 1) * lc]], lc,
           lambda st: _dot(st[0], vone_ref[0:lc, :]), store_ctx)

    n_tiles = seq // SWA_TQ

    def tile(t, carry):
        q0 = t * SWA_TQ
        k0 = jnp.clip(q0 - SWA_WINDOW, 0, seq - SWA_TK)
        qs = pl.multiple_of(lc + q0, SWA_WINDOW)
        ks = pl.multiple_of(lc + k0, SWA_WINDOW)
        q4 = stack_heads(q_ref[0, pl.ds(qs, SWA_TQ), :])
        s_loc = _dot_nt(q4, k_ref[0, pl.ds(ks, SWA_TK), :])
        s_ctx = _dot_nt(q4, k_ref[0, 0:lc, :])
        dpos = (lax.broadcasted_iota(jnp.int32, (SWA_TQ, SWA_TK), 1)
                - lax.broadcasted_iota(jnp.int32, (SWA_TQ, SWA_TK), 0)) + (k0 - q0)
        valid = jnp.abs(dpos) <= SWA_WINDOW

        def parts(g):
            sl = slice(g * SWA_TQ, (g + 1) * SWA_TQ)
            return [jnp.where(valid, s_loc[sl], NEG_INF), s_ctx[sl]]

        def store(o):
            o_ref[0, pl.ds(qs, SWA_TQ), :] = o.astype(o_ref.dtype)

        finish(parts, SWA_TQ,
               lambda st: _dot(st[0], vone_ref[pl.ds(ks, SWA_TK), :]) + _dot(st[1], vone_ref[0:lc, :]),
               store)
        return carry

    lax.fori_loop(0, n_tiles, tile, 0, unroll=8)


def window_attention(qkv, sink, ctx_len, n_heads, n_kv_heads):
    b, p, _ = qkv.shape
    seq = p - ctx_len
    group = n_heads // n_kv_heads
    assert seq % SWA_TQ == 0 and seq >= SWA_TK and ctx_len % SWA_WINDOW == 0
    kv = lambda off: pl.BlockSpec((1, p, HEAD_DIM), lambda b, h: (b, 0, off + h))
    qo = pl.BlockSpec((1, p, group * HEAD_DIM), lambda b, h: (b, 0, h))
    return pl.pallas_call(
        functools.partial(_swa_kernel, ctx_len=ctx_len, seq=seq, group=group),
        grid=(b, n_kv_heads),
        in_specs=[pl.BlockSpec(memory_space=pltpu.SMEM), qo, kv(n_heads), kv(n_heads + n_kv_heads)],
        out_specs=qo,
        out_shape=jax.ShapeDtypeStruct((b, p, n_heads * HEAD_DIM), BF16),
        scratch_shapes=[pltpu.VMEM((p, 2 * LANES), BF16)],
        compiler_params=_cparams(("parallel", "parallel")),
        name="swa_attention",
    )(sink.astype(F32) * LOG2E, qkv, qkv, qkv)


def _mla_down_kernel(x_ref, w_ref, gq_ref, gkv_ref, cos_ref, sin_ref, cq_ref, ckv_ref, kr_ref, *, q_rank, kv_rank):
    acc = _dot(x_ref[0], w_ref[...])

    def rms(x, g):
        ms = jnp.mean(x * x, axis=-1, keepdims=True)
        return x * lax.rsqrt(ms + NORM_EPS) * g

    cq_ref[0] = rms(acc[:, :q_rank], gq_ref[...]).astype(cq_ref.dtype)
    ckv_ref[0] = rms(acc[:, q_rank:q_rank + kv_rank], gkv_ref[...]).astype(ckv_ref.dtype)
    kr_ref[0] = _rope(acc[:, q_rank + kv_rank:], cos_ref[...], sin_ref[...]).astype(kr_ref.dtype)


def mla_down(x, w_down, gq, gkv, cos, sin, q_rank, kv_rank):
    b, p, k = x.shape
    n = w_down.shape[1]
    tm = 544 if p % 544 == 0 else _row_tile(p)
    row = lambda width: pl.BlockSpec((1, tm, width), lambda b, i: (b, i, 0))
    return pl.pallas_call(
        functools.partial(_mla_down_kernel, q_rank=q_rank, kv_rank=kv_rank),
        grid=(b, p // tm),
        in_specs=[row(k), pl.BlockSpec((k, n), lambda b, i: (0, 0)),
                  pl.BlockSpec((1, q_rank), lambda b, i: (0, 0)),
                  pl.BlockSpec((1, kv_rank), lambda b, i: (0, 0)),
                  pl.BlockSpec((tm, LANES), lambda b, i: (i, 0)),
                  pl.BlockSpec((tm, LANES), lambda b, i: (i, 0))],
        out_specs=[row(q_rank), row(kv_rank), row(LANES)],
        out_shape=[jax.ShapeDtypeStruct((b, p, q_rank), BF16),
                   jax.ShapeDtypeStruct((b, p, kv_rank), BF16),
                   jax.ShapeDtypeStruct((b, p, LANES), BF16)],
        compiler_params=_cparams(("parallel", "parallel")),
        name="mla_down",
    )(x, w_down, gq.reshape(1, q_rank), gkv.reshape(1, kv_rank), cos, sin)


def _mla_kernel(cq_ref, ckv_ref, kr_ref, wq_ref, wkv_ref, cos_ref, sin_ref, o_ref, qcat_ref, kcat_ref, vone_ref, *,
                ctx_len):
    lc = ctx_len
    p_all = vone_ref.shape[0]
    kv = _dot(ckv_ref[0], wkv_ref[...])
    kcat_ref[0:LANES, :] = kv[:, :LANES].T.astype(BF16)
    kcat_ref[LANES:2 * LANES, :] = kr_ref[0].astype(F32).T.astype(BF16)
    vone_ref[:, 0:LANES] = kv[:, LANES:].astype(BF16)
    vone_ref[:, LANES:2 * LANES] = jnp.ones((p_all, LANES), BF16)
    q = _dot(cq_ref[0], wq_ref[...])
    qcat_ref[:, 0:LANES] = q[:, :LANES].astype(BF16)
    qcat_ref[:, LANES:2 * LANES] = _rope(q[:, LANES:], cos_ref[...], sin_ref[...]).astype(BF16)

    def attend(qs, n_q, chunks):
        q = qcat_ref[pl.ds(qs, n_q), :]
        m = jnp.full((n_q, 1), NEG_INF, F32)
        acc = jnp.zeros((n_q, 2 * LANES), F32)
        for c0, c1 in chunks:
            s = _dot(q, kcat_ref[:, c0:c1])
            m_new = jnp.maximum(m, _row_reduce(jnp.maximum, jnp.max, [s]))
            p = jnp.exp2((s - m_new).astype(BF16))
            acc = jnp.exp2(m - m_new) * acc + _dot(p, vone_ref[c0:c1, :])
            m = m_new
        o_ref[0, pl.ds(qs, n_q), :] = (acc[:, :LANES] * (1.0 / acc[:, LANES:LANES + 1])).astype(o_ref.dtype)

    attend(0, lc, [(0, lc)])
    all_chunks = [(0, lc)] + [(c, c + MLA_TK) for c in range(lc, p_all, MLA_TK)]

    def tile(i, carry):
        attend(pl.multiple_of(lc + i * MLA_TQ, MLA_TQ), MLA_TQ, all_chunks)
        return carry

    lax.fori_loop(0, (p_all - lc) // MLA_TQ, tile, 0, unroll=4)


def latent_attention(cq, ckv, kr, w_q, w_kv, cos, sin, ctx_len, n_heads):
    b, p, q_rank = cq.shape
    kv_rank = ckv.shape[2]
    assert (p - ctx_len) % MLA_TQ == 0 and (p - ctx_len) % MLA_TK == 0 and ctx_len % 16 == 0
    per_batch = lambda width: pl.BlockSpec((1, p, width), lambda b, h: (b, 0, 0), pipeline_mode=pl.Buffered(1))
    table = pl.BlockSpec((p, LANES), lambda b, h: (0, 0), pipeline_mode=pl.Buffered(1))
    return pl.pallas_call(
        functools.partial(_mla_kernel, ctx_len=ctx_len),
        grid=(b, n_heads),
        in_specs=[per_batch(q_rank), per_batch(kv_rank), per_batch(LANES),
                  pl.BlockSpec((q_rank, 2 * LANES), lambda b, h: (0, h)),
                  pl.BlockSpec((kv_rank, 2 * LANES), lambda b, h: (0, h)),
                  table, table],
        out_specs=pl.BlockSpec((1, p, LANES), lambda b, h: (b, 0, h)),
        out_shape=jax.ShapeDtypeStruct((b, p, n_heads * LANES), BF16),
        scratch_shapes=[pltpu.VMEM((p, 2 * LANES), BF16), pltpu.VMEM((2 * LANES, p), BF16),
                        pltpu.VMEM((p, 2 * LANES), BF16)],
        compiler_params=_cparams(("parallel", "parallel")),
        name="mla_attention",
    )(cq, ckv, kr, w_q, w_kv, cos, sin)


def gather_rows(src, idx):
    n = idx.shape[0]
    width = src.shape[1]
    win = GATHER_WINDOW
    info = plsc.get_sparse_core_info()
    n_workers = info.num_cores * info.num_subcores
    assert n % (n_workers * 2 * win) == 0
    per_worker = n // n_workers
    mesh = plsc.VectorSubcoreMesh(core_axis_name="core", subcore_axis_name="subcore")

    @functools.partial(
        pl.kernel, out_type=jax.ShapeDtypeStruct((n, width), src.dtype), mesh=mesh,
        scratch_types=[pltpu.VMEM((per_worker,), jnp.int32),
                       pltpu.VMEM((win, width), src.dtype), pltpu.VMEM((win, width), src.dtype),
                       pltpu.SemaphoreType.DMA, pltpu.SemaphoreType.DMA,
                       pltpu.SemaphoreType.DMA, pltpu.SemaphoreType.DMA],
        name="gather_rows")
    def gather(src_hbm, idx_hbm, out_hbm, idx_vmem, rows0, rows1, gsem0, gsem1, osem0, osem1):
        worker = lax.axis_index("subcore") * info.num_cores + lax.axis_index("core")
        base = worker * per_worker
        pltpu.sync_copy(idx_hbm.at[pl.ds(base, per_worker)], idx_vmem)

        def fetch(r, rows, sem):
            return pltpu.make_async_copy(src_hbm.at[idx_vmem.at[pl.ds(r, win)]], rows, sem)

        def flush(r, rows, sem):
            return pltpu.make_async_copy(rows, out_hbm.at[pl.ds(base + r, win)], sem)

        fetch(0, rows0, gsem0).start()

        @pl.loop(0, per_worker, step=2 * win)
        def _(r):
            fetch(r, rows0, gsem0).wait()

            @pl.when(r > 0)
            def _():
                flush(r - win, rows1, osem1).wait()

            fetch(r + win, rows1, gsem1).start()
            flush(r, rows0, osem0).start()
            fetch(r + win, rows1, gsem1).wait()
            flush(r, rows0, osem0).wait()

            @pl.when(r + 2 * win < per_worker)
            def _():
                fetch(r + 2 * win, rows0, gsem0).start()

            flush(r + win, rows1, osem1).start()

        flush(per_worker - win, rows1, osem1).wait()

    return gather(src, idx)


def scatter_rows(src, idx, n_out):
    n = idx.shape[0]
    n_src, width = src.shape
    win = GATHER_WINDOW
    info = plsc.get_sparse_core_info()
    n_workers = info.num_cores * info.num_subcores
    assert n % (n_workers * 2 * win) == 0
    per_worker = n // n_workers
    assert n_src % per_worker == 0
    n_steps = per_worker // win
    mesh = plsc.VectorSubcoreMesh(core_axis_name="core", subcore_axis_name="subcore")

    @functools.partial(
        pl.kernel, out_type=jax.ShapeDtypeStruct((n_out, width), src.dtype), mesh=mesh,
        scratch_types=[pltpu.VMEM((n_steps, win), jnp.int32),
                       pltpu.VMEM((win, width), src.dtype), pltpu.VMEM((win, width), src.dtype),
                       pltpu.SemaphoreType.DMA, pltpu.SemaphoreType.DMA,
                       pltpu.SemaphoreType.DMA, pltpu.SemaphoreType.DMA],
        name="scatter_rows")
    def scatter(src_hbm, idx_hbm, out_hbm, idx_vmem, rows0, rows1, lsem0, lsem1, ssem0, ssem1):
        worker = lax.axis_index("subcore") * info.num_cores + lax.axis_index("core")
        src_base = lax.rem(worker * per_worker, n_src)
        pltpu.sync_copy(idx_hbm.at[worker], idx_vmem)

        def load(j, rows, sem):
            return pltpu.make_async_copy(src_hbm.at[pl.ds(src_base + j * win, win)], rows, sem)

        def store(j, rows, sem):
            return pltpu.make_async_copy(rows, out_hbm.at[idx_vmem.at[j]], sem)

        load(0, rows0, lsem0).start()

        @pl.loop(0, n_steps, step=2)
        def _(j):
            load(j, rows0, lsem0).wait()

            @pl.when(j > 0)
            def _():
                store(j - 1, rows1, ssem1).wait()

            load(j + 1, rows1, lsem1).start()
            store(j, rows0, ssem0).start()
            load(j + 1, rows1, lsem1).wait()
            store(j, rows0, ssem0).wait()

            @pl.when(j + 2 < n_steps)
            def _():
                load(j + 2, rows0, lsem0).start()

            store(j + 1, rows1, ssem1).start()

        store(n_steps - 1, rows1, ssem1).wait()

    return scatter(src, idx.reshape(n_workers, n_steps, win))


def _expert_kernel(blk_e_ref, n_used_ref, n_valid_ref, x_ref, wg_ref, wu_ref, wd_ref, y_ref, wg_bf, wu_bf, wd_bf):
    i = pl.program_id(0)
    used = i < n_used_ref[0]
    new_expert = (i == 0) | (blk_e_ref[i] != blk_e_ref[jnp.maximum(i - 1, 0)])

    @pl.when(used & new_expert)
    def _():
        wg_bf[...] = wg_ref[0].astype(BF16)
        wu_bf[...] = wu_ref[0].astype(BF16)
        wd_bf[...] = wd_ref[0].astype(BF16)

    @pl.when(used)
    def _():
        row = lax.broadcasted_iota(jnp.int32, (x_ref.shape[0], 1), 0)
        x = _unpack_bf16_pairs(jnp.where(row < n_valid_ref[i], x_ref[...], 0)).astype(BF16)
        g = _dot(x, wg_bf[...])
        u = _dot(x, wu_bf[...])
        a = (g * jax.nn.sigmoid(g) * u).astype(BF16)
        y_ref[...] = _pack_bf16_pairs(_dot(a, wd_bf[...]))

    @pl.when(jnp.logical_not(used))
    def _():
        y_ref[...] = jnp.zeros_like(y_ref)


def expert_ffn(x_disp, blk_e, n_used, n_valid, w_gate, w_up, w_down, layer):
    rows, half = x_disp.shape
    d = 2 * half
    n_blk = rows // MOE_BLOCK
    de = w_gate.shape[3]
    grid_spec = pltpu.PrefetchScalarGridSpec(
        num_scalar_prefetch=3,
        grid=(n_blk,),
        in_specs=[pl.BlockSpec((MOE_BLOCK, half), lambda i, be, nu, nv: (i, 0)),
                  pl.BlockSpec((None, 1, d, de), lambda i, be, nu, nv: (layer, be[i], 0, 0)),
                  pl.BlockSpec((None, 1, d, de), lambda i, be, nu, nv: (layer, be[i], 0, 0)),
                  pl.BlockSpec((None, 1, de, d), lambda i, be, nu, nv: (layer, be[i], 0, 0))],
        out_specs=pl.BlockSpec((MOE_BLOCK, half), lambda i, be, nu, nv: (i, 0)),
        scratch_shapes=[pltpu.VMEM((d, de), BF16), pltpu.VMEM((d, de), BF16), pltpu.VMEM((de, d), BF16)],
    )
    return pl.pallas_call(
        _expert_kernel,
        grid_spec=grid_spec,
        out_shape=jax.ShapeDtypeStruct((rows, half), jnp.int32),
        compiler_params=_cparams(("arbitrary",)),
        name="expert_ffn",
    )(blk_e, n_used, n_valid, x_disp, w_gate, w_up, w_down)


def _moe_residual(h_ref, y0_ref, y1_ref, wt_ref, mod_ref, gate_idx):
    wt = wt_ref[0]
    moe = wt[:, 0:1] * _unpack_bf16_pairs(y0_ref[0, 0]) + wt[:, 1:2] * _unpack_bf16_pairs(y1_ref[0, 0])
    return h_ref[0] + mod_ref[0, gate_idx:gate_idx + 1, :] * moe


def _combine_next_kernel(h_ref, y0_ref, y1_ref, wt_ref, mod_ref, g_ref, nmod_ref, o_ref, hm_ref, *, gate_idx):
    h_new = _moe_residual(h_ref, y0_ref, y1_ref, wt_ref, mod_ref, gate_idx)
    o_ref[0] = h_new
    hm_ref[0] = _rms_mod(h_new, g_ref[...], nmod_ref[0, 0:1, :], nmod_ref[0, 1:2, :]).astype(hm_ref.dtype)


def _combine_final_kernel(h_ref, y0_ref, y1_ref, wt_ref, mod_ref, g_ref, o_ref, *, gate_idx):
    x = _moe_residual(h_ref, y0_ref, y1_ref, wt_ref, mod_ref, gate_idx)
    ms = jnp.mean(x * x, axis=-1, keepdims=True)
    o_ref[0] = x * lax.rsqrt(ms + NORM_EPS) * g_ref[...]


def combine(h, y_pairs, wt, modtab, gate_idx, next_g, next_modtab, ctx_len):
    b, p, d = h.shape
    last = next_modtab is None
    skip = ctx_len // ROW_TILE if last else 0
    row = pl.BlockSpec((1, ROW_TILE, d), lambda b, j: (b, j + skip, 0))
    mod = pl.BlockSpec((1, N_MOD, d), lambda b, j: (2 * b + jnp.minimum(j + skip, 1), 0, 0))
    in_specs = [row,
                pl.BlockSpec((1, 1, ROW_TILE, d // 2), lambda b, j: (0, b, j + skip, 0)),
                pl.BlockSpec((1, 1, ROW_TILE, d // 2), lambda b, j: (1, b, j + skip, 0)),
                pl.BlockSpec((1, ROW_TILE, LANES), lambda b, j: (b, j + skip, 0)),
                mod, pl.BlockSpec((1, d), lambda b, j: (0, 0))]
    args = [h, y_pairs, y_pairs, wt, modtab.reshape(b * 2, N_MOD, d), next_g.reshape(1, d)]
    out_row = pl.BlockSpec((1, ROW_TILE, d), lambda b, j: (b, j, 0))
    if last:
        return pl.pallas_call(
            functools.partial(_combine_final_kernel, gate_idx=gate_idx),
            grid=(b, (p - ctx_len) // ROW_TILE),
            in_specs=in_specs,
            out_specs=out_row,
            out_shape=jax.ShapeDtypeStruct((b, p - ctx_len, d), F32),
            compiler_params=_cparams(("parallel", "parallel")),
            name="moe_combine_final",
        )(*args)
    return pl.pallas_call(
        functools.partial(_combine_next_kernel, gate_idx=gate_idx),
        grid=(b, p // ROW_TILE),
        in_specs=in_specs + [mod],
        out_specs=[out_row, out_row],
        out_shape=[jax.ShapeDtypeStruct((b, p, d), F32), jax.ShapeDtypeStruct((b, p, d), BF16)],
        compiler_params=_cparams(("parallel", "parallel")),
        name="moe_combine",
    )(*args, next_modtab.reshape(b * 2, N_MOD, d))


def hier_moe(h, f, route, wt, counts, modtab, gate_idx, w_gate, w_up, w_down, layer, next_g, next_modtab, ctx_len):
    b, p, d = h.shape
    n_tok = b * p
    n_assign = 2 * n_tok
    e1, e2, rank1, rank2 = route.reshape(n_tok, LANES)[:, :4].T
    cnt = counts[0, :N_EXPERTS].astype(jnp.int32)
    pcounts = (cnt + MOE_BLOCK - 1) // MOE_BLOCK * MOE_BLOCK
    pend = jnp.cumsum(pcounts)
    pstart = pend - pcounts
    dest_by_slot = jnp.concatenate([pstart[e1] + rank1, pstart[e2] + rank2]).astype(jnp.int32)
    n_blk = -(-n_assign // MOE_BLOCK) + N_EXPERTS
    rows_total = n_blk * MOE_BLOCK
    blk_row0 = jnp.arange(n_blk, dtype=jnp.int32) * MOE_BLOCK
    blk_e = jnp.minimum(jnp.sum((pend[None, :] <= blk_row0[:, None]).astype(jnp.int32), axis=1), N_EXPERTS - 1)
    n_valid = jnp.clip(cnt[blk_e] - (blk_row0 - pstart[blk_e]), 0, MOE_BLOCK).astype(jnp.int32)
    n_used = (pend[-1] // MOE_BLOCK).astype(jnp.int32).reshape(1)

    x_disp = scatter_rows(f.reshape(n_tok, d // 2), dest_by_slot, rows_total)
    y = expert_ffn(x_disp, blk_e, n_used, n_valid, w_gate, w_up, w_down, layer)
    y_pairs = gather_rows(y, dest_by_slot).reshape(2, b, p, d // 2)
    return combine(h, y_pairs, wt, modtab, gate_idx, next_g, next_modtab, ctx_len)


def kernel(x, c, ctx, c_ctx, mod_w, mod_b, norm_mix_g, norm_ffn_g, router_grp_w, router_grp_b, router_exp_w, router_exp_b, exp_w_gate, exp_w_up, exp_w_down, l0_na_w_qkv, l0_na_rpb, l0_na_w_o, l1_swa_w_qkv, l1_swa_sink, l1_swa_w_o, l2_mla_w_dq, l2_mla_q_norm_g, l2_mla_w_uq, l2_mla_w_dkv, l2_mla_kv_norm_g, l2_mla_w_ukv, l2_mla_w_o, l3_na_w_qkv, l3_na_rpb, l3_na_w_o, final_norm_g):
    b, s, d = x.shape
    lc = ctx.shape[1]
    n_heads = d // HEAD_DIM
    n_kv_heads = n_heads // 4
    depth = mod_w.shape[0]
    rows = s // GRID_W

    modtabs = modulation_tables(c, c_ctx, mod_w, mod_b)

    def scale_q_cols(w):
        n_q = n_heads * HEAD_DIM
        q_scale = HEAD_DIM ** -0.5 * LOG2E
        return jnp.concatenate([w[:, :n_q] * q_scale, w[:, n_q:]], axis=1).astype(BF16)

    def na_mixer(hm, w_qkv, rpb, tn, unroll):
        qkv = project(hm, scale_q_cols(w_qkv), tn=tn)
        return neighbourhood_attention(qkv, na_bias_table(rpb, rows), lc, n_heads, unroll)

    def swa_mixer(hm):
        cos, sin = rope_tables_full(s, lc)
        n_rope = (n_heads + n_kv_heads) * HEAD_DIM // 512
        qkv = project(hm, scale_q_cols(l1_swa_w_qkv), tn=512, rope=(cos, sin, 0, n_rope))
        return window_attention(qkv, l1_swa_sink, lc, n_heads, n_kv_heads)

    def mla_mixer(hm):
        q_rank = l2_mla_w_dq.shape[1]
        kv_rank = l2_mla_kv_norm_g.shape[0]
        cos, sin = rope_tables_mla(s, lc)
        w_down = jnp.concatenate([l2_mla_w_dq, l2_mla_w_dkv[:, :kv_rank],
                                  _spread_rope_cols(l2_mla_w_dkv[:, kv_rank:])], axis=1).astype(BF16)
        cq, ckv, kr = mla_down(hm, w_down, l2_mla_q_norm_g, l2_mla_kv_norm_g, cos, sin, q_rank, kv_rank)
        w_uq = l2_mla_w_uq.reshape(q_rank, n_heads, MLA_NOPE_DIM + MLA_ROPE_DIM)
        q_scale = (MLA_NOPE_DIM + MLA_ROPE_DIM) ** -0.5 * LOG2E
        half = MLA_ROPE_DIM // 2
        gap = jnp.zeros((q_rank, n_heads, half), l2_mla_w_uq.dtype)
        w_q = jnp.concatenate([w_uq[:, :, :MLA_NOPE_DIM], w_uq[:, :, MLA_NOPE_DIM:MLA_NOPE_DIM + half], gap,
                               w_uq[:, :, MLA_NOPE_DIM + half:], gap], axis=2).reshape(q_rank, n_heads * 2 * LANES)
        return latent_attention(cq, ckv, kr, (w_q * q_scale).astype(BF16), l2_mla_w_ukv.astype(BF16), cos, sin,
                                lc, n_heads)

    h, hm = join_norm_modulate(ctx, x, norm_mix_g[0], modtabs[0])
    for i in range(depth):
        modtab = modtabs[i]
        mixer = i % 3
        if mixer == 0:
            w_qkv, rpb, w_o = (l0_na_w_qkv, l0_na_rpb, l0_na_w_o) if i == 0 else (l3_na_w_qkv, l3_na_rpb, l3_na_w_o)
            y = na_mixer(hm, w_qkv, rpb, *((1024, 8) if i == 0 else (2048, 16)))
        elif mixer == 1:
            y, w_o = swa_mixer(hm), l1_swa_w_o
        else:
            y, w_o = mla_mixer(hm), l2_mla_w_o
        h, f, route, wt, counts = attn_out_route(y, w_o.astype(BF16), h, modtab, norm_ffn_g[i], router_grp_w[i],
                                                 router_grp_b[i], router_exp_w[i], router_exp_b[i], lc,
                                                 n_sub=1)
        if i + 1 < depth:
            h, hm = hier_moe(h, f, route, wt, counts, modtab, 5, exp_w_gate, exp_w_up, exp_w_down, i,
                             norm_mix_g[i + 1], modtabs[i + 1], lc)
        else:
            return hier_moe(h, f, route, wt, counts, modtab, 5, exp_w_gate, exp_w_up, exp_w_down, i,
                            final_norm_g, None, lc)
```

```python
import functools

import numpy as np
import jax
import jax.numpy as jnp
from jax import lax
from jax.experimental import pallas as pl
from jax.experimental.pallas import tpu as pltpu
from jax.experimental.pallas import tpu_sc as plsc

GRID_W = 64
HEAD_DIM = 128
ROPE_BASE = 10000.0
NORM_EPS = 1e-6
NEG_INF = -1e30
N_MOD = 6

NA_ROWS = 8
NA_COLS = 16
NA_TILE_ROWS = 4
NA_WIN_ROWS = 12

SWA_WINDOW = 128
SWA_TQ = 256
SWA_TK = 512

MLA_NOPE_DIM = 128
MLA_ROPE_DIM = 64
MLA_TQ = 512
MLA_TK = 512
LOG2E = 1.4426950408889634

N_GROUPS = 4
EXPERTS_PER_GROUP = 8
N_EXPERTS = N_GROUPS * EXPERTS_PER_GROUP
MOE_BLOCK = 512
GATHER_WINDOW = 32

LANES = 128
ROW_TILE = 256
VMEM_LIMIT = 56 * 1024 * 1024

BF16 = jnp.bfloat16
F32 = jnp.float32


def _cparams(sem):
    return pltpu.CompilerParams(dimension_semantics=sem, vmem_limit_bytes=VMEM_LIMIT)


def _dot(a, b):
    return jnp.dot(a, b, preferred_element_type=F32)


def _dot_nt(a, b):
    return lax.dot_general(a, b, (((1,), (1,)), ((), ())), preferred_element_type=F32)


def _mod_kernel(x_ref, w_ref, b_ref, o_ref):
    x = x_ref[...]
    sx = (x * jax.nn.sigmoid(x)).astype(BF16)
    o_ref[0] = _dot(sx, w_ref[0].astype(BF16)) + b_ref[0]


def modulation_tables(c, c_ctx, mod_w, mod_b):
    depth, d, n_out = mod_w.shape
    b = c.shape[0]
    rows = 16
    xin = jnp.zeros((rows, d), F32).at[:b].set(c).at[b].set(c_ctx)
    tn = 1024
    out = pl.pallas_call(
        _mod_kernel,
        grid=(depth, n_out // tn),
        in_specs=[pl.BlockSpec((rows, d), lambda i, j: (0, 0)),
                  pl.BlockSpec((1, d, tn), lambda i, j: (i, 0, j)),
                  pl.BlockSpec((1, 1, tn), lambda i, j: (i, 0, j))],
        out_specs=pl.BlockSpec((1, rows, tn), lambda i, j: (i, 0, j)),
        out_shape=jax.ShapeDtypeStruct((depth, rows, n_out), F32),
        compiler_params=_cparams(("parallel", "parallel")),
        name="adaln_mod",
    )(xin, mod_w, mod_b.reshape(depth, 1, n_out))
    lat = out[:, :b].reshape(depth, b, 1, N_MOD, d)
    ctx = jnp.broadcast_to(out[:, b].reshape(depth, 1, 1, N_MOD, d), (depth, b, 1, N_MOD, d))
    return jnp.concatenate([ctx, lat], axis=2)


def _rms_mod(x, g, shift, scale):
    ms = jnp.mean(x * x, axis=-1, keepdims=True)
    y = x * lax.rsqrt(ms + NORM_EPS) * g
    return y * (1.0 + scale) + shift


def _join_norm_mod_kernel(ctx_ref, x_ref, g_ref, mod_ref, h_ref, hm_ref, *, ctx_tiles):
    j = pl.program_id(1)

    def emit(src_ref):
        h_ref[0] = src_ref[0]
        hm_ref[0] = _rms_mod(src_ref[0], g_ref[...], mod_ref[0, 0:1, :], mod_ref[0, 1:2, :]).astype(hm_ref.dtype)

    @pl.when(j < ctx_tiles)
    def _():
        emit(ctx_ref)

    @pl.when(j >= ctx_tiles)
    def _():
        emit(x_ref)


def _route(logits):
    lane = lax.broadcasted_iota(jnp.int32, logits.shape, 1).astype(F32)
    big = float(LANES)

    def first_lane(mask):
        return jnp.min(jnp.where(mask, lane, big), axis=-1, keepdims=True)

    in_grp = lane < N_GROUPS
    lg = jnp.where(in_grp, logits, NEG_INF)
    m_g = jnp.max(lg, axis=-1, keepdims=True)
    g_idx = first_lane(in_grp & (lg == m_g))
    g_w = 1.0 / jnp.sum(jnp.where(in_grp, jnp.exp(lg - m_g), 0.0), axis=-1, keepdims=True)
    e_lo = N_GROUPS + g_idx * EXPERTS_PER_GROUP
    in_e = (lane >= e_lo) & (lane < e_lo + EXPERTS_PER_GROUP)
    le = jnp.where(in_e, logits, NEG_INF)
    m1 = jnp.max(le, axis=-1, keepdims=True)
    e1 = first_lane(in_e & (le == m1))
    s_e = jnp.sum(jnp.where(in_e, jnp.exp(le - m1), 0.0), axis=-1, keepdims=True)
    in_e2 = in_e & (lane != e1)
    le2 = jnp.where(in_e2, logits, NEG_INF)
    m2 = jnp.max(le2, axis=-1, keepdims=True)
    e2 = first_lane(in_e2 & (le2 == m2))
    p1 = 1.0 / s_e
    p2 = jnp.exp(m2 - m1) / s_e
    den = p1 + p2
    return ((e1 - N_GROUPS).astype(jnp.int32), (e2 - N_GROUPS).astype(jnp.int32),
            g_w * p1 / den, g_w * p2 / den)


def _pack_bf16_pairs(x):
    n = x.shape[1] // 2
    xb = x.astype(BF16).astype(F32)
    hi = lax.bitcast_convert_type(xb[:, :n], jnp.int32)
    lo = lax.bitcast_convert_type(xb[:, n:], jnp.int32)
    return (hi & jnp.int32(-65536)) | lax.shift_right_logical(lo, jnp.int32(16))


def _unpack_bf16_pairs(w):
    hi = lax.bitcast_convert_type(w & jnp.int32(-65536), F32)
    lo = lax.bitcast_convert_type(lax.shift_left(w, jnp.int32(16)), F32)
    return jnp.concatenate([hi, lo], axis=1)


def _route_and_rank(f, wr_ref, br_ref, run_ref):
    f_hi = f.astype(BF16)
    f_lo = (f - f_hi.astype(F32)).astype(BF16)
    hi_terms = _dot(f_hi, wr_ref[...])
    logits = (hi_terms[:, :LANES] + _dot(f_lo, wr_ref[:, :LANES]) + hi_terms[:, LANES:]) + br_ref[...]
    e1, e2, w1, w2 = _route(logits)
    lane = lax.broadcasted_iota(jnp.int32, logits.shape, 1)
    pick1, pick2 = lane == e1, lane == e2
    chosen = (pick1 | pick2).astype(F32)
    n_rows = chosen.shape[0]
    earlier = (lax.broadcasted_iota(jnp.int32, (n_rows, n_rows), 1)
               < lax.broadcasted_iota(jnp.int32, (n_rows, n_rows), 0)).astype(BF16)
    before = run_ref[...] + _dot(earlier, chosen.astype(BF16))
    rank1 = jnp.sum(jnp.where(pick1, before, 0.0), axis=-1, keepdims=True).astype(jnp.int32)
    rank2 = jnp.sum(jnp.where(pick2, before, 0.0), axis=-1, keepdims=True).astype(jnp.int32)
    run_ref[...] = run_ref[...] + jnp.sum(chosen, axis=0, keepdims=True)
    route = jnp.where(lane == 0, e1, jnp.where(lane == 1, e2, jnp.where(lane == 2, rank1,
                                                                         jnp.where(lane == 3, rank2, 0))))
    return route, jnp.where(lane == 0, w1, jnp.where(lane == 1, w2, 0.0))


def _attn_out_route_kernel(y_ref, w_ref, h_ref, mod_ref, g_ref, wr_ref, br_ref,
                           ho_ref, f_ref, route_ref, wt_ref, cnt_ref, run_ref, *, ctx_len, tm, n_sub):
    first = (pl.program_id(0) == 0) & (pl.program_id(1) == 0)

    @pl.when(first)
    def _():
        run_ref[...] = jnp.zeros_like(run_ref)

    sub = tm // n_sub
    for s in range(n_sub):
        rows = slice(s * sub, (s + 1) * sub)
        acc = _dot(y_ref[0, rows, :], w_ref[...])
        pos = pl.program_id(1) * tm + s * sub + lax.broadcasted_iota(jnp.int32, (sub, 1), 0)
        is_ctx = pos < ctx_len

        def mod_row(k):
            return jnp.where(is_ctx, mod_ref[0, 0, k:k + 1, :], mod_ref[0, 1, k:k + 1, :])

        h_new = h_ref[0, rows, :] + mod_row(2) * acc
        ho_ref[0, rows, :] = h_new
        f = _rms_mod(h_new, g_ref[...], mod_row(3), mod_row(4))
        f_ref[0, rows, :] = _pack_bf16_pairs(f)
        route, wt = _route_and_rank(f, wr_ref, br_ref, run_ref)
        route_ref[0, rows, :] = route
        wt_ref[0, rows, :] = wt
    cnt_ref[...] = run_ref[...]


def _mod_spec(d):
    return pl.BlockSpec((1, N_MOD, d), lambda b, j: (2 * b + jnp.minimum(j, 1), 0, 0))


def join_norm_modulate(ctx, x, g, modtab):
    b, lc, d = ctx.shape
    p = lc + x.shape[1]
    assert lc == ROW_TILE
    ctx_tiles = lc // ROW_TILE
    row = pl.BlockSpec((1, ROW_TILE, d), lambda b, j: (b, j, 0))
    return pl.pallas_call(
        functools.partial(_join_norm_mod_kernel, ctx_tiles=ctx_tiles),
        grid=(b, p // ROW_TILE),
        in_specs=[pl.BlockSpec((1, ROW_TILE, d), lambda b, j: (b, jnp.minimum(j, ctx_tiles - 1), 0)),
                  pl.BlockSpec((1, ROW_TILE, d), lambda b, j: (b, jnp.maximum(j - ctx_tiles, 0), 0)),
                  pl.BlockSpec((1, d), lambda b, j: (0, 0)), _mod_spec(d)],
        out_specs=[row, row],
        out_shape=[jax.ShapeDtypeStruct((b, p, d), F32), jax.ShapeDtypeStruct((b, p, d), BF16)],
        compiler_params=_cparams(("parallel", "parallel")),
        name="join_norm_mod",
    )(ctx, x, g.reshape(1, d), modtab.reshape(b * 2, N_MOD, d))


def attn_out_route(y, w_o, h, modtab, g, w_grp, b_grp, w_rt, b_rt, ctx_len, n_sub=2):
    b, p, d = h.shape
    n_r = N_GROUPS + N_EXPERTS
    wr = jnp.zeros((d, LANES), F32).at[:, :N_GROUPS].set(w_grp).at[:, N_GROUPS:n_r].set(w_rt)
    br = jnp.zeros((1, LANES), F32).at[0, :N_GROUPS].set(b_grp).at[0, N_GROUPS:n_r].set(b_rt)
    wr_hi = wr.astype(BF16)
    wr = jnp.concatenate([wr_hi, (wr - wr_hi.astype(F32)).astype(BF16)], axis=1)
    tm = 544 if p % 544 == 0 else ROW_TILE
    assert (tm // n_sub) % 16 == 0
    row = lambda width: pl.BlockSpec((1, tm, width), lambda b, i: (b, i, 0))
    once = lambda shape: pl.BlockSpec(shape, lambda b, i: (0,) * len(shape), pipeline_mode=pl.Buffered(1))
    return pl.pallas_call(
        functools.partial(_attn_out_route_kernel, ctx_len=ctx_len, tm=tm, n_sub=n_sub),
        grid=(b, p // tm),
        in_specs=[row(d), once((d, d)), row(d),
                  pl.BlockSpec((1, 2, N_MOD, d), lambda b, i: (b, 0, 0, 0)),
                  once((1, d)), once((d, 2 * LANES)), once((1, LANES))],
        out_specs=[row(d), row(d // 2), row(LANES), row(LANES), pl.BlockSpec((1, LANES), lambda b, i: (0, 0))],
        out_shape=[jax.ShapeDtypeStruct((b, p, d), F32),
                   jax.ShapeDtypeStruct((b, p, d // 2), jnp.int32),
                   jax.ShapeDtypeStruct((b, p, LANES), jnp.int32),
                   jax.ShapeDtypeStruct((b, p, LANES), F32),
                   jax.ShapeDtypeStruct((1, LANES), F32)],
        scratch_shapes=[pltpu.VMEM((1, LANES), F32)],
        compiler_params=_cparams(("arbitrary", "arbitrary")),
        name="attn_out_route",
    )(y, w_o, h, modtab, g.reshape(1, d), wr, br)


def _rope(acc, cos, sin):
    n_blk = acc.shape[1] // LANES
    outs = []
    for c in range(n_blk):
        x = acc[:, c * LANES:(c + 1) * LANES]
        outs.append(x * cos + pltpu.roll(x, LANES // 2, 1) * sin)
    return outs[0] if n_blk == 1 else jnp.concatenate(outs, axis=1)


def _proj_kernel(*refs, rope_lo, rope_hi):
    x_ref, w_ref = refs[0], refs[1]
    o_ref = refs[-1]
    acc = _dot(x_ref[0], w_ref[...])
    if rope_hi > rope_lo:
        cos_ref, sin_ref = refs[2], refs[3]
        j = pl.program_id(2)
        roped = (j >= rope_lo) & (j < rope_hi)

        @pl.when(roped)
        def _():
            o_ref[0] = _rope(acc, cos_ref[...], sin_ref[...]).astype(o_ref.dtype)

        @pl.when(jnp.logical_not(roped))
        def _():
            o_ref[0] = acc.astype(o_ref.dtype)
    else:
        o_ref[0] = acc.astype(o_ref.dtype)


def _row_tile(p):
    for cand in (1088, 1024, 544, 512, 272, 256, 128, 64, 32, 16):
        if p % cand == 0:
            return cand
    raise ValueError(p)


def project(x, w, *, tn=512, out_dtype=BF16, rope=None):
    b, p, k = x.shape
    n = w.shape[1]
    tm = _row_tile(p)
    tn = min(tn, n)
    assert n % tn == 0
    in_specs = [pl.BlockSpec((1, tm, k), lambda b, i, j: (b, i, 0)),
                pl.BlockSpec((k, tn), lambda b, i, j: (0, j))]
    args = [x, w]
    kw = dict(rope_lo=0, rope_hi=0)
    if rope is not None:
        cos, sin, lo, hi = rope
        in_specs += [pl.BlockSpec((tm, LANES), lambda b, i, j: (i, 0)),
                     pl.BlockSpec((tm, LANES), lambda b, i, j: (i, 0))]
        args += [cos, sin]
        kw.update(rope_lo=lo, rope_hi=hi)
    return pl.pallas_call(
        functools.partial(_proj_kernel, **kw),
        grid=(b, p // tm, n // tn),
        in_specs=in_specs,
        out_specs=pl.BlockSpec((1, tm, tn), lambda b, i, j: (b, i, j)),
        out_shape=jax.ShapeDtypeStruct((b, p, n), out_dtype),
        compiler_params=_cparams(("parallel", "parallel", "arbitrary")),
        name="project",
    )(*args)


def _axial_cos_sin(n, rot_dim):
    t = jnp.arange(n, dtype=jnp.int32)
    row = (t // GRID_W).astype(F32)
    col = (t % GRID_W).astype(F32)
    n_freq = rot_dim // 4
    inv = ROPE_BASE ** (-jnp.arange(n_freq, dtype=F32) / n_freq)
    ang = jnp.concatenate([row[:, None] * inv, col[:, None] * inv], axis=-1)
    return jnp.cos(ang), jnp.sin(ang)


def rope_tables_full(s, ctx_len):
    c, sn = _axial_cos_sin(s, HEAD_DIM)
    cos = jnp.concatenate([c, c], axis=1)
    sin = jnp.concatenate([-sn, sn], axis=1)
    ident_c = jnp.ones((ctx_len, LANES), F32)
    ident_s = jnp.zeros((ctx_len, LANES), F32)
    return jnp.concatenate([ident_c, cos], axis=0), jnp.concatenate([ident_s, sin], axis=0)


def rope_tables_mla(s, ctx_len):
    c, sn = _axial_cos_sin(s, MLA_ROPE_DIM)
    one = jnp.ones_like(c)
    zero = jnp.zeros_like(c)
    cos = jnp.concatenate([c, one, c, one], axis=1)
    sin = jnp.concatenate([-sn, zero, sn, zero], axis=1)
    ident_c = jnp.ones((ctx_len, LANES), F32)
    ident_s = jnp.zeros((ctx_len, LANES), F32)
    return jnp.concatenate([ident_c, cos], axis=0), jnp.concatenate([ident_s, sin], axis=0)


def _spread_rope_cols(w_rope):
    k = w_rope.shape[0]
    half = MLA_ROPE_DIM // 2
    z = jnp.zeros((k, half), w_rope.dtype)
    return jnp.concatenate([w_rope[:, :half], z, w_rope[:, half:], z], axis=1)


def _softmax_probs(parts, extra=None):
    m = _row_reduce(jnp.maximum, jnp.max, parts)
    if extra is not None:
        m = jnp.maximum(m, extra)
    return [jnp.exp2((s - m).astype(BF16)) for s in parts], m


def _with_ones(v_ref, vone_ref):
    vone_ref[:, 0:LANES] = v_ref[0]
    vone_ref[:, LANES:2 * LANES] = jnp.ones((vone_ref.shape[0], LANES), BF16)


def _normalise(acc, extra_den=None):
    den = acc[:, LANES:LANES + 1]
    if extra_den is not None:
        den = den + extra_den
    return acc[:, :LANES] * (1.0 / den)


def _row_reduce(combine, reduce, parts):
    blocks = [s[:, c:c + LANES] for s in parts for c in range(0, s.shape[1], LANES)]
    acc = blocks[0]
    for blk in blocks[1:]:
        acc = combine(acc, blk)
    return reduce(acc, axis=-1, keepdims=True)


def na_bias_table(rpb, rows):
    n_tiles = rows // NA_TILE_ROWS
    n_heads, _, n_dcol = rpb.shape
    drow, row_ok = [], []
    for tile in (0, 1, n_tiles - 1):
        kr0 = int(np.clip(NA_TILE_ROWS * tile - NA_ROWS // 2, 0, rows - NA_WIN_ROWS))
        r = NA_TILE_ROWS * tile + np.arange(NA_TILE_ROWS)
        r0 = np.clip(r - NA_ROWS // 2, 0, rows - NA_ROWS)
        krow = kr0 + np.arange(NA_WIN_ROWS)
        row_ok.append((krow[None, :] >= r0[:, None]) & (krow[None, :] < r0[:, None] + NA_ROWS))
        drow.append(np.clip(krow[None, :] - r[:, None] + NA_ROWS - 1, 0, 2 * NA_ROWS - 2))
    drow, row_ok = np.stack(drow), np.stack(row_ok)
    qc = np.arange(GRID_W)
    qcol0 = np.clip(qc - NA_COLS // 2, 0, GRID_W - NA_COLS)
    kc = np.arange(GRID_W)
    col_ok = (kc[None, :] >= qcol0[:, None]) & (kc[None, :] < qcol0[:, None] + NA_COLS)
    dcol = np.clip(kc[None, :] - qc[:, None] + NA_COLS - 1, 0, 2 * NA_COLS - 2)
    pick_col = jnp.asarray(dcol[None] == np.arange(n_dcol)[:, None, None], F32)
    slabs = jnp.einsum('hrd,dqk->hrqk', rpb.astype(F32), pick_col, precision=lax.Precision.HIGHEST)
    slabs = jnp.where(jnp.asarray(col_ok)[None, None], slabs * LOG2E, NEG_INF)
    masked = jnp.full((n_heads, GRID_W, GRID_W), NEG_INF, F32)
    pats = []
    for p in range(3):
        per_row = [jnp.stack([slabs[:, drow[p, a, m]] if row_ok[p, a, m] else masked
                              for m in range(NA_WIN_ROWS)], axis=2)
                   for a in range(NA_TILE_ROWS)]
        pats.append(jnp.stack(per_row, axis=1))
    vals = jnp.stack(pats, axis=1)
    return vals.reshape(n_heads, 3, NA_TILE_ROWS * GRID_W, NA_WIN_ROWS * GRID_W)


def _na_kernel(q_ref, k_ref, v_ref, bias_ref, o_ref, vone_ref, *, ctx_len, rows):
    tq = NA_TILE_ROWS * GRID_W
    tk = NA_WIN_ROWS * GRID_W
    n_tiles = rows // NA_TILE_ROWS
    lc = ctx_len
    _with_ones(v_ref, vone_ref)

    s = _dot_nt(q_ref[0, 0:lc, :], k_ref[0, 0:lc, :])
    (p,), _ = _softmax_probs([s])
    o_ref[0, 0:lc, :] = _normalise(_dot(p, vone_ref[0:lc, :])).astype(o_ref.dtype)

    for i in range(n_tiles):
        qs = lc + i * tq
        kr0 = min(max(NA_TILE_ROWS * i - NA_ROWS // 2, 0), rows - NA_WIN_ROWS)
        ks = lc + kr0 * GRID_W
        pat = 0 if i == 0 else (2 if i == n_tiles - 1 else 1)
        q = q_ref[0, qs:qs + tq, :]
        s_loc = _dot_nt(q, k_ref[0, ks:ks + tk, :]) + bias_ref[0, pat]
        s_ctx = _dot_nt(q, k_ref[0, 0:lc, :])
        (p_loc, p_ctx), _ = _softmax_probs([s_loc, s_ctx])
        acc = _dot(p_loc, vone_ref[ks:ks + tk, :]) + _dot(p_ctx, vone_ref[0:lc, :])
        o_ref[0, qs:qs + tq, :] = _normalise(acc).astype(o_ref.dtype)


def neighbourhood_attention(qkv, bias, ctx_len, n_heads):
    b, p, _ = qkv.shape
    rows = (p - ctx_len) // GRID_W
    assert rows % NA_TILE_ROWS == 0 and rows >= NA_WIN_ROWS
    tq, tk = NA_TILE_ROWS * GRID_W, NA_WIN_ROWS * GRID_W
    assert ctx_len % 16 == 0 and ctx_len % tq == 0
    blk = lambda off: pl.BlockSpec((1, p, HEAD_DIM), lambda h, b: (b, 0, off + h))
    return pl.pallas_call(
        functools.partial(_na_kernel, ctx_len=ctx_len, rows=rows),
        grid=(n_heads, b),
        in_specs=[blk(0), blk(n_heads), blk(2 * n_heads),
                  pl.BlockSpec((1, 3, tq, tk), lambda h, b: (h, 0, 0, 0))],
        out_specs=blk(0),
        out_shape=jax.ShapeDtypeStruct((b, p, n_heads * HEAD_DIM), BF16),
        scratch_shapes=[pltpu.VMEM((p, 2 * LANES), BF16)],
        compiler_params=_cparams(("parallel", "parallel")),
        name="na_attention",
    )(qkv, qkv, qkv, bias)


def _swa_kernel(sink_ref, q_ref, k_ref, v_ref, o_ref, vone_ref, *, ctx_len, seq, group):
    lc = ctx_len
    kvh = pl.program_id(1)
    _with_ones(v_ref, vone_ref)

    def stack_heads(q):
        return jnp.concatenate([q[:, g * HEAD_DIM:(g + 1) * HEAD_DIM] for g in range(group)], axis=0)

    def finish(parts_fn, n_q, pv_fn, store):
        ps_all, sink_terms = [], []
        for g in range(group):
            sink = sink_ref[kvh * group + g]
            ps, m = _softmax_probs(parts_fn(g), extra=sink)
            ps_all.append(ps)
            sink_terms.append(jnp.exp2(sink - m))
        n_parts = len(ps_all[0])
        stacked = [jnp.concatenate([ps_all[g][k] for g in range(group)], axis=0) for k in range(n_parts)]
        acc = pv_fn(stacked)
        store(jnp.concatenate([_normalise(acc[g * n_q:(g + 1) * n_q], sink_terms[g]) for g in range(group)], axis=1))

    s_c = _dot_nt(stack_heads(q_ref[0, 0:lc, :]), k_ref[0, 0:lc, :])

    def store_ctx(o):
        o_ref[0, 0:lc, :] = o.astype(o_ref.dtype)

    finish(lambda g: [s_c[g * lc:(g + 1) * lc]], lc,
           lambda st: _dot(st[0], vone_ref[0:lc, :]), store_ctx)

    n_tiles = seq // SWA_TQ

    def tile(t):
        q0 = t * SWA_TQ
        k0 = min(max(q0 - SWA_WINDOW, 0), seq - SWA_TK)
        qs, ks = lc + q0, lc + k0
        q4 = stack_heads(q_ref[0, qs:qs + SWA_TQ, :])
        s_loc = _dot_nt(q4, k_ref[0, ks:ks + SWA_TK, :])
        s_ctx = _dot_nt(q4, k_ref[0, 0:lc, :])
        dpos = (lax.broadcasted_iota(jnp.int32, (SWA_TQ, SWA_TK), 1)
                - lax.broadcasted_iota(jnp.int32, (SWA_TQ, SWA_TK), 0)) + (k0 - q0)
        valid = jnp.abs(dpos) <= SWA_WINDOW

        def parts(g):
            sl = slice(g * SWA_TQ, (g + 1) * SWA_TQ)
            return [jnp.where(valid, s_loc[sl], NEG_INF), s_ctx[sl]]

        def store(o):
            o_ref[0, qs:qs + SWA_TQ, :] = o.astype(o_ref.dtype)

        finish(parts, SWA_TQ,
               lambda st: _dot(st[0], vone_ref[ks:ks + SWA_TK, :]) + _dot(st[1], vone_ref[0:lc, :]),
               store)

    for t in range(n_tiles):
        tile(t)


def window_attention(qkv, sink, ctx_len, n_heads, n_kv_heads):
    b, p, _ = qkv.shape
    seq = p - ctx_len
    group = n_heads // n_kv_heads
    assert seq % SWA_TQ == 0 and seq >= SWA_TK and ctx_len % SWA_WINDOW == 0
    kv = lambda off: pl.BlockSpec((1, p, HEAD_DIM), lambda b, h: (b, 0, off + h))
    qo = pl.BlockSpec((1, p, group * HEAD_DIM), lambda b, h: (b, 0, h))
    return pl.pallas_call(
        functools.partial(_swa_kernel, ctx_len=ctx_len, seq=seq, group=group),
        grid=(b, n_kv_heads),
        in_specs=[pl.BlockSpec(memory_space=pltpu.SMEM), qo, kv(n_heads), kv(n_heads + n_kv_heads)],
        out_specs=qo,
        out_shape=jax.ShapeDtypeStruct((b, p, n_heads * HEAD_DIM), BF16),
        scratch_shapes=[pltpu.VMEM((p, 2 * LANES), BF16)],
        compiler_params=_cparams(("parallel", "parallel")),
        name="swa_attention",
    )(sink.astype(F32) * LOG2E, qkv, qkv, qkv)


def _mla_down_kernel(x_ref, w_ref, gq_ref, gkv_ref, cos_ref, sin_ref, cq_ref, ckv_ref, kr_ref, *, q_rank, kv_rank):
    acc = _dot(x_ref[0], w_ref[...])

    def rms(x, g):
        ms = jnp.mean(x * x, axis=-1, keepdims=True)
        return x * lax.rsqrt(ms + NORM_EPS) * g

    cq_ref[0] = rms(acc[:, :q_rank], gq_ref[...]).astype(cq_ref.dtype)
    ckv_ref[0] = rms(acc[:, q_rank:q_rank + kv_rank], gkv_ref[...]).astype(ckv_ref.dtype)
    kr_ref[0] = _rope(acc[:, q_rank + kv_rank:], cos_ref[...], sin_ref[...]).astype(kr_ref.dtype)


def mla_down(x, w_down, gq, gkv, cos, sin, q_rank, kv_rank):
    b, p, k = x.shape
    n = w_down.shape[1]
    tm = 544 if p % 544 == 0 else _row_tile(p)
    row = lambda width: pl.BlockSpec((1, tm, width), lambda b, i: (b, i, 0))
    return pl.pallas_call(
        functools.partial(_mla_down_kernel, q_rank=q_rank, kv_rank=kv_rank),
        grid=(b, p // tm),
        in_specs=[row(k), pl.BlockSpec((k, n), lambda b, i: (0, 0)),
                  pl.BlockSpec((1, q_rank), lambda b, i: (0, 0)),
                  pl.BlockSpec((1, kv_rank), lambda b, i: (0, 0)),
                  pl.BlockSpec((tm, LANES), lambda b, i: (i, 0)),
                  pl.BlockSpec((tm, LANES), lambda b, i: (i, 0))],
        out_specs=[row(q_rank), row(kv_rank), row(LANES)],
        out_shape=[jax.ShapeDtypeStruct((b, p, q_rank), BF16),
                   jax.ShapeDtypeStruct((b, p, kv_rank), BF16),
                   jax.ShapeDtypeStruct((b, p, LANES), BF16)],
        compiler_params=_cparams(("parallel", "parallel")),
        name="mla_down",
    )(x, w_down, gq.reshape(1, q_rank), gkv.reshape(1, kv_rank), cos, sin)


def _mla_kernel(cq_ref, ckv_ref, kr_ref, wq_ref, wkv_ref, cos_ref, sin_ref, o_ref, qcat_ref, kcat_ref, vone_ref, *,
                ctx_len):
    lc = ctx_len
    p_all = vone_ref.shape[0]
    kv = _dot(ckv_ref[0], wkv_ref[...])
    kcat_ref[0:LANES, :] = kv[:, :LANES].T.astype(BF16)
    kcat_ref[LANES:2 * LANES, :] = kr_ref[0].astype(F32).T.astype(BF16)
    vone_ref[:, 0:LANES] = kv[:, LANES:].astype(BF16)
    vone_ref[:, LANES:2 * LANES] = jnp.ones((p_all, LANES), BF16)
    q = _dot(cq_ref[0], wq_ref[...])
    qcat_ref[:, 0:LANES] = q[:, :LANES].astype(BF16)
    qcat_ref[:, LANES:2 * LANES] = _rope(q[:, LANES:], cos_ref[...], sin_ref[...]).astype(BF16)

    def attend(qs, n_q, chunks):
        q = qcat_ref[pl.ds(qs, n_q), :]
        m = jnp.full((n_q, 1), NEG_INF, F32)
        acc = jnp.zeros((n_q, 2 * LANES), F32)
        for c0, c1 in chunks:
            s = _dot(q, kcat_ref[:, c0:c1])
            m_new = jnp.maximum(m, _row_reduce(jnp.maximum, jnp.max, [s]))
            p = jnp.exp2((s - m_new).astype(BF16))
            acc = jnp.exp2(m - m_new) * acc + _dot(p, vone_ref[c0:c1, :])
            m = m_new
        o_ref[0, pl.ds(qs, n_q), :] = (acc[:, :LANES] * (1.0 / acc[:, LANES:LANES + 1])).astype(o_ref.dtype)

    attend(0, lc, [(0, lc)])
    all_chunks = [(0, lc)] + [(c, c + MLA_TK) for c in range(lc, p_all, MLA_TK)]

    for qs in range(lc, p_all, MLA_TQ):
        attend(qs, MLA_TQ, all_chunks)


def latent_attention(cq, ckv, kr, w_q, w_kv, cos, sin, ctx_len, n_heads):
    b, p, q_rank = cq.shape
    kv_rank = ckv.shape[2]
    assert (p - ctx_len) % MLA_TQ == 0 and (p - ctx_len) % MLA_TK == 0 and ctx_len % 16 == 0
    per_batch = lambda width: pl.BlockSpec((1, p, width), lambda b, h: (b, 0, 0), pipeline_mode=pl.Buffered(1))
    table = pl.BlockSpec((p, LANES), lambda b, h: (0, 0), pipeline_mode=pl.Buffered(1))
    return pl.pallas_call(
        functools.partial(_mla_kernel, ctx_len=ctx_len),
        grid=(b, n_heads),
        in_specs=[per_batch(q_rank), per_batch(kv_rank), per_batch(LANES),
                  pl.BlockSpec((q_rank, 2 * LANES), lambda b, h: (0, h)),
                  pl.BlockSpec((kv_rank, 2 * LANES), lambda b, h: (0, h)),
                  table, table],
        out_specs=pl.BlockSpec((1, p, LANES), lambda b, h: (b, 0, h)),
        out_shape=jax.ShapeDtypeStruct((b, p, n_heads * LANES), BF16),
        scratch_shapes=[pltpu.VMEM((p, 2 * LANES), BF16), pltpu.VMEM((2 * LANES, p), BF16),
                        pltpu.VMEM((p, 2 * LANES), BF16)],
        compiler_params=_cparams(("parallel", "parallel")),
        name="mla_attention",
    )(cq, ckv, kr, w_q, w_kv, cos, sin)


def gather_rows(src, idx):
    n = idx.shape[0]
    width = src.shape[1]
    win = GATHER_WINDOW
    info = plsc.get_sparse_core_info()
    n_workers = info.num_cores * info.num_subcores
    assert n % (n_workers * 2 * win) == 0
    per_worker = n // n_workers
    mesh = plsc.VectorSubcoreMesh(core_axis_name="core", subcore_axis_name="subcore")

    @functools.partial(
        pl.kernel, out_type=jax.ShapeDtypeStruct((n, width), src.dtype), mesh=mesh,
        scratch_types=[pltpu.VMEM((per_worker,), jnp.int32),
                       pltpu.VMEM((win, width), src.dtype), pltpu.VMEM((win, width), src.dtype),
                       pltpu.SemaphoreType.DMA, pltpu.SemaphoreType.DMA,
                       pltpu.SemaphoreType.DMA, pltpu.SemaphoreType.DMA],
        name="gather_rows")
    def gather(src_hbm, idx_hbm, out_hbm, idx_vmem, rows0, rows1, gsem0, gsem1, osem0, osem1):
        worker = lax.axis_index("subcore") * info.num_cores + lax.axis_index("core")
        base = worker * per_worker
        pltpu.sync_copy(idx_hbm.at[pl.ds(base, per_worker)], idx_vmem)

        def fetch(r, rows, sem):
            return pltpu.make_async_copy(src_hbm.at[idx_vmem.at[pl.ds(r, win)]], rows, sem)

        def flush(r, rows, sem):
            return pltpu.make_async_copy(rows, out_hbm.at[pl.ds(base + r, win)], sem)

        fetch(0, rows0, gsem0).start()

        @pl.loop(0, per_worker, step=2 * win)
        def _(r):
            fetch(r, rows0, gsem0).wait()

            @pl.when(r > 0)
            def _():
                flush(r - win, rows1, osem1).wait()

            fetch(r + win, rows1, gsem1).start()
            flush(r, rows0, osem0).start()
            fetch(r + win, rows1, gsem1).wait()
            flush(r, rows0, osem0).wait()

            @pl.when(r + 2 * win < per_worker)
            def _():
                fetch(r + 2 * win, rows0, gsem0).start()

            flush(r + win, rows1, osem1).start()

        flush(per_worker - win, rows1, osem1).wait()

    return gather(src, idx)


def scatter_rows(src, idx, n_out):
    n = idx.shape[0]
    n_src, width = src.shape
    win = GATHER_WINDOW
    info = plsc.get_sparse_core_info()
    n_workers = info.num_cores * info.num_subcores
    assert n % (n_workers * 2 * win) == 0
    per_worker = n // n_workers
    assert n_src % per_worker == 0
    n_steps = per_worker // win
    mesh = plsc.VectorSubcoreMesh(core_axis_name="core", subcore_axis_name="subcore")

    @functools.partial(
        pl.kernel, out_type=jax.ShapeDtypeStruct((n_out, width), src.dtype), mesh=mesh,
        scratch_types=[pltpu.VMEM((n_steps, win), jnp.int32),
                       pltpu.VMEM((win, width), src.dtype), pltpu.VMEM((win, width), src.dtype),
                       pltpu.SemaphoreType.DMA, pltpu.SemaphoreType.DMA,
                       pltpu.SemaphoreType.DMA, pltpu.SemaphoreType.DMA],
        name="scatter_rows")
    def scatter(src_hbm, idx_hbm, out_hbm, idx_vmem, rows0, rows1, lsem0, lsem1, ssem0, ssem1):
        worker = lax.axis_index("subcore") * info.num_cores + lax.axis_index("core")
        src_base = lax.rem(worker * per_worker, n_src)
        pltpu.sync_copy(idx_hbm.at[worker], idx_vmem)

        def load(j, rows, sem):
            return pltpu.make_async_copy(src_hbm.at[pl.ds(src_base + j * win, win)], rows, sem)

        def store(j, rows, sem):
            return pltpu.make_async_copy(rows, out_hbm.at[idx_vmem.at[j]], sem)

        load(0, rows0, lsem0).start()

        @pl.loop(0, n_steps, step=2)
        def _(j):
            load(j, rows0, lsem0).wait()

            @pl.when(j > 0)
            def _():
                store(j - 1, rows1, ssem1).wait()

            load(j + 1, rows1, lsem1).start()
            store(j, rows0, ssem0).start()
            load(j + 1, rows1, lsem1).wait()
            store(j, rows0, ssem0).wait()

            @pl.when(j + 2 < n_steps)
            def _():
                load(j + 2, rows0, lsem0).start()

            store(j + 1, rows1, ssem1).start()

        store(n_steps - 1, rows1, ssem1).wait()

    return scatter(src, idx.reshape(n_workers, n_steps, win))


def _expert_kernel(blk_e_ref, n_used_ref, n_valid_ref, x_ref, wg_ref, wu_ref, wd_ref, y_ref, wg_bf, wu_bf, wd_bf):
    i = pl.program_id(0)
    used = i < n_used_ref[0]
    new_expert = (i == 0) | (blk_e_ref[i] != blk_e_ref[jnp.maximum(i - 1, 0)])

    @pl.when(used & new_expert)
    def _():
        wg_bf[...] = wg_ref[0].astype(BF16)
        wu_bf[...] = wu_ref[0].astype(BF16)
        wd_bf[...] = wd_ref[0].astype(BF16)

    @pl.when(used)
    def _():
        row = lax.broadcasted_iota(jnp.int32, (x_ref.shape[0], 1), 0)
        x = _unpack_bf16_pairs(jnp.where(row < n_valid_ref[i], x_ref[...], 0)).astype(BF16)
        g = _dot(x, wg_bf[...])
        u = _dot(x, wu_bf[...])
        a = (g * jax.nn.sigmoid(g) * u).astype(BF16)
        y_ref[...] = _pack_bf16_pairs(_dot(a, wd_bf[...]))

    @pl.when(jnp.logical_not(used))
    def _():
        y_ref[...] = jnp.zeros_like(y_ref)


def expert_ffn(x_disp, blk_e, n_used, n_valid, w_gate, w_up, w_down, layer):
    rows, half = x_disp.shape
    d = 2 * half
    n_blk = rows // MOE_BLOCK
    de = w_gate.shape[3]
    grid_spec = pltpu.PrefetchScalarGridSpec(
        num_scalar_prefetch=3,
        grid=(n_blk,),
        in_specs=[pl.BlockSpec((MOE_BLOCK, half), lambda i, be, nu, nv: (i, 0)),
                  pl.BlockSpec((None, 1, d, de), lambda i, be, nu, nv: (layer, be[i], 0, 0)),
                  pl.BlockSpec((None, 1, d, de), lambda i, be, nu, nv: (layer, be[i], 0, 0)),
                  pl.BlockSpec((None, 1, de, d), lambda i, be, nu, nv: (layer, be[i], 0, 0))],
        out_specs=pl.BlockSpec((MOE_BLOCK, half), lambda i, be, nu, nv: (i, 0)),
        scratch_shapes=[pltpu.VMEM((d, de), BF16), pltpu.VMEM((d, de), BF16), pltpu.VMEM((de, d), BF16)],
    )
    return pl.pallas_call(
        _expert_kernel,
        grid_spec=grid_spec,
        out_shape=jax.ShapeDtypeStruct((rows, half), jnp.int32),
        compiler_params=_cparams(("arbitrary",)),
        name="expert_ffn",
    )(blk_e, n_used, n_valid, x_disp, w_gate, w_up, w_down)


def _moe_residual(h_ref, y0_ref, y1_ref, wt_ref, mod_ref, gate_idx):
    wt = wt_ref[0]
    moe = wt[:, 0:1] * _unpack_bf16_pairs(y0_ref[0, 0]) + wt[:, 1:2] * _unpack_bf16_pairs(y1_ref[0, 0])
    return h_ref[0] + mod_ref[0, gate_idx:gate_idx + 1, :] * moe


def _combine_next_kernel(h_ref, y0_ref, y1_ref, wt_ref, mod_ref, g_ref, nmod_ref, o_ref, hm_ref, *, gate_idx):
    h_new = _moe_residual(h_ref, y0_ref, y1_ref, wt_ref, mod_ref, gate_idx)
    o_ref[0] = h_new
    hm_ref[0] = _rms_mod(h_new, g_ref[...], nmod_ref[0, 0:1, :], nmod_ref[0, 1:2, :]).astype(hm_ref.dtype)


def _combine_final_kernel(h_ref, y0_ref, y1_ref, wt_ref, mod_ref, g_ref, o_ref, *, gate_idx):
    x = _moe_residual(h_ref, y0_ref, y1_ref, wt_ref, mod_ref, gate_idx)
    ms = jnp.mean(x * x, axis=-1, keepdims=True)
    o_ref[0] = x * lax.rsqrt(ms + NORM_EPS) * g_ref[...]


def combine(h, y_pairs, wt, modtab, gate_idx, next_g, next_modtab, ctx_len):
    b, p, d = h.shape
    last = next_modtab is None
    skip = ctx_len // ROW_TILE if last else 0
    row = pl.BlockSpec((1, ROW_TILE, d), lambda b, j: (b, j + skip, 0))
    mod = pl.BlockSpec((1, N_MOD, d), lambda b, j: (2 * b + jnp.minimum(j + skip, 1), 0, 0))
    in_specs = [row,
                pl.BlockSpec((1, 1, ROW_TILE, d // 2), lambda b, j: (0, b, j + skip, 0)),
                pl.BlockSpec((1, 1, ROW_TILE, d // 2), lambda b, j: (1, b, j + skip, 0)),
                pl.BlockSpec((1, ROW_TILE, LANES), lambda b, j: (b, j + skip, 0)),
                mod, pl.BlockSpec((1, d), lambda b, j: (0, 0))]
    args = [h, y_pairs, y_pairs, wt, modtab.reshape(b * 2, N_MOD, d), next_g.reshape(1, d)]
    out_row = pl.BlockSpec((1, ROW_TILE, d), lambda b, j: (b, j, 0))
    if last:
        return pl.pallas_call(
            functools.partial(_combine_final_kernel, gate_idx=gate_idx),
            grid=(b, (p - ctx_len) // ROW_TILE),
            in_specs=in_specs,
            out_specs=out_row,
            out_shape=jax.ShapeDtypeStruct((b, p - ctx_len, d), F32),
            compiler_params=_cparams(("parallel", "parallel")),
            name="moe_combine_final",
        )(*args)
    return pl.pallas_call(
        functools.partial(_combine_next_kernel, gate_idx=gate_idx),
        grid=(b, p // ROW_TILE),
        in_specs=in_specs + [mod],
        out_specs=[out_row, out_row],
        out_shape=[jax.ShapeDtypeStruct((b, p, d), F32), jax.ShapeDtypeStruct((b, p, d), BF16)],
        compiler_params=_cparams(("parallel", "parallel")),
        name="moe_combine",
    )(*args, next_modtab.reshape(b * 2, N_MOD, d))


def hier_moe(h, f, route, wt, counts, modtab, gate_idx, w_gate, w_up, w_down, layer, next_g, next_modtab, ctx_len):
    b, p, d = h.shape
    n_tok = b * p
    n_assign = 2 * n_tok
    e1, e2, rank1, rank2 = route.reshape(n_tok, LANES)[:, :4].T
    cnt = counts[0, :N_EXPERTS].astype(jnp.int32)
    pcounts = (cnt + MOE_BLOCK - 1) // MOE_BLOCK * MOE_BLOCK
    pend = jnp.cumsum(pcounts)
    pstart = pend - pcounts
    dest_by_slot = jnp.concatenate([pstart[e1] + rank1, pstart[e2] + rank2]).astype(jnp.int32)
    n_blk = -(-n_assign // MOE_BLOCK) + N_EXPERTS
    rows_total = n_blk * MOE_BLOCK
    blk_row0 = jnp.arange(n_blk, dtype=jnp.int32) * MOE_BLOCK
    blk_e = jnp.minimum(jnp.sum((pend[None, :] <= blk_row0[:, None]).astype(jnp.int32), axis=1), N_EXPERTS - 1)
    n_valid = jnp.clip(cnt[blk_e] - (blk_row0 - pstart[blk_e]), 0, MOE_BLOCK).astype(jnp.int32)
    n_used = (pend[-1] // MOE_BLOCK).astype(jnp.int32).reshape(1)

    x_disp = scatter_rows(f.reshape(n_tok, d // 2), dest_by_slot, rows_total)
    y = expert_ffn(x_disp, blk_e, n_used, n_valid, w_gate, w_up, w_down, layer)
    y_pairs = gather_rows(y, dest_by_slot).reshape(2, b, p, d // 2)
    return combine(h, y_pairs, wt, modtab, gate_idx, next_g, next_modtab, ctx_len)


def kernel(x, c, ctx, c_ctx, mod_w, mod_b, norm_mix_g, norm_ffn_g, router_grp_w, router_grp_b, router_exp_w, router_exp_b, exp_w_gate, exp_w_up, exp_w_down, l0_na_w_qkv, l0_na_rpb, l0_na_w_o, l1_swa_w_qkv, l1_swa_sink, l1_swa_w_o, l2_mla_w_dq, l2_mla_q_norm_g, l2_mla_w_uq, l2_mla_w_dkv, l2_mla_kv_norm_g, l2_mla_w_ukv, l2_mla_w_o, l3_na_w_qkv, l3_na_rpb, l3_na_w_o, final_norm_g):
    b, s, d = x.shape
    lc = ctx.shape[1]
    n_heads = d // HEAD_DIM
    n_kv_heads = n_heads // 4
    depth = mod_w.shape[0]
    rows = s // GRID_W

    modtabs = modulation_tables(c, c_ctx, mod_w, mod_b)

    def scale_q_cols(w):
        n_q = n_heads * HEAD_DIM
        q_scale = HEAD_DIM ** -0.5 * LOG2E
        return jnp.concatenate([w[:, :n_q] * q_scale, w[:, n_q:]], axis=1).astype(BF16)

    def na_mixer(hm, w_qkv, rpb):
        qkv = project(hm, scale_q_cols(w_qkv), tn=2048)
        return neighbourhood_attention(qkv, na_bias_table(rpb, rows), lc, n_heads)

    def swa_mixer(hm):
        cos, sin = rope_tables_full(s, lc)
        n_rope = (n_heads + n_kv_heads) * HEAD_DIM // 512
        qkv = project(hm, scale_q_cols(l1_swa_w_qkv), tn=512, rope=(cos, sin, 0, n_rope))
        return window_attention(qkv, l1_swa_sink, lc, n_heads, n_kv_heads)

    def mla_mixer(hm):
        q_rank = l2_mla_w_dq.shape[1]
        kv_rank = l2_mla_kv_norm_g.shape[0]
        cos, sin = rope_tables_mla(s, lc)
        w_down = jnp.concatenate([l2_mla_w_dq, l2_mla_w_dkv[:, :kv_rank],
                                  _spread_rope_cols(l2_mla_w_dkv[:, kv_rank:])], axis=1).astype(BF16)
        cq, ckv, kr = mla_down(hm, w_down, l2_mla_q_norm_g, l2_mla_kv_norm_g, cos, sin, q_rank, kv_rank)
        w_uq = l2_mla_w_uq.reshape(q_rank, n_heads, MLA_NOPE_DIM + MLA_ROPE_DIM)
        q_scale = (MLA_NOPE_DIM + MLA_ROPE_DIM) ** -0.5 * LOG2E
        half = MLA_ROPE_DIM // 2
        gap = jnp.zeros((q_rank, n_heads, half), l2_mla_w_uq.dtype)
        w_q = jnp.concatenate([w_uq[:, :, :MLA_NOPE_DIM], w_uq[:, :, MLA_NOPE_DIM:MLA_NOPE_DIM + half], gap,
                               w_uq[:, :, MLA_NOPE_DIM + half:], gap], axis=2).reshape(q_rank, n_heads * 2 * LANES)
        return latent_attention(cq, ckv, kr, (w_q * q_scale).astype(BF16), l2_mla_w_ukv.astype(BF16), cos, sin,
                                lc, n_heads)

    h, hm = join_norm_modulate(ctx, x, norm_mix_g[0], modtabs[0])
    for i in range(depth):
        modtab = modtabs[i]
        mixer = i % 3
        if mixer == 0:
            w_qkv, rpb, w_o = (l0_na_w_qkv, l0_na_rpb, l0_na_w_o) if i == 0 else (l3_na_w_qkv, l3_na_rpb, l3_na_w_o)
            y = na_mixer(hm, w_qkv, rpb)
        elif mixer == 1:
            y, w_o = swa_mixer(hm), l1_swa_w_o
        else:
            y, w_o = mla_mixer(hm), l2_mla_w_o
        h, f, route, wt, counts = attn_out_route(y, w_o.astype(BF16), h, modtab, norm_ffn_g[i], router_grp_w[i],
                                                 router_grp_b[i], router_exp_w[i], router_exp_b[i], lc,
                                                 n_sub=1)
        if i + 1 < depth:
            h, hm = hier_moe(h, f, route, wt, counts, modtab, 5, exp_w_gate, exp_w_up, exp_w_down, i,
                             norm_mix_g[i + 1], modtabs[i + 1], lc)
        else:
            return hier_moe(h, f, route, wt, counts, modtab, 5, exp_w_gate, exp_w_up, exp_w_down, i,
                            final_norm_g, None, lc)
```

```python
import functools

import numpy as np
import jax
import jax.numpy as jnp
from jax import lax
from jax.experimental import pallas as pl
from jax.experimental.pallas import tpu as pltpu
from jax.experimental.pallas import tpu_sc as plsc

GRID_W = 64
HEAD_DIM = 128
ROPE_BASE = 10000.0
NORM_EPS = 1e-6
NEG_INF = -1e30
N_MOD = 6

NA_ROWS = 8
NA_COLS = 16
NA_TILE_ROWS = 4
NA_WIN_ROWS = 12

SWA_WINDOW = 128
SWA_TQ = 256
SWA_TK = 512

MLA_NOPE_DIM = 128
MLA_ROPE_DIM = 64
MLA_TQ = 1024
MLA_TK = 512
LOG2E = 1.4426950408889634

N_GROUPS = 4
EXPERTS_PER_GROUP = 8
N_EXPERTS = N_GROUPS * EXPERTS_PER_GROUP
MOE_BLOCK = 512
GATHER_WINDOW = 32

LANES = 128
ROW_TILE = 256
VMEM_LIMIT = 56 * 1024 * 1024

BF16 = jnp.bfloat16
F32 = jnp.float32


def _cparams(sem):
    return pltpu.CompilerParams(dimension_semantics=sem, vmem_limit_bytes=VMEM_LIMIT)


def _dot(a, b):
    return jnp.dot(a, b, preferred_element_type=F32)


def _dot_nt(a, b):
    return lax.dot_general(a, b, (((1,), (1,)), ((), ())), preferred_element_type=F32)


def _mod_kernel(x_ref, w_ref, b_ref, o_ref):
    x = x_ref[...]
    sx = (x * jax.nn.sigmoid(x)).astype(BF16)
    o_ref[0] = _dot(sx, w_ref[0].astype(BF16)) + b_ref[0]


def modulation_tables(c, c_ctx, mod_w, mod_b):
    depth, d, n_out = mod_w.shape
    b = c.shape[0]
    rows = 16
    xin = jnp.zeros((rows, d), F32).at[:b].set(c).at[b].set(c_ctx)
    tn = 1024
    out = pl.pallas_call(
        _mod_kernel,
        grid=(depth, n_out // tn),
        in_specs=[pl.BlockSpec((rows, d), lambda i, j: (0, 0)),
                  pl.BlockSpec((1, d, tn), lambda i, j: (i, 0, j)),
                  pl.BlockSpec((1, 1, tn), lambda i, j: (i, 0, j))],
        out_specs=pl.BlockSpec((1, rows, tn), lambda i, j: (i, 0, j)),
        out_shape=jax.ShapeDtypeStruct((depth, rows, n_out), F32),
        compiler_params=_cparams(("parallel", "parallel")),
        name="adaln_mod",
    )(xin, mod_w, mod_b.reshape(depth, 1, n_out))
    lat = out[:, :b].reshape(depth, b, 1, N_MOD, d)
    ctx = jnp.broadcast_to(out[:, b].reshape(depth, 1, 1, N_MOD, d), (depth, b, 1, N_MOD, d))
    return jnp.concatenate([ctx, lat], axis=2)


def _rms_mod(x, g, shift, scale):
    ms = jnp.mean(x * x, axis=-1, keepdims=True)
    y = x * lax.rsqrt(ms + NORM_EPS) * g
    return y * (1.0 + scale) + shift


def _join_norm_mod_kernel(ctx_ref, x_ref, g_ref, mod_ref, h_ref, hm_ref, *, ctx_tiles):
    j = pl.program_id(1)

    def emit(src_ref):
        h_ref[0] = src_ref[0]
        hm_ref[0] = _rms_mod(src_ref[0], g_ref[...], mod_ref[0, 0:1, :], mod_ref[0, 1:2, :]).astype(hm_ref.dtype)

    @pl.when(j < ctx_tiles)
    def _():
        emit(ctx_ref)

    @pl.when(j >= ctx_tiles)
    def _():
        emit(x_ref)


def _route(logits):
    lane = lax.broadcasted_iota(jnp.int32, logits.shape, 1).astype(F32)
    big = float(LANES)

    def first_lane(mask):
        return jnp.min(jnp.where(mask, lane, big), axis=-1, keepdims=True)

    in_grp = lane < N_GROUPS
    lg = jnp.where(in_grp, logits, NEG_INF)
    m_g = jnp.max(lg, axis=-1, keepdims=True)
    g_idx = first_lane(in_grp & (lg == m_g))
    g_w = 1.0 / jnp.sum(jnp.where(in_grp, jnp.exp(lg - m_g), 0.0), axis=-1, keepdims=True)
    e_lo = N_GROUPS + g_idx * EXPERTS_PER_GROUP
    in_e = (lane >= e_lo) & (lane < e_lo + EXPERTS_PER_GROUP)
    le = jnp.where(in_e, logits, NEG_INF)
    m1 = jnp.max(le, axis=-1, keepdims=True)
    e1 = first_lane(in_e & (le == m1))
    s_e = jnp.sum(jnp.where(in_e, jnp.exp(le - m1), 0.0), axis=-1, keepdims=True)
    in_e2 = in_e & (lane != e1)
    le2 = jnp.where(in_e2, logits, NEG_INF)
    m2 = jnp.max(le2, axis=-1, keepdims=True)
    e2 = first_lane(in_e2 & (le2 == m2))
    p1 = 1.0 / s_e
    p2 = jnp.exp(m2 - m1) / s_e
    den = p1 + p2
    return ((e1 - N_GROUPS).astype(jnp.int32), (e2 - N_GROUPS).astype(jnp.int32),
            g_w * p1 / den, g_w * p2 / den)


def _pack_bf16_pairs(x):
    n = x.shape[1] // 2
    xb = x.astype(BF16).astype(F32)
    hi = lax.bitcast_convert_type(xb[:, :n], jnp.int32)
    lo = lax.bitcast_convert_type(xb[:, n:], jnp.int32)
    return (hi & jnp.int32(-65536)) | lax.shift_right_logical(lo, jnp.int32(16))


def _unpack_bf16_pairs(w):
    hi = lax.bitcast_convert_type(w & jnp.int32(-65536), F32)
    lo = lax.bitcast_convert_type(lax.shift_left(w, jnp.int32(16)), F32)
    return jnp.concatenate([hi, lo], axis=1)


def _route_and_rank(f, wr_ref, br_ref, run_ref):
    f_hi = f.astype(BF16)
    f_lo = (f - f_hi.astype(F32)).astype(BF16)
    hi_terms = _dot(f_hi, wr_ref[...])
    logits = (hi_terms[:, :LANES] + _dot(f_lo, wr_ref[:, :LANES]) + hi_terms[:, LANES:]) + br_ref[...]
    e1, e2, w1, w2 = _route(logits)
    lane = lax.broadcasted_iota(jnp.int32, logits.shape, 1)
    pick1, pick2 = lane == e1, lane == e2
    chosen = (pick1 | pick2).astype(F32)
    n_rows = chosen.shape[0]
    earlier = (lax.broadcasted_iota(jnp.int32, (n_rows, n_rows), 1)
               < lax.broadcasted_iota(jnp.int32, (n_rows, n_rows), 0)).astype(BF16)
    before = run_ref[...] + _dot(earlier, chosen.astype(BF16))
    rank1 = jnp.sum(jnp.where(pick1, before, 0.0), axis=-1, keepdims=True).astype(jnp.int32)
    rank2 = jnp.sum(jnp.where(pick2, before, 0.0), axis=-1, keepdims=True).astype(jnp.int32)
    run_ref[...] = run_ref[...] + jnp.sum(chosen, axis=0, keepdims=True)
    route = jnp.where(lane == 0, e1, jnp.where(lane == 1, e2, jnp.where(lane == 2, rank1,
                                                                         jnp.where(lane == 3, rank2, 0))))
    return route, jnp.where(lane == 0, w1, jnp.where(lane == 1, w2, 0.0))


def _attn_out_route_kernel(y_ref, w_ref, h_ref, mod_ref, g_ref, wr_ref, br_ref,
                           ho_ref, f_ref, route_ref, wt_ref, cnt_ref, run_ref, *, ctx_len, tm, n_sub):
    first = (pl.program_id(0) == 0) & (pl.program_id(1) == 0)

    @pl.when(first)
    def _():
        run_ref[...] = jnp.zeros_like(run_ref)

    sub = tm // n_sub
    for s in range(n_sub):
        rows = slice(s * sub, (s + 1) * sub)
        acc = _dot(y_ref[0, rows, :], w_ref[...])
        pos = pl.program_id(1) * tm + s * sub + lax.broadcasted_iota(jnp.int32, (sub, 1), 0)
        is_ctx = pos < ctx_len

        def mod_row(k):
            return jnp.where(is_ctx, mod_ref[0, 0, k:k + 1, :], mod_ref[0, 1, k:k + 1, :])

        h_new = h_ref[0, rows, :] + mod_row(2) * acc
        ho_ref[0, rows, :] = h_new
        f = _rms_mod(h_new, g_ref[...], mod_row(3), mod_row(4))
        f_ref[0, rows, :] = _pack_bf16_pairs(f)
        route, wt = _route_and_rank(f, wr_ref, br_ref, run_ref)
        route_ref[0, rows, :] = route
        wt_ref[0, rows, :] = wt
    cnt_ref[...] = run_ref[...]


def _mod_spec(d):
    return pl.BlockSpec((1, N_MOD, d), lambda b, j: (2 * b + jnp.minimum(j, 1), 0, 0))


def join_norm_modulate(ctx, x, g, modtab):
    b, lc, d = ctx.shape
    p = lc + x.shape[1]
    assert lc == ROW_TILE
    ctx_tiles = lc // ROW_TILE
    row = pl.BlockSpec((1, ROW_TILE, d), lambda b, j: (b, j, 0))
    return pl.pallas_call(
        functools.partial(_join_norm_mod_kernel, ctx_tiles=ctx_tiles),
        grid=(b, p // ROW_TILE),
        in_specs=[pl.BlockSpec((1, ROW_TILE, d), lambda b, j: (b, jnp.minimum(j, ctx_tiles - 1), 0)),
                  pl.BlockSpec((1, ROW_TILE, d), lambda b, j: (b, jnp.maximum(j - ctx_tiles, 0), 0)),
                  pl.BlockSpec((1, d), lambda b, j: (0, 0)), _mod_spec(d)],
        out_specs=[row, row],
        out_shape=[jax.ShapeDtypeStruct((b, p, d), F32), jax.ShapeDtypeStruct((b, p, d), BF16)],
        compiler_params=_cparams(("parallel", "parallel")),
        name="join_norm_mod",
    )(ctx, x, g.reshape(1, d), modtab.reshape(b * 2, N_MOD, d))


def attn_out_route(y, w_o, h, modtab, g, w_grp, b_grp, w_rt, b_rt, ctx_len, n_sub=2):
    b, p, d = h.shape
    n_r = N_GROUPS + N_EXPERTS
    wr = jnp.zeros((d, LANES), F32).at[:, :N_GROUPS].set(w_grp).at[:, N_GROUPS:n_r].set(w_rt)
    br = jnp.zeros((1, LANES), F32).at[0, :N_GROUPS].set(b_grp).at[0, N_GROUPS:n_r].set(b_rt)
    wr_hi = wr.astype(BF16)
    wr = jnp.concatenate([wr_hi, (wr - wr_hi.astype(F32)).astype(BF16)], axis=1)
    tm = 544 if p % 544 == 0 else ROW_TILE
    assert (tm // n_sub) % 16 == 0
    row = lambda width: pl.BlockSpec((1, tm, width), lambda b, i: (b, i, 0))
    once = lambda shape: pl.BlockSpec(shape, lambda b, i: (0,) * len(shape), pipeline_mode=pl.Buffered(1))
    return pl.pallas_call(
        functools.partial(_attn_out_route_kernel, ctx_len=ctx_len, tm=tm, n_sub=n_sub),
        grid=(b, p // tm),
        in_specs=[row(d), once((d, d)), row(d),
                  pl.BlockSpec((1, 2, N_MOD, d), lambda b, i: (b, 0, 0, 0)),
                  once((1, d)), once((d, 2 * LANES)), once((1, LANES))],
        out_specs=[row(d), row(d // 2), row(LANES), row(LANES), pl.BlockSpec((1, LANES), lambda b, i: (0, 0))],
        out_shape=[jax.ShapeDtypeStruct((b, p, d), F32),
                   jax.ShapeDtypeStruct((b, p, d // 2), jnp.int32),
                   jax.ShapeDtypeStruct((b, p, LANES), jnp.int32),
                   jax.ShapeDtypeStruct((b, p, LANES), F32),
                   jax.ShapeDtypeStruct((1, LANES), F32)],
        scratch_shapes=[pltpu.VMEM((1, LANES), F32)],
        compiler_params=_cparams(("arbitrary", "arbitrary")),
        name="attn_out_route",
    )(y, w_o, h, modtab, g.reshape(1, d), wr, br)


def _rope(acc, cos, sin, n_rope_blocks=None):
    n_blk = acc.shape[1] // LANES
    n_rope_blocks = n_blk if n_rope_blocks is None else n_rope_blocks
    outs = []
    for c in range(n_blk):
        x = acc[:, c * LANES:(c + 1) * LANES]
        outs.append(x * cos + pltpu.roll(x, LANES // 2, 1) * sin if c < n_rope_blocks else x)
    return outs[0] if n_blk == 1 else jnp.concatenate(outs, axis=1)


def _proj_kernel(*refs, n_rope_cols, n_col_tiles):
    x_ref, w_ref = refs[0], refs[1]
    o_ref = refs[-1]
    acc = _dot(x_ref[0], w_ref[...])
    if n_rope_cols:
        cos_ref, sin_ref = refs[2], refs[3]
        tn = acc.shape[1]
        for jj in range(n_col_tiles):
            n_rope_blocks = min(max(n_rope_cols - jj * tn, 0), tn) // LANES

            @pl.when(pl.program_id(2) == jj)
            def _(n_rope_blocks=n_rope_blocks):
                o_ref[0] = _rope(acc, cos_ref[...], sin_ref[...], n_rope_blocks).astype(o_ref.dtype)
    else:
        o_ref[0] = acc.astype(o_ref.dtype)


def _row_tile(p):
    for cand in (1088, 1024, 544, 512, 272, 256, 128, 64, 32, 16):
        if p % cand == 0:
            return cand
    raise ValueError(p)


def project(x, w, *, tn=512, out_dtype=BF16, rope=None):
    b, p, k = x.shape
    n = w.shape[1]
    tm = _row_tile(p)
    tn = min(tn, n)
    assert n % tn == 0
    in_specs = [pl.BlockSpec((1, tm, k), lambda b, i, j: (b, i, 0)),
                pl.BlockSpec((k, tn), lambda b, i, j: (0, j))]
    args = [x, w]
    kw = dict(n_rope_cols=0, n_col_tiles=n // tn)
    if rope is not None:
        cos, sin, n_rope_cols = rope
        assert n_rope_cols % LANES == 0
        in_specs += [pl.BlockSpec((tm, LANES), lambda b, i, j: (i, 0)),
                     pl.BlockSpec((tm, LANES), lambda b, i, j: (i, 0))]
        args += [cos, sin]
        kw.update(n_rope_cols=n_rope_cols)
    return pl.pallas_call(
        functools.partial(_proj_kernel, **kw),
        grid=(b, p // tm, n // tn),
        in_specs=in_specs,
        out_specs=pl.BlockSpec((1, tm, tn), lambda b, i, j: (b, i, j)),
        out_shape=jax.ShapeDtypeStruct((b, p, n), out_dtype),
        compiler_params=_cparams(("parallel", "parallel", "arbitrary")),
        name="project",
    )(*args)


def _axial_cos_sin(n, rot_dim):
    t = jnp.arange(n, dtype=jnp.int32)
    row = (t // GRID_W).astype(F32)
    col = (t % GRID_W).astype(F32)
    n_freq = rot_dim // 4
    inv = ROPE_BASE ** (-jnp.arange(n_freq, dtype=F32) / n_freq)
    ang = jnp.concatenate([row[:, None] * inv, col[:, None] * inv], axis=-1)
    return jnp.cos(ang), jnp.sin(ang)


def rope_tables_full(s, ctx_len):
    c, sn = _axial_cos_sin(s, HEAD_DIM)
    cos = jnp.concatenate([c, c], axis=1)
    sin = jnp.concatenate([-sn, sn], axis=1)
    ident_c = jnp.ones((ctx_len, LANES), F32)
    ident_s = jnp.zeros((ctx_len, LANES), F32)
    return jnp.concatenate([ident_c, cos], axis=0), jnp.concatenate([ident_s, sin], axis=0)


def rope_tables_mla(s, ctx_len):
    c, sn = _axial_cos_sin(s, MLA_ROPE_DIM)
    one = jnp.ones_like(c)
    zero = jnp.zeros_like(c)
    cos = jnp.concatenate([c, one, c, one], axis=1)
    sin = jnp.concatenate([-sn, zero, sn, zero], axis=1)
    ident_c = jnp.ones((ctx_len, LANES), F32)
    ident_s = jnp.zeros((ctx_len, LANES), F32)
    return jnp.concatenate([ident_c, cos], axis=0), jnp.concatenate([ident_s, sin], axis=0)


def _spread_rope_cols(w_rope):
    k = w_rope.shape[0]
    half = MLA_ROPE_DIM // 2
    z = jnp.zeros((k, half), w_rope.dtype)
    return jnp.concatenate([w_rope[:, :half], z, w_rope[:, half:], z], axis=1)


def _softmax_probs(parts, extra=None):
    m = _row_reduce(jnp.maximum, jnp.max, parts)
    if extra is not None:
        m = jnp.maximum(m, extra)
    return [jnp.exp2((s - m).astype(BF16)) for s in parts], m


def _with_ones(v_ref, vone_ref):
    vone_ref[:, 0:LANES] = v_ref[0]
    vone_ref[:, LANES:2 * LANES] = jnp.ones((vone_ref.shape[0], LANES), BF16)


def _normalise(acc, extra_den=None):
    den = acc[:, LANES:LANES + 1]
    if extra_den is not None:
        den = den + extra_den
    return acc[:, :LANES] * (1.0 / den)


def _row_reduce(combine, reduce, parts):
    blocks = [s[:, c:c + LANES] for s in parts for c in range(0, s.shape[1], LANES)]
    acc = blocks[0]
    for blk in blocks[1:]:
        acc = combine(acc, blk)
    return reduce(acc, axis=-1, keepdims=True)


def na_bias_table(rpb, rows):
    n_tiles = rows // NA_TILE_ROWS
    n_heads, _, n_dcol = rpb.shape
    drow, row_ok = [], []
    for tile in (0, 1, n_tiles - 1):
        kr0 = int(np.clip(NA_TILE_ROWS * tile - NA_ROWS // 2, 0, rows - NA_WIN_ROWS))
        r = NA_TILE_ROWS * tile + np.arange(NA_TILE_ROWS)
        r0 = np.clip(r - NA_ROWS // 2, 0, rows - NA_ROWS)
        krow = kr0 + np.arange(NA_WIN_ROWS)
        row_ok.append((krow[None, :] >= r0[:, None]) & (krow[None, :] < r0[:, None] + NA_ROWS))
        drow.append(np.clip(krow[None, :] - r[:, None] + NA_ROWS - 1, 0, 2 * NA_ROWS - 2))
    drow, row_ok = np.stack(drow), np.stack(row_ok)
    qc = np.arange(GRID_W)
    qcol0 = np.clip(qc - NA_COLS // 2, 0, GRID_W - NA_COLS)
    kc = np.arange(GRID_W)
    col_ok = (kc[None, :] >= qcol0[:, None]) & (kc[None, :] < qcol0[:, None] + NA_COLS)
    dcol = np.clip(kc[None, :] - qc[:, None] + NA_COLS - 1, 0, 2 * NA_COLS - 2)
    pick_col = jnp.asarray(dcol[None] == np.arange(n_dcol)[:, None, None], F32)
    slabs = jnp.einsum('hrd,dqk->hrqk', rpb.astype(F32), pick_col, precision=lax.Precision.HIGHEST)
    slabs = jnp.where(jnp.asarray(col_ok)[None, None], slabs * LOG2E, NEG_INF)
    masked = jnp.full((n_heads, GRID_W, GRID_W), NEG_INF, F32)
    pats = []
    for p in range(3):
        per_row = [jnp.stack([slabs[:, drow[p, a, m]] if row_ok[p, a, m] else masked
                              for m in range(NA_WIN_ROWS)], axis=2)
                   for a in range(NA_TILE_ROWS)]
        pats.append(jnp.stack(per_row, axis=1))
    vals = jnp.stack(pats, axis=1)
    return vals.reshape(n_heads, 3, NA_TILE_ROWS * GRID_W, NA_WIN_ROWS * GRID_W)


def _na_kernel(q_ref, k_ref, v_ref, bias_ref, o_ref, vone_ref, *, ctx_len, rows):
    tq = NA_TILE_ROWS * GRID_W
    tk = NA_WIN_ROWS * GRID_W
    n_tiles = rows // NA_TILE_ROWS
    lc = ctx_len
    _with_ones(v_ref, vone_ref)

    s = _dot_nt(q_ref[0, 0:lc, :], k_ref[0, 0:lc, :])
    (p,), _ = _softmax_probs([s])
    o_ref[0, 0:lc, :] = _normalise(_dot(p, vone_ref[0:lc, :])).astype(o_ref.dtype)

    for i in range(n_tiles):
        qs = lc + i * tq
        kr0 = min(max(NA_TILE_ROWS * i - NA_ROWS // 2, 0), rows - NA_WIN_ROWS)
        ks = lc + kr0 * GRID_W
        pat = 0 if i == 0 else (2 if i == n_tiles - 1 else 1)
        q = q_ref[0, qs:qs + tq, :]
        s_loc = _dot_nt(q, k_ref[0, ks:ks + tk, :]) + bias_ref[0, pat]
        s_ctx = _dot_nt(q, k_ref[0, 0:lc, :])
        (p_loc, p_ctx), _ = _softmax_probs([s_loc, s_ctx])
        acc = _dot(p_loc, vone_ref[ks:ks + tk, :]) + _dot(p_ctx, vone_ref[0:lc, :])
        o_ref[0, qs:qs + tq, :] = _normalise(acc).astype(o_ref.dtype)


def neighbourhood_attention(qkv, bias, ctx_len, n_heads):
    b, p, _ = qkv.shape
    rows = (p - ctx_len) // GRID_W
    assert rows % NA_TILE_ROWS == 0 and rows >= NA_WIN_ROWS
    tq, tk = NA_TILE_ROWS * GRID_W, NA_WIN_ROWS * GRID_W
    assert ctx_len % 16 == 0 and ctx_len % tq == 0
    blk = lambda off: pl.BlockSpec((1, p, HEAD_DIM), lambda h, b: (b, 0, off + h))
    return pl.pallas_call(
        functools.partial(_na_kernel, ctx_len=ctx_len, rows=rows),
        grid=(n_heads, b),
        in_specs=[blk(0), blk(n_heads), blk(2 * n_heads),
                  pl.BlockSpec((1, 3, tq, tk), lambda h, b: (h, 0, 0, 0))],
        out_specs=blk(0),
        out_shape=jax.ShapeDtypeStruct((b, p, n_heads * HEAD_DIM), BF16),
        scratch_shapes=[pltpu.VMEM((p, 2 * LANES), BF16)],
        compiler_params=_cparams(("parallel", "parallel")),
        name="na_attention",
    )(qkv, qkv, qkv, bias)


def _swa_kernel(sink_ref, q_ref, k_ref, v_ref, o_ref, vone_ref, *, ctx_len, seq, group):
    lc = ctx_len
    kvh = pl.program_id(1)
    _with_ones(v_ref, vone_ref)

    def stack_heads(q):
        return jnp.concatenate([q[:, g * HEAD_DIM:(g + 1) * HEAD_DIM] for g in range(group)], axis=0)

    def finish(parts_fn, n_q, pv_fn, store):
        ps_all, sink_terms = [], []
        for g in range(group):
            sink = sink_ref[kvh * group + g]
            ps, m = _softmax_probs(parts_fn(g), extra=sink)
            ps_all.append(ps)
            sink_terms.append(jnp.exp2(sink - m))
        n_parts = len(ps_all[0])
        stacked = [jnp.concatenate([ps_all[g][k] for g in range(group)], axis=0) for k in range(n_parts)]
        acc = pv_fn(stacked)
        store(jnp.concatenate([_normalise(acc[g * n_q:(g + 1) * n_q], sink_terms[g]) for g in range(group)], axis=1))

    s_c = _dot_nt(stack_heads(q_ref[0, 0:lc, :]), k_ref[0, 0:lc, :])

    def store_ctx(o):
        o_ref[0, 0:lc, :] = o.astype(o_ref.dtype)

    finish(lambda g: [s_c[g * lc:(g + 1) * lc]], lc,
           lambda st: _dot(st[0], vone_ref[0:lc, :]), store_ctx)

    n_tiles = seq // SWA_TQ

    def tile(t):
        q0 = t * SWA_TQ
        k0 = min(max(q0 - SWA_WINDOW, 0), seq - SWA_TK)
        qs, ks = lc + q0, lc + k0
        q4 = stack_heads(q_ref[0, qs:qs + SWA_TQ, :])
        s_loc = _dot_nt(q4, k_ref[0, ks:ks + SWA_TK, :])
        s_ctx = _dot_nt(q4, k_ref[0, 0:lc, :])
        dpos = (lax.broadcasted_iota(jnp.int32, (SWA_TQ, SWA_TK), 1)
                - lax.broadcasted_iota(jnp.int32, (SWA_TQ, SWA_TK), 0)) + (k0 - q0)
        valid = jnp.abs(dpos) <= SWA_WINDOW

        def parts(g):
            sl = slice(g * SWA_TQ, (g + 1) * SWA_TQ)
            return [jnp.where(valid, s_loc[sl], NEG_INF), s_ctx[sl]]

        def store(o):
            o_ref[0, qs:qs + SWA_TQ, :] = o.astype(o_ref.dtype)

        finish(parts, SWA_TQ,
               lambda st: _dot(st[0], vone_ref[ks:ks + SWA_TK, :]) + _dot(st[1], vone_ref[0:lc, :]),
               store)

    for t in range(n_tiles):
        tile(t)


def window_attention(qkv, sink, ctx_len, n_heads, n_kv_heads):
    b, p, _ = qkv.shape
    seq = p - ctx_len
    group = n_heads // n_kv_heads
    assert seq % SWA_TQ == 0 and seq >= SWA_TK and ctx_len % SWA_WINDOW == 0
    kv = lambda off: pl.BlockSpec((1, p, HEAD_DIM), lambda b, h: (b, 0, off + h))
    qo = pl.BlockSpec((1, p, group * HEAD_DIM), lambda b, h: (b, 0, h))
    return pl.pallas_call(
        functools.partial(_swa_kernel, ctx_len=ctx_len, seq=seq, group=group),
        grid=(b, n_kv_heads),
        in_specs=[pl.BlockSpec(memory_space=pltpu.SMEM), qo, kv(n_heads), kv(n_heads + n_kv_heads)],
        out_specs=qo,
        out_shape=jax.ShapeDtypeStruct((b, p, n_heads * HEAD_DIM), BF16),
        scratch_shapes=[pltpu.VMEM((p, 2 * LANES), BF16)],
        compiler_params=_cparams(("parallel", "parallel")),
        name="swa_attention",
    )(sink.astype(F32) * LOG2E, qkv, qkv, qkv)


def _mla_down_kernel(x_ref, w_ref, gq_ref, gkv_ref, cos_ref, sin_ref, cq_ref, ckv_ref, kr_ref, *, q_rank, kv_rank):
    acc = _dot(x_ref[0], w_ref[...])

    def rms(x, g):
        ms = jnp.mean(x * x, axis=-1, keepdims=True)
        return x * lax.rsqrt(ms + NORM_EPS) * g

    cq_ref[0] = rms(acc[:, :q_rank], gq_ref[...]).astype(cq_ref.dtype)
    ckv_ref[0] = rms(acc[:, q_rank:q_rank + kv_rank], gkv_ref[...]).astype(ckv_ref.dtype)
    kr_ref[0] = _rope(acc[:, q_rank + kv_rank:], cos_ref[...], sin_ref[...]).astype(kr_ref.dtype)


def mla_down(x, w_down, gq, gkv, cos, sin, q_rank, kv_rank):
    b, p, k = x.shape
    n = w_down.shape[1]
    tm = 544 if p % 544 == 0 else _row_tile(p)
    row = lambda width: pl.BlockSpec((1, tm, width), lambda b, i: (b, i, 0))
    return pl.pallas_call(
        functools.partial(_mla_down_kernel, q_rank=q_rank, kv_rank=kv_rank),
        grid=(b, p // tm),
        in_specs=[row(k), pl.BlockSpec((k, n), lambda b, i: (0, 0)),
                  pl.BlockSpec((1, q_rank), lambda b, i: (0, 0)),
                  pl.BlockSpec((1, kv_rank), lambda b, i: (0, 0)),
                  pl.BlockSpec((tm, LANES), lambda b, i: (i, 0)),
                  pl.BlockSpec((tm, LANES), lambda b, i: (i, 0))],
        out_specs=[row(q_rank), row(kv_rank), row(LANES)],
        out_shape=[jax.ShapeDtypeStruct((b, p, q_rank), BF16),
                   jax.ShapeDtypeStruct((b, p, kv_rank), BF16),
                   jax.ShapeDtypeStruct((b, p, LANES), BF16)],
        compiler_params=_cparams(("parallel", "parallel")),
        name="mla_down",
    )(x, w_down, gq.reshape(1, q_rank), gkv.reshape(1, kv_rank), cos, sin)


def _mla_kernel(cq_ref, ckv_ref, kr_ref, wq_ref, wkv_ref, cos_ref, sin_ref, o_ref, qcat_ref, kcat_ref, vone_ref, *,
                ctx_len):
    lc = ctx_len
    p_all = vone_ref.shape[0]
    kv = _dot(ckv_ref[0], wkv_ref[...])
    kcat_ref[0:LANES, :] = kv[:, :LANES].T.astype(BF16)
    kcat_ref[LANES:2 * LANES, :] = kr_ref[0].astype(F32).T.astype(BF16)
    vone_ref[:, 0:LANES] = kv[:, LANES:].astype(BF16)
    vone_ref[:, LANES:2 * LANES] = jnp.ones((p_all, LANES), BF16)
    q = _dot(cq_ref[0], wq_ref[...])
    qcat_ref[:, 0:LANES] = q[:, :LANES].astype(BF16)
    qcat_ref[:, LANES:2 * LANES] = _rope(q[:, LANES:], cos_ref[...], sin_ref[...]).astype(BF16)

    def attend(qs, n_q, chunks):
        q = qcat_ref[pl.ds(qs, n_q), :]
        m = jnp.full((n_q, 1), NEG_INF, F32)
        acc = jnp.zeros((n_q, 2 * LANES), F32)
        for c0, c1 in chunks:
            s = _dot(q, kcat_ref[:, c0:c1])
            m_new = jnp.maximum(m, _row_reduce(jnp.maximum, jnp.max, [s]))
            p = jnp.exp2((s - m_new).astype(BF16))
            acc = jnp.exp2(m - m_new) * acc + _dot(p, vone_ref[c0:c1, :])
            m = m_new
        o_ref[0, pl.ds(qs, n_q), :] = (acc[:, :LANES] * (1.0 / acc[:, LANES:LANES + 1])).astype(o_ref.dtype)

    attend(0, lc, [(0, lc)])
    all_chunks = [(0, lc)] + [(c, c + MLA_TK) for c in range(lc, p_all, MLA_TK)]

    for qs in range(lc, p_all, MLA_TQ):
        attend(qs, MLA_TQ, all_chunks)


def latent_attention(cq, ckv, kr, w_q, w_kv, cos, sin, ctx_len, n_heads):
    b, p, q_rank = cq.shape
    kv_rank = ckv.shape[2]
    assert (p - ctx_len) % MLA_TQ == 0 and (p - ctx_len) % MLA_TK == 0 and ctx_len % 16 == 0
    per_batch = lambda width: pl.BlockSpec((1, p, width), lambda b, h: (b, 0, 0), pipeline_mode=pl.Buffered(1))
    table = pl.BlockSpec((p, LANES), lambda b, h: (0, 0), pipeline_mode=pl.Buffered(1))
    return pl.pallas_call(
        functools.partial(_mla_kernel, ctx_len=ctx_len),
        grid=(b, n_heads),
        in_specs=[per_batch(q_rank), per_batch(kv_rank), per_batch(LANES),
                  pl.BlockSpec((q_rank, 2 * LANES), lambda b, h: (0, h)),
                  pl.BlockSpec((kv_rank, 2 * LANES), lambda b, h: (0, h)),
                  table, table],
        out_specs=pl.BlockSpec((1, p, LANES), lambda b, h: (b, 0, h)),
        out_shape=jax.ShapeDtypeStruct((b, p, n_heads * LANES), BF16),
        scratch_shapes=[pltpu.VMEM((p, 2 * LANES), BF16), pltpu.VMEM((2 * LANES, p), BF16),
                        pltpu.VMEM((p, 2 * LANES), BF16)],
        compiler_params=_cparams(("parallel", "parallel")),
        name="mla_attention",
    )(cq, ckv, kr, w_q, w_kv, cos, sin)


def gather_rows(src, idx):
    n = idx.shape[0]
    width = src.shape[1]
    win = GATHER_WINDOW
    info = plsc.get_sparse_core_info()
    n_workers = info.num_cores * info.num_subcores
    assert n % (n_workers * 2 * win) == 0
    per_worker = n // n_workers
    mesh = plsc.VectorSubcoreMesh(core_axis_name="core", subcore_axis_name="subcore")

    @functools.partial(
        pl.kernel, out_type=jax.ShapeDtypeStruct((n, width), src.dtype), mesh=mesh,
        scratch_types=[pltpu.VMEM((per_worker,), jnp.int32),
                       pltpu.VMEM((win, width), src.dtype), pltpu.VMEM((win, width), src.dtype),
                       pltpu.SemaphoreType.DMA, pltpu.SemaphoreType.DMA,
                       pltpu.SemaphoreType.DMA, pltpu.SemaphoreType.DMA],
        name="gather_rows")
    def gather(src_hbm, idx_hbm, out_hbm, idx_vmem, rows0, rows1, gsem0, gsem1, osem0, osem1):
        worker = lax.axis_index("subcore") * info.num_cores + lax.axis_index("core")
        base = worker * per_worker
        pltpu.sync_copy(idx_hbm.at[pl.ds(base, per_worker)], idx_vmem)

        def fetch(r, rows, sem):
            return pltpu.make_async_copy(src_hbm.at[idx_vmem.at[pl.ds(r, win)]], rows, sem)

        def flush(r, rows, sem):
            return pltpu.make_async_copy(rows, out_hbm.at[pl.ds(base + r, win)], sem)

        fetch(0, rows0, gsem0).start()

        @pl.loop(0, per_worker, step=2 * win)
        def _(r):
            fetch(r, rows0, gsem0).wait()

            @pl.when(r > 0)
            def _():
                flush(r - win, rows1, osem1).wait()

            fetch(r + win, rows1, gsem1).start()
            flush(r, rows0, osem0).start()
            fetch(r + win, rows1, gsem1).wait()
            flush(r, rows0, osem0).wait()

            @pl.when(r + 2 * win < per_worker)
            def _():
                fetch(r + 2 * win, rows0, gsem0).start()

            flush(r + win, rows1, osem1).start()

        flush(per_worker - win, rows1, osem1).wait()

    return gather(src, idx)


def scatter_rows(src, idx, n_out):
    n = idx.shape[0]
    n_src, width = src.shape
    win = GATHER_WINDOW
    info = plsc.get_sparse_core_info()
    n_workers = info.num_cores * info.num_subcores
    assert n % (n_workers * 2 * win) == 0
    per_worker = n // n_workers
    assert n_src % per_worker == 0
    n_steps = per_worker // win
    mesh = plsc.VectorSubcoreMesh(core_axis_name="core", subcore_axis_name="subcore")

    @functools.partial(
        pl.kernel, out_type=jax.ShapeDtypeStruct((n_out, width), src.dtype), mesh=mesh,
        scratch_types=[pltpu.VMEM((n_steps, win), jnp.int32),
                       pltpu.VMEM((win, width), src.dtype), pltpu.VMEM((win, width), src.dtype),
                       pltpu.SemaphoreType.DMA, pltpu.SemaphoreType.DMA,
                       pltpu.SemaphoreType.DMA, pltpu.SemaphoreType.DMA],
        name="scatter_rows")
    def scatter(src_hbm, idx_hbm, out_hbm, idx_vmem, rows0, rows1, lsem0, lsem1, ssem0, ssem1):
        worker = lax.axis_index("subcore") * info.num_cores + lax.axis_index("core")
        src_base = lax.rem(worker * per_worker, n_src)
        pltpu.sync_copy(idx_hbm.at[worker], idx_vmem)

        def load(j, rows, sem):
            return pltpu.make_async_copy(src_hbm.at[pl.ds(src_base + j * win, win)], rows, sem)

        def store(j, rows, sem):
            return pltpu.make_async_copy(rows, out_hbm.at[idx_vmem.at[j]], sem)

        load(0, rows0, lsem0).start()

        @pl.loop(0, n_steps, step=2)
        def _(j):
            load(j, rows0, lsem0).wait()

            @pl.when(j > 0)
            def _():
                store(j - 1, rows1, ssem1).wait()

            load(j + 1, rows1, lsem1).start()
            store(j, rows0, ssem0).start()
            load(j + 1, rows1, lsem1).wait()
            store(j, rows0, ssem0).wait()

            @pl.when(j + 2 < n_steps)
            def _():
                load(j + 2, rows0, lsem0).start()

            store(j + 1, rows1, ssem1).start()

        store(n_steps - 1, rows1, ssem1).wait()

    return scatter(src, idx.reshape(n_workers, n_steps, win))


def _expert_kernel(blk_e_ref, n_used_ref, n_valid_ref, x_ref, wg_ref, wu_ref, wd_ref, y_ref, wg_bf, wu_bf, wd_bf):
    i = pl.program_id(0)
    used = i < n_used_ref[0]
    new_expert = (i == 0) | (blk_e_ref[i] != blk_e_ref[jnp.maximum(i - 1, 0)])

    @pl.when(used & new_expert)
    def _():
        wg_bf[...] = wg_ref[0].astype(BF16)
        wu_bf[...] = wu_ref[0].astype(BF16)
        wd_bf[...] = wd_ref[0].astype(BF16)

    @pl.when(used)
    def _():
        row = lax.broadcasted_iota(jnp.int32, (x_ref.shape[0], 1), 0)
        x = _unpack_bf16_pairs(jnp.where(row < n_valid_ref[i], x_ref[...], 0)).astype(BF16)
        g = _dot(x, wg_bf[...])
        u = _dot(x, wu_bf[...])
        a = (g * jax.nn.sigmoid(g) * u).astype(BF16)
        y_ref[...] = _pack_bf16_pairs(_dot(a, wd_bf[...]))

    @pl.when(jnp.logical_not(used))
    def _():
        y_ref[...] = jnp.zeros_like(y_ref)


def expert_ffn(x_disp, blk_e, n_used, n_valid, w_gate, w_up, w_down, layer):
    rows, half = x_disp.shape
    d = 2 * half
    n_blk = rows // MOE_BLOCK
    de = w_gate.shape[3]
    grid_spec = pltpu.PrefetchScalarGridSpec(
        num_scalar_prefetch=3,
        grid=(n_blk,),
        in_specs=[pl.BlockSpec((MOE_BLOCK, half), lambda i, be, nu, nv: (i, 0)),
                  pl.BlockSpec((None, 1, d, de), lambda i, be, nu, nv: (layer, be[i], 0, 0)),
                  pl.BlockSpec((None, 1, d, de), lambda i, be, nu, nv: (layer, be[i], 0, 0)),
                  pl.BlockSpec((None, 1, de, d), lambda i, be, nu, nv: (layer, be[i], 0, 0))],
        out_specs=pl.BlockSpec((MOE_BLOCK, half), lambda i, be, nu, nv: (i, 0)),
        scratch_shapes=[pltpu.VMEM((d, de), BF16), pltpu.VMEM((d, de), BF16), pltpu.VMEM((de, d), BF16)],
    )
    return pl.pallas_call(
        _expert_kernel,
        grid_spec=grid_spec,
        out_shape=jax.ShapeDtypeStruct((rows, half), jnp.int32),
        compiler_params=_cparams(("arbitrary",)),
        name="expert_ffn",
    )(blk_e, n_used, n_valid, x_disp, w_gate, w_up, w_down)


def _moe_residual(h_ref, y0_ref, y1_ref, wt_ref, mod_ref, gate_idx):
    wt = wt_ref[0]
    moe = wt[:, 0:1] * _unpack_bf16_pairs(y0_ref[0, 0]) + wt[:, 1:2] * _unpack_bf16_pairs(y1_ref[0, 0])
    return h_ref[0] + mod_ref[0, gate_idx:gate_idx + 1, :] * moe


def _combine_next_kernel(h_ref, y0_ref, y1_ref, wt_ref, mod_ref, g_ref, nmod_ref, o_ref, hm_ref, *, gate_idx):
    h_new = _moe_residual(h_ref, y0_ref, y1_ref, wt_ref, mod_ref, gate_idx)
    o_ref[0] = h_new
    hm_ref[0] = _rms_mod(h_new, g_ref[...], nmod_ref[0, 0:1, :], nmod_ref[0, 1:2, :]).astype(hm_ref.dtype)


def _combine_final_kernel(h_ref, y0_ref, y1_ref, wt_ref, mod_ref, g_ref, o_ref, *, gate_idx):
    x = _moe_residual(h_ref, y0_ref, y1_ref, wt_ref, mod_ref, gate_idx)
    ms = jnp.mean(x * x, axis=-1, keepdims=True)
    o_ref[0] = x * lax.rsqrt(ms + NORM_EPS) * g_ref[...]


def combine(h, y_pairs, wt, modtab, gate_idx, next_g, next_modtab, ctx_len):
    b, p, d = h.shape
    last = next_modtab is None
    skip = ctx_len // ROW_TILE if last else 0
    row = pl.BlockSpec((1, ROW_TILE, d), lambda b, j: (b, j + skip, 0))
    mod = pl.BlockSpec((1, N_MOD, d), lambda b, j: (2 * b + jnp.minimum(j + skip, 1), 0, 0))
    in_specs = [row,
                pl.BlockSpec((1, 1, ROW_TILE, d // 2), lambda b, j: (0, b, j + skip, 0)),
                pl.BlockSpec((1, 1, ROW_TILE, d // 2), lambda b, j: (1, b, j + skip, 0)),
                pl.BlockSpec((1, ROW_TILE, LANES), lambda b, j: (b, j + skip, 0)),
                mod, pl.BlockSpec((1, d), lambda b, j: (0, 0))]
    args = [h, y_pairs, y_pairs, wt, modtab.reshape(b * 2, N_MOD, d), next_g.reshape(1, d)]
    out_row = pl.BlockSpec((1, ROW_TILE, d), lambda b, j: (b, j, 0))
    if last:
        return pl.pallas_call(
            functools.partial(_combine_final_kernel, gate_idx=gate_idx),
            grid=(b, (p - ctx_len) // ROW_TILE),
            in_specs=in_specs,
            out_specs=out_row,
            out_shape=jax.ShapeDtypeStruct((b, p - ctx_len, d), F32),
            compiler_params=_cparams(("parallel", "parallel")),
            name="moe_combine_final",
        )(*args)
    return pl.pallas_call(
        functools.partial(_combine_next_kernel, gate_idx=gate_idx),
        grid=(b, p // ROW_TILE),
        in_specs=in_specs + [mod],
        out_specs=[out_row, out_row],
        out_shape=[jax.ShapeDtypeStruct((b, p, d), F32), jax.ShapeDtypeStruct((b, p, d), BF16)],
        compiler_params=_cparams(("parallel", "parallel")),
        name="moe_combine",
    )(*args, next_modtab.reshape(b * 2, N_MOD, d))


def hier_moe(h, f, route, wt, counts, modtab, gate_idx, w_gate, w_up, w_down, layer, next_g, next_modtab, ctx_len):
    b, p, d = h.shape
    n_tok = b * p
    n_assign = 2 * n_tok
    e1, e2, rank1, rank2 = route.reshape(n_tok, LANES)[:, :4].T
    cnt = counts[0, :N_EXPERTS].astype(jnp.int32)
    pcounts = (cnt + MOE_BLOCK - 1) // MOE_BLOCK * MOE_BLOCK
    pend = jnp.cumsum(pcounts)
    pstart = pend - pcounts
    dest_by_slot = jnp.concatenate([pstart[e1] + rank1, pstart[e2] + rank2]).astype(jnp.int32)
    n_blk = -(-n_assign // MOE_BLOCK) + N_EXPERTS
    rows_total = n_blk * MOE_BLOCK
    blk_row0 = jnp.arange(n_blk, dtype=jnp.int32) * MOE_BLOCK
    blk_e = jnp.minimum(jnp.sum((pend[None, :] <= blk_row0[:, None]).astype(jnp.int32), axis=1), N_EXPERTS - 1)
    n_valid = jnp.clip(cnt[blk_e] - (blk_row0 - pstart[blk_e]), 0, MOE_BLOCK).astype(jnp.int32)
    n_used = (pend[-1] // MOE_BLOCK).astype(jnp.int32).reshape(1)

    x_disp = scatter_rows(f.reshape(n_tok, d // 2), dest_by_slot, rows_total)
    y = expert_ffn(x_disp, blk_e, n_used, n_valid, w_gate, w_up, w_down, layer)
    y_pairs = gather_rows(y, dest_by_slot).reshape(2, b, p, d // 2)
    return combine(h, y_pairs, wt, modtab, gate_idx, next_g, next_modtab, ctx_len)


def kernel(x, c, ctx, c_ctx, mod_w, mod_b, norm_mix_g, norm_ffn_g, router_grp_w, router_grp_b, router_exp_w, router_exp_b, exp_w_gate, exp_w_up, exp_w_down, l0_na_w_qkv, l0_na_rpb, l0_na_w_o, l1_swa_w_qkv, l1_swa_sink, l1_swa_w_o, l2_mla_w_dq, l2_mla_q_norm_g, l2_mla_w_uq, l2_mla_w_dkv, l2_mla_kv_norm_g, l2_mla_w_ukv, l2_mla_w_o, l3_na_w_qkv, l3_na_rpb, l3_na_w_o, final_norm_g):
    b, s, d = x.shape
    lc = ctx.shape[1]
    n_heads = d // HEAD_DIM
    n_kv_heads = n_heads // 4
    depth = mod_w.shape[0]
    rows = s // GRID_W

    modtabs = modulation_tables(c, c_ctx, mod_w, mod_b)

    def scale_q_cols(w):
        n_q = n_heads * HEAD_DIM
        q_scale = HEAD_DIM ** -0.5 * LOG2E
        return jnp.concatenate([w[:, :n_q] * q_scale, w[:, n_q:]], axis=1).astype(BF16)

    def na_mixer(hm, w_qkv, rpb):
        qkv = project(hm, scale_q_cols(w_qkv), tn=2048)
        return neighbourhood_attention(qkv, na_bias_table(rpb, rows), lc, n_heads)

    def swa_mixer(hm):
        cos, sin = rope_tables_full(s, lc)
        qkv = project(hm, scale_q_cols(l1_swa_w_qkv), tn=1024, rope=(cos, sin, (n_heads + n_kv_heads) * HEAD_DIM))
        return window_attention(qkv, l1_swa_sink, lc, n_heads, n_kv_heads)

    def mla_mixer(hm):
        q_rank = l2_mla_w_dq.shape[1]
        kv_rank = l2_mla_kv_norm_g.shape[0]
        cos, sin = rope_tables_mla(s, lc)
        w_down = jnp.concatenate([l2_mla_w_dq, l2_mla_w_dkv[:, :kv_rank],
                                  _spread_rope_cols(l2_mla_w_dkv[:, kv_rank:])], axis=1).astype(BF16)
        cq, ckv, kr = mla_down(hm, w_down, l2_mla_q_norm_g, l2_mla_kv_norm_g, cos, sin, q_rank, kv_rank)
        w_uq = l2_mla_w_uq.reshape(q_rank, n_heads, MLA_NOPE_DIM + MLA_ROPE_DIM)
        q_scale = (MLA_NOPE_DIM + MLA_ROPE_DIM) ** -0.5 * LOG2E
        half = MLA_ROPE_DIM // 2
        gap = jnp.zeros((q_rank, n_heads, half), l2_mla_w_uq.dtype)
        w_q = jnp.concatenate([w_uq[:, :, :MLA_NOPE_DIM], w_uq[:, :, MLA_NOPE_DIM:MLA_NOPE_DIM + half], gap,
                               w_uq[:, :, MLA_NOPE_DIM + half:], gap], axis=2).reshape(q_rank, n_heads * 2 * LANES)
        return latent_attention(cq, ckv, kr, (w_q * q_scale).astype(BF16), l2_mla_w_ukv.astype(BF16), cos, sin,
                                lc, n_heads)

    h, hm = join_norm_modulate(ctx, x, norm_mix_g[0], modtabs[0])
    for i in range(depth):
        modtab = modtabs[i]
        mixer = i % 3
        if mixer == 0:
            w_qkv, rpb, w_o = (l0_na_w_qkv, l0_na_rpb, l0_na_w_o) if i == 0 else (l3_na_w_qkv, l3_na_rpb, l3_na_w_o)
            y = na_mixer(hm, w_qkv, rpb)
        elif mixer == 1:
            y, w_o = swa_mixer(hm), l1_swa_w_o
        else:
            y, w_o = mla_mixer(hm), l2_mla_w_o
        h, f, route, wt, counts = attn_out_route(y, w_o.astype(BF16), h, modtab, norm_ffn_g[i], router_grp_w[i],
                                                 router_grp_b[i], router_exp_w[i], router_exp_b[i], lc,
                                                 n_sub=1)
        if i + 1 < depth:
            h, hm = hier_moe(h, f, route, wt, counts, modtab, 5, exp_w_gate, exp_w_up, exp_w_down, i,
                             norm_mix_g[i + 1], modtabs[i + 1], lc)
        else:
            return hier_moe(h, f, route, wt, counts, modtab, 5, exp_w_gate, exp_w_up, exp_w_down, i,
                            final_norm_g, None, lc)
```

```python
import functools

import numpy as np
import jax
import jax.numpy as jnp
from jax import lax
from jax.experimental import pallas as pl
from jax.experimental.pallas import tpu as pltpu
from jax.experimental.pallas import tpu_sc as plsc

GRID_W = 64
HEAD_DIM = 128
ROPE_BASE = 10000.0
NORM_EPS = 1e-6
NEG_INF = -1e30
N_MOD = 6

NA_ROWS = 8
NA_COLS = 16
NA_TILE_ROWS = 4
NA_WIN_ROWS = 12

SWA_WINDOW = 128
SWA_TQ = 256
SWA_TK = 512

MLA_NOPE_DIM = 128
MLA_ROPE_DIM = 64
MLA_TQ = 1024
MLA_TK = 512
LOG2E = 1.4426950408889634

N_GROUPS = 4
EXPERTS_PER_GROUP = 8
N_EXPERTS = N_GROUPS * EXPERTS_PER_GROUP
MOE_BLOCK = 512
GATHER_WINDOW = 32

LANES = 128
ROW_TILE = 256
VMEM_LIMIT = 56 * 1024 * 1024

BF16 = jnp.bfloat16
F32 = jnp.float32


def _cparams(sem):
    return pltpu.CompilerParams(dimension_semantics=sem, vmem_limit_bytes=VMEM_LIMIT)


def _dot(a, b):
    return jnp.dot(a, b, preferred_element_type=F32)


def _dot_nt(a, b):
    return lax.dot_general(a, b, (((1,), (1,)), ((), ())), preferred_element_type=F32)


def _mod_kernel(x_ref, w_ref, b_ref, o_ref):
    x = x_ref[...]
    sx = (x * jax.nn.sigmoid(x)).astype(BF16)
    o_ref[0] = _dot(sx, w_ref[0].astype(BF16)) + b_ref[0]


def modulation_tables(c, c_ctx, mod_w, mod_b):
    depth, d, n_out = mod_w.shape
    b = c.shape[0]
    rows = 16
    xin = jnp.zeros((rows, d), F32).at[:b].set(c).at[b].set(c_ctx)
    tn = 1024
    out = pl.pallas_call(
        _mod_kernel,
        grid=(depth, n_out // tn),
        in_specs=[pl.BlockSpec((rows, d), lambda i, j: (0, 0)),
                  pl.BlockSpec((1, d, tn), lambda i, j: (i, 0, j)),
                  pl.BlockSpec((1, 1, tn), lambda i, j: (i, 0, j))],
        out_specs=pl.BlockSpec((1, rows, tn), lambda i, j: (i, 0, j)),
        out_shape=jax.ShapeDtypeStruct((depth, rows, n_out), F32),
        compiler_params=_cparams(("parallel", "parallel")),
        name="adaln_mod",
    )(xin, mod_w, mod_b.reshape(depth, 1, n_out))
    lat = out[:, :b].reshape(depth, b, 1, N_MOD, d)
    ctx = jnp.broadcast_to(out[:, b].reshape(depth, 1, 1, N_MOD, d), (depth, b, 1, N_MOD, d))
    return jnp.concatenate([ctx, lat], axis=2)


def _rms_mod(x, g, shift, scale):
    ms = jnp.mean(x * x, axis=-1, keepdims=True)
    y = x * lax.rsqrt(ms + NORM_EPS) * g
    return y * (1.0 + scale) + shift


def _join_norm_mod_kernel(ctx_ref, x_ref, g_ref, mod_ref, h_ref, hm_ref, *, ctx_tiles):
    j = pl.program_id(1)

    def emit(src_ref):
        h_ref[0] = src_ref[0]
        hm_ref[0] = _rms_mod(src_ref[0], g_ref[...], mod_ref[0, 0:1, :], mod_ref[0, 1:2, :]).astype(hm_ref.dtype)

    @pl.when(j < ctx_tiles)
    def _():
        emit(ctx_ref)

    @pl.when(j >= ctx_tiles)
    def _():
        emit(x_ref)


def _route(logits):
    lane = lax.broadcasted_iota(jnp.int32, logits.shape, 1).astype(F32)
    big = float(LANES)

    def first_lane(mask):
        return jnp.min(jnp.where(mask, lane, big), axis=-1, keepdims=True)

    in_grp = lane < N_GROUPS
    lg = jnp.where(in_grp, logits, NEG_INF)
    m_g = jnp.max(lg, axis=-1, keepdims=True)
    g_idx = first_lane(in_grp & (lg == m_g))
    g_w = 1.0 / jnp.sum(jnp.where(in_grp, jnp.exp(lg - m_g), 0.0), axis=-1, keepdims=True)
    e_lo = N_GROUPS + g_idx * EXPERTS_PER_GROUP
    in_e = (lane >= e_lo) & (lane < e_lo + EXPERTS_PER_GROUP)
    le = jnp.where(in_e, logits, NEG_INF)
    m1 = jnp.max(le, axis=-1, keepdims=True)
    e1 = first_lane(in_e & (le == m1))
    s_e = jnp.sum(jnp.where(in_e, jnp.exp(le - m1), 0.0), axis=-1, keepdims=True)
    in_e2 = in_e & (lane != e1)
    le2 = jnp.where(in_e2, logits, NEG_INF)
    m2 = jnp.max(le2, axis=-1, keepdims=True)
    e2 = first_lane(in_e2 & (le2 == m2))
    p1 = 1.0 / s_e
    p2 = jnp.exp(m2 - m1) / s_e
    den = p1 + p2
    return ((e1 - N_GROUPS).astype(jnp.int32), (e2 - N_GROUPS).astype(jnp.int32),
            g_w * p1 / den, g_w * p2 / den)


def _pack_bf16_pairs(x):
    n = x.shape[1] // 2
    xb = x.astype(BF16).astype(F32)
    hi = lax.bitcast_convert_type(xb[:, :n], jnp.int32)
    lo = lax.bitcast_convert_type(xb[:, n:], jnp.int32)
    return (hi & jnp.int32(-65536)) | lax.shift_right_logical(lo, jnp.int32(16))


def _unpack_bf16_pairs(w):
    hi = lax.bitcast_convert_type(w & jnp.int32(-65536), F32)
    lo = lax.bitcast_convert_type(lax.shift_left(w, jnp.int32(16)), F32)
    return jnp.concatenate([hi, lo], axis=1)


def _route_and_rank(f, wr_ref, br_ref, run_ref):
    f_hi = f.astype(BF16)
    f_lo = (f - f_hi.astype(F32)).astype(BF16)
    hi_terms = _dot(f_hi, wr_ref[...])
    logits = (hi_terms[:, :LANES] + _dot(f_lo, wr_ref[:, :LANES]) + hi_terms[:, LANES:]) + br_ref[...]
    e1, e2, w1, w2 = _route(logits)
    lane = lax.broadcasted_iota(jnp.int32, logits.shape, 1)
    pick1, pick2 = lane == e1, lane == e2
    chosen = (pick1 | pick2).astype(F32)
    n_rows = chosen.shape[0]
    earlier = (lax.broadcasted_iota(jnp.int32, (n_rows, n_rows), 1)
               < lax.broadcasted_iota(jnp.int32, (n_rows, n_rows), 0)).astype(BF16)
    before = run_ref[...] + _dot(earlier, chosen.astype(BF16))
    rank1 = jnp.sum(jnp.where(pick1, before, 0.0), axis=-1, keepdims=True).astype(jnp.int32)
    rank2 = jnp.sum(jnp.where(pick2, before, 0.0), axis=-1, keepdims=True).astype(jnp.int32)
    run_ref[...] = run_ref[...] + jnp.sum(chosen, axis=0, keepdims=True)
    route = jnp.where(lane == 0, e1, jnp.where(lane == 1, e2, jnp.where(lane == 2, rank1,
                                                                         jnp.where(lane == 3, rank2, 0))))
    return route, jnp.where(lane == 0, w1, jnp.where(lane == 1, w2, 0.0))


def _attn_out_route_kernel(y_ref, w_ref, h_ref, mod_ref, g_ref, wr_ref, br_ref,
                           ho_ref, f_ref, route_ref, wt_ref, cnt_ref, run_ref, *, ctx_len, tm, n_sub):
    first = (pl.program_id(0) == 0) & (pl.program_id(1) == 0)

    @pl.when(first)
    def _():
        run_ref[...] = jnp.zeros_like(run_ref)

    sub = tm // n_sub
    for s in range(n_sub):
        rows = slice(s * sub, (s + 1) * sub)
        acc = _dot(y_ref[0, rows, :], w_ref[...])
        pos = pl.program_id(1) * tm + s * sub + lax.broadcasted_iota(jnp.int32, (sub, 1), 0)
        is_ctx = pos < ctx_len

        def mod_row(k):
            return jnp.where(is_ctx, mod_ref[0, 0, k:k + 1, :], mod_ref[0, 1, k:k + 1, :])

        h_new = h_ref[0, rows, :] + mod_row(2) * acc
        ho_ref[0, rows, :] = h_new
        f = _rms_mod(h_new, g_ref[...], mod_row(3), mod_row(4))
        f_ref[0, rows, :] = _pack_bf16_pairs(f)
        route, wt = _route_and_rank(f, wr_ref, br_ref, run_ref)
        route_ref[0, rows, :] = route
        wt_ref[0, rows, :] = wt
    cnt_ref[...] = run_ref[...]


def _mod_spec(d):
    return pl.BlockSpec((1, N_MOD, d), lambda b, j: (2 * b + jnp.minimum(j, 1), 0, 0))


def join_norm_modulate(ctx, x, g, modtab):
    b, lc, d = ctx.shape
    p = lc + x.shape[1]
    assert lc == ROW_TILE
    ctx_tiles = lc // ROW_TILE
    row = pl.BlockSpec((1, ROW_TILE, d), lambda b, j: (b, j, 0))
    return pl.pallas_call(
        functools.partial(_join_norm_mod_kernel, ctx_tiles=ctx_tiles),
        grid=(b, p // ROW_TILE),
        in_specs=[pl.BlockSpec((1, ROW_TILE, d), lambda b, j: (b, jnp.minimum(j, ctx_tiles - 1), 0)),
                  pl.BlockSpec((1, ROW_TILE, d), lambda b, j: (b, jnp.maximum(j - ctx_tiles, 0), 0)),
                  pl.BlockSpec((1, d), lambda b, j: (0, 0)), _mod_spec(d)],
        out_specs=[row, row],
        out_shape=[jax.ShapeDtypeStruct((b, p, d), F32), jax.ShapeDtypeStruct((b, p, d), BF16)],
        compiler_params=_cparams(("parallel", "parallel")),
        name="join_norm_mod",
    )(ctx, x, g.reshape(1, d), modtab.reshape(b * 2, N_MOD, d))


def attn_out_route(y, w_o, h, modtab, g, w_grp, b_grp, w_rt, b_rt, ctx_len, n_sub=2):
    b, p, d = h.shape
    n_r = N_GROUPS + N_EXPERTS
    wr = jnp.zeros((d, LANES), F32).at[:, :N_GROUPS].set(w_grp).at[:, N_GROUPS:n_r].set(w_rt)
    br = jnp.zeros((1, LANES), F32).at[0, :N_GROUPS].set(b_grp).at[0, N_GROUPS:n_r].set(b_rt)
    wr_hi = wr.astype(BF16)
    wr = jnp.concatenate([wr_hi, (wr - wr_hi.astype(F32)).astype(BF16)], axis=1)
    tm = 544 if p % 544 == 0 else ROW_TILE
    assert (tm // n_sub) % 16 == 0
    row = lambda width: pl.BlockSpec((1, tm, width), lambda b, i: (b, i, 0))
    once = lambda shape: pl.BlockSpec(shape, lambda b, i: (0,) * len(shape), pipeline_mode=pl.Buffered(1))
    return pl.pallas_call(
        functools.partial(_attn_out_route_kernel, ctx_len=ctx_len, tm=tm, n_sub=n_sub),
        grid=(b, p // tm),
        in_specs=[row(d), once((d, d)), row(d),
                  pl.BlockSpec((1, 2, N_MOD, d), lambda b, i: (b, 0, 0, 0)),
                  once((1, d)), once((d, 2 * LANES)), once((1, LANES))],
        out_specs=[row(d), row(d // 2), row(LANES), row(LANES), pl.BlockSpec((1, LANES), lambda b, i: (0, 0))],
        out_shape=[jax.ShapeDtypeStruct((b, p, d), F32),
                   jax.ShapeDtypeStruct((b, p, d // 2), jnp.int32),
                   jax.ShapeDtypeStruct((b, p, LANES), jnp.int32),
                   jax.ShapeDtypeStruct((b, p, LANES), F32),
                   jax.ShapeDtypeStruct((1, LANES), F32)],
        scratch_shapes=[pltpu.VMEM((1, LANES), F32)],
        compiler_params=_cparams(("arbitrary", "arbitrary")),
        name="attn_out_route",
    )(y, w_o, h, modtab, g.reshape(1, d), wr, br)


def _rope(acc, cos, sin, n_rope_blocks=None):
    n_blk = acc.shape[1] // LANES
    n_rope_blocks = n_blk if n_rope_blocks is None else n_rope_blocks
    outs = []
    for c in range(n_blk):
        x = acc[:, c * LANES:(c + 1) * LANES]
        outs.append(x * cos + pltpu.roll(x, LANES // 2, 1) * sin if c < n_rope_blocks else x)
    return outs[0] if n_blk == 1 else jnp.concatenate(outs, axis=1)


def _proj_kernel(*refs, n_rope_cols, n_col_tiles):
    x_ref, w_ref = refs[0], refs[1]
    o_ref = refs[-1]
    acc = _dot(x_ref[0], w_ref[...])
    if n_rope_cols:
        cos_ref, sin_ref = refs[2], refs[3]
        tn = acc.shape[1]
        for jj in range(n_col_tiles):
            n_rope_blocks = min(max(n_rope_cols - jj * tn, 0), tn) // LANES

            @pl.when(pl.program_id(2) == jj)
            def _(n_rope_blocks=n_rope_blocks):
                o_ref[0] = _rope(acc, cos_ref[...], sin_ref[...], n_rope_blocks).astype(o_ref.dtype)
    else:
        o_ref[0] = acc.astype(o_ref.dtype)


def _row_tile(p):
    for cand in (1088, 1024, 544, 512, 272, 256, 128, 64, 32, 16):
        if p % cand == 0:
            return cand
    raise ValueError(p)


def project(x, w, *, tn=512, out_dtype=BF16, rope=None):
    b, p, k = x.shape
    n = w.shape[1]
    tm = _row_tile(p)
    tn = min(tn, n)
    assert n % tn == 0
    in_specs = [pl.BlockSpec((1, tm, k), lambda b, i, j: (b, i, 0)),
                pl.BlockSpec((k, tn), lambda b, i, j: (0, j))]
    args = [x, w]
    kw = dict(n_rope_cols=0, n_col_tiles=n // tn)
    if rope is not None:
        cos, sin, n_rope_cols = rope
        assert n_rope_cols % LANES == 0
        in_specs += [pl.BlockSpec((tm, LANES), lambda b, i, j: (i, 0)),
                     pl.BlockSpec((tm, LANES), lambda b, i, j: (i, 0))]
        args += [cos, sin]
        kw.update(n_rope_cols=n_rope_cols)
    return pl.pallas_call(
        functools.partial(_proj_kernel, **kw),
        grid=(b, p // tm, n // tn),
        in_specs=in_specs,
        out_specs=pl.BlockSpec((1, tm, tn), lambda b, i, j: (b, i, j)),
        out_shape=jax.ShapeDtypeStruct((b, p, n), out_dtype),
        compiler_params=_cparams(("parallel", "parallel", "arbitrary")),
        name="project",
    )(*args)


def _axial_cos_sin(n, rot_dim):
    t = jnp.arange(n, dtype=jnp.int32)
    row = (t // GRID_W).astype(F32)
    col = (t % GRID_W).astype(F32)
    n_freq = rot_dim // 4
    inv = ROPE_BASE ** (-jnp.arange(n_freq, dtype=F32) / n_freq)
    ang = jnp.concatenate([row[:, None] * inv, col[:, None] * inv], axis=-1)
    return jnp.cos(ang), jnp.sin(ang)


def rope_tables_full(s, ctx_len):
    c, sn = _axial_cos_sin(s, HEAD_DIM)
    cos = jnp.concatenate([c, c], axis=1)
    sin = jnp.concatenate([-sn, sn], axis=1)
    ident_c = jnp.ones((ctx_len, LANES), F32)
    ident_s = jnp.zeros((ctx_len, LANES), F32)
    return jnp.concatenate([ident_c, cos], axis=0), jnp.concatenate([ident_s, sin], axis=0)


def rope_tables_mla(s, ctx_len):
    c, sn = _axial_cos_sin(s, MLA_ROPE_DIM)
    one = jnp.ones_like(c)
    zero = jnp.zeros_like(c)
    cos = jnp.concatenate([c, one, c, one], axis=1)
    sin = jnp.concatenate([-sn, zero, sn, zero], axis=1)
    ident_c = jnp.ones((ctx_len, LANES), F32)
    ident_s = jnp.zeros((ctx_len, LANES), F32)
    return jnp.concatenate([ident_c, cos], axis=0), jnp.concatenate([ident_s, sin], axis=0)


def _spread_rope_cols(w_rope):
    k = w_rope.shape[0]
    half = MLA_ROPE_DIM // 2
    z = jnp.zeros((k, half), w_rope.dtype)
    return jnp.concatenate([w_rope[:, :half], z, w_rope[:, half:], z], axis=1)


def _softmax_probs(parts, extra=None):
    m = _row_reduce(jnp.maximum, jnp.max, parts)
    if extra is not None:
        m = jnp.maximum(m, extra)
    return [jnp.exp2((s - m).astype(BF16)) for s in parts], m


def _with_ones(v_ref, vone_ref):
    vone_ref[:, 0:LANES] = v_ref[0]
    vone_ref[:, LANES:2 * LANES] = jnp.ones((vone_ref.shape[0], LANES), BF16)


def _normalise(acc, extra_den=None):
    den = acc[:, LANES:LANES + 1]
    if extra_den is not None:
        den = den + extra_den
    return acc[:, :LANES] * (1.0 / den)


def _row_reduce(combine, reduce, parts):
    blocks = [s[:, c:c + LANES] for s in parts for c in range(0, s.shape[1], LANES)]
    acc = blocks[0]
    for blk in blocks[1:]:
        acc = combine(acc, blk)
    return reduce(acc, axis=-1, keepdims=True)


def na_bias_table(rpb, rows):
    n_tiles = rows // NA_TILE_ROWS
    n_heads, _, n_dcol = rpb.shape
    drow, row_ok = [], []
    for tile in (0, 1, n_tiles - 1):
        kr0 = int(np.clip(NA_TILE_ROWS * tile - NA_ROWS // 2, 0, rows - NA_WIN_ROWS))
        r = NA_TILE_ROWS * tile + np.arange(NA_TILE_ROWS)
        r0 = np.clip(r - NA_ROWS // 2, 0, rows - NA_ROWS)
        krow = kr0 + np.arange(NA_WIN_ROWS)
        row_ok.append((krow[None, :] >= r0[:, None]) & (krow[None, :] < r0[:, None] + NA_ROWS))
        drow.append(np.clip(krow[None, :] - r[:, None] + NA_ROWS - 1, 0, 2 * NA_ROWS - 2))
    drow, row_ok = np.stack(drow), np.stack(row_ok)
    qc = np.arange(GRID_W)
    qcol0 = np.clip(qc - NA_COLS // 2, 0, GRID_W - NA_COLS)
    kc = np.arange(GRID_W)
    col_ok = (kc[None, :] >= qcol0[:, None]) & (kc[None, :] < qcol0[:, None] + NA_COLS)
    dcol = np.clip(kc[None, :] - qc[:, None] + NA_COLS - 1, 0, 2 * NA_COLS - 2)
    pick_col = jnp.asarray(dcol[None] == np.arange(n_dcol)[:, None, None], F32)
    slabs = jnp.einsum('hrd,dqk->hrqk', rpb.astype(F32), pick_col, precision=lax.Precision.HIGHEST)
    slabs = jnp.where(jnp.asarray(col_ok)[None, None], slabs * LOG2E, NEG_INF)
    masked = jnp.full((n_heads, GRID_W, GRID_W), NEG_INF, F32)
    pats = []
    for p in range(3):
        per_row = [jnp.stack([slabs[:, drow[p, a, m]] if row_ok[p, a, m] else masked
                              for m in range(NA_WIN_ROWS)], axis=2)
                   for a in range(NA_TILE_ROWS)]
        pats.append(jnp.stack(per_row, axis=1))
    vals = jnp.stack(pats, axis=1)
    return vals.reshape(n_heads, 3, NA_TILE_ROWS * GRID_W, NA_WIN_ROWS * GRID_W)


def _na_kernel(q_ref, k_ref, v_ref, bias_ref, o_ref, vone_ref, *, ctx_len, rows):
    tq = NA_TILE_ROWS * GRID_W
    tk = NA_WIN_ROWS * GRID_W
    n_tiles = rows // NA_TILE_ROWS
    lc = ctx_len
    _with_ones(v_ref, vone_ref)

    s = _dot_nt(q_ref[0, 0:lc, :], k_ref[0, 0:lc, :])
    (p,), _ = _softmax_probs([s])
    o_ref[0, 0:lc, :] = _normalise(_dot(p, vone_ref[0:lc, :])).astype(o_ref.dtype)

    for i in range(n_tiles):
        qs = lc + i * tq
        kr0 = min(max(NA_TILE_ROWS * i - NA_ROWS // 2, 0), rows - NA_WIN_ROWS)
        ks = lc + kr0 * GRID_W
        pat = 0 if i == 0 else (2 if i == n_tiles - 1 else 1)
        q = q_ref[0, qs:qs + tq, :]
        s_loc = _dot_nt(q, k_ref[0, ks:ks + tk, :]) + bias_ref[0, pat]
        s_ctx = _dot_nt(q, k_ref[0, 0:lc, :])
        (p_loc, p_ctx), _ = _softmax_probs([s_loc, s_ctx])
        acc = _dot(p_loc, vone_ref[ks:ks + tk, :]) + _dot(p_ctx, vone_ref[0:lc, :])
        o_ref[0, qs:qs + tq, :] = _normalise(acc).astype(o_ref.dtype)


def neighbourhood_attention(qkv, bias, bias_head0, ctx_len, n_heads):
    b, p, _ = qkv.shape
    rows = (p - ctx_len) // GRID_W
    assert rows % NA_TILE_ROWS == 0 and rows >= NA_WIN_ROWS
    tq, tk = NA_TILE_ROWS * GRID_W, NA_WIN_ROWS * GRID_W
    assert ctx_len % 16 == 0 and ctx_len % tq == 0
    blk = lambda off: pl.BlockSpec((1, p, HEAD_DIM), lambda h, b: (b, 0, off + h))
    return pl.pallas_call(
        functools.partial(_na_kernel, ctx_len=ctx_len, rows=rows),
        grid=(n_heads, b),
        in_specs=[blk(0), blk(n_heads), blk(2 * n_heads),
                  pl.BlockSpec((1, 3, tq, tk), lambda h, b: (bias_head0 + h, 0, 0, 0))],
        out_specs=blk(0),
        out_shape=jax.ShapeDtypeStruct((b, p, n_heads * HEAD_DIM), BF16),
        scratch_shapes=[pltpu.VMEM((p, 2 * LANES), BF16)],
        compiler_params=_cparams(("parallel", "parallel")),
        name="na_attention",
    )(qkv, qkv, qkv, bias)


def _swa_kernel(sink_ref, q_ref, k_ref, v_ref, o_ref, vone_ref, *, ctx_len, seq, group):
    lc = ctx_len
    kvh = pl.program_id(1)
    _with_ones(v_ref, vone_ref)

    def stack_heads(q):
        return jnp.concatenate([q[:, g * HEAD_DIM:(g + 1) * HEAD_DIM] for g in range(group)], axis=0)

    def finish(parts_fn, n_q, pv_fn, store):
        ps_all, sink_terms = [], []
        for g in range(group):
            sink = sink_ref[kvh * group + g]
            ps, m = _softmax_probs(parts_fn(g), extra=sink)
            ps_all.append(ps)
            sink_terms.append(jnp.exp2(sink - m))
        n_parts = len(ps_all[0])
        stacked = [jnp.concatenate([ps_all[g][k] for g in range(group)], axis=0) for k in range(n_parts)]
        acc = pv_fn(stacked)
        store(jnp.concatenate([_normalise(acc[g * n_q:(g + 1) * n_q], sink_terms[g]) for g in range(group)], axis=1))

    s_c = _dot_nt(stack_heads(q_ref[0, 0:lc, :]), k_ref[0, 0:lc, :])

    def store_ctx(o):
        o_ref[0, 0:lc, :] = o.astype(o_ref.dtype)

    finish(lambda g: [s_c[g * lc:(g + 1) * lc]], lc,
           lambda st: _dot(st[0], vone_ref[0:lc, :]), store_ctx)

    n_tiles = seq // SWA_TQ

    def tile(t):
        q0 = t * SWA_TQ
        k0 = min(max(q0 - SWA_WINDOW, 0), seq - SWA_TK)
        qs, ks = lc + q0, lc + k0
        q4 = stack_heads(q_ref[0, qs:qs + SWA_TQ, :])
        s_loc = _dot_nt(q4, k_ref[0, ks:ks + SWA_TK, :])
        s_ctx = _dot_nt(q4, k_ref[0, 0:lc, :])
        dpos = (lax.broadcasted_iota(jnp.int32, (SWA_TQ, SWA_TK), 1)
                - lax.broadcasted_iota(jnp.int32, (SWA_TQ, SWA_TK), 0)) + (k0 - q0)
        valid = jnp.abs(dpos) <= SWA_WINDOW

        def parts(g):
            sl = slice(g * SWA_TQ, (g + 1) * SWA_TQ)
            return [jnp.where(valid, s_loc[sl], NEG_INF), s_ctx[sl]]

        def store(o):
            o_ref[0, qs:qs + SWA_TQ, :] = o.astype(o_ref.dtype)

        finish(parts, SWA_TQ,
               lambda st: _dot(st[0], vone_ref[ks:ks + SWA_TK, :]) + _dot(st[1], vone_ref[0:lc, :]),
               store)

    for t in range(n_tiles):
        tile(t)


def window_attention(qkv, sink, ctx_len, n_heads, n_kv_heads):
    b, p, _ = qkv.shape
    seq = p - ctx_len
    group = n_heads // n_kv_heads
    assert seq % SWA_TQ == 0 and seq >= SWA_TK and ctx_len % SWA_WINDOW == 0
    kv = lambda off: pl.BlockSpec((1, p, HEAD_DIM), lambda b, h: (b, 0, off + h))
    qo = pl.BlockSpec((1, p, group * HEAD_DIM), lambda b, h: (b, 0, h))
    return pl.pallas_call(
        functools.partial(_swa_kernel, ctx_len=ctx_len, seq=seq, group=group),
        grid=(b, n_kv_heads),
        in_specs=[pl.BlockSpec(memory_space=pltpu.SMEM), qo, kv(n_heads), kv(n_heads + n_kv_heads)],
        out_specs=qo,
        out_shape=jax.ShapeDtypeStruct((b, p, n_heads * HEAD_DIM), BF16),
        scratch_shapes=[pltpu.VMEM((p, 2 * LANES), BF16)],
        compiler_params=_cparams(("parallel", "parallel")),
        name="swa_attention",
    )(sink.astype(F32) * LOG2E, qkv, qkv, qkv)


def _mla_down_kernel(x_ref, w_ref, gq_ref, gkv_ref, cos_ref, sin_ref, cq_ref, ckv_ref, kr_ref, *, q_rank, kv_rank):
    acc = _dot(x_ref[0], w_ref[...])

    def rms(x, g):
        ms = jnp.mean(x * x, axis=-1, keepdims=True)
        return x * lax.rsqrt(ms + NORM_EPS) * g

    cq_ref[0] = rms(acc[:, :q_rank], gq_ref[...]).astype(cq_ref.dtype)
    ckv_ref[0] = rms(acc[:, q_rank:q_rank + kv_rank], gkv_ref[...]).astype(ckv_ref.dtype)
    kr_ref[0] = _rope(acc[:, q_rank + kv_rank:], cos_ref[...], sin_ref[...]).astype(kr_ref.dtype)


def mla_down(x, w_down, gq, gkv, cos, sin, q_rank, kv_rank):
    b, p, k = x.shape
    n = w_down.shape[1]
    tm = 544 if p % 544 == 0 else _row_tile(p)
    row = lambda width: pl.BlockSpec((1, tm, width), lambda b, i: (b, i, 0))
    return pl.pallas_call(
        functools.partial(_mla_down_kernel, q_rank=q_rank, kv_rank=kv_rank),
        grid=(b, p // tm),
        in_specs=[row(k), pl.BlockSpec((k, n), lambda b, i: (0, 0)),
                  pl.BlockSpec((1, q_rank), lambda b, i: (0, 0)),
                  pl.BlockSpec((1, kv_rank), lambda b, i: (0, 0)),
                  pl.BlockSpec((tm, LANES), lambda b, i: (i, 0)),
                  pl.BlockSpec((tm, LANES), lambda b, i: (i, 0))],
        out_specs=[row(q_rank), row(kv_rank), row(LANES)],
        out_shape=[jax.ShapeDtypeStruct((b, p, q_rank), BF16),
                   jax.ShapeDtypeStruct((b, p, kv_rank), BF16),
                   jax.ShapeDtypeStruct((b, p, LANES), BF16)],
        compiler_params=_cparams(("parallel", "parallel")),
        name="mla_down",
    )(x, w_down, gq.reshape(1, q_rank), gkv.reshape(1, kv_rank), cos, sin)


def _mla_kernel(cq_ref, ckv_ref, kr_ref, wq_ref, wkv_ref, cos_ref, sin_ref, o_ref, qcat_ref, kcat_ref, vone_ref, *,
                ctx_len):
    lc = ctx_len
    p_all = vone_ref.shape[0]
    kv = _dot(ckv_ref[0], wkv_ref[...])
    kcat_ref[0:LANES, :] = kv[:, :LANES].T.astype(BF16)
    kcat_ref[LANES:2 * LANES, :] = kr_ref[0].astype(F32).T.astype(BF16)
    vone_ref[:, 0:LANES] = kv[:, LANES:].astype(BF16)
    vone_ref[:, LANES:2 * LANES] = jnp.ones((p_all, LANES), BF16)
    q = _dot(cq_ref[0], wq_ref[...])
    qcat_ref[:, 0:LANES] = q[:, :LANES].astype(BF16)
    qcat_ref[:, LANES:2 * LANES] = _rope(q[:, LANES:], cos_ref[...], sin_ref[...]).astype(BF16)

    def attend(qs, n_q, chunks):
        q = qcat_ref[pl.ds(qs, n_q), :]
        m = jnp.full((n_q, 1), NEG_INF, F32)
        acc = jnp.zeros((n_q, 2 * LANES), F32)
        for c0, c1 in chunks:
            s = _dot(q, kcat_ref[:, c0:c1])
            m_new = jnp.maximum(m, _row_reduce(jnp.maximum, jnp.max, [s]))
            p = jnp.exp2((s - m_new).astype(BF16))
            acc = jnp.exp2(m - m_new) * acc + _dot(p, vone_ref[c0:c1, :])
            m = m_new
        o_ref[0, pl.ds(qs, n_q), :] = (acc[:, :LANES] * (1.0 / acc[:, LANES:LANES + 1])).astype(o_ref.dtype)

    attend(0, lc, [(0, lc)])
    all_chunks = [(0, lc)] + [(c, c + MLA_TK) for c in range(lc, p_all, MLA_TK)]

    for qs in range(lc, p_all, MLA_TQ):
        attend(qs, MLA_TQ, all_chunks)


def latent_attention(cq, ckv, kr, w_q, w_kv, cos, sin, ctx_len, n_heads):
    b, p, q_rank = cq.shape
    kv_rank = ckv.shape[2]
    assert (p - ctx_len) % MLA_TQ == 0 and (p - ctx_len) % MLA_TK == 0 and ctx_len % 16 == 0
    per_batch = lambda width: pl.BlockSpec((1, p, width), lambda b, h: (b, 0, 0), pipeline_mode=pl.Buffered(1))
    table = pl.BlockSpec((p, LANES), lambda b, h: (0, 0), pipeline_mode=pl.Buffered(1))
    return pl.pallas_call(
        functools.partial(_mla_kernel, ctx_len=ctx_len),
        grid=(b, n_heads),
        in_specs=[per_batch(q_rank), per_batch(kv_rank), per_batch(LANES),
                  pl.BlockSpec((q_rank, 2 * LANES), lambda b, h: (0, h)),
                  pl.BlockSpec((kv_rank, 2 * LANES), lambda b, h: (0, h)),
                  table, table],
        out_specs=pl.BlockSpec((1, p, LANES), lambda b, h: (b, 0, h)),
        out_shape=jax.ShapeDtypeStruct((b, p, n_heads * LANES), BF16),
        scratch_shapes=[pltpu.VMEM((p, 2 * LANES), BF16), pltpu.VMEM((2 * LANES, p), BF16),
                        pltpu.VMEM((p, 2 * LANES), BF16)],
        compiler_params=_cparams(("parallel", "parallel")),
        name="mla_attention",
    )(cq, ckv, kr, w_q, w_kv, cos, sin)


def gather_rows(src, idx):
    n = idx.shape[0]
    width = src.shape[1]
    win = GATHER_WINDOW
    info = plsc.get_sparse_core_info()
    n_workers = info.num_cores * info.num_subcores
    assert n % (n_workers * 2 * win) == 0
    per_worker = n // n_workers
    mesh = plsc.VectorSubcoreMesh(core_axis_name="core", subcore_axis_name="subcore")

    @functools.partial(
        pl.kernel, out_type=jax.ShapeDtypeStruct((n, width), src.dtype), mesh=mesh,
        scratch_types=[pltpu.VMEM((per_worker,), jnp.int32),
                       pltpu.VMEM((win, width), src.dtype), pltpu.VMEM((win, width), src.dtype),
                       pltpu.SemaphoreType.DMA, pltpu.SemaphoreType.DMA,
                       pltpu.SemaphoreType.DMA, pltpu.SemaphoreType.DMA],
        name="gather_rows")
    def gather(src_hbm, idx_hbm, out_hbm, idx_vmem, rows0, rows1, gsem0, gsem1, osem0, osem1):
        worker = lax.axis_index("subcore") * info.num_cores + lax.axis_index("core")
        base = worker * per_worker
        pltpu.sync_copy(idx_hbm.at[pl.ds(base, per_worker)], idx_vmem)

        def fetch(r, rows, sem):
            return pltpu.make_async_copy(src_hbm.at[idx_vmem.at[pl.ds(r, win)]], rows, sem)

        def flush(r, rows, sem):
            return pltpu.make_async_copy(rows, out_hbm.at[pl.ds(base + r, win)], sem)

        fetch(0, rows0, gsem0).start()

        @pl.loop(0, per_worker, step=2 * win)
        def _(r):
            fetch(r, rows0, gsem0).wait()

            @pl.when(r > 0)
            def _():
                flush(r - win, rows1, osem1).wait()

            fetch(r + win, rows1, gsem1).start()
            flush(r, rows0, osem0).start()
            fetch(r + win, rows1, gsem1).wait()
            flush(r, rows0, osem0).wait()

            @pl.when(r + 2 * win < per_worker)
            def _():
                fetch(r + 2 * win, rows0, gsem0).start()

            flush(r + win, rows1, osem1).start()

        flush(per_worker - win, rows1, osem1).wait()

    return gather(src, idx)


def scatter_rows(src, idx, n_out):
    n = idx.shape[0]
    n_src, width = src.shape
    win = GATHER_WINDOW
    info = plsc.get_sparse_core_info()
    n_workers = info.num_cores * info.num_subcores
    assert n % (n_workers * 2 * win) == 0
    per_worker = n // n_workers
    assert n_src % per_worker == 0
    n_steps = per_worker // win
    mesh = plsc.VectorSubcoreMesh(core_axis_name="core", subcore_axis_name="subcore")

    @functools.partial(
        pl.kernel, out_type=jax.ShapeDtypeStruct((n_out, width), src.dtype), mesh=mesh,
        scratch_types=[pltpu.VMEM((n_steps, win), jnp.int32),
                       pltpu.VMEM((win, width), src.dtype), pltpu.VMEM((win, width), src.dtype),
                       pltpu.SemaphoreType.DMA, pltpu.SemaphoreType.DMA,
                       pltpu.SemaphoreType.DMA, pltpu.SemaphoreType.DMA],
        name="scatter_rows")
    def scatter(src_hbm, idx_hbm, out_hbm, idx_vmem, rows0, rows1, lsem0, lsem1, ssem0, ssem1):
        worker = lax.axis_index("subcore") * info.num_cores + lax.axis_index("core")
        src_base = lax.rem(worker * per_worker, n_src)
        pltpu.sync_copy(idx_hbm.at[worker], idx_vmem)

        def load(j, rows, sem):
            return pltpu.make_async_copy(src_hbm.at[pl.ds(src_base + j * win, win)], rows, sem)

        def store(j, rows, sem):
            return pltpu.make_async_copy(rows, out_hbm.at[idx_vmem.at[j]], sem)

        load(0, rows0, lsem0).start()

        @pl.loop(0, n_steps, step=2)
        def _(j):
            load(j, rows0, lsem0).wait()

            @pl.when(j > 0)
            def _():
                store(j - 1, rows1, ssem1).wait()

            load(j + 1, rows1, lsem1).start()
            store(j, rows0, ssem0).start()
            load(j + 1, rows1, lsem1).wait()
            store(j, rows0, ssem0).wait()

            @pl.when(j + 2 < n_steps)
            def _():
                load(j + 2, rows0, lsem0).start()

            store(j + 1, rows1, ssem1).start()

        store(n_steps - 1, rows1, ssem1).wait()

    return scatter(src, idx.reshape(n_workers, n_steps, win))


def _expert_kernel(blk_e_ref, n_used_ref, n_valid_ref, x_ref, wg_ref, wu_ref, wd_ref, y_ref, wg_bf, wu_bf, wd_bf):
    i = pl.program_id(0)
    used = i < n_used_ref[0]
    new_expert = (i == 0) | (blk_e_ref[i] != blk_e_ref[jnp.maximum(i - 1, 0)])

    @pl.when(used & new_expert)
    def _():
        wg_bf[...] = wg_ref[0].astype(BF16)
        wu_bf[...] = wu_ref[0].astype(BF16)
        wd_bf[...] = wd_ref[0].astype(BF16)

    @pl.when(used)
    def _():
        row = lax.broadcasted_iota(jnp.int32, (x_ref.shape[0], 1), 0)
        x = _unpack_bf16_pairs(jnp.where(row < n_valid_ref[i], x_ref[...], 0)).astype(BF16)
        g = _dot(x, wg_bf[...])
        u = _dot(x, wu_bf[...])
        a = (g * jax.nn.sigmoid(g) * u).astype(BF16)
        y_ref[...] = _pack_bf16_pairs(_dot(a, wd_bf[...]))

    @pl.when(jnp.logical_not(used))
    def _():
        y_ref[...] = jnp.zeros_like(y_ref)


def expert_ffn(x_disp, blk_e, n_used, n_valid, w_gate, w_up, w_down, layer):
    rows, half = x_disp.shape
    d = 2 * half
    n_blk = rows // MOE_BLOCK
    de = w_gate.shape[3]
    grid_spec = pltpu.PrefetchScalarGridSpec(
        num_scalar_prefetch=3,
        grid=(n_blk,),
        in_specs=[pl.BlockSpec((MOE_BLOCK, half), lambda i, be, nu, nv: (i, 0)),
                  pl.BlockSpec((None, 1, d, de), lambda i, be, nu, nv: (layer, be[i], 0, 0)),
                  pl.BlockSpec((None, 1, d, de), lambda i, be, nu, nv: (layer, be[i], 0, 0)),
                  pl.BlockSpec((None, 1, de, d), lambda i, be, nu, nv: (layer, be[i], 0, 0))],
        out_specs=pl.BlockSpec((MOE_BLOCK, half), lambda i, be, nu, nv: (i, 0)),
        scratch_shapes=[pltpu.VMEM((d, de), BF16), pltpu.VMEM((d, de), BF16), pltpu.VMEM((de, d), BF16)],
    )
    return pl.pallas_call(
        _expert_kernel,
        grid_spec=grid_spec,
        out_shape=jax.ShapeDtypeStruct((rows, half), jnp.int32),
        compiler_params=_cparams(("arbitrary",)),
        name="expert_ffn",
    )(blk_e, n_used, n_valid, x_disp, w_gate, w_up, w_down)


def _moe_residual(h_ref, y0_ref, y1_ref, wt_ref, mod_ref, gate_idx):
    wt = wt_ref[0]
    moe = wt[:, 0:1] * _unpack_bf16_pairs(y0_ref[0, 0]) + wt[:, 1:2] * _unpack_bf16_pairs(y1_ref[0, 0])
    return h_ref[0] + mod_ref[0, gate_idx:gate_idx + 1, :] * moe


def _combine_next_kernel(h_ref, y0_ref, y1_ref, wt_ref, mod_ref, g_ref, nmod_ref, o_ref, hm_ref, *, gate_idx):
    h_new = _moe_residual(h_ref, y0_ref, y1_ref, wt_ref, mod_ref, gate_idx)
    o_ref[0] = h_new
    hm_ref[0] = _rms_mod(h_new, g_ref[...], nmod_ref[0, 0:1, :], nmod_ref[0, 1:2, :]).astype(hm_ref.dtype)


def _combine_final_kernel(h_ref, y0_ref, y1_ref, wt_ref, mod_ref, g_ref, o_ref, *, gate_idx):
    x = _moe_residual(h_ref, y0_ref, y1_ref, wt_ref, mod_ref, gate_idx)
    ms = jnp.mean(x * x, axis=-1, keepdims=True)
    o_ref[0] = x * lax.rsqrt(ms + NORM_EPS) * g_ref[...]


def combine(h, y_pairs, wt, modtab, gate_idx, next_g, next_modtab, ctx_len):
    b, p, d = h.shape
    last = next_modtab is None
    skip = ctx_len // ROW_TILE if last else 0
    row = pl.BlockSpec((1, ROW_TILE, d), lambda b, j: (b, j + skip, 0))
    mod = pl.BlockSpec((1, N_MOD, d), lambda b, j: (2 * b + jnp.minimum(j + skip, 1), 0, 0))
    in_specs = [row,
                pl.BlockSpec((1, 1, ROW_TILE, d // 2), lambda b, j: (0, b, j + skip, 0)),
                pl.BlockSpec((1, 1, ROW_TILE, d // 2), lambda b, j: (1, b, j + skip, 0)),
                pl.BlockSpec((1, ROW_TILE, LANES), lambda b, j: (b, j + skip, 0)),
                mod, pl.BlockSpec((1, d), lambda b, j: (0, 0))]
    args = [h, y_pairs, y_pairs, wt, modtab.reshape(b * 2, N_MOD, d), next_g.reshape(1, d)]
    out_row = pl.BlockSpec((1, ROW_TILE, d), lambda b, j: (b, j, 0))
    if last:
        return pl.pallas_call(
            functools.partial(_combine_final_kernel, gate_idx=gate_idx),
            grid=(b, (p - ctx_len) // ROW_TILE),
            in_specs=in_specs,
            out_specs=out_row,
            out_shape=jax.ShapeDtypeStruct((b, p - ctx_len, d), F32),
            compiler_params=_cparams(("parallel", "parallel")),
            name="moe_combine_final",
        )(*args)
    return pl.pallas_call(
        functools.partial(_combine_next_kernel, gate_idx=gate_idx),
        grid=(b, p // ROW_TILE),
        in_specs=in_specs + [mod],
        out_specs=[out_row, out_row],
        out_shape=[jax.ShapeDtypeStruct((b, p, d), F32), jax.ShapeDtypeStruct((b, p, d), BF16)],
        compiler_params=_cparams(("parallel", "parallel")),
        name="moe_combine",
    )(*args, next_modtab.reshape(b * 2, N_MOD, d))


def hier_moe(h, f, route, wt, counts, modtab, gate_idx, w_gate, w_up, w_down, layer, next_g, next_modtab, ctx_len):
    b, p, d = h.shape
    n_tok = b * p
    n_assign = 2 * n_tok
    e1, e2, rank1, rank2 = route.reshape(n_tok, LANES)[:, :4].T
    cnt = counts[0, :N_EXPERTS].astype(jnp.int32)
    pcounts = (cnt + MOE_BLOCK - 1) // MOE_BLOCK * MOE_BLOCK
    pend = jnp.cumsum(pcounts)
    pstart = pend - pcounts
    dest_by_slot = jnp.concatenate([pstart[e1] + rank1, pstart[e2] + rank2]).astype(jnp.int32)
    n_blk = -(-n_assign // MOE_BLOCK) + N_EXPERTS
    rows_total = n_blk * MOE_BLOCK
    blk_row0 = jnp.arange(n_blk, dtype=jnp.int32) * MOE_BLOCK
    blk_e = jnp.minimum(jnp.sum((pend[None, :] <= blk_row0[:, None]).astype(jnp.int32), axis=1), N_EXPERTS - 1)
    n_valid = jnp.clip(cnt[blk_e] - (blk_row0 - pstart[blk_e]), 0, MOE_BLOCK).astype(jnp.int32)
    n_used = (pend[-1] // MOE_BLOCK).astype(jnp.int32).reshape(1)

    x_disp = scatter_rows(f.reshape(n_tok, d // 2), dest_by_slot, rows_total)
    y = expert_ffn(x_disp, blk_e, n_used, n_valid, w_gate, w_up, w_down, layer)
    y_pairs = gather_rows(y, dest_by_slot).reshape(2, b, p, d // 2)
    return combine(h, y_pairs, wt, modtab, gate_idx, next_g, next_modtab, ctx_len)


def kernel(x, c, ctx, c_ctx, mod_w, mod_b, norm_mix_g, norm_ffn_g, router_grp_w, router_grp_b, router_exp_w, router_exp_b, exp_w_gate, exp_w_up, exp_w_down, l0_na_w_qkv, l0_na_rpb, l0_na_w_o, l1_swa_w_qkv, l1_swa_sink, l1_swa_w_o, l2_mla_w_dq, l2_mla_q_norm_g, l2_mla_w_uq, l2_mla_w_dkv, l2_mla_kv_norm_g, l2_mla_w_ukv, l2_mla_w_o, l3_na_w_qkv, l3_na_rpb, l3_na_w_o, final_norm_g):
    b, s, d = x.shape
    lc = ctx.shape[1]
    n_heads = d // HEAD_DIM
    n_kv_heads = n_heads // 4
    depth = mod_w.shape[0]
    rows = s // GRID_W

    modtabs = modulation_tables(c, c_ctx, mod_w, mod_b)

    def scale_q_cols(w):
        n_q = n_heads * HEAD_DIM
        q_scale = HEAD_DIM ** -0.5 * LOG2E
        return jnp.concatenate([w[:, :n_q] * q_scale, w[:, n_q:]], axis=1).astype(BF16)

    na_bias = na_bias_table(jnp.concatenate([l0_na_rpb, l3_na_rpb], axis=0), rows)

    def na_mixer(hm, w_qkv, bias_head0):
        qkv = project(hm, scale_q_cols(w_qkv), tn=2048)
        return neighbourhood_attention(qkv, na_bias, bias_head0, lc, n_heads)

    def swa_mixer(hm):
        cos, sin = rope_tables_full(s, lc)
        qkv = project(hm, scale_q_cols(l1_swa_w_qkv), tn=1024, rope=(cos, sin, (n_heads + n_kv_heads) * HEAD_DIM))
        return window_attention(qkv, l1_swa_sink, lc, n_heads, n_kv_heads)

    def mla_mixer(hm):
        q_rank = l2_mla_w_dq.shape[1]
        kv_rank = l2_mla_kv_norm_g.shape[0]
        cos, sin = rope_tables_mla(s, lc)
        w_down = jnp.concatenate([l2_mla_w_dq, l2_mla_w_dkv[:, :kv_rank],
                                  _spread_rope_cols(l2_mla_w_dkv[:, kv_rank:])], axis=1).astype(BF16)
        cq, ckv, kr = mla_down(hm, w_down, l2_mla_q_norm_g, l2_mla_kv_norm_g, cos, sin, q_rank, kv_rank)
        w_uq = l2_mla_w_uq.reshape(q_rank, n_heads, MLA_NOPE_DIM + MLA_ROPE_DIM)
        q_scale = (MLA_NOPE_DIM + MLA_ROPE_DIM) ** -0.5 * LOG2E
        half = MLA_ROPE_DIM // 2
        gap = jnp.zeros((q_rank, n_heads, half), l2_mla_w_uq.dtype)
        w_q = jnp.concatenate([w_uq[:, :, :MLA_NOPE_DIM], w_uq[:, :, MLA_NOPE_DIM:MLA_NOPE_DIM + half], gap,
                               w_uq[:, :, MLA_NOPE_DIM + half:], gap], axis=2).reshape(q_rank, n_heads * 2 * LANES)
        return latent_attention(cq, ckv, kr, (w_q * q_scale).astype(BF16), l2_mla_w_ukv.astype(BF16), cos, sin,
                                lc, n_heads)

    h, hm = join_norm_modulate(ctx, x, norm_mix_g[0], modtabs[0])
    for i in range(depth):
        modtab = modtabs[i]
        mixer = i % 3
        if mixer == 0:
            w_qkv, bias_head0, w_o = (l0_na_w_qkv, 0, l0_na_w_o) if i == 0 else (l3_na_w_qkv, n_heads, l3_na_w_o)
            y = na_mixer(hm, w_qkv, bias_head0)
        elif mixer == 1:
            y, w_o = swa_mixer(hm), l1_swa_w_o
        else:
            y, w_o = mla_mixer(hm), l2_mla_w_o
        h, f, route, wt, counts = attn_out_route(y, w_o.astype(BF16), h, modtab, norm_ffn_g[i], router_grp_w[i],
                                                 router_grp_b[i], router_exp_w[i], router_exp_b[i], lc,
                                                 n_sub=1)
        if i + 1 < depth:
            h, hm = hier_moe(h, f, route, wt, counts, modtab, 5, exp_w_gate, exp_w_up, exp_w_down, i,
                             norm_mix_g[i + 1], modtabs[i + 1], lc)
        else:
            return hier_moe(h, f, route, wt, counts, modtab, 5, exp_w_gate, exp_w_up, exp_w_down, i,
                            final_norm_g, None, lc)
```

```python
import functools

import numpy as np
import jax
import jax.numpy as jnp
from jax import lax
from jax.experimental import pallas as pl
from jax.experimental.pallas import tpu as pltpu
from jax.experimental.pallas import tpu_sc as plsc

GRID_W = 64
HEAD_DIM = 128
ROPE_BASE = 10000.0
NORM_EPS = 1e-6
NEG_INF = -1e30
N_MOD = 6

NA_ROWS = 8
NA_COLS = 16
NA_TILE_ROWS = 4
NA_WIN_ROWS = 12

SWA_WINDOW = 128
SWA_TQ = 256
SWA_TK = 512

MLA_NOPE_DIM = 128
MLA_ROPE_DIM = 64
MLA_TQ = 1024
MLA_TK = 512
LOG2E = 1.4426950408889634

N_GROUPS = 4
EXPERTS_PER_GROUP = 8
N_EXPERTS = N_GROUPS * EXPERTS_PER_GROUP
MOE_BLOCK = 512
GATHER_WINDOW = 32

LANES = 128
ROW_TILE = 256
VMEM_LIMIT = 56 * 1024 * 1024

BF16 = jnp.bfloat16
F32 = jnp.float32


def _cparams(sem):
    return pltpu.CompilerParams(dimension_semantics=sem, vmem_limit_bytes=VMEM_LIMIT)


def _dot(a, b):
    return jnp.dot(a, b, preferred_element_type=F32)


def _dot_nt(a, b):
    return lax.dot_general(a, b, (((1,), (1,)), ((), ())), preferred_element_type=F32)


def _mod_kernel(x_ref, w_ref, b_ref, o_ref):
    x = x_ref[...]
    sx = (x * jax.nn.sigmoid(x)).astype(BF16)
    o_ref[0] = _dot(sx, w_ref[0].astype(BF16)) + b_ref[0]


def modulation_tables(c, c_ctx, mod_w, mod_b):
    depth, d, n_out = mod_w.shape
    b = c.shape[0]
    rows = 16
    xin = jnp.zeros((rows, d), F32).at[:b].set(c).at[b].set(c_ctx)
    tn = 1024
    out = pl.pallas_call(
        _mod_kernel,
        grid=(depth, n_out // tn),
        in_specs=[pl.BlockSpec((rows, d), lambda i, j: (0, 0)),
                  pl.BlockSpec((1, d, tn), lambda i, j: (i, 0, j)),
                  pl.BlockSpec((1, 1, tn), lambda i, j: (i, 0, j))],
        out_specs=pl.BlockSpec((1, rows, tn), lambda i, j: (i, 0, j)),
        out_shape=jax.ShapeDtypeStruct((depth, rows, n_out), F32),
        compiler_params=_cparams(("parallel", "parallel")),
        name="adaln_mod",
    )(xin, mod_w, mod_b.reshape(depth, 1, n_out))
    lat = out[:, :b].reshape(depth, b, 1, N_MOD, d)
    ctx = jnp.broadcast_to(out[:, b].reshape(depth, 1, 1, N_MOD, d), (depth, b, 1, N_MOD, d))
    return jnp.concatenate([ctx, lat], axis=2)


def _rms_mod(x, g, shift, scale):
    ms = jnp.mean(x * x, axis=-1, keepdims=True)
    y = x * lax.rsqrt(ms + NORM_EPS) * g
    return y * (1.0 + scale) + shift


def _join_norm_mod_kernel(ctx_ref, x_ref, g_ref, mod_ref, h_ref, hm_ref, *, ctx_tiles):
    j = pl.program_id(1)

    def emit(src_ref):
        h_ref[0] = src_ref[0]
        hm_ref[0] = _rms_mod(src_ref[0], g_ref[...], mod_ref[0, 0:1, :], mod_ref[0, 1:2, :]).astype(hm_ref.dtype)

    @pl.when(j < ctx_tiles)
    def _():
        emit(ctx_ref)

    @pl.when(j >= ctx_tiles)
    def _():
        emit(x_ref)


def _route(logits):
    lane = lax.broadcasted_iota(jnp.int32, logits.shape, 1).astype(F32)
    big = float(LANES)

    def first_lane(mask):
        return jnp.min(jnp.where(mask, lane, big), axis=-1, keepdims=True)

    in_grp = lane < N_GROUPS
    lg = jnp.where(in_grp, logits, NEG_INF)
    m_g = jnp.max(lg, axis=-1, keepdims=True)
    g_idx = first_lane(in_grp & (lg == m_g))
    g_w = 1.0 / jnp.sum(jnp.where(in_grp, jnp.exp(lg - m_g), 0.0), axis=-1, keepdims=True)
    e_lo = N_GROUPS + g_idx * EXPERTS_PER_GROUP
    in_e = (lane >= e_lo) & (lane < e_lo + EXPERTS_PER_GROUP)
    le = jnp.where(in_e, logits, NEG_INF)
    m1 = jnp.max(le, axis=-1, keepdims=True)
    e1 = first_lane(in_e & (le == m1))
    s_e = jnp.sum(jnp.where(in_e, jnp.exp(le - m1), 0.0), axis=-1, keepdims=True)
    in_e2 = in_e & (lane != e1)
    le2 = jnp.where(in_e2, logits, NEG_INF)
    m2 = jnp.max(le2, axis=-1, keepdims=True)
    e2 = first_lane(in_e2 & (le2 == m2))
    p1 = 1.0 / s_e
    p2 = jnp.exp(m2 - m1) / s_e
    den = p1 + p2
    return ((e1 - N_GROUPS).astype(jnp.int32), (e2 - N_GROUPS).astype(jnp.int32),
            g_w * p1 / den, g_w * p2 / den)


def _pack_bf16_pairs(x):
    n = x.shape[1] // 2
    xb = x.astype(BF16).astype(F32)
    hi = lax.bitcast_convert_type(xb[:, :n], jnp.int32)
    lo = lax.bitcast_convert_type(xb[:, n:], jnp.int32)
    return (hi & jnp.int32(-65536)) | lax.shift_right_logical(lo, jnp.int32(16))


def _unpack_bf16_pairs(w):
    hi = lax.bitcast_convert_type(w & jnp.int32(-65536), F32)
    lo = lax.bitcast_convert_type(lax.shift_left(w, jnp.int32(16)), F32)
    return jnp.concatenate([hi, lo], axis=1)


def _route_and_rank(f, wr_ref, br_ref, run_ref):
    f_hi = f.astype(BF16)
    f_lo = (f - f_hi.astype(F32)).astype(BF16)
    hi_terms = _dot(f_hi, wr_ref[...])
    logits = (hi_terms[:, :LANES] + _dot(f_lo, wr_ref[:, :LANES]) + hi_terms[:, LANES:]) + br_ref[...]
    e1, e2, w1, w2 = _route(logits)
    lane = lax.broadcasted_iota(jnp.int32, logits.shape, 1)
    pick1, pick2 = lane == e1, lane == e2
    chosen = (pick1 | pick2).astype(F32)
    n_rows = chosen.shape[0]
    earlier = (lax.broadcasted_iota(jnp.int32, (n_rows, n_rows), 1)
               < lax.broadcasted_iota(jnp.int32, (n_rows, n_rows), 0)).astype(BF16)
    before = run_ref[...] + _dot(earlier, chosen.astype(BF16))
    rank1 = jnp.sum(jnp.where(pick1, before, 0.0), axis=-1, keepdims=True).astype(jnp.int32)
    rank2 = jnp.sum(jnp.where(pick2, before, 0.0), axis=-1, keepdims=True).astype(jnp.int32)
    run_ref[...] = run_ref[...] + jnp.sum(chosen, axis=0, keepdims=True)
    route = jnp.where(lane == 0, e1, jnp.where(lane == 1, e2, jnp.where(lane == 2, rank1,
                                                                         jnp.where(lane == 3, rank2, 0))))
    return route, jnp.where(lane == 0, w1, jnp.where(lane == 1, w2, 0.0))


def _attn_out_route_kernel(y_ref, w_ref, h_ref, mod_ref, g_ref, wr_ref, br_ref,
                           ho_ref, f_ref, route_ref, wt_ref, cnt_ref, run_ref, *, ctx_len, tm, n_sub):
    first = (pl.program_id(0) == 0) & (pl.program_id(1) == 0)

    @pl.when(first)
    def _():
        run_ref[...] = jnp.zeros_like(run_ref)

    sub = tm // n_sub
    for s in range(n_sub):
        rows = slice(s * sub, (s + 1) * sub)
        acc = _dot(y_ref[0, rows, :], w_ref[...])
        pos = pl.program_id(1) * tm + s * sub + lax.broadcasted_iota(jnp.int32, (sub, 1), 0)
        is_ctx = pos < ctx_len

        def mod_row(k):
            return jnp.where(is_ctx, mod_ref[0, 0, k:k + 1, :], mod_ref[0, 1, k:k + 1, :])

        h_new = h_ref[0, rows, :] + mod_row(2) * acc
        ho_ref[0, rows, :] = h_new
        f = _rms_mod(h_new, g_ref[...], mod_row(3), mod_row(4))
        f_ref[0, rows, :] = _pack_bf16_pairs(f)
        route, wt = _route_and_rank(f, wr_ref, br_ref, run_ref)
        route_ref[0, rows, :] = route
        wt_ref[0, rows, :] = wt
    cnt_ref[...] = run_ref[...]


def _mod_spec(d):
    return pl.BlockSpec((1, N_MOD, d), lambda b, j: (2 * b + jnp.minimum(j, 1), 0, 0))


def join_norm_modulate(ctx, x, g, modtab):
    b, lc, d = ctx.shape
    p = lc + x.shape[1]
    assert lc == ROW_TILE
    ctx_tiles = lc // ROW_TILE
    row = pl.BlockSpec((1, ROW_TILE, d), lambda b, j: (b, j, 0))
    return pl.pallas_call(
        functools.partial(_join_norm_mod_kernel, ctx_tiles=ctx_tiles),
        grid=(b, p // ROW_TILE),
        in_specs=[pl.BlockSpec((1, ROW_TILE, d), lambda b, j: (b, jnp.minimum(j, ctx_tiles - 1), 0)),
                  pl.BlockSpec((1, ROW_TILE, d), lambda b, j: (b, jnp.maximum(j - ctx_tiles, 0), 0)),
                  pl.BlockSpec((1, d), lambda b, j: (0, 0)), _mod_spec(d)],
        out_specs=[row, row],
        out_shape=[jax.ShapeDtypeStruct((b, p, d), F32), jax.ShapeDtypeStruct((b, p, d), BF16)],
        compiler_params=_cparams(("parallel", "parallel")),
        name="join_norm_mod",
    )(ctx, x, g.reshape(1, d), modtab.reshape(b * 2, N_MOD, d))


def attn_out_route(y, w_o, h, modtab, g, w_grp, b_grp, w_rt, b_rt, ctx_len, n_sub=2):
    b, p, d = h.shape
    n_r = N_GROUPS + N_EXPERTS
    wr = jnp.zeros((d, LANES), F32).at[:, :N_GROUPS].set(w_grp).at[:, N_GROUPS:n_r].set(w_rt)
    br = jnp.zeros((1, LANES), F32).at[0, :N_GROUPS].set(b_grp).at[0, N_GROUPS:n_r].set(b_rt)
    wr_hi = wr.astype(BF16)
    wr = jnp.concatenate([wr_hi, (wr - wr_hi.astype(F32)).astype(BF16)], axis=1)
    tm = 544 if p % 544 == 0 else ROW_TILE
    assert (tm // n_sub) % 16 == 0
    row = lambda width: pl.BlockSpec((1, tm, width), lambda b, i: (b, i, 0))
    once = lambda shape: pl.BlockSpec(shape, lambda b, i: (0,) * len(shape), pipeline_mode=pl.Buffered(1))
    return pl.pallas_call(
        functools.partial(_attn_out_route_kernel, ctx_len=ctx_len, tm=tm, n_sub=n_sub),
        grid=(b, p // tm),
        in_specs=[row(d), once((d, d)), row(d),
                  pl.BlockSpec((1, 2, N_MOD, d), lambda b, i: (b, 0, 0, 0)),
                  once((1, d)), once((d, 2 * LANES)), once((1, LANES))],
        out_specs=[row(d), row(d // 2), row(LANES), row(LANES), pl.BlockSpec((1, LANES), lambda b, i: (0, 0))],
        out_shape=[jax.ShapeDtypeStruct((b, p, d), F32),
                   jax.ShapeDtypeStruct((b, p, d // 2), jnp.int32),
                   jax.ShapeDtypeStruct((b, p, LANES), jnp.int32),
                   jax.ShapeDtypeStruct((b, p, LANES), F32),
                   jax.ShapeDtypeStruct((1, LANES), F32)],
        scratch_shapes=[pltpu.VMEM((1, LANES), F32)],
        compiler_params=_cparams(("arbitrary", "arbitrary")),
        name="attn_out_route",
    )(y, w_o, h, modtab, g.reshape(1, d), wr, br)


def _rope(acc, cos, sin, n_rope_blocks=None):
    n_blk = acc.shape[1] // LANES
    n_rope_blocks = n_blk if n_rope_blocks is None else n_rope_blocks
    outs = []
    for c in range(n_blk):
        x = acc[:, c * LANES:(c + 1) * LANES]
        outs.append(x * cos + pltpu.roll(x, LANES // 2, 1) * sin if c < n_rope_blocks else x)
    return outs[0] if n_blk == 1 else jnp.concatenate(outs, axis=1)


def _proj_kernel(*refs, n_rope_cols, n_col_tiles):
    x_ref, w_ref = refs[0], refs[1]
    o_ref = refs[-1]
    acc = _dot(x_ref[0], w_ref[...])
    if n_rope_cols:
        cos_ref, sin_ref = refs[2], refs[3]
        tn = acc.shape[1]
        for jj in range(n_col_tiles):
            n_rope_blocks = min(max(n_rope_cols - jj * tn, 0), tn) // LANES

            @pl.when(pl.program_id(2) == jj)
            def _(n_rope_blocks=n_rope_blocks):
                o_ref[0] = _rope(acc, cos_ref[...], sin_ref[...], n_rope_blocks).astype(o_ref.dtype)
    else:
        o_ref[0] = acc.astype(o_ref.dtype)


def _row_tile(p):
    for cand in (1088, 1024, 544, 512, 272, 256, 128, 64, 32, 16):
        if p % cand == 0:
            return cand
    raise ValueError(p)


def project(x, w, *, tn=512, out_dtype=BF16, rope=None):
    b, p, k = x.shape
    n = w.shape[1]
    tm = _row_tile(p)
    tn = min(tn, n)
    assert n % tn == 0
    in_specs = [pl.BlockSpec((1, tm, k), lambda b, i, j: (b, i, 0)),
                pl.BlockSpec((k, tn), lambda b, i, j: (0, j))]
    args = [x, w]
    kw = dict(n_rope_cols=0, n_col_tiles=n // tn)
    if rope is not None:
        cos, sin, n_rope_cols = rope
        assert n_rope_cols % LANES == 0
        in_specs += [pl.BlockSpec((tm, LANES), lambda b, i, j: (i, 0)),
                     pl.BlockSpec((tm, LANES), lambda b, i, j: (i, 0))]
        args += [cos, sin]
        kw.update(n_rope_cols=n_rope_cols)
    return pl.pallas_call(
        functools.partial(_proj_kernel, **kw),
        grid=(b, p // tm, n // tn),
        in_specs=in_specs,
        out_specs=pl.BlockSpec((1, tm, tn), lambda b, i, j: (b, i, j)),
        out_shape=jax.ShapeDtypeStruct((b, p, n), out_dtype),
        compiler_params=_cparams(("parallel", "parallel", "arbitrary")),
        name="project",
    )(*args)


def _axial_cos_sin(n, rot_dim):
    t = jnp.arange(n, dtype=jnp.int32)
    row = (t // GRID_W).astype(F32)
    col = (t % GRID_W).astype(F32)
    n_freq = rot_dim // 4
    inv = ROPE_BASE ** (-jnp.arange(n_freq, dtype=F32) / n_freq)
    ang = jnp.concatenate([row[:, None] * inv, col[:, None] * inv], axis=-1)
    return jnp.cos(ang), jnp.sin(ang)


def rope_tables_full(s, ctx_len):
    c, sn = _axial_cos_sin(s, HEAD_DIM)
    cos = jnp.concatenate([c, c], axis=1)
    sin = jnp.concatenate([-sn, sn], axis=1)
    ident_c = jnp.ones((ctx_len, LANES), F32)
    ident_s = jnp.zeros((ctx_len, LANES), F32)
    return jnp.concatenate([ident_c, cos], axis=0), jnp.concatenate([ident_s, sin], axis=0)


def rope_tables_mla(s, ctx_len):
    c, sn = _axial_cos_sin(s, MLA_ROPE_DIM)
    one = jnp.ones_like(c)
    zero = jnp.zeros_like(c)
    cos = jnp.concatenate([c, one, c, one], axis=1)
    sin = jnp.concatenate([-sn, zero, sn, zero], axis=1)
    ident_c = jnp.ones((ctx_len, LANES), F32)
    ident_s = jnp.zeros((ctx_len, LANES), F32)
    return jnp.concatenate([ident_c, cos], axis=0), jnp.concatenate([ident_s, sin], axis=0)


def _spread_rope_cols(w_rope):
    k = w_rope.shape[0]
    half = MLA_ROPE_DIM // 2
    z = jnp.zeros((k, half), w_rope.dtype)
    return jnp.concatenate([w_rope[:, :half], z, w_rope[:, half:], z], axis=1)


def _softmax_probs(parts, extra=None):
    m = _row_reduce(jnp.maximum, jnp.max, parts)
    if extra is not None:
        m = jnp.maximum(m, extra)
    return [jnp.exp2((s - m).astype(BF16)) for s in parts], m


def _with_ones(v_ref, vone_ref):
    vone_ref[:, 0:LANES] = v_ref[0]
    vone_ref[:, LANES:2 * LANES] = jnp.ones((vone_ref.shape[0], LANES), BF16)


def _normalise(acc, extra_den=None):
    den = acc[:, LANES:LANES + 1]
    if extra_den is not None:
        den = den + extra_den
    return acc[:, :LANES] * (1.0 / den)


def _row_reduce(combine, reduce, parts):
    blocks = [s[:, c:c + LANES] for s in parts for c in range(0, s.shape[1], LANES)]
    acc = blocks[0]
    for blk in blocks[1:]:
        acc = combine(acc, blk)
    return reduce(acc, axis=-1, keepdims=True)


def na_bias_table(rpb, rows):
    n_tiles = rows // NA_TILE_ROWS
    n_heads, _, n_dcol = rpb.shape
    drow, row_ok = [], []
    for tile in (0, 1, n_tiles - 1):
        kr0 = int(np.clip(NA_TILE_ROWS * tile - NA_ROWS // 2, 0, rows - NA_WIN_ROWS))
        r = NA_TILE_ROWS * tile + np.arange(NA_TILE_ROWS)
        r0 = np.clip(r - NA_ROWS // 2, 0, rows - NA_ROWS)
        krow = kr0 + np.arange(NA_WIN_ROWS)
        row_ok.append((krow[None, :] >= r0[:, None]) & (krow[None, :] < r0[:, None] + NA_ROWS))
        drow.append(np.clip(krow[None, :] - r[:, None] + NA_ROWS - 1, 0, 2 * NA_ROWS - 2))
    drow, row_ok = np.stack(drow), np.stack(row_ok)
    qc = np.arange(GRID_W)
    qcol0 = np.clip(qc - NA_COLS // 2, 0, GRID_W - NA_COLS)
    kc = np.arange(GRID_W)
    col_ok = (kc[None, :] >= qcol0[:, None]) & (kc[None, :] < qcol0[:, None] + NA_COLS)
    dcol = np.clip(kc[None, :] - qc[:, None] + NA_COLS - 1, 0, 2 * NA_COLS - 2)
    pick_col = jnp.asarray(dcol[None] == np.arange(n_dcol)[:, None, None], F32)
    slabs = jnp.einsum('hrd,dqk->hrqk', rpb.astype(F32), pick_col, precision=lax.Precision.HIGHEST)
    slabs = jnp.where(jnp.asarray(col_ok)[None, None], slabs * LOG2E, NEG_INF)
    masked = jnp.full((n_heads, GRID_W, GRID_W), NEG_INF, F32)
    pats = []
    for p in range(3):
        per_row = [jnp.stack([slabs[:, drow[p, a, m]] if row_ok[p, a, m] else masked
                              for m in range(NA_WIN_ROWS)], axis=2)
                   for a in range(NA_TILE_ROWS)]
        pats.append(jnp.stack(per_row, axis=1))
    vals = jnp.stack(pats, axis=1)
    return vals.reshape(n_heads, 3, NA_TILE_ROWS * GRID_W, NA_WIN_ROWS * GRID_W)


def _na_kernel(q_ref, k_ref, v_ref, bias_ref, o_ref, vone_ref, *, ctx_len, rows):
    tq = NA_TILE_ROWS * GRID_W
    tk = NA_WIN_ROWS * GRID_W
    n_tiles = rows // NA_TILE_ROWS
    lc = ctx_len
    _with_ones(v_ref, vone_ref)

    s = _dot_nt(q_ref[0, 0:lc, :], k_ref[0, 0:lc, :])
    (p,), _ = _softmax_probs([s])
    o_ref[0, 0:lc, :] = _normalise(_dot(p, vone_ref[0:lc, :])).astype(o_ref.dtype)

    for i in range(n_tiles):
        qs = lc + i * tq
        kr0 = min(max(NA_TILE_ROWS * i - NA_ROWS // 2, 0), rows - NA_WIN_ROWS)
        ks = lc + kr0 * GRID_W
        pat = 0 if i == 0 else (2 if i == n_tiles - 1 else 1)
        q = q_ref[0, qs:qs + tq, :]
        s_loc = _dot_nt(q, k_ref[0, ks:ks + tk, :]) + bias_ref[0, pat]
        s_ctx = _dot_nt(q, k_ref[0, 0:lc, :])
        (p_loc, p_ctx), _ = _softmax_probs([s_loc, s_ctx])
        acc = _dot(p_loc, vone_ref[ks:ks + tk, :]) + _dot(p_ctx, vone_ref[0:lc, :])
        o_ref[0, qs:qs + tq, :] = _normalise(acc).astype(o_ref.dtype)


def neighbourhood_attention(qkv, bias, ctx_len, n_heads):
    b, p, _ = qkv.shape
    rows = (p - ctx_len) // GRID_W
    assert rows % NA_TILE_ROWS == 0 and rows >= NA_WIN_ROWS
    tq, tk = NA_TILE_ROWS * GRID_W, NA_WIN_ROWS * GRID_W
    assert ctx_len % 16 == 0 and ctx_len % tq == 0
    blk = lambda off: pl.BlockSpec((1, p, HEAD_DIM), lambda h, b: (b, 0, off + h))
    return pl.pallas_call(
        functools.partial(_na_kernel, ctx_len=ctx_len, rows=rows),
        grid=(n_heads, b),
        in_specs=[blk(0), blk(n_heads), blk(2 * n_heads),
                  pl.BlockSpec((1, 3, tq, tk), lambda h, b: (h, 0, 0, 0))],
        out_specs=blk(0),
        out_shape=jax.ShapeDtypeStruct((b, p, n_heads * HEAD_DIM), BF16),
        scratch_shapes=[pltpu.VMEM((p, 2 * LANES), BF16)],
        compiler_params=_cparams(("parallel", "parallel")),
        name="na_attention",
    )(qkv, qkv, qkv, bias)


def _swa_kernel(sink_ref, q_ref, k_ref, v_ref, o_ref, vone_ref, *, ctx_len, seq, group):
    lc = ctx_len
    kvh = pl.program_id(1)
    _with_ones(v_ref, vone_ref)

    def stack_heads(q):
        return jnp.concatenate([q[:, g * HEAD_DIM:(g + 1) * HEAD_DIM] for g in range(group)], axis=0)

    def finish(parts_fn, n_q, pv_fn, store):
        ps_all, sink_terms = [], []
        for g in range(group):
            sink = sink_ref[kvh * group + g]
            ps, m = _softmax_probs(parts_fn(g), extra=sink)
            ps_all.append(ps)
            sink_terms.append(jnp.exp2(sink - m))
        n_parts = len(ps_all[0])
        stacked = [jnp.concatenate([ps_all[g][k] for g in range(group)], axis=0) for k in range(n_parts)]
        acc = pv_fn(stacked)
        store(jnp.concatenate([_normalise(acc[g * n_q:(g + 1) * n_q], sink_terms[g]) for g in range(group)], axis=1))

    s_c = _dot_nt(stack_heads(q_ref[0, 0:lc, :]), k_ref[0, 0:lc, :])

    def store_ctx(o):
        o_ref[0, 0:lc, :] = o.astype(o_ref.dtype)

    finish(lambda g: [s_c[g * lc:(g + 1) * lc]], lc,
           lambda st: _dot(st[0], vone_ref[0:lc, :]), store_ctx)

    n_tiles = seq // SWA_TQ

    def tile(t):
        q0 = t * SWA_TQ
        k0 = min(max(q0 - SWA_WINDOW, 0), seq - SWA_TK)
        qs, ks = lc + q0, lc + k0
        q4 = stack_heads(q_ref[0, qs:qs + SWA_TQ, :])
        s_loc = _dot_nt(q4, k_ref[0, ks:ks + SWA_TK, :])
        s_ctx = _dot_nt(q4, k_ref[0, 0:lc, :])
        dpos = (lax.broadcasted_iota(jnp.int32, (SWA_TQ, SWA_TK), 1)
                - lax.broadcasted_iota(jnp.int32, (SWA_TQ, SWA_TK), 0)) + (k0 - q0)
        valid = jnp.abs(dpos) <= SWA_WINDOW

        def parts(g):
            sl = slice(g * SWA_TQ, (g + 1) * SWA_TQ)
            return [jnp.where(valid, s_loc[sl], NEG_INF), s_ctx[sl]]

        def store(o):
            o_ref[0, qs:qs + SWA_TQ, :] = o.astype(o_ref.dtype)

        finish(parts, SWA_TQ,
               lambda st: _dot(st[0], vone_ref[ks:ks + SWA_TK, :]) + _dot(st[1], vone_ref[0:lc, :]),
               store)

    for t in range(n_tiles):
        tile(t)


def window_attention(qkv, sink, ctx_len, n_heads, n_kv_heads):
    b, p, _ = qkv.shape
    seq = p - ctx_len
    group = n_heads // n_kv_heads
    assert seq % SWA_TQ == 0 and seq >= SWA_TK and ctx_len % SWA_WINDOW == 0
    kv = lambda off: pl.BlockSpec((1, p, HEAD_DIM), lambda b, h: (b, 0, off + h))
    qo = pl.BlockSpec((1, p, group * HEAD_DIM), lambda b, h: (b, 0, h))
    return pl.pallas_call(
        functools.partial(_swa_kernel, ctx_len=ctx_len, seq=seq, group=group),
        grid=(b, n_kv_heads),
        in_specs=[pl.BlockSpec(memory_space=pltpu.SMEM), qo, kv(n_heads), kv(n_heads + n_kv_heads)],
        out_specs=qo,
        out_shape=jax.ShapeDtypeStruct((b, p, n_heads * HEAD_DIM), BF16),
        scratch_shapes=[pltpu.VMEM((p, 2 * LANES), BF16)],
        compiler_params=_cparams(("parallel", "parallel")),
        name="swa_attention",
    )(sink.astype(F32) * LOG2E, qkv, qkv, qkv)


def _mla_down_kernel(x_ref, w_ref, gq_ref, gkv_ref, cos_ref, sin_ref, cq_ref, ckv_ref, kr_ref, *, q_rank, kv_rank):
    acc = _dot(x_ref[0], w_ref[...])

    def rms(x, g):
        ms = jnp.mean(x * x, axis=-1, keepdims=True)
        return x * lax.rsqrt(ms + NORM_EPS) * g

    cq_ref[0] = rms(acc[:, :q_rank], gq_ref[...]).astype(cq_ref.dtype)
    ckv_ref[0] = rms(acc[:, q_rank:q_rank + kv_rank], gkv_ref[...]).astype(ckv_ref.dtype)
    kr_ref[0] = _rope(acc[:, q_rank + kv_rank:], cos_ref[...], sin_ref[...]).astype(kr_ref.dtype)


def mla_down(x, w_down, gq, gkv, cos, sin, q_rank, kv_rank):
    b, p, k = x.shape
    n = w_down.shape[1]
    tm = 544 if p % 544 == 0 else _row_tile(p)
    row = lambda width: pl.BlockSpec((1, tm, width), lambda b, i: (b, i, 0))
    return pl.pallas_call(
        functools.partial(_mla_down_kernel, q_rank=q_rank, kv_rank=kv_rank),
        grid=(b, p // tm),
        in_specs=[row(k), pl.BlockSpec((k, n), lambda b, i: (0, 0)),
                  pl.BlockSpec((1, q_rank), lambda b, i: (0, 0)),
                  pl.BlockSpec((1, kv_rank), lambda b, i: (0, 0)),
                  pl.BlockSpec((tm, LANES), lambda b, i: (i, 0)),
                  pl.BlockSpec((tm, LANES), lambda b, i: (i, 0))],
        out_specs=[row(q_rank), row(kv_rank), row(LANES)],
        out_shape=[jax.ShapeDtypeStruct((b, p, q_rank), BF16),
                   jax.ShapeDtypeStruct((b, p, kv_rank), BF16),
                   jax.ShapeDtypeStruct((b, p, LANES), BF16)],
        compiler_params=_cparams(("parallel", "parallel")),
        name="mla_down",
    )(x, w_down, gq.reshape(1, q_rank), gkv.reshape(1, kv_rank), cos, sin)


def _mla_kernel(cq_ref, ckv_ref, kr_ref, wq_ref, wkv_ref, cos_ref, sin_ref, o_ref, qcat_ref, kcat_ref, vone_ref, *,
                ctx_len):
    lc = ctx_len
    p_all = vone_ref.shape[0]
    kv = _dot(ckv_ref[0], wkv_ref[...])
    kcat_ref[0:LANES, :] = kv[:, :LANES].T.astype(BF16)
    kcat_ref[LANES:2 * LANES, :] = kr_ref[0].astype(F32).T.astype(BF16)
    vone_ref[:, 0:LANES] = kv[:, LANES:].astype(BF16)
    vone_ref[:, LANES:2 * LANES] = jnp.ones((p_all, LANES), BF16)
    q = _dot(cq_ref[0], wq_ref[...])
    qcat_ref[:, 0:LANES] = q[:, :LANES].astype(BF16)
    qcat_ref[:, LANES:2 * LANES] = _rope(q[:, LANES:], cos_ref[...], sin_ref[...]).astype(BF16)

    def attend(qs, n_q, chunks):
        q = qcat_ref[pl.ds(qs, n_q), :]
        m = jnp.full((n_q, 1), NEG_INF, F32)
        acc = jnp.zeros((n_q, 2 * LANES), F32)
        for c0, c1 in chunks:
            s = _dot(q, kcat_ref[:, c0:c1])
            m_new = jnp.maximum(m, _row_reduce(jnp.maximum, jnp.max, [s]))
            p = jnp.exp2((s - m_new).astype(BF16))
            acc = jnp.exp2(m - m_new) * acc + _dot(p, vone_ref[c0:c1, :])
            m = m_new
        o_ref[0, pl.ds(qs, n_q), :] = (acc[:, :LANES] * (1.0 / acc[:, LANES:LANES + 1])).astype(o_ref.dtype)

    attend(0, lc, [(0, lc)])
    all_chunks = [(0, lc)] + [(c, c + MLA_TK) for c in range(lc, p_all, MLA_TK)]

    for qs in range(lc, p_all, MLA_TQ):
        attend(qs, MLA_TQ, all_chunks)


def latent_attention(cq, ckv, kr, w_q, w_kv, cos, sin, ctx_len, n_heads):
    b, p, q_rank = cq.shape
    kv_rank = ckv.shape[2]
    assert (p - ctx_len) % MLA_TQ == 0 and (p - ctx_len) % MLA_TK == 0 and ctx_len % 16 == 0
    per_batch = lambda width: pl.BlockSpec((1, p, width), lambda b, h: (b, 0, 0), pipeline_mode=pl.Buffered(1))
    table = pl.BlockSpec((p, LANES), lambda b, h: (0, 0), pipeline_mode=pl.Buffered(1))
    return pl.pallas_call(
        functools.partial(_mla_kernel, ctx_len=ctx_len),
        grid=(b, n_heads),
        in_specs=[per_batch(q_rank), per_batch(kv_rank), per_batch(LANES),
                  pl.BlockSpec((q_rank, 2 * LANES), lambda b, h: (0, h)),
                  pl.BlockSpec((kv_rank, 2 * LANES), lambda b, h: (0, h)),
                  table, table],
        out_specs=pl.BlockSpec((1, p, LANES), lambda b, h: (b, 0, h)),
        out_shape=jax.ShapeDtypeStruct((b, p, n_heads * LANES), BF16),
        scratch_shapes=[pltpu.VMEM((p, 2 * LANES), BF16), pltpu.VMEM((2 * LANES, p), BF16),
                        pltpu.VMEM((p, 2 * LANES), BF16)],
        compiler_params=_cparams(("parallel", "parallel")),
        name="mla_attention",
    )(cq, ckv, kr, w_q, w_kv, cos, sin)


def gather_rows(src, idx):
    n = idx.shape[0]
    width = src.shape[1]
    win = GATHER_WINDOW
    info = plsc.get_sparse_core_info()
    n_workers = info.num_cores * info.num_subcores
    assert n % (n_workers * 2 * win) == 0
    per_worker = n // n_workers
    mesh = plsc.VectorSubcoreMesh(core_axis_name="core", subcore_axis_name="subcore")

    @functools.partial(
        pl.kernel, out_type=jax.ShapeDtypeStruct((n, width), src.dtype), mesh=mesh,
        scratch_types=[pltpu.VMEM((per_worker,), jnp.int32),
                       pltpu.VMEM((win, width), src.dtype), pltpu.VMEM((win, width), src.dtype),
                       pltpu.SemaphoreType.DMA, pltpu.SemaphoreType.DMA,
                       pltpu.SemaphoreType.DMA, pltpu.SemaphoreType.DMA],
        name="gather_rows")
    def gather(src_hbm, idx_hbm, out_hbm, idx_vmem, rows0, rows1, gsem0, gsem1, osem0, osem1):
        worker = lax.axis_index("subcore") * info.num_cores + lax.axis_index("core")
        base = worker * per_worker
        pltpu.sync_copy(idx_hbm.at[pl.ds(base, per_worker)], idx_vmem)

        def fetch(r, rows, sem):
            return pltpu.make_async_copy(src_hbm.at[idx_vmem.at[pl.ds(r, win)]], rows, sem)

        def flush(r, rows, sem):
            return pltpu.make_async_copy(rows, out_hbm.at[pl.ds(base + r, win)], sem)

        fetch(0, rows0, gsem0).start()

        @pl.loop(0, per_worker, step=2 * win)
        def _(r):
            fetch(r, rows0, gsem0).wait()

            @pl.when(r > 0)
            def _():
                flush(r - win, rows1, osem1).wait()

            fetch(r + win, rows1, gsem1).start()
            flush(r, rows0, osem0).start()
            fetch(r + win, rows1, gsem1).wait()
            flush(r, rows0, osem0).wait()

            @pl.when(r + 2 * win < per_worker)
            def _():
                fetch(r + 2 * win, rows0, gsem0).start()

            flush(r + win, rows1, osem1).start()

        flush(per_worker - win, rows1, osem1).wait()

    return gather(src, idx)


def scatter_rows(src, idx, n_out):
    n = idx.shape[0]
    n_src, width = src.shape
    win = GATHER_WINDOW
    info = plsc.get_sparse_core_info()
    n_workers = info.num_cores * info.num_subcores
    assert n % (n_workers * 2 * win) == 0
    per_worker = n // n_workers
    assert n_src % per_worker == 0
    n_steps = per_worker // win
    mesh = plsc.VectorSubcoreMesh(core_axis_name="core", subcore_axis_name="subcore")

    @functools.partial(
        pl.kernel, out_type=jax.ShapeDtypeStruct((n_out, width), src.dtype), mesh=mesh,
        scratch_types=[pltpu.VMEM((n_steps, win), jnp.int32),
                       pltpu.VMEM((win, width), src.dtype), pltpu.VMEM((win, width), src.dtype),
                       pltpu.SemaphoreType.DMA, pltpu.SemaphoreType.DMA,
                       pltpu.SemaphoreType.DMA, pltpu.SemaphoreType.DMA],
        name="scatter_rows")
    def scatter(src_hbm, idx_hbm, out_hbm, idx_vmem, rows0, rows1, lsem0, lsem1, ssem0, ssem1):
        worker = lax.axis_index("subcore") * info.num_cores + lax.axis_index("core")
        src_base = lax.rem(worker * per_worker, n_src)
        pltpu.sync_copy(idx_hbm.at[worker], idx_vmem)

        def load(j, rows, sem):
            return pltpu.make_async_copy(src_hbm.at[pl.ds(src_base + j * win, win)], rows, sem)

        def store(j, rows, sem):
            return pltpu.make_async_copy(rows, out_hbm.at[idx_vmem.at[j]], sem)

        load(0, rows0, lsem0).start()

        @pl.loop(0, n_steps, step=2)
        def _(j):
            load(j, rows0, lsem0).wait()

            @pl.when(j > 0)
            def _():
                store(j - 1, rows1, ssem1).wait()

            load(j + 1, rows1, lsem1).start()
            store(j, rows0, ssem0).start()
            load(j + 1, rows1, lsem1).wait()
            store(j, rows0, ssem0).wait()

            @pl.when(j + 2 < n_steps)
            def _():
                load(j + 2, rows0, lsem0).start()

            store(j + 1, rows1, ssem1).start()

        store(n_steps - 1, rows1, ssem1).wait()

    return scatter(src, idx.reshape(n_workers, n_steps, win))


def _expert_kernel(blk_e_ref, n_used_ref, n_valid_ref, x_ref, wg_ref, wu_ref, wd_ref, y_ref, wg_bf, wu_bf, wd_bf):
    i = pl.program_id(0)
    used = i < n_used_ref[0]
    new_expert = (i == 0) | (blk_e_ref[i] != blk_e_ref[jnp.maximum(i - 1, 0)])

    @pl.when(used & new_expert)
    def _():
        wg_bf[...] = wg_ref[0].astype(BF16)
        wu_bf[...] = wu_ref[0].astype(BF16)
        wd_bf[...] = wd_ref[0].astype(BF16)

    @pl.when(used)
    def _():
        row = lax.broadcasted_iota(jnp.int32, (x_ref.shape[0], 1), 0)
        x = _unpack_bf16_pairs(jnp.where(row < n_valid_ref[i], x_ref[...], 0)).astype(BF16)
        de = wg_bf.shape[1]
        chunk = de // 2
        y = None
        for c0 in range(0, de, chunk):
            g = _dot(x, wg_bf[:, c0:c0 + chunk])
            u = _dot(x, wu_bf[:, c0:c0 + chunk])
            a = (g * jax.nn.sigmoid(g) * u).astype(BF16)
            part = _dot(a, wd_bf[c0:c0 + chunk, :])
            y = part if y is None else y + part
        y_ref[...] = _pack_bf16_pairs(y)

    @pl.when(jnp.logical_not(used))
    def _():
        y_ref[...] = jnp.zeros_like(y_ref)


def expert_ffn(x_disp, blk_e, n_used, n_valid, w_gate, w_up, w_down, layer):
    rows, half = x_disp.shape
    d = 2 * half
    n_blk = rows // MOE_BLOCK
    de = w_gate.shape[3]
    grid_spec = pltpu.PrefetchScalarGridSpec(
        num_scalar_prefetch=3,
        grid=(n_blk,),
        in_specs=[pl.BlockSpec((MOE_BLOCK, half), lambda i, be, nu, nv: (i, 0)),
                  pl.BlockSpec((None, 1, d, de), lambda i, be, nu, nv: (layer, be[i], 0, 0)),
                  pl.BlockSpec((None, 1, d, de), lambda i, be, nu, nv: (layer, be[i], 0, 0)),
                  pl.BlockSpec((None, 1, de, d), lambda i, be, nu, nv: (layer, be[i], 0, 0))],
        out_specs=pl.BlockSpec((MOE_BLOCK, half), lambda i, be, nu, nv: (i, 0)),
        scratch_shapes=[pltpu.VMEM((d, de), BF16), pltpu.VMEM((d, de), BF16), pltpu.VMEM((de, d), BF16)],
    )
    return pl.pallas_call(
        _expert_kernel,
        grid_spec=grid_spec,
        out_shape=jax.ShapeDtypeStruct((rows, half), jnp.int32),
        compiler_params=_cparams(("arbitrary",)),
        name="expert_ffn",
    )(blk_e, n_used, n_valid, x_disp, w_gate, w_up, w_down)


def _moe_residual(h_ref, y0_ref, y1_ref, wt_ref, mod_ref, gate_idx):
    wt = wt_ref[0]
    moe = wt[:, 0:1] * _unpack_bf16_pairs(y0_ref[0, 0]) + wt[:, 1:2] * _unpack_bf16_pairs(y1_ref[0, 0])
    return h_ref[0] + mod_ref[0, gate_idx:gate_idx + 1, :] * moe


def _combine_next_kernel(h_ref, y0_ref, y1_ref, wt_ref, mod_ref, g_ref, nmod_ref, o_ref, hm_ref, *, gate_idx):
    h_new = _moe_residual(h_ref, y0_ref, y1_ref, wt_ref, mod_ref, gate_idx)
    o_ref[0] = h_new
    hm_ref[0] = _rms_mod(h_new, g_ref[...], nmod_ref[0, 0:1, :], nmod_ref[0, 1:2, :]).astype(hm_ref.dtype)


def _combine_final_kernel(h_ref, y0_ref, y1_ref, wt_ref, mod_ref, g_ref, o_ref, *, gate_idx):
    x = _moe_residual(h_ref, y0_ref, y1_ref, wt_ref, mod_ref, gate_idx)
    ms = jnp.mean(x * x, axis=-1, keepdims=True)
    o_ref[0] = x * lax.rsqrt(ms + NORM_EPS) * g_ref[...]


def combine(h, y_pairs, wt, modtab, gate_idx, next_g, next_modtab, ctx_len):
    b, p, d = h.shape
    last = next_modtab is None
    skip = ctx_len // ROW_TILE if last else 0
    row = pl.BlockSpec((1, ROW_TILE, d), lambda b, j: (b, j + skip, 0))
    mod = pl.BlockSpec((1, N_MOD, d), lambda b, j: (2 * b + jnp.minimum(j + skip, 1), 0, 0))
    in_specs = [row,
                pl.BlockSpec((1, 1, ROW_TILE, d // 2), lambda b, j: (0, b, j + skip, 0)),
                pl.BlockSpec((1, 1, ROW_TILE, d // 2), lambda b, j: (1, b, j + skip, 0)),
                pl.BlockSpec((1, ROW_TILE, LANES), lambda b, j: (b, j + skip, 0)),
                mod, pl.BlockSpec((1, d), lambda b, j: (0, 0))]
    args = [h, y_pairs, y_pairs, wt, modtab.reshape(b * 2, N_MOD, d), next_g.reshape(1, d)]
    out_row = pl.BlockSpec((1, ROW_TILE, d), lambda b, j: (b, j, 0))
    if last:
        return pl.pallas_call(
            functools.partial(_combine_final_kernel, gate_idx=gate_idx),
            grid=(b, (p - ctx_len) // ROW_TILE),
            in_specs=in_specs,
            out_specs=out_row,
            out_shape=jax.ShapeDtypeStruct((b, p - ctx_len, d), F32),
            compiler_params=_cparams(("parallel", "parallel")),
            name="moe_combine_final",
        )(*args)
    return pl.pallas_call(
        functools.partial(_combine_next_kernel, gate_idx=gate_idx),
        grid=(b, p // ROW_TILE),
        in_specs=in_specs + [mod],
        out_specs=[out_row, out_row],
        out_shape=[jax.ShapeDtypeStruct((b, p, d), F32), jax.ShapeDtypeStruct((b, p, d), BF16)],
        compiler_params=_cparams(("parallel", "parallel")),
        name="moe_combine",
    )(*args, next_modtab.reshape(b * 2, N_MOD, d))


def hier_moe(h, f, route, wt, counts, modtab, gate_idx, w_gate, w_up, w_down, layer, next_g, next_modtab, ctx_len):
    b, p, d = h.shape
    n_tok = b * p
    n_assign = 2 * n_tok
    e1, e2, rank1, rank2 = route.reshape(n_tok, LANES)[:, :4].T
    cnt = counts[0, :N_EXPERTS].astype(jnp.int32)
    pcounts = (cnt + MOE_BLOCK - 1) // MOE_BLOCK * MOE_BLOCK
    pend = jnp.cumsum(pcounts)
    pstart = pend - pcounts
    dest_by_slot = jnp.concatenate([pstart[e1] + rank1, pstart[e2] + rank2]).astype(jnp.int32)
    n_blk = -(-n_assign // MOE_BLOCK) + N_EXPERTS
    rows_total = n_blk * MOE_BLOCK
    blk_row0 = jnp.arange(n_blk, dtype=jnp.int32) * MOE_BLOCK
    blk_e = jnp.minimum(jnp.sum((pend[None, :] <= blk_row0[:, None]).astype(jnp.int32), axis=1), N_EXPERTS - 1)
    n_valid = jnp.clip(cnt[blk_e] - (blk_row0 - pstart[blk_e]), 0, MOE_BLOCK).astype(jnp.int32)
    n_used = (pend[-1] // MOE_BLOCK).astype(jnp.int32).reshape(1)

    x_disp = scatter_rows(f.reshape(n_tok, d // 2), dest_by_slot, rows_total)
    y = expert_ffn(x_disp, blk_e, n_used, n_valid, w_gate, w_up, w_down, layer)
    y_pairs = gather_rows(y, dest_by_slot).reshape(2, b, p, d // 2)
    return combine(h, y_pairs, wt, modtab, gate_idx, next_g, next_modtab, ctx_len)


def kernel(x, c, ctx, c_ctx, mod_w, mod_b, norm_mix_g, norm_ffn_g, router_grp_w, router_grp_b, router_exp_w, router_exp_b, exp_w_gate, exp_w_up, exp_w_down, l0_na_w_qkv, l0_na_rpb, l0_na_w_o, l1_swa_w_qkv, l1_swa_sink, l1_swa_w_o, l2_mla_w_dq, l2_mla_q_norm_g, l2_mla_w_uq, l2_mla_w_dkv, l2_mla_kv_norm_g, l2_mla_w_ukv, l2_mla_w_o, l3_na_w_qkv, l3_na_rpb, l3_na_w_o, final_norm_g):
    b, s, d = x.shape
    lc = ctx.shape[1]
    n_heads = d // HEAD_DIM
    n_kv_heads = n_heads // 4
    depth = mod_w.shape[0]
    rows = s // GRID_W

    modtabs = modulation_tables(c, c_ctx, mod_w, mod_b)

    def scale_q_cols(w):
        n_q = n_heads * HEAD_DIM
        q_scale = HEAD_DIM ** -0.5 * LOG2E
        return jnp.concatenate([w[:, :n_q] * q_scale, w[:, n_q:]], axis=1).astype(BF16)

    def na_mixer(hm, w_qkv, rpb):
        qkv = project(hm, scale_q_cols(w_qkv), tn=2048)
        return neighbourhood_attention(qkv, na_bias_table(rpb, rows), lc, n_heads)

    def swa_mixer(hm):
        cos, sin = rope_tables_full(s, lc)
        qkv = project(hm, scale_q_cols(l1_swa_w_qkv), tn=1024, rope=(cos, sin, (n_heads + n_kv_heads) * HEAD_DIM))
        return window_attention(qkv, l1_swa_sink, lc, n_heads, n_kv_heads)

    def mla_mixer(hm):
        q_rank = l2_mla_w_dq.shape[1]
        kv_rank = l2_mla_kv_norm_g.shape[0]
        cos, sin = rope_tables_mla(s, lc)
        w_down = jnp.concatenate([l2_mla_w_dq, l2_mla_w_dkv[:, :kv_rank],
                                  _spread_rope_cols(l2_mla_w_dkv[:, kv_rank:])], axis=1).astype(BF16)
        cq, ckv, kr = mla_down(hm, w_down, l2_mla_q_norm_g, l2_mla_kv_norm_g, cos, sin, q_rank, kv_rank)
        w_uq = l2_mla_w_uq.reshape(q_rank, n_heads, MLA_NOPE_DIM + MLA_ROPE_DIM)
        q_scale = (MLA_NOPE_DIM + MLA_ROPE_DIM) ** -0.5 * LOG2E
        half = MLA_ROPE_DIM // 2
        gap = jnp.zeros((q_rank, n_heads, half), l2_mla_w_uq.dtype)
        w_q = jnp.concatenate([w_uq[:, :, :MLA_NOPE_DIM], w_uq[:, :, MLA_NOPE_DIM:MLA_NOPE_DIM + half], gap,
                               w_uq[:, :, MLA_NOPE_DIM + half:], gap], axis=2).reshape(q_rank, n_heads * 2 * LANES)
        return latent_attention(cq, ckv, kr, (w_q * q_scale).astype(BF16), l2_mla_w_ukv.astype(BF16), cos, sin,
                                lc, n_heads)

    h, hm = join_norm_modulate(ctx, x, norm_mix_g[0], modtabs[0])
    for i in range(depth):
        modtab = modtabs[i]
        mixer = i % 3
        if mixer == 0:
            w_qkv, rpb, w_o = (l0_na_w_qkv, l0_na_rpb, l0_na_w_o) if i == 0 else (l3_na_w_qkv, l3_na_rpb, l3_na_w_o)
            y = na_mixer(hm, w_qkv, rpb)
        elif mixer == 1:
            y, w_o = swa_mixer(hm), l1_swa_w_o
        else:
            y, w_o = mla_mixer(hm), l2_mla_w_o
        h, f, route, wt, counts = attn_out_route(y, w_o.astype(BF16), h, modtab, norm_ffn_g[i], router_grp_w[i],
                                                 router_grp_b[i], router_exp_w[i], router_exp_b[i], lc,
                                                 n_sub=1)
        if i + 1 < depth:
            h, hm = hier_moe(h, f, route, wt, counts, modtab, 5, exp_w_gate, exp_w_up, exp_w_down, i,
                             norm_mix_g[i + 1], modtabs[i + 1], lc)
        else:
            return hier_moe(h, f, route, wt, counts, modtab, 5, exp_w_gate, exp_w_up, exp_w_down, i,
                            final_norm_g, None, lc)
```
